```python
import math
import jax, jax.numpy as jnp
from jax import lax
import numpy as np


D_MODEL = 1024
BATCH = 8
SEQ = 2048
DEPTH = 1

N_HEADS_A = 4
HEAD_QK = 64
HEAD_V = 2 * HEAD_QK
ATTN_WIDTH = N_HEADS_A * HEAD_V
Q_BLOCK = 128
CONV_WIDTH = 512
CONV_KERNEL = 31
Q_COLS = N_HEADS_A * 2 * HEAD_QK
K_COLS = N_HEADS_A * 2 * HEAD_QK
V_COLS = ATTN_WIDTH
GLU_COLS = 2 * CONV_WIDTH
GATE_COLS = 2 * D_MODEL
IN_COLS = Q_COLS + K_COLS + V_COLS + GLU_COLS + GATE_COLS
D_FF = ((8 * D_MODEL + 3 * 256 - 1) // (3 * 256)) * 256
N_BUCKETS = 32
MAX_EXACT = 16
MAX_DISTANCE = 128
EPS = 1e-6
NEG_INF = -1e30

kernel_name = 'hybrid_diffattn_conformer_adaln_block'


def rmsnorm(x, g):
    xf = x.astype(jnp.float32)
    y = xf * lax.rsqrt(jnp.mean(xf * xf, axis=-1, keepdims=True) + EPS)
    return y.astype(x.dtype) * g


def layernorm(x, g, b):
    xf = x.astype(jnp.float32)
    mu = jnp.mean(xf, axis=-1, keepdims=True)
    var = jnp.mean(jnp.square(xf - mu), axis=-1, keepdims=True)
    y = (xf - mu) * lax.rsqrt(var + EPS)
    return y.astype(x.dtype) * g + b


def t5_bucket(dist):
    n = jnp.maximum(dist, 0)
    large = MAX_EXACT + (jnp.log(jnp.maximum(n, 1).astype(jnp.float32) / MAX_EXACT)
                         / math.log(MAX_DISTANCE / MAX_EXACT)
                         * (N_BUCKETS - MAX_EXACT)).astype(jnp.int32)
    large = jnp.minimum(large, N_BUCKETS - 1)
    return jnp.where(n < MAX_EXACT, n, large)


def diff_attention(q, k, v, lam, rel_bias):
    b, s = q.shape[0], q.shape[1]
    nb = s // Q_BLOCK
    scale = HEAD_QK ** -0.5
    q1 = jnp.transpose(q[:, :, :, 0], (0, 2, 1, 3))
    q2 = jnp.transpose(q[:, :, :, 1], (0, 2, 1, 3))
    k1 = jnp.transpose(k[:, :, :, 0], (0, 2, 1, 3))
    k2 = jnp.transpose(k[:, :, :, 1], (0, 2, 1, 3))
    vh = jnp.transpose(v, (0, 2, 1, 3))
    q1b = jnp.moveaxis(q1.reshape(b, N_HEADS_A, nb, Q_BLOCK, HEAD_QK), 2, 0)
    q2b = jnp.moveaxis(q2.reshape(b, N_HEADS_A, nb, Q_BLOCK, HEAD_QK), 2, 0)
    starts = jnp.arange(nb, dtype=jnp.int32) * Q_BLOCK
    k_pos = jnp.arange(s, dtype=jnp.int32)

    def block(args):
        qa, qb, s0 = args
        q_pos = s0 + jnp.arange(Q_BLOCK, dtype=jnp.int32)
        dist = q_pos[:, None] - k_pos[None, :]
        bias = jnp.transpose(rel_bias[t5_bucket(dist)], (2, 0, 1)).astype(jnp.float32)
        causal = (dist >= 0)[None, None]
        s1 = jnp.einsum('bhqd,bhkd->bhqk', qa, k1).astype(jnp.float32) * scale + bias[None]
        s2 = jnp.einsum('bhqd,bhkd->bhqk', qb, k2).astype(jnp.float32) * scale + bias[None]
        p1 = jax.nn.softmax(jnp.where(causal, s1, NEG_INF), axis=-1)
        p2 = jax.nn.softmax(jnp.where(causal, s2, NEG_INF), axis=-1)
        w = (p1 - lam.astype(jnp.float32) * p2).astype(vh.dtype)
        return jnp.einsum('bhqk,bhkv->bhqv', w, vh)

    out = lax.map(block, (q1b, q2b, starts))
    out = jnp.moveaxis(out, 0, 2).reshape(b, N_HEADS_A, s, HEAD_V)
    return jnp.transpose(out, (0, 2, 1, 3))


def causal_depthwise_conv(u, w, bias):
    out = lax.conv_general_dilated(
        u, w[:, None, :], window_strides=(1,), padding=[(CONV_KERNEL - 1, 0)],
        dimension_numbers=('NWC', 'WIO', 'NWC'), feature_group_count=u.shape[-1])
    return out + bias


def setup_inputs(seed: int = 0) -> dict:
    key = jax.random.key(seed)
    ks = jax.random.split(key, 24)
    f = jnp.float32
    nrm = lambda k, shape, s: jax.random.normal(k, shape, f) * s
    L, D = DEPTH, D_MODEL
    return {
        'x': nrm(ks[0], (BATCH, SEQ, D), 1.0),
        'c': nrm(ks[1], (BATCH, D), 1.0),
        'w_ada': nrm(ks[2], (L, D, 6 * D), D ** -0.5),
        'b_ada': nrm(ks[3], (L, 6 * D), 0.02),
        'norm1_g': 1.0 + nrm(ks[4], (L, D), 0.02),
        'norm2_g': 1.0 + nrm(ks[5], (L, D), 0.02),
        'final_g': 1.0 + nrm(ks[6], (D,), 0.02),
        'w_in': nrm(ks[7], (L, D, IN_COLS), D ** -0.5),
        'lambda_q1': nrm(ks[8], (L, HEAD_QK), 0.1),
        'lambda_k1': nrm(ks[9], (L, HEAD_QK), 0.1),
        'lambda_q2': nrm(ks[10], (L, HEAD_QK), 0.1),
        'lambda_k2': nrm(ks[11], (L, HEAD_QK), 0.1),
        'rel_bias': nrm(ks[12], (N_BUCKETS, N_HEADS_A), 0.5),
        'attn_sub_g': 1.0 + nrm(ks[13], (L, HEAD_V), 0.02),
        'w_o_attn': nrm(ks[14], (L, ATTN_WIDTH, D), ATTN_WIDTH ** -0.5),
        'conv_w': nrm(ks[15], (L, CONV_KERNEL, CONV_WIDTH), CONV_KERNEL ** -0.5),
        'conv_b': nrm(ks[16], (L, CONV_WIDTH), 0.02),
        'conv_ln_g': 1.0 + nrm(ks[17], (L, CONV_WIDTH), 0.02),
        'conv_ln_b': nrm(ks[18], (L, CONV_WIDTH), 0.02),
        'w_o_conv': nrm(ks[19], (L, CONV_WIDTH, D), CONV_WIDTH ** -0.5),
        'b_o_conv': nrm(ks[20], (L, D), 0.02),
        'w_out': nrm(ks[21], (L, D, D), D ** -0.5),
        'w_ffn_in': nrm(ks[22], (L, D, 2 * D_FF), D ** -0.5),
        'w_ffn_out': nrm(ks[23], (L, D_FF, D), D_FF ** -0.5),
    }


def reference(x, c, w_ada, b_ada, norm1_g, norm2_g, final_g, w_in,
              lambda_q1, lambda_k1, lambda_q2, lambda_k2, rel_bias, attn_sub_g,
              w_o_attn, conv_w, conv_b, conv_ln_g, conv_ln_b, w_o_conv, b_o_conv,
              w_out, w_ffn_in, w_ffn_out):
    b, s, _ = x.shape
    c_act = jax.nn.silu(c)
    for l in range(DEPTH):
        ada = c_act @ w_ada[l] + b_ada[l]
        sh1, sc1, gt1, sh2, sc2, gt2 = [a[:, None, :] for a in jnp.split(ada, 6, axis=-1)]

        h = rmsnorm(x, norm1_g[l]) * (1.0 + sc1) + sh1
        proj = h @ w_in[l]
        q, k, v, glu, gates = jnp.split(
            proj, np.cumsum([Q_COLS, K_COLS, V_COLS, GLU_COLS]).tolist(), axis=-1)
        g_attn, g_conv = jnp.split(gates, 2, axis=-1)

        lam_init = 0.8 - 0.6 * math.exp(-0.3 * l)
        lam = (jnp.exp(jnp.sum(lambda_q1[l] * lambda_k1[l]))
               - jnp.exp(jnp.sum(lambda_q2[l] * lambda_k2[l])) + lam_init)
        q = q.reshape(b, s, N_HEADS_A, 2, HEAD_QK)
        k = k.reshape(b, s, N_HEADS_A, 2, HEAD_QK)
        v = v.reshape(b, s, N_HEADS_A, HEAD_V)
        a = diff_attention(q, k, v, lam, rel_bias)
        a = rmsnorm(a, attn_sub_g[l]) * (1.0 - lam_init)
        a = a.reshape(b, s, ATTN_WIDTH) @ w_o_attn[l]

        u_lin, u_gate = jnp.split(glu, 2, axis=-1)
        u = u_lin * jax.nn.sigmoid(u_gate)
        u = causal_depthwise_conv(u, conv_w[l], conv_b[l])
        u = jax.nn.silu(layernorm(u, conv_ln_g[l], conv_ln_b[l]))
        cv = u @ w_o_conv[l] + b_o_conv[l]

        y = jax.nn.sigmoid(g_attn) * a + jax.nn.sigmoid(g_conv) * cv
        x = x + gt1 * (y @ w_out[l])

        h2 = rmsnorm(x, norm2_g[l]) * (1.0 + sc2) + sh2
        fg, fu = jnp.split(h2 @ w_ffn_in[l], 2, axis=-1)
        x = x + gt2 * ((jax.nn.silu(fg) * fu) @ w_ffn_out[l])
    return rmsnorm(x, final_g)
```

```python
import functools
import math

import jax
import jax.numpy as jnp
from jax import lax
from jax.experimental import pallas as pl
from jax.experimental.pallas import tpu as pltpu

D_MODEL = 1024
N_HEADS_A = 4
HEAD_QK = 64
HEAD_V = 2 * HEAD_QK
ATTN_WIDTH = N_HEADS_A * HEAD_V
CONV_WIDTH = 512
CONV_KERNEL = 31
QKV_COLS = 3 * ATTN_WIDTH
GLU_COLS = 2 * CONV_WIDTH
GATE_COLS = 2 * D_MODEL
IN_COLS = QKV_COLS + GLU_COLS + GATE_COLS
D_FF = 2816
N_BUCKETS = 32
MAX_EXACT = 16
MAX_DISTANCE = 128
EPS = 1e-6
NEG_INF = -1e30
LAM_INIT = 0.8 - 0.6 * math.exp(-0.3 * 0)

V7X_VMEM_BYTES = 64 * 1024 * 1024
VMEM_LIMIT = V7X_VMEM_BYTES - 12 * 1024 * 1024

ROW_TILE = 512
ATTN_TILE = 256
CONV_TILE = 512
CONV_HALO = 32
CONV_CHUNK = 32
PROJ_CHUNK = 512
FF_CHUNK = 256

BF16 = jnp.bfloat16
F32 = jnp.float32


def _sigmoid(x):
    return 1.0 / (1.0 + jnp.exp(-x))


def _silu(x):
    return x * _sigmoid(x)


def _params(sem, vmem=VMEM_LIMIT):
    return pltpu.CompilerParams(dimension_semantics=sem, vmem_limit_bytes=vmem)


def _resident(shape):
    nd = len(shape)
    return pl.BlockSpec(shape, lambda *_: (0,) * nd, pipeline_mode=pl.Buffered(1))


def _ada_kernel(c_ref, w_ref, b_ref, o_ref):
    ca = _silu(c_ref[...])
    o_ref[...] = jnp.dot(ca, w_ref[...], preferred_element_type=F32,
                         precision=lax.Precision.HIGHEST) + b_ref[...]


def _ada(c, w, b):
    bsz, d = c.shape
    n = w.shape[1]
    tn = 1024
    return pl.pallas_call(
        _ada_kernel,
        grid=(n // tn,),
        in_specs=[pl.BlockSpec((bsz, d), lambda j: (0, 0)),
                  pl.BlockSpec((d, tn), lambda j: (0, j)),
                  pl.BlockSpec((1, tn), lambda j: (0, j))],
        out_specs=pl.BlockSpec((bsz, tn), lambda j: (0, j)),
        out_shape=jax.ShapeDtypeStruct((bsz, n), F32),
        compiler_params=_params(("arbitrary",)),
        name="ada",
    )(c, w, b.reshape(1, n))


def _bias_kernel(rel_ref, o_ref):
    h = pl.program_id(0)
    t = pl.program_id(1)
    T = ATTN_TILE
    r = lax.broadcasted_iota(jnp.int32, (T, T), 0)
    c = lax.broadcasted_iota(jnp.int32, (T, T), 1)
    dist = (1 - t) * T + r - c
    n = jnp.maximum(dist, 0)
    large = MAX_EXACT + (jnp.log(jnp.maximum(n, 1).astype(F32) / MAX_EXACT)
                         / math.log(MAX_DISTANCE / MAX_EXACT)
                         * (N_BUCKETS - MAX_EXACT)).astype(jnp.int32)
    large = jnp.minimum(large, N_BUCKETS - 1)
    bucket = jnp.where(n < MAX_EXACT, n, large)
    far = rel_ref[N_BUCKETS - 1, h]
    bias = jnp.zeros((T, T), F32)
    for b in range(N_BUCKETS - 1):
        bias = jnp.where(bucket == b, rel_ref[b, h] - far, bias)
    o_ref[...] = jnp.where(dist >= 0, bias, NEG_INF)


def _bias_tiles(rel_bias):
    T = ATTN_TILE
    return pl.pallas_call(
        _bias_kernel,
        grid=(N_HEADS_A, 2),
        in_specs=[pl.BlockSpec(memory_space=pltpu.SMEM)],
        out_specs=pl.BlockSpec((None, None, T, T), lambda h, t: (h, t, 0, 0)),
        out_shape=jax.ShapeDtypeStruct((N_HEADS_A, 2, T, T), F32),
        compiler_params=_params(("arbitrary", "arbitrary")),
        name="bias_tiles",
    )(rel_bias)


def _inproj_kernel(x_ref, ada_ref, g_ref, w_ref, qkv_ref, glu_ref, gate_ref):
    x = x_ref[...]
    ms = jnp.mean(x * x, axis=-1, keepdims=True)
    y = x * lax.rsqrt(ms + EPS)
    h = (y * g_ref[...]) * (1.0 + ada_ref[1:2, :]) + ada_ref[0:1, :]
    hb = h.astype(BF16)
    base = 0
    for ref in (qkv_ref, glu_ref, gate_ref):
        width = ref.shape[-1]
        for c in range(0, width, PROJ_CHUNK):
            ref[:, c:c + PROJ_CHUNK] = jnp.dot(
                hb, w_ref[:, base + c:base + c + PROJ_CHUNK],
                preferred_element_type=F32).astype(BF16)
        base += width


def _in_proj(x, ada3, g, w):
    bsz, s, d = x.shape
    tm = ROW_TILE
    row = lambda n: pl.BlockSpec((None, tm, n), lambda b, i: (b, i, 0))
    return pl.pallas_call(
        _inproj_kernel,
        grid=(bsz, s // tm),
        in_specs=[row(d),
                  pl.BlockSpec((None, 6, d), lambda b, i: (b, 0, 0)),
                  _resident((1, d)),
                  _resident((d, IN_COLS))],
        out_specs=[row(QKV_COLS), row(GLU_COLS), row(GATE_COLS)],
        out_shape=[jax.ShapeDtypeStruct((bsz, s, QKV_COLS), BF16),
                   jax.ShapeDtypeStruct((bsz, s, GLU_COLS), BF16),
                   jax.ShapeDtypeStruct((bsz, s, GATE_COLS), BF16)],
        compiler_params=_params(("parallel", "arbitrary")),
        name="in_proj",
    )(x, ada3, g.reshape(1, d), w)


def _attn_kernel(lamv_ref, subg_ref, q_ref, k_ref, v_ref, bias_ref, o_ref,
                 m1_ref, l1_ref, a1_ref, m2_ref, l2_ref, a2_ref):
    T = ATTN_TILE
    i = pl.program_id(2)
    lane = lax.broadcasted_iota(jnp.int32, (T, HEAD_V), 1)
    qs = q_ref[...] * jnp.asarray(HEAD_QK ** -0.5, BF16)
    zero = jnp.zeros_like(qs)
    q1 = jnp.where(lane < HEAD_QK, qs, zero)
    q2 = jnp.where(lane >= HEAD_QK, qs, zero)
    state = ((q1, m1_ref, l1_ref, a1_ref), (q2, m2_ref, l2_ref, a2_ref))
    for _, m_ref, l_ref, a_ref in state:
        m_ref[...] = jnp.full(m_ref.shape, NEG_INF, F32)
        l_ref[...] = jnp.zeros(l_ref.shape, F32)
        a_ref[...] = jnp.zeros(a_ref.shape, F32)

    def process(start, bias):
        kc = k_ref[pl.ds(start, T), :]
        vc = v_ref[pl.ds(start, T), :]
        for qz, m_ref, l_ref, a_ref in state:
            s = lax.dot_general(qz, kc, (((1,), (1,)), ((), ())),
                                preferred_element_type=F32)
            if bias is not None:
                s = s + bias
            m_old = m_ref[...]
            m_new = jnp.maximum(m_old, jnp.max(s, axis=-1, keepdims=True))
            p = jnp.exp(s - m_new)
            alpha = jnp.exp(m_old - m_new)
            l_ref[...] = alpha * l_ref[...] + jnp.sum(p, axis=-1, keepdims=True)
            a_ref[...] = alpha * a_ref[...] + jnp.dot(
                p.astype(BF16), vc, preferred_element_type=F32)
            m_ref[...] = m_new

    def far_body(j, carry):
        process(pl.multiple_of(j * T, T), None)
        return carry

    lax.fori_loop(0, i - 1, far_body, 0)

    @pl.when(i >= 1)
    def _():
        process(pl.multiple_of((i - 1) * T, T), bias_ref[0])

    process(pl.multiple_of(i * T, T), bias_ref[1])

    lv = lamv_ref[...]
    lam = (jnp.exp(jnp.sum(lv[0:1] * lv[1:2], axis=-1, keepdims=True))
           - jnp.exp(jnp.sum(lv[2:3] * lv[3:4], axis=-1, keepdims=True)) + LAM_INIT)
    o = a1_ref[...] / l1_ref[...] - lam * (a2_ref[...] / l2_ref[...])
    ms = jnp.mean(o * o, axis=-1, keepdims=True)
    y = o * lax.rsqrt(ms + EPS)
    o_ref[...] = ((y * subg_ref[...]) * (1.0 - LAM_INIT)).astype(BF16)


def _attention(qkv, bias_tiles, lamv, subg):
    bsz, s, _ = qkv.shape
    T = ATTN_TILE
    h_ = N_HEADS_A
    return pl.pallas_call(
        _attn_kernel,
        grid=(bsz, h_, s // T),
        in_specs=[_resident((4, HEAD_QK)),
                  _resident((1, HEAD_V)),
                  pl.BlockSpec((None, T, HEAD_V), lambda b, h, i: (b, i, h)),
                  pl.BlockSpec((None, s, HEAD_V), lambda b, h, i: (b, 0, h_ + h)),
                  pl.BlockSpec((None, s, HEAD_V), lambda b, h, i: (b, 0, 2 * h_ + h)),
                  pl.BlockSpec((None, 2, T, T), lambda b, h, i: (h, 0, 0, 0))],
        out_specs=pl.BlockSpec((None, T, HEAD_V), lambda b, h, i: (b, i, h)),
        out_shape=jax.ShapeDtypeStruct((bsz, s, ATTN_WIDTH), BF16),
        scratch_shapes=[pltpu.VMEM((T, 1), F32), pltpu.VMEM((T, 1), F32),
                        pltpu.VMEM((T, HEAD_V), F32),
                        pltpu.VMEM((T, 1), F32), pltpu.VMEM((T, 1), F32),
                        pltpu.VMEM((T, HEAD_V), F32)],
        compiler_params=_params(("parallel", "parallel", "arbitrary")),
        name="diff_attn",
    )(lamv, subg.reshape(1, HEAD_V), qkv, qkv, qkv, bias_tiles)


def _conv_kernel(glu_ref, w_ref, cb_ref, lng_ref, lnb_ref, o_ref, ubuf):
    TS, H, C = CONV_TILE, CONV_HALO, CONV_WIDTH
    s = pl.program_id(1)

    @pl.when(s == 0)
    def _():
        ubuf[0:H, :] = jnp.zeros((H, C), F32)

    @pl.when(s > 0)
    def _():
        ubuf[0:H, :] = ubuf[TS:TS + H, :]

    g = glu_ref[...]
    ubuf[H:H + TS, :] = g[:, :C].astype(F32) * _sigmoid(g[:, C:].astype(F32))

    off = H - (CONV_KERNEL - 1)

    for r0 in range(0, TS, CONV_CHUNK):
        acc = jnp.zeros((CONV_CHUNK, C), F32)
        for k in range(CONV_KERNEL):
            acc = acc + ubuf[r0 + off + k:r0 + off + k + CONV_CHUNK, :] * w_ref[k:k + 1, :]
        u = acc + cb_ref[...]
        mu = jnp.mean(u, axis=-1, keepdims=True)
        d = u - mu
        var = jnp.mean(d * d, axis=-1, keepdims=True)
        y = d * lax.rsqrt(var + EPS) * lng_ref[...] + lnb_ref[...]
        o_ref[r0:r0 + CONV_CHUNK, :] = _silu(y).astype(BF16)


def _conv(glu, w, cb, lng, lnb):
    bsz, s, _ = glu.shape
    TS, C = CONV_TILE, CONV_WIDTH
    vec = lambda a: a.reshape(1, C)
    return pl.pallas_call(
        _conv_kernel,
        grid=(bsz, s // TS),
        in_specs=[pl.BlockSpec((None, TS, GLU_COLS), lambda b, i: (b, i, 0)),
                  _resident((CONV_KERNEL, C)),
                  _resident((1, C)), _resident((1, C)), _resident((1, C))],
        out_specs=pl.BlockSpec((None, TS, C), lambda b, i: (b, i, 0)),
        out_shape=jax.ShapeDtypeStruct((bsz, s, C), BF16),
        scratch_shapes=[pltpu.VMEM((TS + CONV_HALO, C), F32)],
        compiler_params=_params(("arbitrary", "arbitrary")),
        name="conv_branch",
    )(glu, w, vec(cb), vec(lng), vec(lnb))


def _mix_kernel(x_ref, ada_ref, a_ref, u_ref, gate_ref, woa_ref, woc_ref, boc_ref,
                wout_ref, g2_ref, x1_ref, h2_ref):
    D = D_MODEL
    a = jnp.dot(a_ref[...], woa_ref[...], preferred_element_type=F32)
    cv = jnp.dot(u_ref[...], woc_ref[...], preferred_element_type=F32) + boc_ref[...]
    gate = gate_ref[...]
    y = (_sigmoid(gate[:, :D].astype(F32)) * a
         + _sigmoid(gate[:, D:].astype(F32)) * cv)
    z = jnp.dot(y.astype(BF16), wout_ref[...], preferred_element_type=F32)
    x1 = x_ref[...] + ada_ref[2:3, :] * z
    x1_ref[...] = x1
    ms = jnp.mean(x1 * x1, axis=-1, keepdims=True)
    yn = x1 * lax.rsqrt(ms + EPS)
    h2 = (yn * g2_ref[...]) * (1.0 + ada_ref[4:5, :]) + ada_ref[3:4, :]
    h2_ref[...] = h2.astype(BF16)


def _mix(x, ada3, a_n, u, gates, woa, woc, boc, wout, g2):
    bsz, s, d = x.shape
    tm = ROW_TILE
    row = lambda n: pl.BlockSpec((None, tm, n), lambda b, i: (b, i, 0))
    return pl.pallas_call(
        _mix_kernel,
        grid=(bsz, s // tm),
        in_specs=[row(d),
                  pl.BlockSpec((None, 6, d), lambda b, i: (b, 0, 0)),
                  row(ATTN_WIDTH), row(CONV_WIDTH), row(GATE_COLS),
                  _resident((ATTN_WIDTH, d)), _resident((CONV_WIDTH, d)),
                  _resident((1, d)), _resident((d, d)), _resident((1, d))],
        out_specs=[row(d), row(d)],
        out_shape=[jax.ShapeDtypeStruct((bsz, s, d), F32),
                   jax.ShapeDtypeStruct((bsz, s, d), BF16)],
        compiler_params=_params(("parallel", "arbitrary")),
        name="mix",
    )(x, ada3, a_n, u, gates, woa, woc, boc.reshape(1, d), wout, g2.reshape(1, d))


def _ffn_kernel(x1_ref, ada_ref, h2_ref, w1_ref, w2_ref, gf_ref, o_ref, act_ref):
    h = h2_ref[...]
    for c in range(0, D_FF, FF_CHUNK):
        fg = jnp.dot(h, w1_ref[:, c:c + FF_CHUNK], preferred_element_type=F32)
        fu = jnp.dot(h, w1_ref[:, D_FF + c:D_FF + c + FF_CHUNK],
                     preferred_element_type=F32)
        act_ref[:, c:c + FF_CHUNK] = (_silu(fg) * fu).astype(BF16)
    z = jnp.dot(act_ref[...], w2_ref[...], preferred_element_type=F32)
    x2 = x1_ref[...] + ada_ref[5:6, :] * z
    ms = jnp.mean(x2 * x2, axis=-1, keepdims=True)
    o_ref[...] = (x2 * lax.rsqrt(ms + EPS)) * gf_ref[...]


def _ffn(x1, ada3, h2, w1, w2, gf):
    bsz, s, d = x1.shape
    tm = ROW_TILE
    row = lambda n: pl.BlockSpec((None, tm, n), lambda b, i: (b, i, 0))
    return pl.pallas_call(
        _ffn_kernel,
        grid=(bsz, s // tm),
        in_specs=[row(d),
                  pl.BlockSpec((None, 6, d), lambda b, i: (b, 0, 0)),
                  row(d),
                  _resident((d, 2 * D_FF)), _resident((D_FF, d)), _resident((1, d))],
        out_specs=row(d),
        out_shape=jax.ShapeDtypeStruct((bsz, s, d), F32),
        scratch_shapes=[pltpu.VMEM((tm, D_FF), BF16)],
        compiler_params=_params(("parallel", "arbitrary")),
        name="ffn",
    )(x1, ada3, h2, w1, w2, gf.reshape(1, d))


def kernel(x, c, w_ada, b_ada, norm1_g, norm2_g, final_g, w_in, lambda_q1, lambda_k1,
           lambda_q2, lambda_k2, rel_bias, attn_sub_g, w_o_attn, conv_w, conv_b,
           conv_ln_g, conv_ln_b, w_o_conv, b_o_conv, w_out, w_ffn_in, w_ffn_out):
    bsz, s, d = x.shape
    assert w_ada.shape[0] == 1, "single-layer block"
    assert (d, s % ROW_TILE, s % ATTN_TILE, s % CONV_TILE) == (D_MODEL, 0, 0, 0)
    l = 0
    ada3 = _ada(c, w_ada[l], b_ada[l]).reshape(bsz, 6, d)
    bias_tiles = _bias_tiles(rel_bias)
    qkv, glu, gates = _in_proj(x, ada3, norm1_g[l], w_in[l].astype(BF16))
    lamv = jnp.stack([lambda_q1[l], lambda_k1[l], lambda_q2[l], lambda_k2[l]])
    a_n = _attention(qkv, bias_tiles, lamv, attn_sub_g[l])
    u = _conv(glu, conv_w[l], conv_b[l], conv_ln_g[l], conv_ln_b[l])
    x1, h2 = _mix(x, ada3, a_n, u, gates, w_o_attn[l].astype(BF16),
                  w_o_conv[l].astype(BF16), b_o_conv[l], w_out[l].astype(BF16),
                  norm2_g[l])
    return _ffn(x1, ada3, h2, w_ffn_in[l].astype(BF16), w_ffn_out[l].astype(BF16),
                final_g)
```

```python
import functools
import math

import jax
import jax.numpy as jnp
from jax import lax
from jax.experimental import pallas as pl
from jax.experimental.pallas import tpu as pltpu

D_MODEL = 1024
N_HEADS_A = 4
HEAD_QK = 64
HEAD_V = 2 * HEAD_QK
ATTN_WIDTH = N_HEADS_A * HEAD_V
CONV_WIDTH = 512
CONV_KERNEL = 31
QKV_COLS = 3 * ATTN_WIDTH
GLU_COLS = 2 * CONV_WIDTH
GATE_COLS = 2 * D_MODEL
IN_COLS = QKV_COLS + GLU_COLS + GATE_COLS
D_FF = 2816
N_BUCKETS = 32
MAX_EXACT = 16
MAX_DISTANCE = 128
EPS = 1e-6
NEG_INF = -1e30
LAM_INIT = 0.8 - 0.6 * math.exp(-0.3 * 0)

V7X_VMEM_BYTES = 64 * 1024 * 1024
VMEM_LIMIT = V7X_VMEM_BYTES - 12 * 1024 * 1024

ROW_TILE = 512
ATTN_TILE = 256
CONV_TILE = 512
CONV_HALO = 32
CONV_CHUNK = 32
PROJ_CHUNK = 512
FF_CHUNK = 256

BF16 = jnp.bfloat16
F32 = jnp.float32


def _sigmoid(x):
    return 1.0 / (1.0 + jnp.exp(-x))


def _silu(x):
    return x * _sigmoid(x)


def _params(sem, vmem=VMEM_LIMIT):
    return pltpu.CompilerParams(dimension_semantics=sem, vmem_limit_bytes=vmem)


def _resident(shape):
    nd = len(shape)
    return pl.BlockSpec(shape, lambda *_: (0,) * nd, pipeline_mode=pl.Buffered(1))


def _ada_kernel(c_ref, w_ref, b_ref, o_ref):
    ca = _silu(c_ref[...])
    o_ref[...] = jnp.dot(ca, w_ref[...], preferred_element_type=F32,
                         precision=lax.Precision.HIGHEST) + b_ref[...]


def _ada(c, w, b):
    bsz, d = c.shape
    n = w.shape[1]
    tn = 1024
    return pl.pallas_call(
        _ada_kernel,
        grid=(n // tn,),
        in_specs=[pl.BlockSpec((bsz, d), lambda j: (0, 0)),
                  pl.BlockSpec((d, tn), lambda j: (0, j)),
                  pl.BlockSpec((1, tn), lambda j: (0, j))],
        out_specs=pl.BlockSpec((bsz, tn), lambda j: (0, j)),
        out_shape=jax.ShapeDtypeStruct((bsz, n), F32),
        compiler_params=_params(("arbitrary",)),
        name="ada",
    )(c, w, b.reshape(1, n))


def _bias_kernel(rel_ref, o_ref):
    h = pl.program_id(0)
    t = pl.program_id(1)
    T = ATTN_TILE
    key = lax.broadcasted_iota(jnp.int32, (T, T), 0)
    qry = lax.broadcasted_iota(jnp.int32, (T, T), 1)
    dist = (1 - t) * T + qry - key
    n = jnp.maximum(dist, 0)
    large = MAX_EXACT + (jnp.log(jnp.maximum(n, 1).astype(F32) / MAX_EXACT)
                         / math.log(MAX_DISTANCE / MAX_EXACT)
                         * (N_BUCKETS - MAX_EXACT)).astype(jnp.int32)
    large = jnp.minimum(large, N_BUCKETS - 1)
    bucket = jnp.where(n < MAX_EXACT, n, large)
    far = rel_ref[N_BUCKETS - 1, h]
    bias = jnp.zeros((T, T), F32)
    for b in range(N_BUCKETS - 1):
        bias = jnp.where(bucket == b, rel_ref[b, h] - far, bias)
    o_ref[...] = jnp.where(dist >= 0, bias, NEG_INF)


def _bias_tiles(rel_bias):
    T = ATTN_TILE
    return pl.pallas_call(
        _bias_kernel,
        grid=(N_HEADS_A, 2),
        in_specs=[pl.BlockSpec(memory_space=pltpu.SMEM)],
        out_specs=pl.BlockSpec((None, None, T, T), lambda h, t: (h, t, 0, 0)),
        out_shape=jax.ShapeDtypeStruct((N_HEADS_A, 2, T, T), F32),
        compiler_params=_params(("arbitrary", "arbitrary")),
        name="bias_tiles",
    )(rel_bias)


def _inproj_kernel(x_ref, ada_ref, g_ref, w_ref, qkv_ref, glu_ref, gate_ref):
    x = x_ref[...]
    ms = jnp.mean(x * x, axis=-1, keepdims=True)
    y = x * lax.rsqrt(ms + EPS)
    h = (y * g_ref[...]) * (1.0 + ada_ref[1:2, :]) + ada_ref[0:1, :]
    hb = h.astype(BF16)
    base = 0
    for ref in (qkv_ref, glu_ref, gate_ref):
        width = ref.shape[-1]
        for c in range(0, width, PROJ_CHUNK):
            ref[:, c:c + PROJ_CHUNK] = jnp.dot(
                hb, w_ref[:, base + c:base + c + PROJ_CHUNK],
                preferred_element_type=F32).astype(BF16)
        base += width


def _in_proj(x, ada3, g, w):
    bsz, s, d = x.shape
    tm = ROW_TILE
    row = lambda n: pl.BlockSpec((None, tm, n), lambda b, i: (b, i, 0))
    return pl.pallas_call(
        _inproj_kernel,
        grid=(bsz, s // tm),
        in_specs=[row(d),
                  pl.BlockSpec((None, 6, d), lambda b, i: (b, 0, 0)),
                  _resident((1, d)),
                  _resident((d, IN_COLS))],
        out_specs=[row(QKV_COLS), row(GLU_COLS), row(GATE_COLS)],
        out_shape=[jax.ShapeDtypeStruct((bsz, s, QKV_COLS), BF16),
                   jax.ShapeDtypeStruct((bsz, s, GLU_COLS), BF16),
                   jax.ShapeDtypeStruct((bsz, s, GATE_COLS), BF16)],
        compiler_params=_params(("parallel", "arbitrary")),
        name="in_proj",
    )(x, ada3, g.reshape(1, d), w)


def _attn_kernel(lamv_ref, subg_ref, q_ref, k_ref, v_ref, bias_ref, o_ref,
                 vt_ref, qz_ref, m_ref, l_ref, a_ref):
    T = ATTN_TILE
    NS = 2 * N_HEADS_A
    i = pl.program_id(1)
    hcols = lambda h: slice(h * HEAD_V, (h + 1) * HEAD_V)

    @pl.when(i == 0)
    def _():
        vt_ref[...] = v_ref[...].T

    lane = lax.broadcasted_iota(jnp.int32, (T, HEAD_V), 1)
    for h in range(N_HEADS_A):
        qs = q_ref[:, hcols(h)] * jnp.asarray(HEAD_QK ** -0.5, BF16)
        zero = jnp.zeros_like(qs)
        qz_ref[2 * h] = jnp.where(lane < HEAD_QK, qs, zero)
        qz_ref[2 * h + 1] = jnp.where(lane >= HEAD_QK, qs, zero)
    m_ref[...] = jnp.full(m_ref.shape, NEG_INF, F32)
    l_ref[...] = jnp.zeros(l_ref.shape, F32)
    a_ref[...] = jnp.zeros(a_ref.shape, F32)

    def process(start, tile):
        scores = []
        for n in range(NS):
            kc = k_ref[pl.ds(start, T), hcols(n // 2)]
            scores.append(lax.dot_general(kc, qz_ref[n], (((1,), (1,)), ((), ())),
                                          preferred_element_type=F32))
        probs, alphas = [], []
        for n, s in enumerate(scores):
            if tile is not None:
                s = s + bias_ref[n // 2, tile]
            m_old = m_ref[n]
            m_new = jnp.maximum(m_old, jnp.max(s, axis=0, keepdims=True))
            p = jnp.exp(s - m_new)
            alpha = jnp.exp(m_old - m_new)
            l_ref[n] = alpha * l_ref[n] + jnp.sum(p, axis=0, keepdims=True)
            m_ref[n] = m_new
            probs.append(p.astype(BF16))
            alphas.append(alpha)
        for n in range(NS):
            vc = vt_ref[hcols(n // 2), pl.ds(start, T)]
            a_ref[n] = alphas[n] * a_ref[n] + jnp.dot(
                vc, probs[n], preferred_element_type=F32)

    def far_body(j, carry):
        process(pl.multiple_of(j * T, T), None)
        return carry

    lax.fori_loop(0, i - 1, far_body, 0)

    @pl.when(i >= 1)
    def _():
        process(pl.multiple_of((i - 1) * T, T), 0)

    process(pl.multiple_of(i * T, T), 1)

    lv = lamv_ref[...]
    lam = (jnp.exp(jnp.sum(lv[0:1] * lv[1:2], axis=-1, keepdims=True))
           - jnp.exp(jnp.sum(lv[2:3] * lv[3:4], axis=-1, keepdims=True)) + LAM_INIT)
    for h in range(N_HEADS_A):
        n1, n2 = 2 * h, 2 * h + 1
        o = a_ref[n1] / l_ref[n1] - lam * (a_ref[n2] / l_ref[n2])
        ms = jnp.mean(o * o, axis=0, keepdims=True)
        y = (o * lax.rsqrt(ms + EPS)).T
        o_ref[:, hcols(h)] = ((y * subg_ref[...]) * (1.0 - LAM_INIT)).astype(BF16)


def _attention(qkv, bias_tiles, lamv, subg):
    bsz, s, _ = qkv.shape
    T = ATTN_TILE
    W = ATTN_WIDTH
    NS = 2 * N_HEADS_A
    return pl.pallas_call(
        _attn_kernel,
        grid=(bsz, s // T),
        in_specs=[_resident((4, HEAD_QK)),
                  _resident((1, HEAD_V)),
                  pl.BlockSpec((None, T, W), lambda b, i: (b, i, 0)),
                  pl.BlockSpec((None, s, W), lambda b, i: (b, 0, 1)),
                  pl.BlockSpec((None, s, W), lambda b, i: (b, 0, 2)),
                  _resident((N_HEADS_A, 2, T, T))],
        out_specs=pl.BlockSpec((None, T, W), lambda b, i: (b, i, 0)),
        out_shape=jax.ShapeDtypeStruct((bsz, s, W), BF16),
        scratch_shapes=[pltpu.VMEM((W, s), BF16),
                        pltpu.VMEM((NS, T, HEAD_V), BF16),
                        pltpu.VMEM((NS, 1, T), F32),
                        pltpu.VMEM((NS, 1, T), F32),
                        pltpu.VMEM((NS, HEAD_V, T), F32)],
        compiler_params=_params(("parallel", "arbitrary")),
        name="diff_attn",
    )(lamv, subg.reshape(1, HEAD_V), qkv, qkv, qkv, bias_tiles)


def _conv_kernel(glu_ref, w_ref, cb_ref, lng_ref, lnb_ref, o_ref, ubuf):
    TS, H, C = CONV_TILE, CONV_HALO, CONV_WIDTH
    s = pl.program_id(1)

    @pl.when(s == 0)
    def _():
        ubuf[0:H, :] = jnp.zeros((H, C), F32)

    @pl.when(s > 0)
    def _():
        ubuf[0:H, :] = ubuf[TS:TS + H, :]

    g = glu_ref[...]
    ubuf[H:H + TS, :] = g[:, :C].astype(F32) * _sigmoid(g[:, C:].astype(F32))

    off = H - (CONV_KERNEL - 1)

    for r0 in range(0, TS, CONV_CHUNK):
        acc = jnp.zeros((CONV_CHUNK, C), F32)
        for k in range(CONV_KERNEL):
            acc = acc + ubuf[r0 + off + k:r0 + off + k + CONV_CHUNK, :] * w_ref[k:k + 1, :]
        u = acc + cb_ref[...]
        mu = jnp.mean(u, axis=-1, keepdims=True)
        d = u - mu
        var = jnp.mean(d * d, axis=-1, keepdims=True)
        y = d * lax.rsqrt(var + EPS) * lng_ref[...] + lnb_ref[...]
        o_ref[r0:r0 + CONV_CHUNK, :] = _silu(y).astype(BF16)


def _conv(glu, w, cb, lng, lnb):
    bsz, s, _ = glu.shape
    TS, C = CONV_TILE, CONV_WIDTH
    vec = lambda a: a.reshape(1, C)
    return pl.pallas_call(
        _conv_kernel,
        grid=(bsz, s // TS),
        in_specs=[pl.BlockSpec((None, TS, GLU_COLS), lambda b, i: (b, i, 0)),
                  _resident((CONV_KERNEL, C)),
                  _resident((1, C)), _resident((1, C)), _resident((1, C))],
        out_specs=pl.BlockSpec((None, TS, C), lambda b, i: (b, i, 0)),
        out_shape=jax.ShapeDtypeStruct((bsz, s, C), BF16),
        scratch_shapes=[pltpu.VMEM((TS + CONV_HALO, C), F32)],
        compiler_params=_params(("arbitrary", "arbitrary")),
        name="conv_branch",
    )(glu, w, vec(cb), vec(lng), vec(lnb))


def _mix_kernel(x_ref, ada_ref, a_ref, u_ref, gate_ref, woa_ref, woc_ref, boc_ref,
                wout_ref, g2_ref, x1_ref, h2_ref):
    D = D_MODEL
    a = jnp.dot(a_ref[...], woa_ref[...], preferred_element_type=F32)
    cv = jnp.dot(u_ref[...], woc_ref[...], preferred_element_type=F32) + boc_ref[...]
    gate = gate_ref[...]
    y = (_sigmoid(gate[:, :D].astype(F32)) * a
         + _sigmoid(gate[:, D:].astype(F32)) * cv)
    z = jnp.dot(y.astype(BF16), wout_ref[...], preferred_element_type=F32)
    x1 = x_ref[...] + ada_ref[2:3, :] * z
    x1_ref[...] = x1
    ms = jnp.mean(x1 * x1, axis=-1, keepdims=True)
    yn = x1 * lax.rsqrt(ms + EPS)
    h2 = (yn * g2_ref[...]) * (1.0 + ada_ref[4:5, :]) + ada_ref[3:4, :]
    h2_ref[...] = h2.astype(BF16)


def _mix(x, ada3, a_n, u, gates, woa, woc, boc, wout, g2):
    bsz, s, d = x.shape
    tm = ROW_TILE
    row = lambda n: pl.BlockSpec((None, tm, n), lambda b, i: (b, i, 0))
    return pl.pallas_call(
        _mix_kernel,
        grid=(bsz, s // tm),
        in_specs=[row(d),
                  pl.BlockSpec((None, 6, d), lambda b, i: (b, 0, 0)),
                  row(ATTN_WIDTH), row(CONV_WIDTH), row(GATE_COLS),
                  _resident((ATTN_WIDTH, d)), _resident((CONV_WIDTH, d)),
                  _resident((1, d)), _resident((d, d)), _resident((1, d))],
        out_specs=[row(d), row(d)],
        out_shape=[jax.ShapeDtypeStruct((bsz, s, d), F32),
                   jax.ShapeDtypeStruct((bsz, s, d), BF16)],
        compiler_params=_params(("parallel", "arbitrary")),
        name="mix",
    )(x, ada3, a_n, u, gates, woa, woc, boc.reshape(1, d), wout, g2.reshape(1, d))


def _ffn_kernel(x1_ref, ada_ref, h2_ref, w1_ref, w2_ref, gf_ref, o_ref, act_ref):
    h = h2_ref[...]
    for c in range(0, D_FF, FF_CHUNK):
        fg = jnp.dot(h, w1_ref[:, c:c + FF_CHUNK], preferred_element_type=F32)
        fu = jnp.dot(h, w1_ref[:, D_FF + c:D_FF + c + FF_CHUNK],
                     preferred_element_type=F32)
        act_ref[:, c:c + FF_CHUNK] = (_silu(fg) * fu).astype(BF16)
    z = jnp.dot(act_ref[...], w2_ref[...], preferred_element_type=F32)
    x2 = x1_ref[...] + ada_ref[5:6, :] * z
    ms = jnp.mean(x2 * x2, axis=-1, keepdims=True)
    o_ref[...] = (x2 * lax.rsqrt(ms + EPS)) * gf_ref[...]


def _ffn(x1, ada3, h2, w1, w2, gf):
    bsz, s, d = x1.shape
    tm = ROW_TILE
    row = lambda n: pl.BlockSpec((None, tm, n), lambda b, i: (b, i, 0))
    return pl.pallas_call(
        _ffn_kernel,
        grid=(bsz, s // tm),
        in_specs=[row(d),
                  pl.BlockSpec((None, 6, d), lambda b, i: (b, 0, 0)),
                  row(d),
                  _resident((d, 2 * D_FF)), _resident((D_FF, d)), _resident((1, d))],
        out_specs=row(d),
        out_shape=jax.ShapeDtypeStruct((bsz, s, d), F32),
        scratch_shapes=[pltpu.VMEM((tm, D_FF), BF16)],
        compiler_params=_params(("parallel", "arbitrary")),
        name="ffn",
    )(x1, ada3, h2, w1, w2, gf.reshape(1, d))


def kernel(x, c, w_ada, b_ada, norm1_g, norm2_g, final_g, w_in, lambda_q1, lambda_k1,
           lambda_q2, lambda_k2, rel_bias, attn_sub_g, w_o_attn, conv_w, conv_b,
           conv_ln_g, conv_ln_b, w_o_conv, b_o_conv, w_out, w_ffn_in, w_ffn_out):
    bsz, s, d = x.shape
    assert w_ada.shape[0] == 1, "single-layer block"
    assert (d, s % ROW_TILE, s % ATTN_TILE, s % CONV_TILE) == (D_MODEL, 0, 0, 0)
    l = 0
    ada3 = _ada(c, w_ada[l], b_ada[l]).reshape(bsz, 6, d)
    bias_tiles = _bias_tiles(rel_bias)
    qkv, glu, gates = _in_proj(x, ada3, norm1_g[l], w_in[l].astype(BF16))
    lamv = jnp.stack([lambda_q1[l], lambda_k1[l], lambda_q2[l], lambda_k2[l]])
    a_n = _attention(qkv, bias_tiles, lamv, attn_sub_g[l])
    u = _conv(glu, conv_w[l], conv_b[l], conv_ln_g[l], conv_ln_b[l])
    x1, h2 = _mix(x, ada3, a_n, u, gates, w_o_attn[l].astype(BF16),
                  w_o_conv[l].astype(BF16), b_o_conv[l], w_out[l].astype(BF16),
                  norm2_g[l])
    return _ffn(x1, ada3, h2, w_ffn_in[l].astype(BF16), w_ffn_out[l].astype(BF16),
                final_g)
```

```python
import functools
import math

import jax
import jax.numpy as jnp
from jax import lax
from jax.experimental import pallas as pl
from jax.experimental.pallas import tpu as pltpu

D_MODEL = 1024
N_HEADS_A = 4
HEAD_QK = 64
HEAD_V = 2 * HEAD_QK
ATTN_WIDTH = N_HEADS_A * HEAD_V
CONV_WIDTH = 512
CONV_KERNEL = 31
QKV_COLS = 3 * ATTN_WIDTH
GLU_COLS = 2 * CONV_WIDTH
GATE_COLS = 2 * D_MODEL
IN_COLS = QKV_COLS + GLU_COLS + GATE_COLS
D_FF = 2816
N_BUCKETS = 32
MAX_EXACT = 16
MAX_DISTANCE = 128
EPS = 1e-6
NEG_INF = -1e30
LAM_INIT = 0.8 - 0.6 * math.exp(-0.3 * 0)

V7X_VMEM_BYTES = 64 * 1024 * 1024
VMEM_LIMIT = V7X_VMEM_BYTES - 12 * 1024 * 1024

ROW_TILE = 512
ATTN_TILE = 256
ATTN_LAG = 2
ONES_ROWS = 16
LOG2E = math.log2(math.e)
Q_SCALE = HEAD_QK ** -0.5 * LOG2E
CONV_TILE = 512
CONV_HALO = 32
CONV_CHUNK = 32
PROJ_CHUNK = 512
FF_CHUNK = 256

BF16 = jnp.bfloat16
F32 = jnp.float32


def _sigmoid(x):
    return 1.0 / (1.0 + jnp.exp(-x))


def _silu(x):
    return x * _sigmoid(x)


def _params(sem, vmem=VMEM_LIMIT):
    return pltpu.CompilerParams(dimension_semantics=sem, vmem_limit_bytes=vmem)


def _resident(shape):
    nd = len(shape)
    return pl.BlockSpec(shape, lambda *_: (0,) * nd, pipeline_mode=pl.Buffered(1))


def _ada_kernel(c_ref, w_ref, b_ref, o_ref):
    ca = _silu(c_ref[...])
    o_ref[...] = jnp.dot(ca, w_ref[...], preferred_element_type=F32,
                         precision=lax.Precision.HIGHEST) + b_ref[...]


def _ada(c, w, b):
    bsz, d = c.shape
    n = w.shape[1]
    tn = 1024
    return pl.pallas_call(
        _ada_kernel,
        grid=(n // tn,),
        in_specs=[pl.BlockSpec((bsz, d), lambda j: (0, 0)),
                  pl.BlockSpec((d, tn), lambda j: (0, j)),
                  pl.BlockSpec((1, tn), lambda j: (0, j))],
        out_specs=pl.BlockSpec((bsz, tn), lambda j: (0, j)),
        out_shape=jax.ShapeDtypeStruct((bsz, n), F32),
        compiler_params=_params(("arbitrary",)),
        name="ada",
    )(c, w, b.reshape(1, n))


def _bias_kernel(rel_ref, o_ref):
    h = pl.program_id(0)
    t = pl.program_id(1)
    T = ATTN_TILE
    key = lax.broadcasted_iota(jnp.int32, (T, T), 0)
    qry = lax.broadcasted_iota(jnp.int32, (T, T), 1)
    dist = (1 - t) * T + qry - key
    n = jnp.maximum(dist, 0)
    large = MAX_EXACT + (jnp.log(jnp.maximum(n, 1).astype(F32) / MAX_EXACT)
                         / math.log(MAX_DISTANCE / MAX_EXACT)
                         * (N_BUCKETS - MAX_EXACT)).astype(jnp.int32)
    large = jnp.minimum(large, N_BUCKETS - 1)
    bucket = jnp.where(n < MAX_EXACT, n, large)
    far = rel_ref[N_BUCKETS - 1, h]
    bias = jnp.zeros((T, T), F32)
    for b in range(N_BUCKETS - 1):
        bias = jnp.where(bucket == b, (rel_ref[b, h] - far) * LOG2E, bias)
    o_ref[...] = jnp.where(dist >= 0, bias, NEG_INF)


def _bias_tiles(rel_bias):
    T = ATTN_TILE
    return pl.pallas_call(
        _bias_kernel,
        grid=(N_HEADS_A, 2),
        in_specs=[pl.BlockSpec(memory_space=pltpu.SMEM)],
        out_specs=pl.BlockSpec((None, None, T, T), lambda h, t: (h, t, 0, 0)),
        out_shape=jax.ShapeDtypeStruct((N_HEADS_A, 2, T, T), F32),
        compiler_params=_params(("arbitrary", "arbitrary")),
        name="bias_tiles",
    )(rel_bias)


def _inproj_kernel(x_ref, ada_ref, g_ref, w_ref, qkv_ref, glu_ref, gate_ref):
    x = x_ref[...]
    ms = jnp.mean(x * x, axis=-1, keepdims=True)
    y = x * lax.rsqrt(ms + EPS)
    h = (y * g_ref[...]) * (1.0 + ada_ref[1:2, :]) + ada_ref[0:1, :]
    hb = h.astype(BF16)
    base = 0
    for ref in (qkv_ref, glu_ref, gate_ref):
        width = ref.shape[-1]
        for c in range(0, width, PROJ_CHUNK):
            r = jnp.dot(hb, w_ref[:, base + c:base + c + PROJ_CHUNK],
                        preferred_element_type=F32)
            if base + c < ATTN_WIDTH:
                r = r * Q_SCALE
            ref[:, c:c + PROJ_CHUNK] = r.astype(BF16)
        base += width


def _in_proj(x, ada3, g, w):
    bsz, s, d = x.shape
    tm = ROW_TILE
    row = lambda n: pl.BlockSpec((None, tm, n), lambda b, i: (b, i, 0))
    return pl.pallas_call(
        _inproj_kernel,
        grid=(bsz, s // tm),
        in_specs=[row(d),
                  pl.BlockSpec((None, 6, d), lambda b, i: (b, 0, 0)),
                  _resident((1, d)),
                  _resident((d, IN_COLS))],
        out_specs=[row(QKV_COLS), row(GLU_COLS), row(GATE_COLS)],
        out_shape=[jax.ShapeDtypeStruct((bsz, s, QKV_COLS), BF16),
                   jax.ShapeDtypeStruct((bsz, s, GLU_COLS), BF16),
                   jax.ShapeDtypeStruct((bsz, s, GATE_COLS), BF16)],
        compiler_params=_params(("parallel", "arbitrary")),
        name="in_proj",
    )(x, ada3, g.reshape(1, d), w)


def _attn_kernel(lamv_ref, subg_ref, q_ref, k_ref, v_ref, bias_ref, o_ref,
                 vt_ref, qz_ref, m_ref, a_ref):
    T = ATTN_TILE
    NS = 2 * N_HEADS_A
    i = pl.program_id(1)
    hcols = lambda h: slice(h * HEAD_V, (h + 1) * HEAD_V)

    @pl.when(i == 0)
    def _():
        vt = v_ref[...].T
        for h in range(N_HEADS_A):
            vt_ref[h, 0:HEAD_V, :] = vt[hcols(h), :]
            vt_ref[h, HEAD_V:, :] = jnp.ones((ONES_ROWS, vt.shape[1]), BF16)

    lane = lax.broadcasted_iota(jnp.int32, (T, HEAD_V), 1)
    for h in range(N_HEADS_A):
        qs = q_ref[:, hcols(h)]
        zero = jnp.zeros_like(qs)
        qz_ref[2 * h] = jnp.where(lane < HEAD_QK, qs, zero)
        qz_ref[2 * h + 1] = jnp.where(lane >= HEAD_QK, qs, zero)
    m_ref[...] = jnp.full(m_ref.shape, NEG_INF, F32)
    a_ref[...] = jnp.zeros(a_ref.shape, F32)

    def process(tiles):
        units = [(start, tile, n) for start, tile in tiles for n in range(NS)]
        scores, probs, alphas = {}, {}, {}

        def score(u):
            start, _, n = units[u]
            kc = k_ref[pl.ds(start, T), hcols(n // 2)]
            scores[u] = lax.dot_general(kc, qz_ref[n], (((1,), (1,)), ((), ())),
                                        preferred_element_type=F32)

        def softmax(u):
            _, tile, n = units[u]
            s = scores.pop(u)
            if tile is not None:
                s = s + bias_ref[n // 2, tile]
            m_old = m_ref[n]
            m_new = jnp.maximum(m_old, jnp.max(s, axis=0, keepdims=True))
            probs[u] = jnp.exp2(s - m_new).astype(BF16)
            alphas[u] = jnp.exp2(m_old - m_new)
            m_ref[n] = m_new

        def value(u):
            start, _, n = units[u]
            vc = vt_ref[n // 2, :, pl.ds(start, T)]
            a_ref[n] = alphas.pop(u) * a_ref[n] + jnp.dot(
                vc, probs.pop(u), preferred_element_type=F32)

        for step in range(len(units) + 2 * ATTN_LAG):
            if step < len(units):
                score(step)
            if 0 <= step - ATTN_LAG < len(units):
                softmax(step - ATTN_LAG)
            if 0 <= step - 2 * ATTN_LAG < len(units):
                value(step - 2 * ATTN_LAG)

    n_far = jnp.maximum(i - 1, 0)

    def far_pair(j, carry):
        start = pl.multiple_of(j * (2 * T), 2 * T)
        process([(start, None), (start + T, None)])
        return carry

    lax.fori_loop(0, n_far // 2, far_pair, 0)

    @pl.when(n_far % 2 == 1)
    def _():
        process([(pl.multiple_of((n_far - 1) * T, T), None)])

    @pl.when(i >= 1)
    def _():
        process([(pl.multiple_of((i - 1) * T, T), 0), (pl.multiple_of(i * T, T), 1)])

    @pl.when(i == 0)
    def _():
        process([(0, 1)])

    lv = lamv_ref[...]
    lam = (jnp.exp(jnp.sum(lv[0:1] * lv[1:2], axis=-1, keepdims=True))
           - jnp.exp(jnp.sum(lv[2:3] * lv[3:4], axis=-1, keepdims=True)) + LAM_INIT)
    for h in range(N_HEADS_A):
        a1, a2 = a_ref[2 * h], a_ref[2 * h + 1]
        o = (a1[:HEAD_V] / a1[HEAD_V:HEAD_V + 1]
             - lam * (a2[:HEAD_V] / a2[HEAD_V:HEAD_V + 1]))
        ms = jnp.mean(o * o, axis=0, keepdims=True)
        y = (o * lax.rsqrt(ms + EPS)).T
        o_ref[:, hcols(h)] = ((y * subg_ref[...]) * (1.0 - LAM_INIT)).astype(BF16)


def _attention(qkv, bias_tiles, lamv, subg):
    bsz, s, _ = qkv.shape
    T = ATTN_TILE
    W = ATTN_WIDTH
    NS = 2 * N_HEADS_A
    return pl.pallas_call(
        _attn_kernel,
        grid=(bsz, s // T),
        in_specs=[_resident((4, HEAD_QK)),
                  _resident((1, HEAD_V)),
                  pl.BlockSpec((None, T, W), lambda b, i: (b, i, 0)),
                  pl.BlockSpec((None, s, W), lambda b, i: (b, 0, 1)),
                  pl.BlockSpec((None, s, W), lambda b, i: (b, 0, 2)),
                  _resident((N_HEADS_A, 2, T, T))],
        out_specs=pl.BlockSpec((None, T, W), lambda b, i: (b, i, 0)),
        out_shape=jax.ShapeDtypeStruct((bsz, s, W), BF16),
        scratch_shapes=[pltpu.VMEM((N_HEADS_A, HEAD_V + ONES_ROWS, s), BF16),
                        pltpu.VMEM((NS, T, HEAD_V), BF16),
                        pltpu.VMEM((NS, 1, T), F32),
                        pltpu.VMEM((NS, HEAD_V + ONES_ROWS, T), F32)],
        compiler_params=_params(("parallel", "arbitrary")),
        name="diff_attn",
    )(lamv, subg.reshape(1, HEAD_V), qkv, qkv, qkv, bias_tiles)


def _conv_kernel(glu_ref, w_ref, cb_ref, lng_ref, lnb_ref, o_ref, ubuf):
    TS, H, C = CONV_TILE, CONV_HALO, CONV_WIDTH
    s = pl.program_id(1)

    @pl.when(s == 0)
    def _():
        ubuf[0:H, :] = jnp.zeros((H, C), F32)

    @pl.when(s > 0)
    def _():
        ubuf[0:H, :] = ubuf[TS:TS + H, :]

    g = glu_ref[...]
    ubuf[H:H + TS, :] = g[:, :C].astype(F32) * _sigmoid(g[:, C:].astype(F32))

    off = H - (CONV_KERNEL - 1)

    WIN = CONV_CHUNK + H

    def chunk(r, carry):
        r0 = pl.multiple_of(r * CONV_CHUNK, CONV_CHUNK)
        win = ubuf[pl.ds(r0, WIN), :]
        acc = jnp.zeros((CONV_CHUNK, C), F32)
        for b in range(8):
            wb = win if b == 0 else pltpu.roll(win, WIN - b, axis=0)
            for a in range(WIN // 8):
                k = 8 * a + b - off
                if 0 <= k < CONV_KERNEL:
                    acc = acc + wb[8 * a:8 * a + CONV_CHUNK, :] * w_ref[k:k + 1, :]
        u = acc + cb_ref[...]
        mu = jnp.mean(u, axis=-1, keepdims=True)
        d = u - mu
        var = jnp.mean(d * d, axis=-1, keepdims=True)
        y = d * lax.rsqrt(var + EPS) * lng_ref[...] + lnb_ref[...]
        o_ref[pl.ds(r0, CONV_CHUNK), :] = _silu(y).astype(BF16)
        return carry

    lax.fori_loop(0, TS // CONV_CHUNK, chunk, 0, unroll=4)


def _conv(glu, w, cb, lng, lnb):
    bsz, s, _ = glu.shape
    TS, C = CONV_TILE, CONV_WIDTH
    vec = lambda a: a.reshape(1, C)
    return pl.pallas_call(
        _conv_kernel,
        grid=(bsz, s // TS),
        in_specs=[pl.BlockSpec((None, TS, GLU_COLS), lambda b, i: (b, i, 0)),
                  _resident((CONV_KERNEL, C)),
                  _resident((1, C)), _resident((1, C)), _resident((1, C))],
        out_specs=pl.BlockSpec((None, TS, C), lambda b, i: (b, i, 0)),
        out_shape=jax.ShapeDtypeStruct((bsz, s, C), BF16),
        scratch_shapes=[pltpu.VMEM((TS + CONV_HALO, C), F32)],
        compiler_params=_params(("arbitrary", "arbitrary")),
        name="conv_branch",
    )(glu, w, vec(cb), vec(lng), vec(lnb))


def _mix_kernel(x_ref, ada_ref, a_ref, u_ref, gate_ref, woa_ref, woc_ref, boc_ref,
                wout_ref, g2_ref, x1_ref, h2_ref):
    D = D_MODEL
    a = jnp.dot(a_ref[...], woa_ref[...], preferred_element_type=F32)
    cv = jnp.dot(u_ref[...], woc_ref[...], preferred_element_type=F32) + boc_ref[...]
    gate = gate_ref[...]
    y = (_sigmoid(gate[:, :D].astype(F32)) * a
         + _sigmoid(gate[:, D:].astype(F32)) * cv)
    z = jnp.dot(y.astype(BF16), wout_ref[...], preferred_element_type=F32)
    x1 = x_ref[...] + ada_ref[2:3, :] * z
    x1_ref[...] = x1
    ms = jnp.mean(x1 * x1, axis=-1, keepdims=True)
    yn = x1 * lax.rsqrt(ms + EPS)
    h2 = (yn * g2_ref[...]) * (1.0 + ada_ref[4:5, :]) + ada_ref[3:4, :]
    h2_ref[...] = h2.astype(BF16)


def _mix(x, ada3, a_n, u, gates, woa, woc, boc, wout, g2):
    bsz, s, d = x.shape
    tm = ROW_TILE
    row = lambda n: pl.BlockSpec((None, tm, n), lambda b, i: (b, i, 0))
    return pl.pallas_call(
        _mix_kernel,
        grid=(bsz, s // tm),
        in_specs=[row(d),
                  pl.BlockSpec((None, 6, d), lambda b, i: (b, 0, 0)),
                  row(ATTN_WIDTH), row(CONV_WIDTH), row(GATE_COLS),
                  _resident((ATTN_WIDTH, d)), _resident((CONV_WIDTH, d)),
                  _resident((1, d)), _resident((d, d)), _resident((1, d))],
        out_specs=[row(d), row(d)],
        out_shape=[jax.ShapeDtypeStruct((bsz, s, d), F32),
                   jax.ShapeDtypeStruct((bsz, s, d), BF16)],
        compiler_params=_params(("parallel", "arbitrary")),
        name="mix",
    )(x, ada3, a_n, u, gates, woa, woc, boc.reshape(1, d), wout, g2.reshape(1, d))


def _ffn_kernel(x1_ref, ada_ref, h2_ref, w1_ref, w2_ref, gf_ref, o_ref, act_ref):
    h = h2_ref[...]
    for c in range(0, D_FF, FF_CHUNK):
        fg = jnp.dot(h, w1_ref[:, c:c + FF_CHUNK], preferred_element_type=F32)
        fu = jnp.dot(h, w1_ref[:, D_FF + c:D_FF + c + FF_CHUNK],
                     preferred_element_type=F32)
        act_ref[:, c:c + FF_CHUNK] = (_silu(fg) * fu).astype(BF16)
    z = jnp.dot(act_ref[...], w2_ref[...], preferred_element_type=F32)
    x2 = x1_ref[...] + ada_ref[5:6, :] * z
    ms = jnp.mean(x2 * x2, axis=-1, keepdims=True)
    o_ref[...] = (x2 * lax.rsqrt(ms + EPS)) * gf_ref[...]


def _ffn(x1, ada3, h2, w1, w2, gf):
    bsz, s, d = x1.shape
    tm = ROW_TILE
    row = lambda n: pl.BlockSpec((None, tm, n), lambda b, i: (b, i, 0))
    return pl.pallas_call(
        _ffn_kernel,
        grid=(bsz, s // tm),
        in_specs=[row(d),
                  pl.BlockSpec((None, 6, d), lambda b, i: (b, 0, 0)),
                  row(d),
                  _resident((d, 2 * D_FF)), _resident((D_FF, d)), _resident((1, d))],
        out_specs=row(d),
        out_shape=jax.ShapeDtypeStruct((bsz, s, d), F32),
        scratch_shapes=[pltpu.VMEM((tm, D_FF), BF16)],
        compiler_params=_params(("parallel", "arbitrary")),
        name="ffn",
    )(x1, ada3, h2, w1, w2, gf.reshape(1, d))


def kernel(x, c, w_ada, b_ada, norm1_g, norm2_g, final_g, w_in, lambda_q1, lambda_k1,
           lambda_q2, lambda_k2, rel_bias, attn_sub_g, w_o_attn, conv_w, conv_b,
           conv_ln_g, conv_ln_b, w_o_conv, b_o_conv, w_out, w_ffn_in, w_ffn_out):
    bsz, s, d = x.shape
    assert w_ada.shape[0] == 1, "single-layer block"
    assert (d, s % ROW_TILE, s % ATTN_TILE, s % CONV_TILE) == (D_MODEL, 0, 0, 0)
    l = 0
    ada3 = _ada(c, w_ada[l], b_ada[l]).reshape(bsz, 6, d)
    bias_tiles = _bias_tiles(rel_bias)
    qkv, glu, gates = _in_proj(x, ada3, norm1_g[l], w_in[l].astype(BF16))
    lamv = jnp.stack([lambda_q1[l], lambda_k1[l], lambda_q2[l], lambda_k2[l]])
    a_n = _attention(qkv, bias_tiles, lamv, attn_sub_g[l])
    u = _conv(glu, conv_w[l], conv_b[l], conv_ln_g[l], conv_ln_b[l])
    x1, h2 = _mix(x, ada3, a_n, u, gates, w_o_attn[l].astype(BF16),
                  w_o_conv[l].astype(BF16), b_o_conv[l], w_out[l].astype(BF16),
                  norm2_g[l])
    return _ffn(x1, ada3, h2, w_ffn_in[l].astype(BF16), w_ffn_out[l].astype(BF16),
                final_g)
```

```python
import functools
import math

import jax
import jax.numpy as jnp
from jax import lax
from jax.experimental import pallas as pl
from jax.experimental.pallas import tpu as pltpu

D_MODEL = 1024
N_HEADS_A = 4
HEAD_QK = 64
HEAD_V = 2 * HEAD_QK
ATTN_WIDTH = N_HEADS_A * HEAD_V
CONV_WIDTH = 512
CONV_KERNEL = 31
QKV_COLS = 3 * ATTN_WIDTH
GLU_COLS = 2 * CONV_WIDTH
GATE_COLS = 2 * D_MODEL
IN_COLS = QKV_COLS + GLU_COLS + GATE_COLS
D_FF = 2816
N_BUCKETS = 32
MAX_EXACT = 16
MAX_DISTANCE = 128
EPS = 1e-6
NEG_INF = -1e30
LAM_INIT = 0.8 - 0.6 * math.exp(-0.3 * 0)

V7X_VMEM_BYTES = 64 * 1024 * 1024
VMEM_LIMIT = V7X_VMEM_BYTES - 12 * 1024 * 1024

ROW_TILE = 512
ATTN_TILE = 256
ATTN_LAG = 4
ONES_ROWS = 16
LOG2E = math.log2(math.e)
Q_SCALE = HEAD_QK ** -0.5 * LOG2E
CONV_TILE = 512
CONV_HALO = 32
CONV_CHUNK = 32
PROJ_CHUNK = 512
FF_CHUNK = 256

BF16 = jnp.bfloat16
F32 = jnp.float32


def _sigmoid(x):
    return 1.0 / (1.0 + jnp.exp(-x))


def _silu(x):
    return x * _sigmoid(x)


def _params(sem, vmem=VMEM_LIMIT, flags=None):
    return pltpu.CompilerParams(dimension_semantics=sem, vmem_limit_bytes=vmem, flags=flags)


def _resident(shape):
    nd = len(shape)
    return pl.BlockSpec(shape, lambda *_: (0,) * nd, pipeline_mode=pl.Buffered(1))


def _ada_kernel(c_ref, w_ref, b_ref, o_ref):
    ca = _silu(c_ref[...])
    o_ref[...] = jnp.dot(ca, w_ref[...], preferred_element_type=F32,
                         precision=lax.Precision.HIGHEST) + b_ref[...]


def _ada(c, w, b):
    bsz, d = c.shape
    n = w.shape[1]
    tn = 1024
    return pl.pallas_call(
        _ada_kernel,
        grid=(n // tn,),
        in_specs=[pl.BlockSpec((bsz, d), lambda j: (0, 0)),
                  pl.BlockSpec((d, tn), lambda j: (0, j)),
                  pl.BlockSpec((1, tn), lambda j: (0, j))],
        out_specs=pl.BlockSpec((bsz, tn), lambda j: (0, j)),
        out_shape=jax.ShapeDtypeStruct((bsz, n), F32),
        compiler_params=_params(("arbitrary",)),
        name="ada",
    )(c, w, b.reshape(1, n))


def _bias_kernel(rel_ref, o_ref):
    h = pl.program_id(0)
    t = pl.program_id(1)
    T = ATTN_TILE
    key = lax.broadcasted_iota(jnp.int32, (T, T), 0)
    qry = lax.broadcasted_iota(jnp.int32, (T, T), 1)
    dist = (1 - t) * T + qry - key
    n = jnp.maximum(dist, 0)
    large = MAX_EXACT + (jnp.log(jnp.maximum(n, 1).astype(F32) / MAX_EXACT)
                         / math.log(MAX_DISTANCE / MAX_EXACT)
                         * (N_BUCKETS - MAX_EXACT)).astype(jnp.int32)
    large = jnp.minimum(large, N_BUCKETS - 1)
    bucket = jnp.where(n < MAX_EXACT, n, large)
    far = rel_ref[N_BUCKETS - 1, h]
    bias = jnp.zeros((T, T), F32)
    for b in range(N_BUCKETS - 1):
        bias = jnp.where(bucket == b, (rel_ref[b, h] - far) * LOG2E, bias)
    o_ref[...] = jnp.where(dist >= 0, bias, NEG_INF)


def _bias_tiles(rel_bias):
    T = ATTN_TILE
    return pl.pallas_call(
        _bias_kernel,
        grid=(N_HEADS_A, 2),
        in_specs=[pl.BlockSpec(memory_space=pltpu.SMEM)],
        out_specs=pl.BlockSpec((None, None, T, T), lambda h, t: (h, t, 0, 0)),
        out_shape=jax.ShapeDtypeStruct((N_HEADS_A, 2, T, T), F32),
        compiler_params=_params(("arbitrary", "arbitrary")),
        name="bias_tiles",
    )(rel_bias)


def _inproj_kernel(x_ref, ada_ref, g_ref, w_ref, qkv_ref, glu_ref, gate_ref):
    x = x_ref[...]
    ms = jnp.mean(x * x, axis=-1, keepdims=True)
    y = x * lax.rsqrt(ms + EPS)
    h = (y * g_ref[...]) * (1.0 + ada_ref[1:2, :]) + ada_ref[0:1, :]
    hb = h.astype(BF16)
    base = 0
    for ref in (qkv_ref, glu_ref, gate_ref):
        width = ref.shape[-1]
        for c in range(0, width, PROJ_CHUNK):
            r = jnp.dot(hb, w_ref[:, base + c:base + c + PROJ_CHUNK],
                        preferred_element_type=F32)
            if base + c < ATTN_WIDTH:
                r = r * Q_SCALE
            ref[:, c:c + PROJ_CHUNK] = r.astype(BF16)
        base += width


def _in_proj(x, ada3, g, w):
    bsz, s, d = x.shape
    tm = ROW_TILE
    row = lambda n: pl.BlockSpec((None, tm, n), lambda b, i: (b, i, 0))
    return pl.pallas_call(
        _inproj_kernel,
        grid=(bsz, s // tm),
        in_specs=[row(d),
                  pl.BlockSpec((None, 6, d), lambda b, i: (b, 0, 0)),
                  _resident((1, d)),
                  _resident((d, IN_COLS))],
        out_specs=[row(QKV_COLS), row(GLU_COLS), row(GATE_COLS)],
        out_shape=[jax.ShapeDtypeStruct((bsz, s, QKV_COLS), BF16),
                   jax.ShapeDtypeStruct((bsz, s, GLU_COLS), BF16),
                   jax.ShapeDtypeStruct((bsz, s, GATE_COLS), BF16)],
        compiler_params=_params(("parallel", "arbitrary")),
        name="in_proj",
    )(x, ada3, g.reshape(1, d), w)


def _attn_kernel(lamv_ref, subg_ref, q_ref, k_ref, v_ref, bias_ref, o_ref,
                 vt_ref, qz_ref, m_ref, a_ref):
    T = ATTN_TILE
    NS = 2 * N_HEADS_A
    i = pl.program_id(1)
    hcols = lambda h: slice(h * HEAD_V, (h + 1) * HEAD_V)

    @pl.when(i == 0)
    def _():
        vt = v_ref[...].T
        for h in range(N_HEADS_A):
            vt_ref[h, 0:HEAD_V, :] = vt[hcols(h), :]
            vt_ref[h, HEAD_V:, :] = jnp.ones((ONES_ROWS, vt.shape[1]), BF16)

    lane = lax.broadcasted_iota(jnp.int32, (T, HEAD_V), 1)
    for h in range(N_HEADS_A):
        qs = q_ref[:, hcols(h)]
        zero = jnp.zeros_like(qs)
        qz_ref[2 * h] = jnp.where(lane < HEAD_QK, qs, zero)
        qz_ref[2 * h + 1] = jnp.where(lane >= HEAD_QK, qs, zero)
    m_ref[...] = jnp.full(m_ref.shape, NEG_INF, F32)
    a_ref[...] = jnp.zeros(a_ref.shape, F32)

    def process(tiles):
        units = [(start, tile, n) for start, tile in tiles for n in range(NS)]
        scores, probs, alphas = {}, {}, {}

        def score(u):
            start, _, n = units[u]
            kc = k_ref[pl.ds(start, T), hcols(n // 2)]
            scores[u] = lax.dot_general(kc, qz_ref[n], (((1,), (1,)), ((), ())),
                                        preferred_element_type=F32)

        def softmax(u):
            _, tile, n = units[u]
            s = scores.pop(u)
            if tile is not None:
                s = s + bias_ref[n // 2, tile]
            m_old = m_ref[n]
            m_new = jnp.maximum(m_old, jnp.max(s, axis=0, keepdims=True))
            probs[u] = jnp.exp2(s - m_new).astype(BF16)
            alphas[u] = jnp.exp2(m_old - m_new)
            m_ref[n] = m_new

        def value(u):
            start, _, n = units[u]
            vc = vt_ref[n // 2, :, pl.ds(start, T)]
            a_ref[n] = alphas.pop(u) * a_ref[n] + jnp.dot(
                vc, probs.pop(u), preferred_element_type=F32)

        for step in range(len(units) + 2 * ATTN_LAG):
            if step < len(units):
                score(step)
            if 0 <= step - ATTN_LAG < len(units):
                softmax(step - ATTN_LAG)
            if 0 <= step - 2 * ATTN_LAG < len(units):
                value(step - 2 * ATTN_LAG)

    n_far = jnp.maximum(i - 1, 0)

    def far_pair(j, carry):
        start = pl.multiple_of(j * (2 * T), 2 * T)
        process([(start, None), (start + T, None)])
        return carry

    lax.fori_loop(0, n_far // 2, far_pair, 0)

    @pl.when(n_far % 2 == 1)
    def _():
        process([(pl.multiple_of((n_far - 1) * T, T), None)])

    @pl.when(i >= 1)
    def _():
        process([(pl.multiple_of((i - 1) * T, T), 0), (pl.multiple_of(i * T, T), 1)])

    @pl.when(i == 0)
    def _():
        process([(0, 1)])

    lv = lamv_ref[...]
    lam = (jnp.exp(jnp.sum(lv[0:1] * lv[1:2], axis=-1, keepdims=True))
           - jnp.exp(jnp.sum(lv[2:3] * lv[3:4], axis=-1, keepdims=True)) + LAM_INIT)
    for h in range(N_HEADS_A):
        a1, a2 = a_ref[2 * h], a_ref[2 * h + 1]
        o = (a1[:HEAD_V] / a1[HEAD_V:HEAD_V + 1]
             - lam * (a2[:HEAD_V] / a2[HEAD_V:HEAD_V + 1]))
        ms = jnp.mean(o * o, axis=0, keepdims=True)
        y = (o * lax.rsqrt(ms + EPS)).T
        o_ref[:, hcols(h)] = ((y * subg_ref[...]) * (1.0 - LAM_INIT)).astype(BF16)


def _attention(qkv, bias_tiles, lamv, subg):
    bsz, s, _ = qkv.shape
    T = ATTN_TILE
    W = ATTN_WIDTH
    NS = 2 * N_HEADS_A
    return pl.pallas_call(
        _attn_kernel,
        grid=(bsz, s // T),
        in_specs=[_resident((4, HEAD_QK)),
                  _resident((1, HEAD_V)),
                  pl.BlockSpec((None, T, W), lambda b, i: (b, i, 0)),
                  pl.BlockSpec((None, s, W), lambda b, i: (b, 0, 1)),
                  pl.BlockSpec((None, s, W), lambda b, i: (b, 0, 2)),
                  _resident((N_HEADS_A, 2, T, T))],
        out_specs=pl.BlockSpec((None, T, W), lambda b, i: (b, i, 0)),
        out_shape=jax.ShapeDtypeStruct((bsz, s, W), BF16),
        scratch_shapes=[pltpu.VMEM((N_HEADS_A, HEAD_V + ONES_ROWS, s), BF16),
                        pltpu.VMEM((NS, T, HEAD_V), BF16),
                        pltpu.VMEM((NS, 1, T), F32),
                        pltpu.VMEM((NS, HEAD_V + ONES_ROWS, T), F32)],
        compiler_params=_params(("parallel", "arbitrary")),
        name="diff_attn",
    )(lamv, subg.reshape(1, HEAD_V), qkv, qkv, qkv, bias_tiles)


def _conv_kernel(glu_ref, w_ref, cb_ref, lng_ref, lnb_ref, o_ref, ubuf):
    TS, H, C = CONV_TILE, CONV_HALO, CONV_WIDTH
    s = pl.program_id(1)

    @pl.when(s == 0)
    def _():
        ubuf[0:H, :] = jnp.zeros((H, C), F32)

    @pl.when(s > 0)
    def _():
        ubuf[0:H, :] = ubuf[TS:TS + H, :]

    g = glu_ref[...]
    ubuf[H:H + TS, :] = g[:, :C].astype(F32) * _sigmoid(g[:, C:].astype(F32))

    off = H - (CONV_KERNEL - 1)

    WIN = CONV_CHUNK + H

    def chunk(r, carry):
        r0 = pl.multiple_of(r * CONV_CHUNK, CONV_CHUNK)
        win = ubuf[pl.ds(r0, WIN), :]
        acc = jnp.zeros((CONV_CHUNK, C), F32)
        for b in range(8):
            wb = win if b == 0 else pltpu.roll(win, WIN - b, axis=0)
            for a in range(WIN // 8):
                k = 8 * a + b - off
                if 0 <= k < CONV_KERNEL:
                    acc = acc + wb[8 * a:8 * a + CONV_CHUNK, :] * w_ref[k:k + 1, :]
        u = acc + cb_ref[...]
        mu = jnp.mean(u, axis=-1, keepdims=True)
        d = u - mu
        var = jnp.mean(d * d, axis=-1, keepdims=True)
        y = d * lax.rsqrt(var + EPS) * lng_ref[...] + lnb_ref[...]
        o_ref[pl.ds(r0, CONV_CHUNK), :] = _silu(y).astype(BF16)
        return carry

    lax.fori_loop(0, TS // CONV_CHUNK, chunk, 0, unroll=4)


def _conv(glu, w, cb, lng, lnb):
    bsz, s, _ = glu.shape
    TS, C = CONV_TILE, CONV_WIDTH
    vec = lambda a: a.reshape(1, C)
    return pl.pallas_call(
        _conv_kernel,
        grid=(bsz, s // TS),
        in_specs=[pl.BlockSpec((None, TS, GLU_COLS), lambda b, i: (b, i, 0)),
                  _resident((CONV_KERNEL, C)),
                  _resident((1, C)), _resident((1, C)), _resident((1, C))],
        out_specs=pl.BlockSpec((None, TS, C), lambda b, i: (b, i, 0)),
        out_shape=jax.ShapeDtypeStruct((bsz, s, C), BF16),
        scratch_shapes=[pltpu.VMEM((TS + CONV_HALO, C), F32)],
        compiler_params=_params(("arbitrary", "arbitrary")),
        name="conv_branch",
    )(glu, w, vec(cb), vec(lng), vec(lnb))


def _mix_kernel(x_ref, ada_ref, a_ref, u_ref, gate_ref, woa_ref, woc_ref, boc_ref,
                wout_ref, g2_ref, x1_ref, h2_ref):
    D = D_MODEL
    a = jnp.dot(a_ref[...], woa_ref[...], preferred_element_type=F32)
    cv = jnp.dot(u_ref[...], woc_ref[...], preferred_element_type=F32) + boc_ref[...]
    gate = gate_ref[...]
    y = (_sigmoid(gate[:, :D].astype(F32)) * a
         + _sigmoid(gate[:, D:].astype(F32)) * cv)
    z = jnp.dot(y.astype(BF16), wout_ref[...], preferred_element_type=F32)
    x1 = x_ref[...] + ada_ref[2:3, :] * z
    x1_ref[...] = x1
    ms = jnp.mean(x1 * x1, axis=-1, keepdims=True)
    yn = x1 * lax.rsqrt(ms + EPS)
    h2 = (yn * g2_ref[...]) * (1.0 + ada_ref[4:5, :]) + ada_ref[3:4, :]
    h2_ref[...] = h2.astype(BF16)


def _mix(x, ada3, a_n, u, gates, woa, woc, boc, wout, g2):
    bsz, s, d = x.shape
    tm = ROW_TILE
    row = lambda n: pl.BlockSpec((None, tm, n), lambda b, i: (b, i, 0))
    return pl.pallas_call(
        _mix_kernel,
        grid=(bsz, s // tm),
        in_specs=[row(d),
                  pl.BlockSpec((None, 6, d), lambda b, i: (b, 0, 0)),
                  row(ATTN_WIDTH), row(CONV_WIDTH), row(GATE_COLS),
                  _resident((ATTN_WIDTH, d)), _resident((CONV_WIDTH, d)),
                  _resident((1, d)), _resident((d, d)), _resident((1, d))],
        out_specs=[row(d), row(d)],
        out_shape=[jax.ShapeDtypeStruct((bsz, s, d), F32),
                   jax.ShapeDtypeStruct((bsz, s, d), BF16)],
        compiler_params=_params(("parallel", "arbitrary")),
        name="mix",
    )(x, ada3, a_n, u, gates, woa, woc, boc.reshape(1, d), wout, g2.reshape(1, d))


def _ffn_kernel(x1_ref, ada_ref, h2_ref, w1_ref, w2_ref, gf_ref, o_ref, act_ref):
    h = h2_ref[...]
    for c in range(0, D_FF, FF_CHUNK):
        fg = jnp.dot(h, w1_ref[:, c:c + FF_CHUNK], preferred_element_type=F32)
        fu = jnp.dot(h, w1_ref[:, D_FF + c:D_FF + c + FF_CHUNK],
                     preferred_element_type=F32)
        act_ref[:, c:c + FF_CHUNK] = (_silu(fg) * fu).astype(BF16)
    z = jnp.dot(act_ref[...], w2_ref[...], preferred_element_type=F32)
    x2 = x1_ref[...] + ada_ref[5:6, :] * z
    ms = jnp.mean(x2 * x2, axis=-1, keepdims=True)
    o_ref[...] = (x2 * lax.rsqrt(ms + EPS)) * gf_ref[...]


def _ffn(x1, ada3, h2, w1, w2, gf):
    bsz, s, d = x1.shape
    tm = ROW_TILE
    row = lambda n: pl.BlockSpec((None, tm, n), lambda b, i: (b, i, 0))
    return pl.pallas_call(
        _ffn_kernel,
        grid=(bsz, s // tm),
        in_specs=[row(d),
                  pl.BlockSpec((None, 6, d), lambda b, i: (b, 0, 0)),
                  row(d),
                  _resident((d, 2 * D_FF)), _resident((D_FF, d)), _resident((1, d))],
        out_specs=row(d),
        out_shape=jax.ShapeDtypeStruct((bsz, s, d), F32),
        scratch_shapes=[pltpu.VMEM((tm, D_FF), BF16)],
        compiler_params=_params(("parallel", "arbitrary")),
        name="ffn",
    )(x1, ada3, h2, w1, w2, gf.reshape(1, d))


def kernel(x, c, w_ada, b_ada, norm1_g, norm2_g, final_g, w_in, lambda_q1, lambda_k1,
           lambda_q2, lambda_k2, rel_bias, attn_sub_g, w_o_attn, conv_w, conv_b,
           conv_ln_g, conv_ln_b, w_o_conv, b_o_conv, w_out, w_ffn_in, w_ffn_out):
    bsz, s, d = x.shape
    assert w_ada.shape[0] == 1, "single-layer block"
    assert (d, s % ROW_TILE, s % ATTN_TILE, s % CONV_TILE) == (D_MODEL, 0, 0, 0)
    l = 0
    ada3 = _ada(c, w_ada[l], b_ada[l]).reshape(bsz, 6, d)
    bias_tiles = _bias_tiles(rel_bias)
    qkv, glu, gates = _in_proj(x, ada3, norm1_g[l], w_in[l].astype(BF16))
    lamv = jnp.stack([lambda_q1[l], lambda_k1[l], lambda_q2[l], lambda_k2[l]])
    a_n = _attention(qkv, bias_tiles, lamv, attn_sub_g[l])
    u = _conv(glu, conv_w[l], conv_b[l], conv_ln_g[l], conv_ln_b[l])
    x1, h2 = _mix(x, ada3, a_n, u, gates, w_o_attn[l].astype(BF16),
                  w_o_conv[l].astype(BF16), b_o_conv[l], w_out[l].astype(BF16),
                  norm2_g[l])
    return _ffn(x1, ada3, h2, w_ffn_in[l].astype(BF16), w_ffn_out[l].astype(BF16),
                final_g)
```

```python
import functools
import math

import jax
import jax.numpy as jnp
from jax import lax
from jax.experimental import pallas as pl
from jax.experimental.pallas import tpu as pltpu

D_MODEL = 1024
N_HEADS_A = 4
HEAD_QK = 64
HEAD_V = 2 * HEAD_QK
ATTN_WIDTH = N_HEADS_A * HEAD_V
CONV_WIDTH = 512
CONV_KERNEL = 31
QKV_COLS = 3 * ATTN_WIDTH
GLU_COLS = 2 * CONV_WIDTH
GATE_COLS = 2 * D_MODEL
IN_COLS = QKV_COLS + GLU_COLS + GATE_COLS
D_FF = 2816
N_BUCKETS = 32
MAX_EXACT = 16
MAX_DISTANCE = 128
EPS = 1e-6
NEG_INF = -1e30
LAM_INIT = 0.8 - 0.6 * math.exp(-0.3 * 0)

V7X_VMEM_BYTES = 64 * 1024 * 1024
VMEM_LIMIT = V7X_VMEM_BYTES - 12 * 1024 * 1024

ROW_TILE = 512
ATTN_TILE = 256
ATTN_LAG = 4
ONES_ROWS = 16
LOG2E = math.log2(math.e)
Q_SCALE = HEAD_QK ** -0.5 * LOG2E
CONV_HALO = 32
CONV_CHUNK = 32
PROJ_CHUNK = 512
FF_CHUNK = 256

BF16 = jnp.bfloat16
F32 = jnp.float32


def _sigmoid(x):
    return 1.0 / (1.0 + jnp.exp(-x))


def _silu(x):
    return x * _sigmoid(x)


def _params(sem, vmem=VMEM_LIMIT, flags=None):
    return pltpu.CompilerParams(dimension_semantics=sem, vmem_limit_bytes=vmem, flags=flags)


def _resident(shape):
    nd = len(shape)
    return pl.BlockSpec(shape, lambda *_: (0,) * nd, pipeline_mode=pl.Buffered(1))


def _ada_kernel(c_ref, w_ref, b_ref, o_ref):
    ca = _silu(c_ref[...])
    o_ref[...] = jnp.dot(ca, w_ref[...], preferred_element_type=F32,
                         precision=lax.Precision.HIGHEST) + b_ref[...]


def _ada(c, w, b):
    bsz, d = c.shape
    n = w.shape[1]
    tn = 1024
    return pl.pallas_call(
        _ada_kernel,
        grid=(n // tn,),
        in_specs=[pl.BlockSpec((bsz, d), lambda j: (0, 0)),
                  pl.BlockSpec((d, tn), lambda j: (0, j)),
                  pl.BlockSpec((1, tn), lambda j: (0, j))],
        out_specs=pl.BlockSpec((bsz, tn), lambda j: (0, j)),
        out_shape=jax.ShapeDtypeStruct((bsz, n), F32),
        compiler_params=_params(("arbitrary",)),
        name="ada",
    )(c, w, b.reshape(1, n))


def _bias_kernel(rel_ref, o_ref):
    h = pl.program_id(0)
    t = pl.program_id(1)
    T = ATTN_TILE
    key = lax.broadcasted_iota(jnp.int32, (T, T), 0)
    qry = lax.broadcasted_iota(jnp.int32, (T, T), 1)
    dist = (1 - t) * T + qry - key
    n = jnp.maximum(dist, 0)
    large = MAX_EXACT + (jnp.log(jnp.maximum(n, 1).astype(F32) / MAX_EXACT)
                         / math.log(MAX_DISTANCE / MAX_EXACT)
                         * (N_BUCKETS - MAX_EXACT)).astype(jnp.int32)
    large = jnp.minimum(large, N_BUCKETS - 1)
    bucket = jnp.where(n < MAX_EXACT, n, large)
    far = rel_ref[N_BUCKETS - 1, h]
    bias = jnp.zeros((T, T), F32)
    for b in range(N_BUCKETS - 1):
        bias = jnp.where(bucket == b, (rel_ref[b, h] - far) * LOG2E, bias)
    o_ref[...] = jnp.where(dist >= 0, bias, NEG_INF)


def _bias_tiles(rel_bias):
    T = ATTN_TILE
    return pl.pallas_call(
        _bias_kernel,
        grid=(N_HEADS_A, 2),
        in_specs=[pl.BlockSpec(memory_space=pltpu.SMEM)],
        out_specs=pl.BlockSpec((None, None, T, T), lambda h, t: (h, t, 0, 0)),
        out_shape=jax.ShapeDtypeStruct((N_HEADS_A, 2, T, T), F32),
        compiler_params=_params(("arbitrary", "arbitrary")),
        name="bias_tiles",
    )(rel_bias)


def _conv_chunk(r0, ubuf, w_ref, cb_ref, lng_ref, lnb_ref, o_ref):
    off = CONV_HALO - (CONV_KERNEL - 1)
    WIN = CONV_CHUNK + CONV_HALO
    win = ubuf[r0:r0 + WIN, :]
    acc = jnp.zeros((CONV_CHUNK, CONV_WIDTH), F32)
    for b in range(8):
        wb = win if b == 0 else pltpu.roll(win, WIN - b, axis=0)
        for a in range(WIN // 8):
            k = 8 * a + b - off
            if 0 <= k < CONV_KERNEL:
                acc = acc + wb[8 * a:8 * a + CONV_CHUNK, :] * w_ref[k:k + 1, :]
    u = acc + cb_ref[...]
    mu = jnp.mean(u, axis=-1, keepdims=True)
    d = u - mu
    var = jnp.mean(d * d, axis=-1, keepdims=True)
    y = d * lax.rsqrt(var + EPS) * lng_ref[...] + lnb_ref[...]
    o_ref[r0:r0 + CONV_CHUNK, :] = _silu(y).astype(BF16)


def _inproj_kernel(x_ref, ada_ref, g_ref, w_ref, cw_ref, cb_ref, lng_ref, lnb_ref,
                   qkv_ref, u_ref, gate_ref, ubuf):
    TS, H, C = ROW_TILE, CONV_HALO, CONV_WIDTH
    i = pl.program_id(1)

    @pl.when(i == 0)
    def _():
        ubuf[0:H, :] = jnp.zeros((H, C), F32)

    @pl.when(i > 0)
    def _():
        ubuf[0:H, :] = ubuf[TS:TS + H, :]

    x = x_ref[...]
    ms = jnp.mean(x * x, axis=-1, keepdims=True)
    y = x * lax.rsqrt(ms + EPS)
    h = (y * g_ref[...]) * (1.0 + ada_ref[1:2, :]) + ada_ref[0:1, :]
    hb = h.astype(BF16)

    def proj(base):
        return jnp.dot(hb, w_ref[:, base:base + PROJ_CHUNK], preferred_element_type=F32)

    ubuf[H:H + TS, :] = proj(QKV_COLS) * _sigmoid(proj(QKV_COLS + C))
    jobs = ([(qkv_ref, c, c) for c in range(0, QKV_COLS, PROJ_CHUNK)]
            + [(gate_ref, c, QKV_COLS + GLU_COLS + c) for c in range(0, GATE_COLS, PROJ_CHUNK)])
    rows = list(range(0, TS, CONV_CHUNK))
    for n, (ref, c, base) in enumerate(jobs):
        r = proj(base)
        if base < ATTN_WIDTH:
            r = r * Q_SCALE
        ref[:, c:c + PROJ_CHUNK] = r.astype(BF16)
        for r0 in rows[n * len(rows) // len(jobs):(n + 1) * len(rows) // len(jobs)]:
            _conv_chunk(r0, ubuf, cw_ref, cb_ref, lng_ref, lnb_ref, u_ref)


def _in_proj(x, ada3, g, w, cw, cb, lng, lnb):
    bsz, s, d = x.shape
    tm = ROW_TILE
    C = CONV_WIDTH
    row = lambda n: pl.BlockSpec((None, tm, n), lambda b, i: (b, i, 0))
    vec = lambda a: a.reshape(1, C)
    return pl.pallas_call(
        _inproj_kernel,
        grid=(bsz, s // tm),
        in_specs=[row(d),
                  pl.BlockSpec((None, 6, d), lambda b, i: (b, 0, 0)),
                  _resident((1, d)),
                  _resident((d, IN_COLS)),
                  _resident((CONV_KERNEL, C)),
                  _resident((1, C)), _resident((1, C)), _resident((1, C))],
        out_specs=[row(QKV_COLS), row(C), row(GATE_COLS)],
        out_shape=[jax.ShapeDtypeStruct((bsz, s, QKV_COLS), BF16),
                   jax.ShapeDtypeStruct((bsz, s, C), BF16),
                   jax.ShapeDtypeStruct((bsz, s, GATE_COLS), BF16)],
        scratch_shapes=[pltpu.VMEM((tm + CONV_HALO, C), F32)],
        compiler_params=_params(("arbitrary", "arbitrary")),
        name="in_proj",
    )(x, ada3, g.reshape(1, d), w, cw, vec(cb), vec(lng), vec(lnb))


def _attn_kernel(lamv_ref, subg_ref, q_ref, k_ref, v_ref, bias_ref, o_ref,
                 vt_ref, qz_ref, m_ref, a_ref):
    T = ATTN_TILE
    NS = 2 * N_HEADS_A
    i = pl.program_id(1)
    hcols = lambda h: slice(h * HEAD_V, (h + 1) * HEAD_V)

    @pl.when(i == 0)
    def _():
        vt = v_ref[...].T
        for h in range(N_HEADS_A):
            vt_ref[h, 0:HEAD_V, :] = vt[hcols(h), :]
            vt_ref[h, HEAD_V:, :] = jnp.ones((ONES_ROWS, vt.shape[1]), BF16)

    lane = lax.broadcasted_iota(jnp.int32, (T, HEAD_V), 1)
    for h in range(N_HEADS_A):
        qs = q_ref[:, hcols(h)]
        zero = jnp.zeros_like(qs)
        qz_ref[2 * h] = jnp.where(lane < HEAD_QK, qs, zero)
        qz_ref[2 * h + 1] = jnp.where(lane >= HEAD_QK, qs, zero)
    m_ref[...] = jnp.full(m_ref.shape, NEG_INF, F32)
    a_ref[...] = jnp.zeros(a_ref.shape, F32)

    def process(tiles):
        units = [(start, tile, n) for start, tile in tiles for n in range(NS)]
        scores, probs, alphas = {}, {}, {}

        def score(u):
            start, _, n = units[u]
            kc = k_ref[pl.ds(start, T), hcols(n // 2)]
            scores[u] = lax.dot_general(kc, qz_ref[n], (((1,), (1,)), ((), ())),
                                        preferred_element_type=F32)

        def softmax(u):
            _, tile, n = units[u]
            s = scores.pop(u)
            if tile is not None:
                s = s + bias_ref[n // 2, tile]
            m_old = m_ref[n]
            m_new = jnp.maximum(m_old, jnp.max(s, axis=0, keepdims=True))
            probs[u] = jnp.exp2(s - m_new).astype(BF16)
            alphas[u] = jnp.exp2(m_old - m_new)
            m_ref[n] = m_new

        def value(u):
            start, _, n = units[u]
            vc = vt_ref[n // 2, :, pl.ds(start, T)]
            a_ref[n] = alphas.pop(u) * a_ref[n] + jnp.dot(
                vc, probs.pop(u), preferred_element_type=F32)

        for step in range(len(units) + 2 * ATTN_LAG):
            if step < len(units):
                score(step)
            if 0 <= step - ATTN_LAG < len(units):
                softmax(step - ATTN_LAG)
            if 0 <= step - 2 * ATTN_LAG < len(units):
                value(step - 2 * ATTN_LAG)

    n_far = jnp.maximum(i - 1, 0)

    def far_pair(j, carry):
        start = pl.multiple_of(j * (2 * T), 2 * T)
        process([(start, None), (start + T, None)])
        return carry

    lax.fori_loop(0, n_far // 2, far_pair, 0)

    @pl.when(n_far % 2 == 1)
    def _():
        process([(pl.multiple_of((n_far - 1) * T, T), None)])

    @pl.when(i >= 1)
    def _():
        process([(pl.multiple_of((i - 1) * T, T), 0), (pl.multiple_of(i * T, T), 1)])

    @pl.when(i == 0)
    def _():
        process([(0, 1)])

    lv = lamv_ref[...]
    lam = (jnp.exp(jnp.sum(lv[0:1] * lv[1:2], axis=-1, keepdims=True))
           - jnp.exp(jnp.sum(lv[2:3] * lv[3:4], axis=-1, keepdims=True)) + LAM_INIT)
    for h in range(N_HEADS_A):
        a1, a2 = a_ref[2 * h], a_ref[2 * h + 1]
        o = (a1[:HEAD_V] / a1[HEAD_V:HEAD_V + 1]
             - lam * (a2[:HEAD_V] / a2[HEAD_V:HEAD_V + 1]))
        ms = jnp.mean(o * o, axis=0, keepdims=True)
        y = (o * lax.rsqrt(ms + EPS)).T
        o_ref[:, hcols(h)] = ((y * subg_ref[...]) * (1.0 - LAM_INIT)).astype(BF16)


def _attention(qkv, bias_tiles, lamv, subg):
    bsz, s, _ = qkv.shape
    T = ATTN_TILE
    W = ATTN_WIDTH
    NS = 2 * N_HEADS_A
    return pl.pallas_call(
        _attn_kernel,
        grid=(bsz, s // T),
        in_specs=[_resident((4, HEAD_QK)),
                  _resident((1, HEAD_V)),
                  pl.BlockSpec((None, T, W), lambda b, i: (b, i, 0)),
                  pl.BlockSpec((None, s, W), lambda b, i: (b, 0, 1)),
                  pl.BlockSpec((None, s, W), lambda b, i: (b, 0, 2)),
                  _resident((N_HEADS_A, 2, T, T))],
        out_specs=pl.BlockSpec((None, T, W), lambda b, i: (b, i, 0)),
        out_shape=jax.ShapeDtypeStruct((bsz, s, W), BF16),
        scratch_shapes=[pltpu.VMEM((N_HEADS_A, HEAD_V + ONES_ROWS, s), BF16),
                        pltpu.VMEM((NS, T, HEAD_V), BF16),
                        pltpu.VMEM((NS, 1, T), F32),
                        pltpu.VMEM((NS, HEAD_V + ONES_ROWS, T), F32)],
        compiler_params=_params(("parallel", "arbitrary")),
        name="diff_attn",
    )(lamv, subg.reshape(1, HEAD_V), qkv, qkv, qkv, bias_tiles)


def _mix_kernel(x_ref, ada_ref, a_ref, u_ref, gate_ref, woa_ref, woc_ref, boc_ref,
                wout_ref, g2_ref, x1_ref, h2_ref):
    D = D_MODEL
    a = jnp.dot(a_ref[...], woa_ref[...], preferred_element_type=F32)
    cv = jnp.dot(u_ref[...], woc_ref[...], preferred_element_type=F32) + boc_ref[...]
    gate = gate_ref[...]
    y = (_sigmoid(gate[:, :D].astype(F32)) * a
         + _sigmoid(gate[:, D:].astype(F32)) * cv)
    z = jnp.dot(y.astype(BF16), wout_ref[...], preferred_element_type=F32)
    x1 = x_ref[...] + ada_ref[2:3, :] * z
    x1_ref[...] = x1
    ms = jnp.mean(x1 * x1, axis=-1, keepdims=True)
    yn = x1 * lax.rsqrt(ms + EPS)
    h2 = (yn * g2_ref[...]) * (1.0 + ada_ref[4:5, :]) + ada_ref[3:4, :]
    h2_ref[...] = h2.astype(BF16)


def _mix(x, ada3, a_n, u, gates, woa, woc, boc, wout, g2):
    bsz, s, d = x.shape
    tm = ROW_TILE
    row = lambda n: pl.BlockSpec((None, tm, n), lambda b, i: (b, i, 0))
    return pl.pallas_call(
        _mix_kernel,
        grid=(bsz, s // tm),
        in_specs=[row(d),
                  pl.BlockSpec((None, 6, d), lambda b, i: (b, 0, 0)),
                  row(ATTN_WIDTH), row(CONV_WIDTH), row(GATE_COLS),
                  _resident((ATTN_WIDTH, d)), _resident((CONV_WIDTH, d)),
                  _resident((1, d)), _resident((d, d)), _resident((1, d))],
        out_specs=[row(d), row(d)],
        out_shape=[jax.ShapeDtypeStruct((bsz, s, d), F32),
                   jax.ShapeDtypeStruct((bsz, s, d), BF16)],
        compiler_params=_params(("parallel", "arbitrary")),
        name="mix",
    )(x, ada3, a_n, u, gates, woa, woc, boc.reshape(1, d), wout, g2.reshape(1, d))


def _ffn_kernel(x1_ref, ada_ref, h2_ref, w1_ref, w2_ref, gf_ref, o_ref, act_ref):
    h = h2_ref[...]
    for c in range(0, D_FF, FF_CHUNK):
        fg = jnp.dot(h, w1_ref[:, c:c + FF_CHUNK], preferred_element_type=F32)
        fu = jnp.dot(h, w1_ref[:, D_FF + c:D_FF + c + FF_CHUNK],
                     preferred_element_type=F32)
        act_ref[:, c:c + FF_CHUNK] = (_silu(fg) * fu).astype(BF16)
    z = jnp.dot(act_ref[...], w2_ref[...], preferred_element_type=F32)
    x2 = x1_ref[...] + ada_ref[5:6, :] * z
    ms = jnp.mean(x2 * x2, axis=-1, keepdims=True)
    o_ref[...] = (x2 * lax.rsqrt(ms + EPS)) * gf_ref[...]


def _ffn(x1, ada3, h2, w1, w2, gf):
    bsz, s, d = x1.shape
    tm = ROW_TILE
    row = lambda n: pl.BlockSpec((None, tm, n), lambda b, i: (b, i, 0))
    return pl.pallas_call(
        _ffn_kernel,
        grid=(bsz, s // tm),
        in_specs=[row(d),
                  pl.BlockSpec((None, 6, d), lambda b, i: (b, 0, 0)),
                  row(d),
                  _resident((d, 2 * D_FF)), _resident((D_FF, d)), _resident((1, d))],
        out_specs=row(d),
        out_shape=jax.ShapeDtypeStruct((bsz, s, d), F32),
        scratch_shapes=[pltpu.VMEM((tm, D_FF), BF16)],
        compiler_params=_params(("parallel", "arbitrary")),
        name="ffn",
    )(x1, ada3, h2, w1, w2, gf.reshape(1, d))


def kernel(x, c, w_ada, b_ada, norm1_g, norm2_g, final_g, w_in, lambda_q1, lambda_k1,
           lambda_q2, lambda_k2, rel_bias, attn_sub_g, w_o_attn, conv_w, conv_b,
           conv_ln_g, conv_ln_b, w_o_conv, b_o_conv, w_out, w_ffn_in, w_ffn_out):
    bsz, s, d = x.shape
    assert w_ada.shape[0] == 1, "single-layer block"
    assert (d, s % ROW_TILE, s % ATTN_TILE) == (D_MODEL, 0, 0)
    assert PROJ_CHUNK == CONV_WIDTH and ATTN_WIDTH % PROJ_CHUNK == 0
    l = 0
    ada3 = _ada(c, w_ada[l], b_ada[l]).reshape(bsz, 6, d)
    bias_tiles = _bias_tiles(rel_bias)
    qkv, u, gates = _in_proj(x, ada3, norm1_g[l], w_in[l].astype(BF16), conv_w[l],
                             conv_b[l], conv_ln_g[l], conv_ln_b[l])
    lamv = jnp.stack([lambda_q1[l], lambda_k1[l], lambda_q2[l], lambda_k2[l]])
    a_n = _attention(qkv, bias_tiles, lamv, attn_sub_g[l])
    x1, h2 = _mix(x, ada3, a_n, u, gates, w_o_attn[l].astype(BF16),
                  w_o_conv[l].astype(BF16), b_o_conv[l], w_out[l].astype(BF16),
                  norm2_g[l])
    return _ffn(x1, ada3, h2, w_ffn_in[l].astype(BF16), w_ffn_out[l].astype(BF16),
                final_g)
```

```python
import functools
import math

import jax
import jax.numpy as jnp
from jax import lax
from jax.experimental import pallas as pl
from jax.experimental.pallas import tpu as pltpu

D_MODEL = 1024
N_HEADS_A = 4
HEAD_QK = 64
HEAD_V = 2 * HEAD_QK
ATTN_WIDTH = N_HEADS_A * HEAD_V
CONV_WIDTH = 512
CONV_KERNEL = 31
QKV_COLS = 3 * ATTN_WIDTH
GLU_COLS = 2 * CONV_WIDTH
GATE_COLS = 2 * D_MODEL
IN_COLS = QKV_COLS + GLU_COLS + GATE_COLS
D_FF = 2816
N_BUCKETS = 32
MAX_EXACT = 16
MAX_DISTANCE = 128
EPS = 1e-6
NEG_INF = -1e30
LAM_INIT = 0.8 - 0.6 * math.exp(-0.3 * 0)

V7X_VMEM_BYTES = 64 * 1024 * 1024
VMEM_LIMIT = V7X_VMEM_BYTES - 12 * 1024 * 1024

ROW_TILE = 512
ATTN_TILE = 256
ATTN_LAG = 4
ONES_ROWS = 16
LOG2E = math.log2(math.e)
Q_SCALE = HEAD_QK ** -0.5 * LOG2E
CONV_HALO = 32
CONV_CHUNK = 32
PROJ_CHUNK = 512
FF_CHUNK = 256

BF16 = jnp.bfloat16
F32 = jnp.float32


def _sigmoid(x):
    return 1.0 / (1.0 + jnp.exp(-x))


def _silu(x):
    return x * _sigmoid(x)


def _params(sem, vmem=VMEM_LIMIT, flags=None):
    return pltpu.CompilerParams(dimension_semantics=sem, vmem_limit_bytes=vmem, flags=flags)


def _resident(shape):
    nd = len(shape)
    return pl.BlockSpec(shape, lambda *_: (0,) * nd, pipeline_mode=pl.Buffered(1))


def _ada_kernel(c_ref, w_ref, b_ref, o_ref):
    ca = _silu(c_ref[...])
    o_ref[...] = jnp.dot(ca, w_ref[...], preferred_element_type=F32,
                         precision=lax.Precision.HIGHEST) + b_ref[...]


def _ada(c, w, b):
    bsz, d = c.shape
    n = w.shape[1]
    tn = 1024
    return pl.pallas_call(
        _ada_kernel,
        grid=(n // tn,),
        in_specs=[pl.BlockSpec((bsz, d), lambda j: (0, 0)),
                  pl.BlockSpec((d, tn), lambda j: (0, j)),
                  pl.BlockSpec((1, tn), lambda j: (0, j))],
        out_specs=pl.BlockSpec((bsz, tn), lambda j: (0, j)),
        out_shape=jax.ShapeDtypeStruct((bsz, n), F32),
        compiler_params=_params(("arbitrary",)),
        name="ada",
    )(c, w, b.reshape(1, n))


def _bias_kernel(rel_ref, o_ref):
    h = pl.program_id(0)
    t = pl.program_id(1)
    T = ATTN_TILE
    key = lax.broadcasted_iota(jnp.int32, (T, T), 0)
    qry = lax.broadcasted_iota(jnp.int32, (T, T), 1)
    dist = (1 - t) * T + qry - key
    n = jnp.maximum(dist, 0)
    large = MAX_EXACT + (jnp.log(jnp.maximum(n, 1).astype(F32) / MAX_EXACT)
                         / math.log(MAX_DISTANCE / MAX_EXACT)
                         * (N_BUCKETS - MAX_EXACT)).astype(jnp.int32)
    large = jnp.minimum(large, N_BUCKETS - 1)
    bucket = jnp.where(n < MAX_EXACT, n, large)
    far = rel_ref[N_BUCKETS - 1, h]
    bias = jnp.zeros((T, T), F32)
    for b in range(N_BUCKETS - 1):
        bias = jnp.where(bucket == b, (rel_ref[b, h] - far) * LOG2E, bias)
    o_ref[...] = jnp.where(dist >= 0, bias, NEG_INF)


def _bias_tiles(rel_bias):
    T = ATTN_TILE
    return pl.pallas_call(
        _bias_kernel,
        grid=(N_HEADS_A, 2),
        in_specs=[pl.BlockSpec(memory_space=pltpu.SMEM)],
        out_specs=pl.BlockSpec((None, None, T, T), lambda h, t: (h, t, 0, 0)),
        out_shape=jax.ShapeDtypeStruct((N_HEADS_A, 2, T, T), F32),
        compiler_params=_params(("arbitrary", "arbitrary")),
        name="bias_tiles",
    )(rel_bias)


def _conv_chunk(r0, ubuf, w_ref, cb_ref, lng_ref, lnb_ref, o_ref):
    off = CONV_HALO - (CONV_KERNEL - 1)
    acc = jnp.zeros((CONV_CHUNK // 8, 8, CONV_WIDTH), F32)
    for k in range(CONV_KERNEL):
        a, b = divmod(k + off, 8)
        rows = ubuf[b, r0 + 8 * a:r0 + 8 * a + CONV_CHUNK, :]
        acc = acc + rows.reshape(acc.shape) * w_ref[k][None]
    u = acc.reshape(CONV_CHUNK, CONV_WIDTH) + cb_ref[...]
    mu = jnp.mean(u, axis=-1, keepdims=True)
    d = u - mu
    var = jnp.mean(d * d, axis=-1, keepdims=True)
    y = d * lax.rsqrt(var + EPS) * lng_ref[...] + lnb_ref[...]
    o_ref[r0:r0 + CONV_CHUNK, :] = _silu(y).astype(BF16)


def _inproj_kernel(x_ref, ada_ref, g_ref, w_ref, cw_ref, cb_ref, lng_ref, lnb_ref,
                   qkv_ref, u_ref, gate_ref, ubuf):
    TS, H, C = ROW_TILE, CONV_HALO, CONV_WIDTH
    i = pl.program_id(1)

    @pl.when(i == 0)
    def _():
        ubuf[0, 0:H, :] = jnp.zeros((H, C), F32)

    @pl.when(i > 0)
    def _():
        ubuf[0, 0:H, :] = ubuf[0, TS:TS + H, :]

    x = x_ref[...]
    ms = jnp.mean(x * x, axis=-1, keepdims=True)
    y = x * lax.rsqrt(ms + EPS)
    h = (y * g_ref[...]) * (1.0 + ada_ref[1:2, :]) + ada_ref[0:1, :]
    hb = h.astype(BF16)

    def proj(base):
        return jnp.dot(hb, w_ref[:, base:base + PROJ_CHUNK], preferred_element_type=F32)

    ubuf[0, H:H + TS, :] = proj(QKV_COLS) * _sigmoid(proj(QKV_COLS + C))
    n = TS + H - 8
    for b in range(1, 8):
        ubuf[b, 0:n, :] = ubuf[0, b:b + n, :]
    jobs = ([(qkv_ref, c, c) for c in range(0, QKV_COLS, PROJ_CHUNK)]
            + [(gate_ref, c, QKV_COLS + GLU_COLS + c) for c in range(0, GATE_COLS, PROJ_CHUNK)])
    rows = list(range(0, TS, CONV_CHUNK))
    for n, (ref, c, base) in enumerate(jobs):
        r = proj(base)
        if base < ATTN_WIDTH:
            r = r * Q_SCALE
        ref[:, c:c + PROJ_CHUNK] = r.astype(BF16)
        for r0 in rows[n * len(rows) // len(jobs):(n + 1) * len(rows) // len(jobs)]:
            _conv_chunk(r0, ubuf, cw_ref, cb_ref, lng_ref, lnb_ref, u_ref)


def _in_proj(x, ada3, g, w, cw, cb, lng, lnb):
    bsz, s, d = x.shape
    tm = ROW_TILE
    C = CONV_WIDTH
    row = lambda n: pl.BlockSpec((None, tm, n), lambda b, i: (b, i, 0))
    vec = lambda a: a.reshape(1, C)
    return pl.pallas_call(
        _inproj_kernel,
        grid=(bsz, s // tm),
        in_specs=[row(d),
                  pl.BlockSpec((None, 6, d), lambda b, i: (b, 0, 0)),
                  _resident((1, d)),
                  _resident((d, IN_COLS)),
                  _resident((CONV_KERNEL, 8, C)),
                  _resident((1, C)), _resident((1, C)), _resident((1, C))],
        out_specs=[row(QKV_COLS), row(C), row(GATE_COLS)],
        out_shape=[jax.ShapeDtypeStruct((bsz, s, QKV_COLS), BF16),
                   jax.ShapeDtypeStruct((bsz, s, C), BF16),
                   jax.ShapeDtypeStruct((bsz, s, GATE_COLS), BF16)],
        scratch_shapes=[pltpu.VMEM((8, tm + CONV_HALO, C), F32)],
        compiler_params=_params(("arbitrary", "arbitrary")),
        name="in_proj",
    )(x, ada3, g.reshape(1, d), w, jnp.broadcast_to(cw[:, None, :], (CONV_KERNEL, 8, C)),
      vec(cb), vec(lng), vec(lnb))


def _attn_kernel(lamv_ref, subg_ref, q_ref, k_ref, v_ref, bias_ref, o_ref,
                 vt_ref, qz_ref, m_ref, a_ref):
    T = ATTN_TILE
    NS = 2 * N_HEADS_A
    i = pl.program_id(1)
    hcols = lambda h: slice(h * HEAD_V, (h + 1) * HEAD_V)

    @pl.when(i == 0)
    def _():
        vt = v_ref[...].T
        for h in range(N_HEADS_A):
            vt_ref[h, 0:HEAD_V, :] = vt[hcols(h), :]
            vt_ref[h, HEAD_V:, :] = jnp.ones((ONES_ROWS, vt.shape[1]), BF16)

    lane = lax.broadcasted_iota(jnp.int32, (T, HEAD_V), 1)
    for h in range(N_HEADS_A):
        qs = q_ref[:, hcols(h)]
        zero = jnp.zeros_like(qs)
        qz_ref[2 * h] = jnp.where(lane < HEAD_QK, qs, zero)
        qz_ref[2 * h + 1] = jnp.where(lane >= HEAD_QK, qs, zero)
    m_ref[...] = jnp.full(m_ref.shape, NEG_INF, F32)
    a_ref[...] = jnp.zeros(a_ref.shape, F32)

    def process(tiles):
        units = [(start, tile, n) for start, tile in tiles for n in range(NS)]
        scores, probs, alphas = {}, {}, {}

        def score(u):
            start, _, n = units[u]
            kc = k_ref[pl.ds(start, T), hcols(n // 2)]
            scores[u] = lax.dot_general(kc, qz_ref[n], (((1,), (1,)), ((), ())),
                                        preferred_element_type=F32)

        def softmax(u):
            _, tile, n = units[u]
            s = scores.pop(u)
            if tile is not None:
                s = s + bias_ref[n // 2, tile]
            m_old = m_ref[n]
            m_new = jnp.maximum(m_old, jnp.max(s, axis=0, keepdims=True))
            probs[u] = jnp.exp2(s - m_new).astype(BF16)
            alphas[u] = jnp.exp2(m_old - m_new)
            m_ref[n] = m_new

        def value(u):
            start, _, n = units[u]
            vc = vt_ref[n // 2, :, pl.ds(start, T)]
            a_ref[n] = alphas.pop(u) * a_ref[n] + jnp.dot(
                vc, probs.pop(u), preferred_element_type=F32)

        for step in range(len(units) + 2 * ATTN_LAG):
            if step < len(units):
                score(step)
            if 0 <= step - ATTN_LAG < len(units):
                softmax(step - ATTN_LAG)
            if 0 <= step - 2 * ATTN_LAG < len(units):
                value(step - 2 * ATTN_LAG)

    n_far = jnp.maximum(i - 1, 0)

    def far_pair(j, carry):
        start = pl.multiple_of(j * (2 * T), 2 * T)
        process([(start, None), (start + T, None)])
        return carry

    lax.fori_loop(0, n_far // 2, far_pair, 0)

    @pl.when(n_far % 2 == 1)
    def _():
        process([(pl.multiple_of((n_far - 1) * T, T), None)])

    @pl.when(i >= 1)
    def _():
        process([(pl.multiple_of((i - 1) * T, T), 0), (pl.multiple_of(i * T, T), 1)])

    @pl.when(i == 0)
    def _():
        process([(0, 1)])

    lv = lamv_ref[...]
    lam = (jnp.exp(jnp.sum(lv[0:1] * lv[1:2], axis=-1, keepdims=True))
           - jnp.exp(jnp.sum(lv[2:3] * lv[3:4], axis=-1, keepdims=True)) + LAM_INIT)
    for h in range(N_HEADS_A):
        a1, a2 = a_ref[2 * h], a_ref[2 * h + 1]
        o = (a1[:HEAD_V] / a1[HEAD_V:HEAD_V + 1]
             - lam * (a2[:HEAD_V] / a2[HEAD_V:HEAD_V + 1]))
        ms = jnp.mean(o * o, axis=0, keepdims=True)
        y = (o * lax.rsqrt(ms + EPS)).T
        o_ref[:, hcols(h)] = ((y * subg_ref[...]) * (1.0 - LAM_INIT)).astype(BF16)


def _attention(qkv, bias_tiles, lamv, subg):
    bsz, s, _ = qkv.shape
    T = ATTN_TILE
    W = ATTN_WIDTH
    NS = 2 * N_HEADS_A
    return pl.pallas_call(
        _attn_kernel,
        grid=(bsz, s // T),
        in_specs=[_resident((4, HEAD_QK)),
                  _resident((1, HEAD_V)),
                  pl.BlockSpec((None, T, W), lambda b, i: (b, i, 0)),
                  pl.BlockSpec((None, s, W), lambda b, i: (b, 0, 1)),
                  pl.BlockSpec((None, s, W), lambda b, i: (b, 0, 2)),
                  _resident((N_HEADS_A, 2, T, T))],
        out_specs=pl.BlockSpec((None, T, W), lambda b, i: (b, i, 0)),
        out_shape=jax.ShapeDtypeStruct((bsz, s, W), BF16),
        scratch_shapes=[pltpu.VMEM((N_HEADS_A, HEAD_V + ONES_ROWS, s), BF16),
                        pltpu.VMEM((NS, T, HEAD_V), BF16),
                        pltpu.VMEM((NS, 1, T), F32),
                        pltpu.VMEM((NS, HEAD_V + ONES_ROWS, T), F32)],
        compiler_params=_params(("parallel", "arbitrary")),
        name="diff_attn",
    )(lamv, subg.reshape(1, HEAD_V), qkv, qkv, qkv, bias_tiles)


def _mixffn_kernel(x_ref, ada_ref, a_ref, u_ref, gate_ref, woa_ref, woc_ref, boc_ref,
                   wout_ref, g2_ref, w1_ref, w2_ref, gf_ref, o_ref,
                   x1_ref, h2_ref, act_ref):
    D = D_MODEL
    a = jnp.dot(a_ref[...], woa_ref[...], preferred_element_type=F32)
    cv = jnp.dot(u_ref[...], woc_ref[...], preferred_element_type=F32) + boc_ref[...]
    gate = gate_ref[...]
    y = (_sigmoid(gate[:, :D].astype(F32)) * a
         + _sigmoid(gate[:, D:].astype(F32)) * cv)
    z = jnp.dot(y.astype(BF16), wout_ref[...], preferred_element_type=F32)
    x1 = x_ref[...] + ada_ref[2:3, :] * z
    x1_ref[...] = x1
    ms = jnp.mean(x1 * x1, axis=-1, keepdims=True)
    yn = x1 * lax.rsqrt(ms + EPS)
    h2_ref[...] = ((yn * g2_ref[...]) * (1.0 + ada_ref[4:5, :]) + ada_ref[3:4, :]).astype(BF16)

    h = h2_ref[...]
    for c in range(0, D_FF, FF_CHUNK):
        fg = jnp.dot(h, w1_ref[:, c:c + FF_CHUNK], preferred_element_type=F32)
        fu = jnp.dot(h, w1_ref[:, D_FF + c:D_FF + c + FF_CHUNK],
                     preferred_element_type=F32)
        act_ref[:, c:c + FF_CHUNK] = (_silu(fg) * fu).astype(BF16)
    z2 = jnp.dot(act_ref[...], w2_ref[...], preferred_element_type=F32)
    x2 = x1_ref[...] + ada_ref[5:6, :] * z2
    ms2 = jnp.mean(x2 * x2, axis=-1, keepdims=True)
    o_ref[...] = (x2 * lax.rsqrt(ms2 + EPS)) * gf_ref[...]


def _mix_ffn(x, ada3, a_n, u, gates, woa, woc, boc, wout, g2, w1, w2, gf):
    bsz, s, d = x.shape
    tm = ROW_TILE
    row = lambda n: pl.BlockSpec((None, tm, n), lambda b, i: (b, i, 0))
    return pl.pallas_call(
        _mixffn_kernel,
        grid=(bsz, s // tm),
        in_specs=[row(d),
                  pl.BlockSpec((None, 6, d), lambda b, i: (b, 0, 0)),
                  row(ATTN_WIDTH), row(CONV_WIDTH), row(GATE_COLS),
                  _resident((ATTN_WIDTH, d)), _resident((CONV_WIDTH, d)),
                  _resident((1, d)), _resident((d, d)), _resident((1, d)),
                  _resident((d, 2 * D_FF)), _resident((D_FF, d)), _resident((1, d))],
        out_specs=row(d),
        out_shape=jax.ShapeDtypeStruct((bsz, s, d), F32),
        scratch_shapes=[pltpu.VMEM((tm, d), F32), pltpu.VMEM((tm, d), BF16),
                        pltpu.VMEM((tm, D_FF), BF16)],
        compiler_params=_params(("parallel", "arbitrary")),
        name="mix_ffn",
    )(x, ada3, a_n, u, gates, woa, woc, boc.reshape(1, d), wout, g2.reshape(1, d),
      w1, w2, gf.reshape(1, d))


def kernel(x, c, w_ada, b_ada, norm1_g, norm2_g, final_g, w_in, lambda_q1, lambda_k1,
           lambda_q2, lambda_k2, rel_bias, attn_sub_g, w_o_attn, conv_w, conv_b,
           conv_ln_g, conv_ln_b, w_o_conv, b_o_conv, w_out, w_ffn_in, w_ffn_out):
    bsz, s, d = x.shape
    assert w_ada.shape[0] == 1, "single-layer block"
    assert (d, s % ROW_TILE, s % ATTN_TILE) == (D_MODEL, 0, 0)
    assert PROJ_CHUNK == CONV_WIDTH and ATTN_WIDTH % PROJ_CHUNK == 0
    l = 0
    ada3 = _ada(c, w_ada[l], b_ada[l]).reshape(bsz, 6, d)
    bias_tiles = _bias_tiles(rel_bias)
    qkv, u, gates = _in_proj(x, ada3, norm1_g[l], w_in[l].astype(BF16), conv_w[l],
                             conv_b[l], conv_ln_g[l], conv_ln_b[l])
    lamv = jnp.stack([lambda_q1[l], lambda_k1[l], lambda_q2[l], lambda_k2[l]])
    a_n = _attention(qkv, bias_tiles, lamv, attn_sub_g[l])
    return _mix_ffn(x, ada3, a_n, u, gates, w_o_attn[l].astype(BF16),
                    w_o_conv[l].astype(BF16), b_o_conv[l], w_out[l].astype(BF16),
                    norm2_g[l], w_ffn_in[l].astype(BF16), w_ffn_out[l].astype(BF16),
                    final_g)
```

```python
import functools
import math

import jax
import jax.numpy as jnp
from jax import lax
from jax.experimental import pallas as pl
from jax.experimental.pallas import tpu as pltpu

D_MODEL = 1024
N_HEADS_A = 4
HEAD_QK = 64
HEAD_V = 2 * HEAD_QK
ATTN_WIDTH = N_HEADS_A * HEAD_V
CONV_WIDTH = 512
CONV_KERNEL = 31
QKV_COLS = 3 * ATTN_WIDTH
GLU_COLS = 2 * CONV_WIDTH
GATE_COLS = 2 * D_MODEL
IN_COLS = QKV_COLS + GLU_COLS + GATE_COLS
D_FF = 2816
N_BUCKETS = 32
MAX_EXACT = 16
MAX_DISTANCE = 128
EPS = 1e-6
NEG_INF = -1e30
LAM_INIT = 0.8 - 0.6 * math.exp(-0.3 * 0)

V7X_VMEM_BYTES = 64 * 1024 * 1024
VMEM_LIMIT = V7X_VMEM_BYTES - 12 * 1024 * 1024

ADA_STEPS = 8
ROW_TILE = 512
ATTN_TILE = 256
ATTN_LAG = 4
ONES_ROWS = 16
CAST_ROWS = 16
LOG2E = math.log2(math.e)
Q_SCALE = HEAD_QK ** -0.5 * LOG2E
CONV_HALO = 32
CONV_CHUNK = 32
PROJ_CHUNK = 512
FF_CHUNK = 256

BF16 = jnp.bfloat16
F32 = jnp.float32


def _sigmoid(x):
    return 1.0 / (1.0 + jnp.exp(-x))


def _silu(x):
    return x * _sigmoid(x)


def _params(sem, vmem=VMEM_LIMIT, flags=None):
    return pltpu.CompilerParams(dimension_semantics=sem, vmem_limit_bytes=vmem, flags=flags)


def _resident(shape):
    nd = len(shape)
    return pl.BlockSpec(shape, lambda *_: (0,) * nd, pipeline_mode=pl.Buffered(1))


def _ada_kernel(c_ref, w_ref, b_ref, win_ref, o_ref, win_bf_ref):
    ca = _silu(c_ref[...])
    o_ref[...] = jnp.dot(ca, w_ref[...], preferred_element_type=F32,
                         precision=lax.Precision.HIGHEST) + b_ref[...]
    win_bf_ref[...] = win_ref[...].astype(BF16)


def _ada(c, w, b, w_in):
    bsz, d = c.shape
    n = w.shape[1]
    steps = ADA_STEPS
    tn = n // steps
    rows, cols = w_in.shape
    tr = rows // steps
    return pl.pallas_call(
        _ada_kernel,
        grid=(steps,),
        in_specs=[pl.BlockSpec((bsz, d), lambda j: (0, 0)),
                  pl.BlockSpec((d, tn), lambda j: (0, j)),
                  pl.BlockSpec((1, tn), lambda j: (0, j)),
                  pl.BlockSpec((tr, cols), lambda j: (j, 0))],
        out_specs=[pl.BlockSpec((bsz, tn), lambda j: (0, j)),
                   pl.BlockSpec((tr, cols), lambda j: (j, 0))],
        out_shape=[jax.ShapeDtypeStruct((bsz, n), F32),
                   jax.ShapeDtypeStruct((rows, cols), BF16)],
        compiler_params=_params(("arbitrary",)),
        name="ada",
    )(c, w, b.reshape(1, n), w_in)


def _bias_kernel(rel_ref, o_ref):
    h = pl.program_id(0)
    t = pl.program_id(1)
    T = ATTN_TILE
    key = lax.broadcasted_iota(jnp.int32, (T, T), 0)
    qry = lax.broadcasted_iota(jnp.int32, (T, T), 1)
    dist = (1 - t) * T + qry - key
    n = jnp.maximum(dist, 0)
    large = MAX_EXACT + (jnp.log(jnp.maximum(n, 1).astype(F32) / MAX_EXACT)
                         / math.log(MAX_DISTANCE / MAX_EXACT)
                         * (N_BUCKETS - MAX_EXACT)).astype(jnp.int32)
    large = jnp.minimum(large, N_BUCKETS - 1)
    bucket = jnp.where(n < MAX_EXACT, n, large)
    far = rel_ref[N_BUCKETS - 1, h]
    bias = jnp.zeros((T, T), F32)
    for b in range(N_BUCKETS - 1):
        bias = jnp.where(bucket == b, (rel_ref[b, h] - far) * LOG2E, bias)
    o_ref[...] = jnp.where(dist >= 0, bias, NEG_INF)


def _bias_tiles(rel_bias):
    T = ATTN_TILE
    return pl.pallas_call(
        _bias_kernel,
        grid=(N_HEADS_A, 2),
        in_specs=[pl.BlockSpec(memory_space=pltpu.SMEM)],
        out_specs=pl.BlockSpec((None, None, T, T), lambda h, t: (h, t, 0, 0)),
        out_shape=jax.ShapeDtypeStruct((N_HEADS_A, 2, T, T), F32),
        compiler_params=_params(("arbitrary", "arbitrary")),
        name="bias_tiles",
    )(rel_bias)


def _conv_chunk(r0, ubuf, w_ref, cb_ref, lng_ref, lnb_ref, o_ref):
    off = CONV_HALO - (CONV_KERNEL - 1)
    acc = jnp.zeros((CONV_CHUNK // 8, 8, CONV_WIDTH), F32)
    for k in range(CONV_KERNEL):
        a, b = divmod(k + off, 8)
        rows = ubuf[b, r0 + 8 * a:r0 + 8 * a + CONV_CHUNK, :]
        acc = acc + rows.reshape(acc.shape) * w_ref[k][None]
    u = acc.reshape(CONV_CHUNK, CONV_WIDTH) + cb_ref[...]
    mu = jnp.mean(u, axis=-1, keepdims=True)
    d = u - mu
    var = jnp.mean(d * d, axis=-1, keepdims=True)
    y = d * lax.rsqrt(var + EPS) * lng_ref[...] + lnb_ref[...]
    o_ref[r0:r0 + CONV_CHUNK, :] = _silu(y).astype(BF16)


def _inproj_kernel(x_ref, ada_ref, g_ref, w_ref, cw_ref, cb_ref, lng_ref, lnb_ref,
                   qkv_ref, u_ref, gate_ref, ubuf):
    TS, H, C = ROW_TILE, CONV_HALO, CONV_WIDTH
    i = pl.program_id(1)

    @pl.when(i == 0)
    def _():
        ubuf[0, 0:H, :] = jnp.zeros((H, C), F32)

    @pl.when(i > 0)
    def _():
        ubuf[0, 0:H, :] = ubuf[0, TS:TS + H, :]

    x = x_ref[...]
    ms = jnp.mean(x * x, axis=-1, keepdims=True)
    y = x * lax.rsqrt(ms + EPS)
    h = (y * g_ref[...]) * (1.0 + ada_ref[1:2, :]) + ada_ref[0:1, :]
    hb = h.astype(BF16)

    def proj(base):
        return jnp.dot(hb, w_ref[:, base:base + PROJ_CHUNK], preferred_element_type=F32)

    ubuf[0, H:H + TS, :] = proj(QKV_COLS) * _sigmoid(proj(QKV_COLS + C))
    n = TS + H - 8
    for b in range(1, 8):
        ubuf[b, 0:n, :] = ubuf[0, b:b + n, :]
    jobs = ([(qkv_ref, c, c) for c in range(0, QKV_COLS, PROJ_CHUNK)]
            + [(gate_ref, c, QKV_COLS + GLU_COLS + c) for c in range(0, GATE_COLS, PROJ_CHUNK)])
    rows = list(range(0, TS, CONV_CHUNK))
    for n, (ref, c, base) in enumerate(jobs):
        r = proj(base)
        if base < ATTN_WIDTH:
            r = r * Q_SCALE
        ref[:, c:c + PROJ_CHUNK] = r.astype(BF16)
        for r0 in rows[n * len(rows) // len(jobs):(n + 1) * len(rows) // len(jobs)]:
            _conv_chunk(r0, ubuf, cw_ref, cb_ref, lng_ref, lnb_ref, u_ref)


def _in_proj(x, ada3, g, w, cw, cb, lng, lnb):
    bsz, s, d = x.shape
    tm = ROW_TILE
    C = CONV_WIDTH
    row = lambda n: pl.BlockSpec((None, tm, n), lambda b, i: (b, i, 0))
    vec = lambda a: a.reshape(1, C)
    return pl.pallas_call(
        _inproj_kernel,
        grid=(bsz, s // tm),
        in_specs=[row(d),
                  pl.BlockSpec((None, 6, d), lambda b, i: (b, 0, 0)),
                  _resident((1, d)),
                  _resident((d, IN_COLS)),
                  _resident((CONV_KERNEL, 8, C)),
                  _resident((1, C)), _resident((1, C)), _resident((1, C))],
        out_specs=[row(QKV_COLS), row(C), row(GATE_COLS)],
        out_shape=[jax.ShapeDtypeStruct((bsz, s, QKV_COLS), BF16),
                   jax.ShapeDtypeStruct((bsz, s, C), BF16),
                   jax.ShapeDtypeStruct((bsz, s, GATE_COLS), BF16)],
        scratch_shapes=[pltpu.VMEM((8, tm + CONV_HALO, C), F32)],
        compiler_params=_params(("arbitrary", "arbitrary")),
        name="in_proj",
    )(x, ada3, g.reshape(1, d), w, jnp.broadcast_to(cw[:, None, :], (CONV_KERNEL, 8, C)),
      vec(cb), vec(lng), vec(lnb))


def _attn_kernel(lamv_ref, subg_ref, q_ref, k_ref, v_ref, bias_ref, *rest):
    n_cast = (len(rest) - 5) // 2
    cast_in, o_ref, cast_out = rest[:n_cast], rest[n_cast], rest[n_cast + 1:2 * n_cast + 1]
    vt_ref, qz_ref, m_ref, a_ref = rest[2 * n_cast + 1:]
    T = ATTN_TILE
    NS = 2 * N_HEADS_A
    i = pl.program_id(1)
    hcols = lambda h: slice(h * HEAD_V, (h + 1) * HEAD_V)

    @pl.when(i == 0)
    def _():
        vt = v_ref[...].T
        for h in range(N_HEADS_A):
            vt_ref[h, 0:HEAD_V, :] = vt[hcols(h), :]
            vt_ref[h, HEAD_V:, :] = jnp.ones((ONES_ROWS, vt.shape[1]), BF16)

    lane = lax.broadcasted_iota(jnp.int32, (T, HEAD_V), 1)
    for h in range(N_HEADS_A):
        qs = q_ref[:, hcols(h)]
        zero = jnp.zeros_like(qs)
        qz_ref[2 * h] = jnp.where(lane < HEAD_QK, qs, zero)
        qz_ref[2 * h + 1] = jnp.where(lane >= HEAD_QK, qs, zero)
    m_ref[...] = jnp.full(m_ref.shape, NEG_INF, F32)
    a_ref[...] = jnp.zeros(a_ref.shape, F32)

    def process(tiles):
        units = [(start, tile, n) for start, tile in tiles for n in range(NS)]
        scores, probs, alphas = {}, {}, {}

        def score(u):
            start, _, n = units[u]
            kc = k_ref[pl.ds(start, T), hcols(n // 2)]
            scores[u] = lax.dot_general(kc, qz_ref[n], (((1,), (1,)), ((), ())),
                                        preferred_element_type=F32)

        def softmax(u):
            _, tile, n = units[u]
            s = scores.pop(u)
            if tile is not None:
                s = s + bias_ref[n // 2, tile]
            m_old = m_ref[n]
            m_new = jnp.maximum(m_old, jnp.max(s, axis=0, keepdims=True))
            probs[u] = jnp.exp2(s - m_new).astype(BF16)
            alphas[u] = jnp.exp2(m_old - m_new)
            m_ref[n] = m_new

        def value(u):
            start, _, n = units[u]
            vc = vt_ref[n // 2, :, pl.ds(start, T)]
            a_ref[n] = alphas.pop(u) * a_ref[n] + jnp.dot(
                vc, probs.pop(u), preferred_element_type=F32)

        for step in range(len(units) + 2 * ATTN_LAG):
            if step < len(units):
                score(step)
            if 0 <= step - ATTN_LAG < len(units):
                softmax(step - ATTN_LAG)
            if 0 <= step - 2 * ATTN_LAG < len(units):
                value(step - 2 * ATTN_LAG)

    n_far = jnp.maximum(i - 1, 0)

    def far_pair(j, carry):
        start = pl.multiple_of(j * (2 * T), 2 * T)
        process([(start, None), (start + T, None)])
        return carry

    lax.fori_loop(0, n_far // 2, far_pair, 0)

    @pl.when(n_far % 2 == 1)
    def _():
        process([(pl.multiple_of((n_far - 1) * T, T), None)])

    @pl.when(i >= 1)
    def _():
        process([(pl.multiple_of((i - 1) * T, T), 0), (pl.multiple_of(i * T, T), 1)])

    @pl.when(i == 0)
    def _():
        process([(0, 1)])

    lv = lamv_ref[...]
    lam = (jnp.exp(jnp.sum(lv[0:1] * lv[1:2], axis=-1, keepdims=True))
           - jnp.exp(jnp.sum(lv[2:3] * lv[3:4], axis=-1, keepdims=True)) + LAM_INIT)
    for h in range(N_HEADS_A):
        a1, a2 = a_ref[2 * h], a_ref[2 * h + 1]
        o = (a1[:HEAD_V] / a1[HEAD_V:HEAD_V + 1]
             - lam * (a2[:HEAD_V] / a2[HEAD_V:HEAD_V + 1]))
        ms = jnp.mean(o * o, axis=0, keepdims=True)
        y = (o * lax.rsqrt(ms + EPS)).T
        o_ref[:, hcols(h)] = ((y * subg_ref[...]) * (1.0 - LAM_INIT)).astype(BF16)

    for w_ref, wb_ref in zip(cast_in, cast_out):
        wb_ref[...] = w_ref[...].astype(BF16)


def _attention(qkv, bias_tiles, lamv, subg, weights):
    bsz, s, _ = qkv.shape
    T = ATTN_TILE
    W = ATTN_WIDTH
    NS = 2 * N_HEADS_A
    n_q = s // T
    slab_rows = bsz * n_q * CAST_ROWS
    slabs = [w.reshape(slab_rows, w.size // slab_rows) for w in weights]
    assert all(w.size % slab_rows == 0 and v.shape[1] % 128 == 0 for w, v in zip(weights, slabs))
    slab_spec = lambda v: pl.BlockSpec((CAST_ROWS, v.shape[1]), lambda b, i: (b * n_q + i, 0))
    out = pl.pallas_call(
        _attn_kernel,
        grid=(bsz, n_q),
        in_specs=[_resident((4, HEAD_QK)),
                  _resident((1, HEAD_V)),
                  pl.BlockSpec((None, T, W), lambda b, i: (b, i, 0)),
                  pl.BlockSpec((None, s, W), lambda b, i: (b, 0, 1)),
                  pl.BlockSpec((None, s, W), lambda b, i: (b, 0, 2)),
                  _resident((N_HEADS_A, 2, T, T))] + [slab_spec(v) for v in slabs],
        out_specs=[pl.BlockSpec((None, T, W), lambda b, i: (b, i, 0))]
        + [slab_spec(v) for v in slabs],
        out_shape=[jax.ShapeDtypeStruct((bsz, s, W), BF16)]
        + [jax.ShapeDtypeStruct(v.shape, BF16) for v in slabs],
        scratch_shapes=[pltpu.VMEM((N_HEADS_A, HEAD_V + ONES_ROWS, s), BF16),
                        pltpu.VMEM((NS, T, HEAD_V), BF16),
                        pltpu.VMEM((NS, 1, T), F32),
                        pltpu.VMEM((NS, HEAD_V + ONES_ROWS, T), F32)],
        compiler_params=_params(("parallel", "arbitrary")),
        name="diff_attn",
    )(lamv, subg.reshape(1, HEAD_V), qkv, qkv, qkv, bias_tiles, *slabs)
    return out[0], [v.reshape(w.shape) for v, w in zip(out[1:], weights)]


def _mixffn_kernel(x_ref, ada_ref, a_ref, u_ref, gate_ref, woa_ref, woc_ref, boc_ref,
                   wout_ref, g2_ref, w1_ref, w2_ref, gf_ref, o_ref,
                   x1_ref, h2_ref, act_ref):
    D = D_MODEL
    a = jnp.dot(a_ref[...], woa_ref[...], preferred_element_type=F32)
    cv = jnp.dot(u_ref[...], woc_ref[...], preferred_element_type=F32) + boc_ref[...]
    gate = gate_ref[...]
    y = (_sigmoid(gate[:, :D].astype(F32)) * a
         + _sigmoid(gate[:, D:].astype(F32)) * cv)
    z = jnp.dot(y.astype(BF16), wout_ref[...], preferred_element_type=F32)
    x1 = x_ref[...] + ada_ref[2:3, :] * z
    x1_ref[...] = x1
    ms = jnp.mean(x1 * x1, axis=-1, keepdims=True)
    yn = x1 * lax.rsqrt(ms + EPS)
    h2_ref[...] = ((yn * g2_ref[...]) * (1.0 + ada_ref[4:5, :]) + ada_ref[3:4, :]).astype(BF16)

    h = h2_ref[...]
    for c in range(0, D_FF, FF_CHUNK):
        fg = jnp.dot(h, w1_ref[:, c:c + FF_CHUNK], preferred_element_type=F32)
        fu = jnp.dot(h, w1_ref[:, D_FF + c:D_FF + c + FF_CHUNK],
                     preferred_element_type=F32)
        act_ref[:, c:c + FF_CHUNK] = (_silu(fg) * fu).astype(BF16)
    z2 = jnp.dot(act_ref[...], w2_ref[...], preferred_element_type=F32)
    x2 = x1_ref[...] + ada_ref[5:6, :] * z2
    ms2 = jnp.mean(x2 * x2, axis=-1, keepdims=True)
    o_ref[...] = (x2 * lax.rsqrt(ms2 + EPS)) * gf_ref[...]


def _mix_ffn(x, ada3, a_n, u, gates, woa, woc, boc, wout, g2, w1, w2, gf):
    bsz, s, d = x.shape
    tm = ROW_TILE
    row = lambda n: pl.BlockSpec((None, tm, n), lambda b, i: (b, i, 0))
    return pl.pallas_call(
        _mixffn_kernel,
        grid=(bsz, s // tm),
        in_specs=[row(d),
                  pl.BlockSpec((None, 6, d), lambda b, i: (b, 0, 0)),
                  row(ATTN_WIDTH), row(CONV_WIDTH), row(GATE_COLS),
                  _resident((ATTN_WIDTH, d)), _resident((CONV_WIDTH, d)),
                  _resident((1, d)), _resident((d, d)), _resident((1, d)),
                  _resident((d, 2 * D_FF)), _resident((D_FF, d)), _resident((1, d))],
        out_specs=row(d),
        out_shape=jax.ShapeDtypeStruct((bsz, s, d), F32),
        scratch_shapes=[pltpu.VMEM((tm, d), F32), pltpu.VMEM((tm, d), BF16),
                        pltpu.VMEM((tm, D_FF), BF16)],
        compiler_params=_params(("parallel", "arbitrary")),
        name="mix_ffn",
    )(x, ada3, a_n, u, gates, woa, woc, boc.reshape(1, d), wout, g2.reshape(1, d),
      w1, w2, gf.reshape(1, d))


def kernel(x, c, w_ada, b_ada, norm1_g, norm2_g, final_g, w_in, lambda_q1, lambda_k1,
           lambda_q2, lambda_k2, rel_bias, attn_sub_g, w_o_attn, conv_w, conv_b,
           conv_ln_g, conv_ln_b, w_o_conv, b_o_conv, w_out, w_ffn_in, w_ffn_out):
    bsz, s, d = x.shape
    assert w_ada.shape[0] == 1, "single-layer block"
    assert (d, s % ROW_TILE, s % ATTN_TILE) == (D_MODEL, 0, 0)
    assert PROJ_CHUNK == CONV_WIDTH and ATTN_WIDTH % PROJ_CHUNK == 0
    l = 0
    ada, w_in_bf = _ada(c, w_ada[l], b_ada[l], w_in[l])
    ada3 = ada.reshape(bsz, 6, d)
    bias_tiles = _bias_tiles(rel_bias)
    qkv, u, gates = _in_proj(x, ada3, norm1_g[l], w_in_bf, conv_w[l],
                             conv_b[l], conv_ln_g[l], conv_ln_b[l])
    lamv = jnp.stack([lambda_q1[l], lambda_k1[l], lambda_q2[l], lambda_k2[l]])
    a_n, (woa, woc, wout, w1, w2) = _attention(
        qkv, bias_tiles, lamv, attn_sub_g[l],
        [w_o_attn[l], w_o_conv[l], w_out[l], w_ffn_in[l], w_ffn_out[l]])
    return _mix_ffn(x, ada3, a_n, u, gates, woa, woc, b_o_conv[l], wout, norm2_g[l],
                    w1, w2, final_g)
```

```python
import functools
import math

import jax
import jax.numpy as jnp
from jax import lax
from jax.experimental import pallas as pl
from jax.experimental.pallas import tpu as pltpu

D_MODEL = 1024
N_HEADS_A = 4
HEAD_QK = 64
HEAD_V = 2 * HEAD_QK
ATTN_WIDTH = N_HEADS_A * HEAD_V
CONV_WIDTH = 512
CONV_KERNEL = 31
QKV_COLS = 3 * ATTN_WIDTH
GLU_COLS = 2 * CONV_WIDTH
GATE_COLS = 2 * D_MODEL
IN_COLS = QKV_COLS + GLU_COLS + GATE_COLS
D_FF = 2816
N_BUCKETS = 32
MAX_EXACT = 16
MAX_DISTANCE = 128
EPS = 1e-6
NEG_INF = -1e30
LAM_INIT = 0.8 - 0.6 * math.exp(-0.3 * 0)

V7X_VMEM_BYTES = 64 * 1024 * 1024
VMEM_LIMIT = V7X_VMEM_BYTES - 12 * 1024 * 1024

ADA_STEPS = 8
ROW_TILE = 512
ATTN_TILE = 256
ATTN_LAG = 4
ONES_ROWS = 16
CAST_ROWS = 16
LOG2E = math.log2(math.e)
Q_SCALE = HEAD_QK ** -0.5 * LOG2E
CONV_HALO = 32
CONV_CHUNK = 32
PROJ_CHUNK = 512
FF_CHUNK = 256

BF16 = jnp.bfloat16
F32 = jnp.float32


def _sigmoid(x):
    return 1.0 / (1.0 + jnp.exp(-x))


def _silu(x):
    return x * _sigmoid(x)


def _params(sem, vmem=VMEM_LIMIT, flags=None):
    return pltpu.CompilerParams(dimension_semantics=sem, vmem_limit_bytes=vmem, flags=flags)


def _resident(shape):
    nd = len(shape)
    return pl.BlockSpec(shape, lambda *_: (0,) * nd, pipeline_mode=pl.Buffered(1))


def _ada_kernel(c_ref, w_ref, b_ref, win_ref, o_ref, win_bf_ref):
    ca = _silu(c_ref[...])
    o_ref[...] = jnp.dot(ca, w_ref[...], preferred_element_type=F32,
                         precision=lax.Precision.HIGHEST) + b_ref[...]
    win_bf_ref[...] = win_ref[...].astype(BF16)


def _ada(c, w, b, w_in):
    bsz, d = c.shape
    n = w.shape[-1]
    steps = ADA_STEPS
    tn = n // steps
    _, rows, cols = w_in.shape
    tr = rows // steps
    return pl.pallas_call(
        _ada_kernel,
        grid=(steps,),
        in_specs=[pl.BlockSpec((bsz, d), lambda j: (0, 0)),
                  pl.BlockSpec((None, d, tn), lambda j: (0, 0, j)),
                  pl.BlockSpec((1, tn), lambda j: (0, j)),
                  pl.BlockSpec((None, tr, cols), lambda j: (0, j, 0))],
        out_specs=[pl.BlockSpec((bsz, tn), lambda j: (0, j)),
                   pl.BlockSpec((tr, cols), lambda j: (j, 0))],
        out_shape=[jax.ShapeDtypeStruct((bsz, n), F32),
                   jax.ShapeDtypeStruct((rows, cols), BF16)],
        compiler_params=_params(("arbitrary",)),
        name="ada",
    )(c, w, b, w_in)


def _bias_kernel(rel_ref, o_ref):
    h = pl.program_id(0)
    t = pl.program_id(1)
    T = ATTN_TILE
    key = lax.broadcasted_iota(jnp.int32, (T, T), 0)
    qry = lax.broadcasted_iota(jnp.int32, (T, T), 1)
    dist = (1 - t) * T + qry - key
    n = jnp.maximum(dist, 0)
    large = MAX_EXACT + (jnp.log(jnp.maximum(n, 1).astype(F32) / MAX_EXACT)
                         / math.log(MAX_DISTANCE / MAX_EXACT)
                         * (N_BUCKETS - MAX_EXACT)).astype(jnp.int32)
    large = jnp.minimum(large, N_BUCKETS - 1)
    bucket = jnp.where(n < MAX_EXACT, n, large)
    far = rel_ref[N_BUCKETS - 1, h]
    bias = jnp.zeros((T, T), F32)
    for b in range(N_BUCKETS - 1):
        bias = jnp.where(bucket == b, (rel_ref[b, h] - far) * LOG2E, bias)
    o_ref[...] = jnp.where(dist >= 0, bias, NEG_INF)


def _bias_tiles(rel_bias):
    T = ATTN_TILE
    return pl.pallas_call(
        _bias_kernel,
        grid=(N_HEADS_A, 2),
        in_specs=[pl.BlockSpec(memory_space=pltpu.SMEM)],
        out_specs=pl.BlockSpec((None, None, T, T), lambda h, t: (h, t, 0, 0)),
        out_shape=jax.ShapeDtypeStruct((N_HEADS_A, 2, T, T), F32),
        compiler_params=_params(("arbitrary", "arbitrary")),
        name="bias_tiles",
    )(rel_bias)


def _conv_chunk(r0, ubuf, w_ref, cb_ref, lng_ref, lnb_ref, o_ref):
    off = CONV_HALO - (CONV_KERNEL - 1)
    acc = jnp.zeros((CONV_CHUNK // 8, 8, CONV_WIDTH), F32)
    for k in range(CONV_KERNEL):
        a, b = divmod(k + off, 8)
        rows = ubuf[b, r0 + 8 * a:r0 + 8 * a + CONV_CHUNK, :]
        acc = acc + rows.reshape(acc.shape) * w_ref[k][None]
    u = acc.reshape(CONV_CHUNK, CONV_WIDTH) + cb_ref[...]
    mu = jnp.mean(u, axis=-1, keepdims=True)
    d = u - mu
    var = jnp.mean(d * d, axis=-1, keepdims=True)
    y = d * lax.rsqrt(var + EPS) * lng_ref[...] + lnb_ref[...]
    o_ref[r0:r0 + CONV_CHUNK, :] = _silu(y).astype(BF16)


def _inproj_kernel(x_ref, ada_ref, g_ref, w_ref, cw_ref, cb_ref, lng_ref, lnb_ref,
                   qkv_ref, u_ref, gate_ref, ubuf):
    TS, H, C = ROW_TILE, CONV_HALO, CONV_WIDTH
    i = pl.program_id(1)

    @pl.when(i == 0)
    def _():
        ubuf[0, 0:H, :] = jnp.zeros((H, C), F32)

    @pl.when(i > 0)
    def _():
        ubuf[0, 0:H, :] = ubuf[0, TS:TS + H, :]

    x = x_ref[...]
    ms = jnp.mean(x * x, axis=-1, keepdims=True)
    y = x * lax.rsqrt(ms + EPS)
    h = (y * g_ref[...]) * (1.0 + ada_ref[1:2, :]) + ada_ref[0:1, :]
    hb = h.astype(BF16)

    def proj(base):
        return jnp.dot(hb, w_ref[:, base:base + PROJ_CHUNK], preferred_element_type=F32)

    ubuf[0, H:H + TS, :] = proj(QKV_COLS) * _sigmoid(proj(QKV_COLS + C))
    n = TS + H - 8
    for b in range(1, 8):
        ubuf[b, 0:n, :] = ubuf[0, b:b + n, :]
    jobs = ([(qkv_ref, c, c) for c in range(0, QKV_COLS, PROJ_CHUNK)]
            + [(gate_ref, c, QKV_COLS + GLU_COLS + c) for c in range(0, GATE_COLS, PROJ_CHUNK)])
    rows = list(range(0, TS, CONV_CHUNK))
    for n, (ref, c, base) in enumerate(jobs):
        r = proj(base)
        if base < ATTN_WIDTH:
            r = r * Q_SCALE
        ref[:, c:c + PROJ_CHUNK] = r.astype(BF16)
        for r0 in rows[n * len(rows) // len(jobs):(n + 1) * len(rows) // len(jobs)]:
            _conv_chunk(r0, ubuf, cw_ref, cb_ref, lng_ref, lnb_ref, u_ref)


def _in_proj(x, ada3, g, w, cw, cb, lng, lnb):
    bsz, s, d = x.shape
    tm = ROW_TILE
    C = CONV_WIDTH
    row = lambda n: pl.BlockSpec((None, tm, n), lambda b, i: (b, i, 0))
    vec = lambda a: a.reshape(1, C)
    return pl.pallas_call(
        _inproj_kernel,
        grid=(bsz, s // tm),
        in_specs=[row(d),
                  pl.BlockSpec((None, 6, d), lambda b, i: (b, 0, 0)),
                  _resident((1, d)),
                  _resident((d, IN_COLS)),
                  _resident((CONV_KERNEL, 8, C)),
                  _resident((1, C)), _resident((1, C)), _resident((1, C))],
        out_specs=[row(QKV_COLS), row(C), row(GATE_COLS)],
        out_shape=[jax.ShapeDtypeStruct((bsz, s, QKV_COLS), BF16),
                   jax.ShapeDtypeStruct((bsz, s, C), BF16),
                   jax.ShapeDtypeStruct((bsz, s, GATE_COLS), BF16)],
        scratch_shapes=[pltpu.VMEM((8, tm + CONV_HALO, C), F32)],
        compiler_params=_params(("arbitrary", "arbitrary")),
        name="in_proj",
    )(x, ada3, g.reshape(1, d), w, jnp.broadcast_to(cw[:, None, :], (CONV_KERNEL, 8, C)),
      vec(cb), vec(lng), vec(lnb))


def _attn_kernel(lamv_ref, subg_ref, q_ref, k_ref, v_ref, bias_ref, *rest):
    n_cast = (len(rest) - 5) // 2
    cast_in, o_ref, cast_out = rest[:n_cast], rest[n_cast], rest[n_cast + 1:2 * n_cast + 1]
    vt_ref, qz_ref, m_ref, a_ref = rest[2 * n_cast + 1:]
    T = ATTN_TILE
    NS = 2 * N_HEADS_A
    i = pl.program_id(1)
    hcols = lambda h: slice(h * HEAD_V, (h + 1) * HEAD_V)

    @pl.when(i == 0)
    def _():
        vt = v_ref[...].T
        for h in range(N_HEADS_A):
            vt_ref[h, 0:HEAD_V, :] = vt[hcols(h), :]
            vt_ref[h, HEAD_V:, :] = jnp.ones((ONES_ROWS, vt.shape[1]), BF16)

    lane = lax.broadcasted_iota(jnp.int32, (T, HEAD_V), 1)
    for h in range(N_HEADS_A):
        qs = q_ref[:, hcols(h)]
        zero = jnp.zeros_like(qs)
        qz_ref[2 * h] = jnp.where(lane < HEAD_QK, qs, zero)
        qz_ref[2 * h + 1] = jnp.where(lane >= HEAD_QK, qs, zero)
    m_ref[...] = jnp.full(m_ref.shape, NEG_INF, F32)
    a_ref[...] = jnp.zeros(a_ref.shape, F32)

    def process(tiles):
        units = [(start, tile, n) for start, tile in tiles for n in range(NS)]
        scores, probs, alphas = {}, {}, {}

        def score(u):
            start, _, n = units[u]
            kc = k_ref[pl.ds(start, T), hcols(n // 2)]
            scores[u] = lax.dot_general(kc, qz_ref[n], (((1,), (1,)), ((), ())),
                                        preferred_element_type=F32)

        def softmax(u):
            _, tile, n = units[u]
            s = scores.pop(u)
            if tile is not None:
                s = s + bias_ref[n // 2, tile]
            m_old = m_ref[n]
            m_new = jnp.maximum(m_old, jnp.max(s, axis=0, keepdims=True))
            probs[u] = jnp.exp2(s - m_new).astype(BF16)
            alphas[u] = jnp.exp2(m_old - m_new)
            m_ref[n] = m_new

        def value(u):
            start, _, n = units[u]
            vc = vt_ref[n // 2, :, pl.ds(start, T)]
            a_ref[n] = alphas.pop(u) * a_ref[n] + jnp.dot(
                vc, probs.pop(u), preferred_element_type=F32)

        for step in range(len(units) + 2 * ATTN_LAG):
            if step < len(units):
                score(step)
            if 0 <= step - ATTN_LAG < len(units):
                softmax(step - ATTN_LAG)
            if 0 <= step - 2 * ATTN_LAG < len(units):
                value(step - 2 * ATTN_LAG)

    n_far = jnp.maximum(i - 1, 0)

    def far_pair(j, carry):
        start = pl.multiple_of(j * (2 * T), 2 * T)
        process([(start, None), (start + T, None)])
        return carry

    lax.fori_loop(0, n_far // 2, far_pair, 0)

    @pl.when(n_far % 2 == 1)
    def _():
        process([(pl.multiple_of((n_far - 1) * T, T), None)])

    @pl.when(i >= 1)
    def _():
        process([(pl.multiple_of((i - 1) * T, T), 0), (pl.multiple_of(i * T, T), 1)])

    @pl.when(i == 0)
    def _():
        process([(0, 1)])

    lv = lamv_ref[...]
    lam = (jnp.exp(jnp.sum(lv[0:1] * lv[1:2], axis=-1, keepdims=True))
           - jnp.exp(jnp.sum(lv[2:3] * lv[3:4], axis=-1, keepdims=True)) + LAM_INIT)
    for h in range(N_HEADS_A):
        a1, a2 = a_ref[2 * h], a_ref[2 * h + 1]
        o = (a1[:HEAD_V] / a1[HEAD_V:HEAD_V + 1]
             - lam * (a2[:HEAD_V] / a2[HEAD_V:HEAD_V + 1]))
        ms = jnp.mean(o * o, axis=0, keepdims=True)
        y = (o * lax.rsqrt(ms + EPS)).T
        o_ref[:, hcols(h)] = ((y * subg_ref[...]) * (1.0 - LAM_INIT)).astype(BF16)

    for w_ref, wb_ref in zip(cast_in, cast_out):
        wb_ref[...] = w_ref[...].astype(BF16)


def _attention(qkv, bias_tiles, lamv, subg, weights):
    bsz, s, _ = qkv.shape
    T = ATTN_TILE
    W = ATTN_WIDTH
    NS = 2 * N_HEADS_A
    n_q = s // T
    steps = bsz * n_q

    def slab_spec(w, squeeze):
        rows, cols = w.shape[-2:]
        n_slabs = max(n for n in range(1, steps + 1)
                      if steps % n == 0 and rows % (n * CAST_ROWS) == 0)
        per = steps // n_slabs
        block = (rows // n_slabs, cols)
        if squeeze:
            return pl.BlockSpec((None,) + block, lambda b, i: (0, (b * n_q + i) // per, 0))
        return pl.BlockSpec(block, lambda b, i: ((b * n_q + i) // per, 0))

    out = pl.pallas_call(
        _attn_kernel,
        grid=(bsz, n_q),
        in_specs=[_resident((4, HEAD_QK)),
                  _resident((1, HEAD_V)),
                  pl.BlockSpec((None, T, W), lambda b, i: (b, i, 0)),
                  pl.BlockSpec((None, s, W), lambda b, i: (b, 0, 1)),
                  pl.BlockSpec((None, s, W), lambda b, i: (b, 0, 2)),
                  _resident((N_HEADS_A, 2, T, T))] + [slab_spec(w, True) for w in weights],
        out_specs=[pl.BlockSpec((None, T, W), lambda b, i: (b, i, 0))]
        + [slab_spec(w, False) for w in weights],
        out_shape=[jax.ShapeDtypeStruct((bsz, s, W), BF16)]
        + [jax.ShapeDtypeStruct(w.shape[-2:], BF16) for w in weights],
        scratch_shapes=[pltpu.VMEM((N_HEADS_A, HEAD_V + ONES_ROWS, s), BF16),
                        pltpu.VMEM((NS, T, HEAD_V), BF16),
                        pltpu.VMEM((NS, 1, T), F32),
                        pltpu.VMEM((NS, HEAD_V + ONES_ROWS, T), F32)],
        compiler_params=_params(("parallel", "arbitrary")),
        name="diff_attn",
    )(lamv, subg.reshape(1, HEAD_V), qkv, qkv, qkv, bias_tiles, *weights)
    return out[0], out[1:]


def _mixffn_kernel(x_ref, ada_ref, a_ref, u_ref, gate_ref, woa_ref, woc_ref, boc_ref,
                   wout_ref, g2_ref, w1_ref, w2_ref, gf_ref, o_ref,
                   x1_ref, h2_ref, act_ref):
    D = D_MODEL
    a = jnp.dot(a_ref[...], woa_ref[...], preferred_element_type=F32)
    cv = jnp.dot(u_ref[...], woc_ref[...], preferred_element_type=F32) + boc_ref[...]
    gate = gate_ref[...]
    y = (_sigmoid(gate[:, :D].astype(F32)) * a
         + _sigmoid(gate[:, D:].astype(F32)) * cv)
    z = jnp.dot(y.astype(BF16), wout_ref[...], preferred_element_type=F32)
    x1 = x_ref[...] + ada_ref[2:3, :] * z
    x1_ref[...] = x1
    ms = jnp.mean(x1 * x1, axis=-1, keepdims=True)
    yn = x1 * lax.rsqrt(ms + EPS)
    h2_ref[...] = ((yn * g2_ref[...]) * (1.0 + ada_ref[4:5, :]) + ada_ref[3:4, :]).astype(BF16)

    h = h2_ref[...]
    for c in range(0, D_FF, FF_CHUNK):
        fg = jnp.dot(h, w1_ref[:, c:c + FF_CHUNK], preferred_element_type=F32)
        fu = jnp.dot(h, w1_ref[:, D_FF + c:D_FF + c + FF_CHUNK],
                     preferred_element_type=F32)
        act_ref[:, c:c + FF_CHUNK] = (_silu(fg) * fu).astype(BF16)
    z2 = jnp.dot(act_ref[...], w2_ref[...], preferred_element_type=F32)
    x2 = x1_ref[...] + ada_ref[5:6, :] * z2
    ms2 = jnp.mean(x2 * x2, axis=-1, keepdims=True)
    o_ref[...] = (x2 * lax.rsqrt(ms2 + EPS)) * gf_ref[...]


def _mix_ffn(x, ada3, a_n, u, gates, woa, woc, boc, wout, g2, w1, w2, gf):
    bsz, s, d = x.shape
    tm = ROW_TILE
    row = lambda n: pl.BlockSpec((None, tm, n), lambda b, i: (b, i, 0))
    return pl.pallas_call(
        _mixffn_kernel,
        grid=(bsz, s // tm),
        in_specs=[row(d),
                  pl.BlockSpec((None, 6, d), lambda b, i: (b, 0, 0)),
                  row(ATTN_WIDTH), row(CONV_WIDTH), row(GATE_COLS),
                  _resident((ATTN_WIDTH, d)), _resident((CONV_WIDTH, d)),
                  _resident((1, d)), _resident((d, d)), _resident((1, d)),
                  _resident((d, 2 * D_FF)), _resident((D_FF, d)), _resident((1, d))],
        out_specs=row(d),
        out_shape=jax.ShapeDtypeStruct((bsz, s, d), F32),
        scratch_shapes=[pltpu.VMEM((tm, d), F32), pltpu.VMEM((tm, d), BF16),
                        pltpu.VMEM((tm, D_FF), BF16)],
        compiler_params=_params(("parallel", "arbitrary")),
        name="mix_ffn",
    )(x, ada3, a_n, u, gates, woa, woc, boc.reshape(1, d), wout, g2.reshape(1, d),
      w1, w2, gf.reshape(1, d))


def kernel(x, c, w_ada, b_ada, norm1_g, norm2_g, final_g, w_in, lambda_q1, lambda_k1,
           lambda_q2, lambda_k2, rel_bias, attn_sub_g, w_o_attn, conv_w, conv_b,
           conv_ln_g, conv_ln_b, w_o_conv, b_o_conv, w_out, w_ffn_in, w_ffn_out):
    bsz, s, d = x.shape
    assert w_ada.shape[0] == 1, "single-layer block"
    assert (d, s % ROW_TILE, s % ATTN_TILE) == (D_MODEL, 0, 0)
    assert PROJ_CHUNK == CONV_WIDTH and ATTN_WIDTH % PROJ_CHUNK == 0
    l = 0
    ada, w_in_bf = _ada(c, w_ada, b_ada, w_in)
    ada3 = ada.reshape(bsz, 6, d)
    bias_tiles = _bias_tiles(rel_bias)
    qkv, u, gates = _in_proj(x, ada3, norm1_g[l], w_in_bf, conv_w[l],
                             conv_b[l], conv_ln_g[l], conv_ln_b[l])
    lamv = jnp.stack([lambda_q1[l], lambda_k1[l], lambda_q2[l], lambda_k2[l]])
    a_n, (woa, woc, wout, w1, w2) = _attention(
        qkv, bias_tiles, lamv, attn_sub_g[l],
        [w_o_attn, w_o_conv, w_out, w_ffn_in, w_ffn_out])
    return _mix_ffn(x, ada3, a_n, u, gates, woa, woc, b_o_conv[l], wout, norm2_g[l],
                    w1, w2, final_g)
```

```python
import math

import jax
import jax.numpy as jnp
from jax import lax
from jax.experimental import pallas as pl
from jax.experimental.pallas import tpu as pltpu

D_MODEL = 1024
N_HEADS_A = 4
HEAD_QK = 64
HEAD_V = 2 * HEAD_QK
ATTN_WIDTH = N_HEADS_A * HEAD_V
CONV_WIDTH = 512
CONV_KERNEL = 31
QKV_COLS = 3 * ATTN_WIDTH
GLU_COLS = 2 * CONV_WIDTH
GATE_COLS = 2 * D_MODEL
IN_COLS = QKV_COLS + GLU_COLS + GATE_COLS
D_FF = 2816
N_BUCKETS = 32
MAX_EXACT = 16
MAX_DISTANCE = 128
EPS = 1e-6
NEG_INF = -1e30
LAM_INIT = 0.8 - 0.6 * math.exp(-0.3 * 0)

V7X_VMEM_BYTES = 64 * 1024 * 1024
VMEM_LIMIT = V7X_VMEM_BYTES - 12 * 1024 * 1024

ADA_STEPS = 8
ROW_TILE = 512
ATTN_TILE = 256
ATTN_LAG = 4
ONES_ROWS = 16
CAST_ROWS = 16
LOG2E = math.log2(math.e)
Q_SCALE = HEAD_QK ** -0.5 * LOG2E
CONV_HALO = 32
CONV_CHUNK = 32
PROJ_CHUNK = 512
FF_CHUNK = 256

BF16 = jnp.bfloat16
F32 = jnp.float32


def _sigmoid(x):
    return 1.0 / (1.0 + jnp.exp(-x))


def _silu(x):
    return x * _sigmoid(x)


def _params(sem, vmem=VMEM_LIMIT, flags=None):
    return pltpu.CompilerParams(dimension_semantics=sem, vmem_limit_bytes=vmem, flags=flags)


def _resident(shape):
    nd = len(shape)
    return pl.BlockSpec(shape, lambda *_: (0,) * nd, pipeline_mode=pl.Buffered(1))


def _ada_kernel(c_ref, w_ref, b_ref, win_ref, o_ref, win_bf_ref):
    ca = _silu(c_ref[...])
    o_ref[...] = jnp.dot(ca, w_ref[...], preferred_element_type=F32,
                         precision=lax.Precision.HIGHEST) + b_ref[...]
    win_bf_ref[...] = win_ref[...].astype(BF16)


def _ada(c, w, b, w_in):
    bsz, d = c.shape
    n = w.shape[-1]
    steps = ADA_STEPS
    tn = n // steps
    _, rows, cols = w_in.shape
    tr = rows // steps
    return pl.pallas_call(
        _ada_kernel,
        grid=(steps,),
        in_specs=[pl.BlockSpec((bsz, d), lambda j: (0, 0)),
                  pl.BlockSpec((None, d, tn), lambda j: (0, 0, j)),
                  pl.BlockSpec((1, tn), lambda j: (0, j)),
                  pl.BlockSpec((None, tr, cols), lambda j: (0, j, 0))],
        out_specs=[pl.BlockSpec((bsz, tn), lambda j: (0, j)),
                   pl.BlockSpec((tr, cols), lambda j: (j, 0))],
        out_shape=[jax.ShapeDtypeStruct((bsz, n), F32),
                   jax.ShapeDtypeStruct((rows, cols), BF16)],
        compiler_params=_params(("arbitrary",)),
        name="ada",
    )(c, w, b, w_in)


def _bias_kernel(rel_ref, o_ref):
    h = pl.program_id(0)
    t = pl.program_id(1)
    T = ATTN_TILE
    key = lax.broadcasted_iota(jnp.int32, (T, T), 0)
    qry = lax.broadcasted_iota(jnp.int32, (T, T), 1)
    dist = (1 - t) * T + qry - key
    n = jnp.maximum(dist, 0)
    large = MAX_EXACT + (jnp.log(jnp.maximum(n, 1).astype(F32) / MAX_EXACT)
                         / math.log(MAX_DISTANCE / MAX_EXACT)
                         * (N_BUCKETS - MAX_EXACT)).astype(jnp.int32)
    large = jnp.minimum(large, N_BUCKETS - 1)
    bucket = jnp.where(n < MAX_EXACT, n, large)
    far = rel_ref[N_BUCKETS - 1, h]
    bias = jnp.zeros((T, T), F32)
    for b in range(N_BUCKETS - 1):
        bias = jnp.where(bucket == b, (rel_ref[b, h] - far) * LOG2E, bias)
    o_ref[...] = jnp.where(dist >= 0, bias, NEG_INF)


def _bias_tiles(rel_bias):
    T = ATTN_TILE
    return pl.pallas_call(
        _bias_kernel,
        grid=(N_HEADS_A, 2),
        in_specs=[pl.BlockSpec(memory_space=pltpu.SMEM)],
        out_specs=pl.BlockSpec((None, None, T, T), lambda h, t: (h, t, 0, 0)),
        out_shape=jax.ShapeDtypeStruct((N_HEADS_A, 2, T, T), F32),
        compiler_params=_params(("arbitrary", "arbitrary")),
        name="bias_tiles",
    )(rel_bias)


def _conv_chunk(r0, ubuf, w_ref, cb_ref, lng_ref, lnb_ref, o_ref):
    off = CONV_HALO - (CONV_KERNEL - 1)
    acc = jnp.zeros((CONV_CHUNK // 8, 8, CONV_WIDTH), F32)
    for k in range(CONV_KERNEL):
        a, b = divmod(k + off, 8)
        rows = ubuf[b, r0 + 8 * a:r0 + 8 * a + CONV_CHUNK, :]
        acc = acc + rows.reshape(acc.shape) * w_ref[k][None]
    u = acc.reshape(CONV_CHUNK, CONV_WIDTH) + cb_ref[...]
    mu = jnp.mean(u, axis=-1, keepdims=True)
    d = u - mu
    var = jnp.mean(d * d, axis=-1, keepdims=True)
    y = d * lax.rsqrt(var + EPS) * lng_ref[...] + lnb_ref[...]
    o_ref[r0:r0 + CONV_CHUNK, :] = _silu(y).astype(BF16)


def _inproj_kernel(x_ref, ada_ref, g_ref, w_ref, cw_ref, cb_ref, lng_ref, lnb_ref,
                   qkv_ref, u_ref, gate_ref, ubuf):
    TS, H, C = ROW_TILE, CONV_HALO, CONV_WIDTH
    i = pl.program_id(1)

    @pl.when(i == 0)
    def _():
        ubuf[0, 0:H, :] = jnp.zeros((H, C), F32)

    @pl.when(i > 0)
    def _():
        ubuf[0, 0:H, :] = ubuf[0, TS:TS + H, :]

    x = x_ref[...]
    ms = jnp.mean(x * x, axis=-1, keepdims=True)
    y = x * lax.rsqrt(ms + EPS)
    h = (y * g_ref[...]) * (1.0 + ada_ref[1:2, :]) + ada_ref[0:1, :]
    hb = h.astype(BF16)

    def proj(base):
        return jnp.dot(hb, w_ref[:, base:base + PROJ_CHUNK], preferred_element_type=F32)

    ubuf[0, H:H + TS, :] = proj(QKV_COLS) * _sigmoid(proj(QKV_COLS + C))
    n = TS + H - 8
    for b in range(1, 8):
        ubuf[b, 0:n, :] = ubuf[0, b:b + n, :]
    jobs = ([(qkv_ref, c, c) for c in range(0, QKV_COLS, PROJ_CHUNK)]
            + [(gate_ref, c, QKV_COLS + GLU_COLS + c) for c in range(0, GATE_COLS, PROJ_CHUNK)])
    rows = list(range(0, TS, CONV_CHUNK))
    for n, (ref, c, base) in enumerate(jobs):
        r = proj(base)
        if base < ATTN_WIDTH:
            r = r * Q_SCALE
        ref[:, c:c + PROJ_CHUNK] = r.astype(BF16)
        for r0 in rows[n * len(rows) // len(jobs):(n + 1) * len(rows) // len(jobs)]:
            _conv_chunk(r0, ubuf, cw_ref, cb_ref, lng_ref, lnb_ref, u_ref)


def _in_proj(x, ada3, g, w, cw, cb, lng, lnb):
    bsz, s, d = x.shape
    tm = ROW_TILE
    C = CONV_WIDTH
    row = lambda n: pl.BlockSpec((None, tm, n), lambda b, i: (b, i, 0))
    vec = lambda a: a.reshape(1, C)
    return pl.pallas_call(
        _inproj_kernel,
        grid=(bsz, s // tm),
        in_specs=[row(d),
                  pl.BlockSpec((None, 6, d), lambda b, i: (b, 0, 0)),
                  _resident((1, d)),
                  _resident((d, IN_COLS)),
                  _resident((CONV_KERNEL, 8, C)),
                  _resident((1, C)), _resident((1, C)), _resident((1, C))],
        out_specs=[row(QKV_COLS), row(C), row(GATE_COLS)],
        out_shape=[jax.ShapeDtypeStruct((bsz, s, QKV_COLS), BF16),
                   jax.ShapeDtypeStruct((bsz, s, C), BF16),
                   jax.ShapeDtypeStruct((bsz, s, GATE_COLS), BF16)],
        scratch_shapes=[pltpu.VMEM((8, tm + CONV_HALO, C), F32)],
        compiler_params=_params(("arbitrary", "arbitrary")),
        name="in_proj",
    )(x, ada3, g.reshape(1, d), w, jnp.broadcast_to(cw[:, None, :], (CONV_KERNEL, 8, C)),
      vec(cb), vec(lng), vec(lnb))


def _attn_kernel(lamv_ref, subg_ref, q_ref, k_ref, v_ref, bias_ref, *rest):
    n_cast = (len(rest) - 5) // 2
    cast_in, o_ref, cast_out = rest[:n_cast], rest[n_cast], rest[n_cast + 1:2 * n_cast + 1]
    vt_ref, qz_ref, m_ref, a_ref = rest[2 * n_cast + 1:]
    T = ATTN_TILE
    NS = 2 * N_HEADS_A
    n_q = q_ref.shape[0] // T
    hcols = lambda h: slice(h * HEAD_V, (h + 1) * HEAD_V)
    rows = lambda t: slice(t * T, (t + 1) * T)

    vt = v_ref[...].T
    for h in range(N_HEADS_A):
        vt_ref[h, 0:HEAD_V, :] = vt[hcols(h), :]
        vt_ref[h, HEAD_V:, :] = jnp.ones((ONES_ROWS, vt.shape[1]), BF16)

    lv = lamv_ref[...]
    lam = (jnp.exp(jnp.sum(lv[0:1] * lv[1:2], axis=-1, keepdims=True))
           - jnp.exp(jnp.sum(lv[2:3] * lv[3:4], axis=-1, keepdims=True)) + LAM_INIT)
    lane = lax.broadcasted_iota(jnp.int32, (T, HEAD_V), 1)

    def prepare(i):
        slot = i % 2
        for h in range(N_HEADS_A):
            qs = q_ref[rows(i), hcols(h)]
            zero = jnp.zeros_like(qs)
            qz_ref[slot, 2 * h] = jnp.where(lane < HEAD_QK, qs, zero)
            qz_ref[slot, 2 * h + 1] = jnp.where(lane >= HEAD_QK, qs, zero)
        m_ref[slot] = jnp.full(m_ref.shape[1:], NEG_INF, F32)
        a_ref[slot] = jnp.zeros(a_ref.shape[1:], F32)

    def finalize(i):
        slot = i % 2
        for h in range(N_HEADS_A):
            a1, a2 = a_ref[slot, 2 * h], a_ref[slot, 2 * h + 1]
            o = (a1[:HEAD_V] / a1[HEAD_V:HEAD_V + 1]
                 - lam * (a2[:HEAD_V] / a2[HEAD_V:HEAD_V + 1]))
            ms = jnp.mean(o * o, axis=0, keepdims=True)
            y = (o * lax.rsqrt(ms + EPS)).T
            o_ref[rows(i), hcols(h)] = ((y * subg_ref[...]) * (1.0 - LAM_INIT)).astype(BF16)

    units = [(i, j, n) for i in range(n_q) for j in range(i + 1) for n in range(NS)]
    scores, probs, alphas = {}, {}, {}

    def score(u):
        i, j, n = units[u]
        kc = k_ref[rows(j), hcols(n // 2)]
        scores[u] = lax.dot_general(kc, qz_ref[i % 2, n], (((1,), (1,)), ((), ())),
                                    preferred_element_type=F32)

    def softmax(u):
        i, j, n = units[u]
        s = scores.pop(u)
        if j >= i - 1:
            s = s + bias_ref[n // 2, j - (i - 1)]
        m_old = m_ref[i % 2, n]
        m_new = jnp.maximum(m_old, jnp.max(s, axis=0, keepdims=True))
        probs[u] = jnp.exp2(s - m_new).astype(BF16)
        alphas[u] = jnp.exp2(m_old - m_new)
        m_ref[i % 2, n] = m_new

    def value(u):
        i, j, n = units[u]
        vc = vt_ref[n // 2, :, rows(j)]
        a_ref[i % 2, n] = alphas.pop(u) * a_ref[i % 2, n] + jnp.dot(
            vc, probs.pop(u), preferred_element_type=F32)

    for step in range(len(units) + 2 * ATTN_LAG):
        if step < len(units):
            if units[step][1:] == (0, 0):
                prepare(units[step][0])
            score(step)
        if 0 <= step - ATTN_LAG < len(units):
            softmax(step - ATTN_LAG)
        done = step - 2 * ATTN_LAG
        if 0 <= done < len(units):
            value(done)
            i, j, n = units[done]
            if (j, n) == (i, NS - 1):
                finalize(i)

    for w_ref, wb_ref in zip(cast_in, cast_out):
        wb_ref[...] = w_ref[...].astype(BF16)


def _attention(qkv, bias_tiles, lamv, subg, weights):
    bsz, s, _ = qkv.shape
    T = ATTN_TILE
    W = ATTN_WIDTH
    NS = 2 * N_HEADS_A
    steps = bsz

    def slab_spec(w, squeeze):
        rows, cols = w.shape[-2:]
        assert rows % (steps * CAST_ROWS) == 0
        block = (rows // steps, cols)
        if squeeze:
            return pl.BlockSpec((None,) + block, lambda b: (0, b, 0))
        return pl.BlockSpec(block, lambda b: (b, 0))

    seq = lambda col: pl.BlockSpec((None, s, W), lambda b: (b, 0, col))
    out = pl.pallas_call(
        _attn_kernel,
        grid=(bsz,),
        in_specs=[_resident((4, HEAD_QK)),
                  _resident((1, HEAD_V)),
                  seq(0), seq(1), seq(2),
                  _resident((N_HEADS_A, 2, T, T))] + [slab_spec(w, True) for w in weights],
        out_specs=[seq(0)] + [slab_spec(w, False) for w in weights],
        out_shape=[jax.ShapeDtypeStruct((bsz, s, W), BF16)]
        + [jax.ShapeDtypeStruct(w.shape[-2:], BF16) for w in weights],
        scratch_shapes=[pltpu.VMEM((N_HEADS_A, HEAD_V + ONES_ROWS, s), BF16),
                        pltpu.VMEM((2, NS, T, HEAD_V), BF16),
                        pltpu.VMEM((2, NS, 1, T), F32),
                        pltpu.VMEM((2, NS, HEAD_V + ONES_ROWS, T), F32)],
        compiler_params=_params(("arbitrary",)),
        name="diff_attn",
    )(lamv, subg.reshape(1, HEAD_V), qkv, qkv, qkv, bias_tiles, *weights)
    return out[0], out[1:]


def _mixffn_kernel(x_ref, ada_ref, a_ref, u_ref, gate_ref, woa_ref, woc_ref, boc_ref,
                   wout_ref, g2_ref, w1_ref, w2_ref, gf_ref, o_ref,
                   x1_ref, h2_ref, act_ref):
    D = D_MODEL
    a = jnp.dot(a_ref[...], woa_ref[...], preferred_element_type=F32)
    cv = jnp.dot(u_ref[...], woc_ref[...], preferred_element_type=F32) + boc_ref[...]
    gate = gate_ref[...]
    y = (_sigmoid(gate[:, :D].astype(F32)) * a
         + _sigmoid(gate[:, D:].astype(F32)) * cv)
    z = jnp.dot(y.astype(BF16), wout_ref[...], preferred_element_type=F32)
    x1 = x_ref[...] + ada_ref[2:3, :] * z
    x1_ref[...] = x1
    ms = jnp.mean(x1 * x1, axis=-1, keepdims=True)
    yn = x1 * lax.rsqrt(ms + EPS)
    h2_ref[...] = ((yn * g2_ref[...]) * (1.0 + ada_ref[4:5, :]) + ada_ref[3:4, :]).astype(BF16)

    h = h2_ref[...]
    for c in range(0, D_FF, FF_CHUNK):
        fg = jnp.dot(h, w1_ref[:, c:c + FF_CHUNK], preferred_element_type=F32)
        fu = jnp.dot(h, w1_ref[:, D_FF + c:D_FF + c + FF_CHUNK],
                     preferred_element_type=F32)
        act_ref[:, c:c + FF_CHUNK] = (_silu(fg) * fu).astype(BF16)
    z2 = jnp.dot(act_ref[...], w2_ref[...], preferred_element_type=F32)
    x2 = x1_ref[...] + ada_ref[5:6, :] * z2
    ms2 = jnp.mean(x2 * x2, axis=-1, keepdims=True)
    o_ref[...] = (x2 * lax.rsqrt(ms2 + EPS)) * gf_ref[...]


def _mix_ffn(x, ada3, a_n, u, gates, woa, woc, boc, wout, g2, w1, w2, gf):
    bsz, s, d = x.shape
    tm = ROW_TILE
    row = lambda n: pl.BlockSpec((None, tm, n), lambda b, i: (b, i, 0))
    return pl.pallas_call(
        _mixffn_kernel,
        grid=(bsz, s // tm),
        in_specs=[row(d),
                  pl.BlockSpec((None, 6, d), lambda b, i: (b, 0, 0)),
                  row(ATTN_WIDTH), row(CONV_WIDTH), row(GATE_COLS),
                  _resident((ATTN_WIDTH, d)), _resident((CONV_WIDTH, d)),
                  _resident((1, d)), _resident((d, d)), _resident((1, d)),
                  _resident((d, 2 * D_FF)), _resident((D_FF, d)), _resident((1, d))],
        out_specs=row(d),
        out_shape=jax.ShapeDtypeStruct((bsz, s, d), F32),
        scratch_shapes=[pltpu.VMEM((tm, d), F32), pltpu.VMEM((tm, d), BF16),
                        pltpu.VMEM((tm, D_FF), BF16)],
        compiler_params=_params(("parallel", "arbitrary")),
        name="mix_ffn",
    )(x, ada3, a_n, u, gates, woa, woc, boc.reshape(1, d), wout, g2.reshape(1, d),
      w1, w2, gf.reshape(1, d))


def kernel(x, c, w_ada, b_ada, norm1_g, norm2_g, final_g, w_in, lambda_q1, lambda_k1,
           lambda_q2, lambda_k2, rel_bias, attn_sub_g, w_o_attn, conv_w, conv_b,
           conv_ln_g, conv_ln_b, w_o_conv, b_o_conv, w_out, w_ffn_in, w_ffn_out):
    bsz, s, d = x.shape
    assert w_ada.shape[0] == 1, "single-layer block"
    assert (d, s % ROW_TILE, s % ATTN_TILE) == (D_MODEL, 0, 0)
    assert PROJ_CHUNK == CONV_WIDTH and ATTN_WIDTH % PROJ_CHUNK == 0
    l = 0
    ada, w_in_bf = _ada(c, w_ada, b_ada, w_in)
    ada3 = ada.reshape(bsz, 6, d)
    bias_tiles = _bias_tiles(rel_bias)
    qkv, u, gates = _in_proj(x, ada3, norm1_g[l], w_in_bf, conv_w[l],
                             conv_b[l], conv_ln_g[l], conv_ln_b[l])
    lamv = jnp.stack([lambda_q1[l], lambda_k1[l], lambda_q2[l], lambda_k2[l]])
    a_n, (woa, woc, wout, w1, w2) = _attention(
        qkv, bias_tiles, lamv, attn_sub_g[l],
        [w_o_attn, w_o_conv, w_out, w_ffn_in, w_ffn_out])
    return _mix_ffn(x, ada3, a_n, u, gates, woa, woc, b_o_conv[l], wout, norm2_g[l],
                    w1, w2, final_g)
```

```python
import math

import jax
import jax.numpy as jnp
from jax import lax
from jax.experimental import pallas as pl
from jax.experimental.pallas import tpu as pltpu

D_MODEL = 1024
N_HEADS_A = 4
HEAD_QK = 64
HEAD_V = 2 * HEAD_QK
ATTN_WIDTH = N_HEADS_A * HEAD_V
CONV_WIDTH = 512
CONV_KERNEL = 31
QKV_COLS = 3 * ATTN_WIDTH
GLU_COLS = 2 * CONV_WIDTH
GATE_COLS = 2 * D_MODEL
IN_COLS = QKV_COLS + GLU_COLS + GATE_COLS
D_FF = 2816
N_BUCKETS = 32
MAX_EXACT = 16
MAX_DISTANCE = 128
EPS = 1e-6
NEG_INF = -1e30
LAM_INIT = 0.8 - 0.6 * math.exp(-0.3 * 0)

V7X_VMEM_BYTES = 64 * 1024 * 1024
VMEM_LIMIT = V7X_VMEM_BYTES - 12 * 1024 * 1024

ADA_STEPS = 8
ROW_TILE = 512
ATTN_TILE = 256
ATTN_LAG = 6
ONES_ROWS = 16
CAST_ROWS = 16
LOG2E = math.log2(math.e)
Q_SCALE = HEAD_QK ** -0.5 * LOG2E
CONV_HALO = 32
CONV_CHUNK = 32
PROJ_CHUNK = 512
FF_CHUNK = 256

BF16 = jnp.bfloat16
F32 = jnp.float32


def _sigmoid(x):
    return 1.0 / (1.0 + jnp.exp(-x))


def _silu(x):
    return x * _sigmoid(x)


def _params(sem, vmem=VMEM_LIMIT, flags=None):
    return pltpu.CompilerParams(dimension_semantics=sem, vmem_limit_bytes=vmem, flags=flags)


def _resident(shape):
    nd = len(shape)
    return pl.BlockSpec(shape, lambda *_: (0,) * nd, pipeline_mode=pl.Buffered(1))


def _ada_kernel(c_ref, w_ref, b_ref, win_ref, o_ref, win_bf_ref):
    ca = _silu(c_ref[...])
    o_ref[...] = jnp.dot(ca, w_ref[...], preferred_element_type=F32,
                         precision=lax.Precision.HIGHEST) + b_ref[...]
    win_bf_ref[...] = win_ref[...].astype(BF16)


def _ada(c, w, b, w_in):
    bsz, d = c.shape
    n = w.shape[-1]
    steps = ADA_STEPS
    tn = n // steps
    _, rows, cols = w_in.shape
    tr = rows // steps
    return pl.pallas_call(
        _ada_kernel,
        grid=(steps,),
        in_specs=[pl.BlockSpec((bsz, d), lambda j: (0, 0)),
                  pl.BlockSpec((None, d, tn), lambda j: (0, 0, j)),
                  pl.BlockSpec((1, tn), lambda j: (0, j)),
                  pl.BlockSpec((None, tr, cols), lambda j: (0, j, 0))],
        out_specs=[pl.BlockSpec((bsz, tn), lambda j: (0, j)),
                   pl.BlockSpec((tr, cols), lambda j: (j, 0))],
        out_shape=[jax.ShapeDtypeStruct((bsz, n), F32),
                   jax.ShapeDtypeStruct((rows, cols), BF16)],
        compiler_params=_params(("arbitrary",)),
        name="ada",
    )(c, w, b, w_in)


def _bias_kernel(rel_ref, o_ref):
    h = pl.program_id(0)
    t = pl.program_id(1)
    T = ATTN_TILE
    key = lax.broadcasted_iota(jnp.int32, (T, T), 0)
    qry = lax.broadcasted_iota(jnp.int32, (T, T), 1)
    dist = (1 - t) * T + qry - key
    n = jnp.maximum(dist, 0)
    large = MAX_EXACT + (jnp.log(jnp.maximum(n, 1).astype(F32) / MAX_EXACT)
                         / math.log(MAX_DISTANCE / MAX_EXACT)
                         * (N_BUCKETS - MAX_EXACT)).astype(jnp.int32)
    large = jnp.minimum(large, N_BUCKETS - 1)
    bucket = jnp.where(n < MAX_EXACT, n, large)
    far = rel_ref[N_BUCKETS - 1, h]
    bias = jnp.zeros((T, T), F32)
    for b in range(N_BUCKETS - 1):
        bias = jnp.where(bucket == b, (rel_ref[b, h] - far) * LOG2E, bias)
    o_ref[...] = jnp.where(dist >= 0, bias, NEG_INF)


def _bias_tiles(rel_bias):
    T = ATTN_TILE
    return pl.pallas_call(
        _bias_kernel,
        grid=(N_HEADS_A, 2),
        in_specs=[pl.BlockSpec(memory_space=pltpu.SMEM)],
        out_specs=pl.BlockSpec((None, None, T, T), lambda h, t: (h, t, 0, 0)),
        out_shape=jax.ShapeDtypeStruct((N_HEADS_A, 2, T, T), F32),
        compiler_params=_params(("arbitrary", "arbitrary")),
        name="bias_tiles",
    )(rel_bias)


def _conv_chunk(r0, ubuf, w_ref, cb_ref, lng_ref, lnb_ref, o_ref):
    off = CONV_HALO - (CONV_KERNEL - 1)
    acc = jnp.zeros((CONV_CHUNK // 8, 8, CONV_WIDTH), F32)
    for k in range(CONV_KERNEL):
        a, b = divmod(k + off, 8)
        rows = ubuf[b, r0 + 8 * a:r0 + 8 * a + CONV_CHUNK, :]
        acc = acc + rows.reshape(acc.shape) * w_ref[k][None]
    u = acc.reshape(CONV_CHUNK, CONV_WIDTH) + cb_ref[...]
    mu = jnp.mean(u, axis=-1, keepdims=True)
    d = u - mu
    var = jnp.mean(d * d, axis=-1, keepdims=True)
    y = d * lax.rsqrt(var + EPS) * lng_ref[...] + lnb_ref[...]
    o_ref[r0:r0 + CONV_CHUNK, :] = _silu(y).astype(BF16)


def _inproj_kernel(x_ref, ada_ref, g_ref, w_ref, cw_ref, cb_ref, lng_ref, lnb_ref,
                   qkv_ref, u_ref, gate_ref, ubuf):
    TS, H, C = ROW_TILE, CONV_HALO, CONV_WIDTH
    i = pl.program_id(1)

    @pl.when(i == 0)
    def _():
        ubuf[0, 0:H, :] = jnp.zeros((H, C), F32)

    @pl.when(i > 0)
    def _():
        ubuf[0, 0:H, :] = ubuf[0, TS:TS + H, :]

    x = x_ref[...]
    ms = jnp.mean(x * x, axis=-1, keepdims=True)
    y = x * lax.rsqrt(ms + EPS)
    h = (y * g_ref[...]) * (1.0 + ada_ref[1:2, :]) + ada_ref[0:1, :]
    hb = h.astype(BF16)

    def proj(base):
        return jnp.dot(hb, w_ref[:, base:base + PROJ_CHUNK], preferred_element_type=F32)

    ubuf[0, H:H + TS, :] = proj(QKV_COLS) * _sigmoid(proj(QKV_COLS + C))
    n = TS + H - 8
    for b in range(1, 8):
        ubuf[b, 0:n, :] = ubuf[0, b:b + n, :]
    jobs = ([(qkv_ref, c, c) for c in range(0, QKV_COLS, PROJ_CHUNK)]
            + [(gate_ref, c, QKV_COLS + GLU_COLS + c) for c in range(0, GATE_COLS, PROJ_CHUNK)])
    rows = list(range(0, TS, CONV_CHUNK))
    for n, (ref, c, base) in enumerate(jobs):
        r = proj(base)
        if base < ATTN_WIDTH:
            r = r * Q_SCALE
        ref[:, c:c + PROJ_CHUNK] = r.astype(BF16)
        for r0 in rows[n * len(rows) // len(jobs):(n + 1) * len(rows) // len(jobs)]:
            _conv_chunk(r0, ubuf, cw_ref, cb_ref, lng_ref, lnb_ref, u_ref)


def _in_proj(x, ada3, g, w, cw, cb, lng, lnb):
    bsz, s, d = x.shape
    tm = ROW_TILE
    C = CONV_WIDTH
    row = lambda n: pl.BlockSpec((None, tm, n), lambda b, i: (b, i, 0))
    vec = lambda a: a.reshape(1, C)
    return pl.pallas_call(
        _inproj_kernel,
        grid=(bsz, s // tm),
        in_specs=[row(d),
                  pl.BlockSpec((None, 6, d), lambda b, i: (b, 0, 0)),
                  _resident((1, d)),
                  _resident((d, IN_COLS)),
                  _resident((CONV_KERNEL, 8, C)),
                  _resident((1, C)), _resident((1, C)), _resident((1, C))],
        out_specs=[row(QKV_COLS), row(C), row(GATE_COLS)],
        out_shape=[jax.ShapeDtypeStruct((bsz, s, QKV_COLS), BF16),
                   jax.ShapeDtypeStruct((bsz, s, C), BF16),
                   jax.ShapeDtypeStruct((bsz, s, GATE_COLS), BF16)],
        scratch_shapes=[pltpu.VMEM((8, tm + CONV_HALO, C), F32)],
        compiler_params=_params(("arbitrary", "arbitrary")),
        name="in_proj",
    )(x, ada3, g.reshape(1, d), w, jnp.broadcast_to(cw[:, None, :], (CONV_KERNEL, 8, C)),
      vec(cb), vec(lng), vec(lnb))


def _attn_kernel(lamv_ref, subg_ref, q_ref, k_ref, v_ref, bias_ref, *rest):
    n_cast = (len(rest) - 5) // 2
    cast_in, o_ref, cast_out = rest[:n_cast], rest[n_cast], rest[n_cast + 1:2 * n_cast + 1]
    vt_ref, qz_ref, m_ref, a_ref = rest[2 * n_cast + 1:]
    T = ATTN_TILE
    NS = 2 * N_HEADS_A
    n_q = q_ref.shape[0] // T
    hcols = lambda h: slice(h * HEAD_V, (h + 1) * HEAD_V)
    rows = lambda t: slice(t * T, (t + 1) * T)

    vt = v_ref[...].T
    for h in range(N_HEADS_A):
        vt_ref[h, 0:HEAD_V, :] = vt[hcols(h), :]
        vt_ref[h, HEAD_V:, :] = jnp.ones((ONES_ROWS, vt.shape[1]), BF16)

    lv = lamv_ref[...]
    lam = (jnp.exp(jnp.sum(lv[0:1] * lv[1:2], axis=-1, keepdims=True))
           - jnp.exp(jnp.sum(lv[2:3] * lv[3:4], axis=-1, keepdims=True)) + LAM_INIT)
    lane = lax.broadcasted_iota(jnp.int32, (T, HEAD_V), 1)

    def prepare(i):
        slot = i % 2
        for h in range(N_HEADS_A):
            qs = q_ref[rows(i), hcols(h)]
            zero = jnp.zeros_like(qs)
            qz_ref[slot, 2 * h] = jnp.where(lane < HEAD_QK, qs, zero)
            qz_ref[slot, 2 * h + 1] = jnp.where(lane >= HEAD_QK, qs, zero)
        m_ref[slot] = jnp.full(m_ref.shape[1:], NEG_INF, F32)
        a_ref[slot] = jnp.zeros(a_ref.shape[1:], F32)

    def finalize(i):
        slot = i % 2
        for h in range(N_HEADS_A):
            a1, a2 = a_ref[slot, 2 * h], a_ref[slot, 2 * h + 1]
            o = (a1[:HEAD_V] / a1[HEAD_V:HEAD_V + 1]
                 - lam * (a2[:HEAD_V] / a2[HEAD_V:HEAD_V + 1]))
            ms = jnp.mean(o * o, axis=0, keepdims=True)
            y = (o * lax.rsqrt(ms + EPS)).T
            o_ref[rows(i), hcols(h)] = ((y * subg_ref[...]) * (1.0 - LAM_INIT)).astype(BF16)

    units = [(i, j, n) for i in range(n_q) for j in range(i + 1) for n in range(NS)]
    scores, probs, alphas = {}, {}, {}

    def score(u):
        i, j, n = units[u]
        kc = k_ref[rows(j), hcols(n // 2)]
        scores[u] = lax.dot_general(kc, qz_ref[i % 2, n], (((1,), (1,)), ((), ())),
                                    preferred_element_type=F32)

    def softmax(u):
        i, j, n = units[u]
        s = scores.pop(u)
        if j >= i - 1:
            s = s + bias_ref[n // 2, j - (i - 1)]
        m_old = m_ref[i % 2, n]
        m_new = jnp.maximum(m_old, jnp.max(s, axis=0, keepdims=True))
        probs[u] = jnp.exp2(s - m_new).astype(BF16)
        alphas[u] = jnp.exp2(m_old - m_new)
        m_ref[i % 2, n] = m_new

    def value(u):
        i, j, n = units[u]
        vc = vt_ref[n // 2, :, rows(j)]
        a_ref[i % 2, n] = alphas.pop(u) * a_ref[i % 2, n] + jnp.dot(
            vc, probs.pop(u), preferred_element_type=F32)

    for step in range(len(units) + 2 * ATTN_LAG):
        if step < len(units):
            if units[step][1:] == (0, 0):
                prepare(units[step][0])
            score(step)
        if 0 <= step - ATTN_LAG < len(units):
            softmax(step - ATTN_LAG)
        done = step - 2 * ATTN_LAG
        if 0 <= done < len(units):
            value(done)
            i, j, n = units[done]
            if (j, n) == (i, NS - 1):
                finalize(i)

    for w_ref, wb_ref in zip(cast_in, cast_out):
        wb_ref[...] = w_ref[...].astype(BF16)


def _attention(qkv, bias_tiles, lamv, subg, weights):
    bsz, s, _ = qkv.shape
    T = ATTN_TILE
    W = ATTN_WIDTH
    NS = 2 * N_HEADS_A
    steps = bsz

    def slab_spec(w, squeeze):
        rows, cols = w.shape[-2:]
        assert rows % (steps * CAST_ROWS) == 0
        block = (rows // steps, cols)
        if squeeze:
            return pl.BlockSpec((None,) + block, lambda b: (0, b, 0))
        return pl.BlockSpec(block, lambda b: (b, 0))

    seq = lambda col: pl.BlockSpec((None, s, W), lambda b: (b, 0, col))
    out = pl.pallas_call(
        _attn_kernel,
        grid=(bsz,),
        in_specs=[_resident((4, HEAD_QK)),
                  _resident((1, HEAD_V)),
                  seq(0), seq(1), seq(2),
                  _resident((N_HEADS_A, 2, T, T))] + [slab_spec(w, True) for w in weights],
        out_specs=[seq(0)] + [slab_spec(w, False) for w in weights],
        out_shape=[jax.ShapeDtypeStruct((bsz, s, W), BF16)]
        + [jax.ShapeDtypeStruct(w.shape[-2:], BF16) for w in weights],
        scratch_shapes=[pltpu.VMEM((N_HEADS_A, HEAD_V + ONES_ROWS, s), BF16),
                        pltpu.VMEM((2, NS, T, HEAD_V), BF16),
                        pltpu.VMEM((2, NS, 1, T), F32),
                        pltpu.VMEM((2, NS, HEAD_V + ONES_ROWS, T), F32)],
        compiler_params=_params(("arbitrary",)),
        name="diff_attn",
    )(lamv, subg.reshape(1, HEAD_V), qkv, qkv, qkv, bias_tiles, *weights)
    return out[0], out[1:]


def _mixffn_kernel(x_ref, ada_ref, a_ref, u_ref, gate_ref, woa_ref, woc_ref, boc_ref,
                   wout_ref, g2_ref, w1_ref, w2_ref, gf_ref, o_ref,
                   x1_ref, h2_ref, act_ref):
    D = D_MODEL
    a = jnp.dot(a_ref[...], woa_ref[...], preferred_element_type=F32)
    cv = jnp.dot(u_ref[...], woc_ref[...], preferred_element_type=F32) + boc_ref[...]
    gate = gate_ref[...]
    y = (_sigmoid(gate[:, :D].astype(F32)) * a
         + _sigmoid(gate[:, D:].astype(F32)) * cv)
    z = jnp.dot(y.astype(BF16), wout_ref[...], preferred_element_type=F32)
    x1 = x_ref[...] + ada_ref[2:3, :] * z
    x1_ref[...] = x1
    ms = jnp.mean(x1 * x1, axis=-1, keepdims=True)
    yn = x1 * lax.rsqrt(ms + EPS)
    h2_ref[...] = ((yn * g2_ref[...]) * (1.0 + ada_ref[4:5, :]) + ada_ref[3:4, :]).astype(BF16)

    h = h2_ref[...]
    for c in range(0, D_FF, FF_CHUNK):
        fg = jnp.dot(h, w1_ref[:, c:c + FF_CHUNK], preferred_element_type=F32)
        fu = jnp.dot(h, w1_ref[:, D_FF + c:D_FF + c + FF_CHUNK],
                     preferred_element_type=F32)
        act_ref[:, c:c + FF_CHUNK] = (_silu(fg) * fu).astype(BF16)
    z2 = jnp.dot(act_ref[...], w2_ref[...], preferred_element_type=F32)
    x2 = x1_ref[...] + ada_ref[5:6, :] * z2
    ms2 = jnp.mean(x2 * x2, axis=-1, keepdims=True)
    o_ref[...] = (x2 * lax.rsqrt(ms2 + EPS)) * gf_ref[...]


def _mix_ffn(x, ada3, a_n, u, gates, woa, woc, boc, wout, g2, w1, w2, gf):
    bsz, s, d = x.shape
    tm = ROW_TILE
    row = lambda n: pl.BlockSpec((None, tm, n), lambda b, i: (b, i, 0))
    return pl.pallas_call(
        _mixffn_kernel,
        grid=(bsz, s // tm),
        in_specs=[row(d),
                  pl.BlockSpec((None, 6, d), lambda b, i: (b, 0, 0)),
                  row(ATTN_WIDTH), row(CONV_WIDTH), row(GATE_COLS),
                  _resident((ATTN_WIDTH, d)), _resident((CONV_WIDTH, d)),
                  _resident((1, d)), _resident((d, d)), _resident((1, d)),
                  _resident((d, 2 * D_FF)), _resident((D_FF, d)), _resident((1, d))],
        out_specs=row(d),
        out_shape=jax.ShapeDtypeStruct((bsz, s, d), F32),
        scratch_shapes=[pltpu.VMEM((tm, d), F32), pltpu.VMEM((tm, d), BF16),
                        pltpu.VMEM((tm, D_FF), BF16)],
        compiler_params=_params(("parallel", "arbitrary")),
        name="mix_ffn",
    )(x, ada3, a_n, u, gates, woa, woc, boc.reshape(1, d), wout, g2.reshape(1, d),
      w1, w2, gf.reshape(1, d))


def kernel(x, c, w_ada, b_ada, norm1_g, norm2_g, final_g, w_in, lambda_q1, lambda_k1,
           lambda_q2, lambda_k2, rel_bias, attn_sub_g, w_o_attn, conv_w, conv_b,
           conv_ln_g, conv_ln_b, w_o_conv, b_o_conv, w_out, w_ffn_in, w_ffn_out):
    bsz, s, d = x.shape
    assert w_ada.shape[0] == 1, "single-layer block"
    assert (d, s % ROW_TILE, s % ATTN_TILE) == (D_MODEL, 0, 0)
    assert PROJ_CHUNK == CONV_WIDTH and ATTN_WIDTH % PROJ_CHUNK == 0
    l = 0
    ada, w_in_bf = _ada(c, w_ada, b_ada, w_in)
    ada3 = ada.reshape(bsz, 6, d)
    bias_tiles = _bias_tiles(rel_bias)
    qkv, u, gates = _in_proj(x, ada3, norm1_g[l], w_in_bf, conv_w[l],
                             conv_b[l], conv_ln_g[l], conv_ln_b[l])
    lamv = jnp.stack([lambda_q1[l], lambda_k1[l], lambda_q2[l], lambda_k2[l]])
    a_n, (woa, woc, wout, w1, w2) = _attention(
        qkv, bias_tiles, lamv, attn_sub_g[l],
        [w_o_attn, w_o_conv, w_out, w_ffn_in, w_ffn_out])
    return _mix_ffn(x, ada3, a_n, u, gates, woa, woc, b_o_conv[l], wout, norm2_g[l],
                    w1, w2, final_g)
```

```python
import math

import jax
import jax.numpy as jnp
from jax import lax
from jax.experimental import pallas as pl
from jax.experimental.pallas import tpu as pltpu

D_MODEL = 1024
N_HEADS_A = 4
HEAD_QK = 64
HEAD_V = 2 * HEAD_QK
ATTN_WIDTH = N_HEADS_A * HEAD_V
CONV_WIDTH = 512
CONV_KERNEL = 31
QKV_COLS = 3 * ATTN_WIDTH
GLU_COLS = 2 * CONV_WIDTH
GATE_COLS = 2 * D_MODEL
IN_COLS = QKV_COLS + GLU_COLS + GATE_COLS
D_FF = 2816
N_BUCKETS = 32
MAX_EXACT = 16
MAX_DISTANCE = 128
EPS = 1e-6
NEG_INF = -1e30
LAM_INIT = 0.8 - 0.6 * math.exp(-0.3 * 0)

V7X_VMEM_BYTES = 64 * 1024 * 1024
VMEM_LIMIT = V7X_VMEM_BYTES - 12 * 1024 * 1024

ADA_STEPS = 8
ROW_TILE = 512
ATTN_TILE = 256
ATTN_LAG = 2
ONES_ROWS = 16
CAST_ROWS = 16
LOG2E = math.log2(math.e)
Q_SCALE = HEAD_QK ** -0.5 * LOG2E
CONV_HALO = 32
CONV_CHUNK = 32
PROJ_CHUNK = 512
FF_CHUNK = 256

BF16 = jnp.bfloat16
F32 = jnp.float32


def _sigmoid(x):
    return 1.0 / (1.0 + jnp.exp(-x))


def _silu(x):
    return x * _sigmoid(x)


def _params(sem, vmem=VMEM_LIMIT, flags=None):
    return pltpu.CompilerParams(dimension_semantics=sem, vmem_limit_bytes=vmem, flags=flags)


def _resident(shape):
    nd = len(shape)
    return pl.BlockSpec(shape, lambda *_: (0,) * nd, pipeline_mode=pl.Buffered(1))


def _ada_kernel(c_ref, w_ref, b_ref, win_ref, o_ref, win_bf_ref):
    ca = _silu(c_ref[...])
    o_ref[...] = jnp.dot(ca, w_ref[...], preferred_element_type=F32,
                         precision=lax.Precision.HIGHEST) + b_ref[...]
    win_bf_ref[...] = win_ref[...].astype(BF16)


def _ada(c, w, b, w_in):
    bsz, d = c.shape
    n = w.shape[-1]
    steps = ADA_STEPS
    tn = n // steps
    _, rows, cols = w_in.shape
    tr = rows // steps
    return pl.pallas_call(
        _ada_kernel,
        grid=(steps,),
        in_specs=[pl.BlockSpec((bsz, d), lambda j: (0, 0)),
                  pl.BlockSpec((None, d, tn), lambda j: (0, 0, j)),
                  pl.BlockSpec((1, tn), lambda j: (0, j)),
                  pl.BlockSpec((None, tr, cols), lambda j: (0, j, 0))],
        out_specs=[pl.BlockSpec((bsz, tn), lambda j: (0, j)),
                   pl.BlockSpec((tr, cols), lambda j: (j, 0))],
        out_shape=[jax.ShapeDtypeStruct((bsz, n), F32),
                   jax.ShapeDtypeStruct((rows, cols), BF16)],
        compiler_params=_params(("arbitrary",)),
        name="ada",
    )(c, w, b, w_in)


def _bias_kernel(rel_ref, o_ref):
    h = pl.program_id(0)
    t = pl.program_id(1)
    T = ATTN_TILE
    key = lax.broadcasted_iota(jnp.int32, (T, T), 0)
    qry = lax.broadcasted_iota(jnp.int32, (T, T), 1)
    dist = (1 - t) * T + qry - key
    n = jnp.maximum(dist, 0)
    large = MAX_EXACT + (jnp.log(jnp.maximum(n, 1).astype(F32) / MAX_EXACT)
                         / math.log(MAX_DISTANCE / MAX_EXACT)
                         * (N_BUCKETS - MAX_EXACT)).astype(jnp.int32)
    large = jnp.minimum(large, N_BUCKETS - 1)
    bucket = jnp.where(n < MAX_EXACT, n, large)
    far = rel_ref[N_BUCKETS - 1, h]
    bias = jnp.zeros((T, T), F32)
    for b in range(N_BUCKETS - 1):
        bias = jnp.where(bucket == b, (rel_ref[b, h] - far) * LOG2E, bias)
    o_ref[...] = jnp.where(dist >= 0, bias, NEG_INF)


def _bias_tiles(rel_bias):
    T = ATTN_TILE
    return pl.pallas_call(
        _bias_kernel,
        grid=(N_HEADS_A, 2),
        in_specs=[pl.BlockSpec(memory_space=pltpu.SMEM)],
        out_specs=pl.BlockSpec((None, None, T, T), lambda h, t: (h, t, 0, 0)),
        out_shape=jax.ShapeDtypeStruct((N_HEADS_A, 2, T, T), F32),
        compiler_params=_params(("arbitrary", "arbitrary")),
        name="bias_tiles",
    )(rel_bias)


def _conv_chunk(r0, ubuf, w_ref, cb_ref, lng_ref, lnb_ref, o_ref):
    off = CONV_HALO - (CONV_KERNEL - 1)
    acc = jnp.zeros((CONV_CHUNK // 8, 8, CONV_WIDTH), F32)
    for k in range(CONV_KERNEL):
        a, b = divmod(k + off, 8)
        rows = ubuf[b, r0 + 8 * a:r0 + 8 * a + CONV_CHUNK, :]
        acc = acc + rows.reshape(acc.shape) * w_ref[k][None]
    u = acc.reshape(CONV_CHUNK, CONV_WIDTH) + cb_ref[...]
    mu = jnp.mean(u, axis=-1, keepdims=True)
    d = u - mu
    var = jnp.mean(d * d, axis=-1, keepdims=True)
    y = d * lax.rsqrt(var + EPS) * lng_ref[...] + lnb_ref[...]
    o_ref[r0:r0 + CONV_CHUNK, :] = _silu(y).astype(BF16)


def _inproj_kernel(x_ref, ada_ref, g_ref, w_ref, cw_ref, cb_ref, lng_ref, lnb_ref,
                   qkv_ref, u_ref, gate_ref, ubuf):
    TS, H, C = ROW_TILE, CONV_HALO, CONV_WIDTH
    i = pl.program_id(1)

    @pl.when(i == 0)
    def _():
        ubuf[0, 0:H, :] = jnp.zeros((H, C), F32)

    @pl.when(i > 0)
    def _():
        ubuf[0, 0:H, :] = ubuf[0, TS:TS + H, :]

    x = x_ref[...]
    ms = jnp.mean(x * x, axis=-1, keepdims=True)
    y = x * lax.rsqrt(ms + EPS)
    h = (y * g_ref[...]) * (1.0 + ada_ref[1:2, :]) + ada_ref[0:1, :]
    hb = h.astype(BF16)

    def proj(base):
        return jnp.dot(hb, w_ref[:, base:base + PROJ_CHUNK], preferred_element_type=F32)

    ubuf[0, H:H + TS, :] = proj(QKV_COLS) * _sigmoid(proj(QKV_COLS + C))
    n = TS + H - 8
    for b in range(1, 8):
        ubuf[b, 0:n, :] = ubuf[0, b:b + n, :]
    jobs = ([(qkv_ref, c, c) for c in range(0, QKV_COLS, PROJ_CHUNK)]
            + [(gate_ref, c, QKV_COLS + GLU_COLS + c) for c in range(0, GATE_COLS, PROJ_CHUNK)])
    rows = list(range(0, TS, CONV_CHUNK))
    for n, (ref, c, base) in enumerate(jobs):
        r = proj(base)
        if base < ATTN_WIDTH:
            r = r * Q_SCALE
        ref[:, c:c + PROJ_CHUNK] = r.astype(BF16)
        for r0 in rows[n * len(rows) // len(jobs):(n + 1) * len(rows) // len(jobs)]:
            _conv_chunk(r0, ubuf, cw_ref, cb_ref, lng_ref, lnb_ref, u_ref)


def _in_proj(x, ada3, g, w, cw, cb, lng, lnb):
    bsz, s, d = x.shape
    tm = ROW_TILE
    C = CONV_WIDTH
    row = lambda n: pl.BlockSpec((None, tm, n), lambda b, i: (b, i, 0))
    vec = lambda a: a.reshape(1, C)
    return pl.pallas_call(
        _inproj_kernel,
        grid=(bsz, s // tm),
        in_specs=[row(d),
                  pl.BlockSpec((None, 6, d), lambda b, i: (b, 0, 0)),
                  _resident((1, d)),
                  _resident((d, IN_COLS)),
                  _resident((CONV_KERNEL, 8, C)),
                  _resident((1, C)), _resident((1, C)), _resident((1, C))],
        out_specs=[row(QKV_COLS), row(C), row(GATE_COLS)],
        out_shape=[jax.ShapeDtypeStruct((bsz, s, QKV_COLS), BF16),
                   jax.ShapeDtypeStruct((bsz, s, C), BF16),
                   jax.ShapeDtypeStruct((bsz, s, GATE_COLS), BF16)],
        scratch_shapes=[pltpu.VMEM((8, tm + CONV_HALO, C), F32)],
        compiler_params=_params(("arbitrary", "arbitrary")),
        name="in_proj",
    )(x, ada3, g.reshape(1, d), w, jnp.broadcast_to(cw[:, None, :], (CONV_KERNEL, 8, C)),
      vec(cb), vec(lng), vec(lnb))


def _attn_kernel(lamv_ref, subg_ref, q_ref, k_ref, v_ref, bias_ref, *rest):
    n_cast = (len(rest) - 5) // 2
    cast_in, o_ref, cast_out = rest[:n_cast], rest[n_cast], rest[n_cast + 1:2 * n_cast + 1]
    vt_ref, qz_ref, m_ref, a_ref = rest[2 * n_cast + 1:]
    T = ATTN_TILE
    NS = 2 * N_HEADS_A
    n_q = q_ref.shape[0] // T
    hcols = lambda h: slice(h * HEAD_V, (h + 1) * HEAD_V)
    rows = lambda t: slice(t * T, (t + 1) * T)

    vt = v_ref[...].T
    for h in range(N_HEADS_A):
        vt_ref[h, 0:HEAD_V, :] = vt[hcols(h), :]
        vt_ref[h, HEAD_V:, :] = jnp.ones((ONES_ROWS, vt.shape[1]), BF16)

    lv = lamv_ref[...]
    lam = (jnp.exp(jnp.sum(lv[0:1] * lv[1:2], axis=-1, keepdims=True))
           - jnp.exp(jnp.sum(lv[2:3] * lv[3:4], axis=-1, keepdims=True)) + LAM_INIT)
    lane = lax.broadcasted_iota(jnp.int32, (T, HEAD_V), 1)

    def prepare(i):
        slot = i % 2
        for h in range(N_HEADS_A):
            qs = q_ref[rows(i), hcols(h)]
            zero = jnp.zeros_like(qs)
            qz_ref[slot, 2 * h] = jnp.where(lane < HEAD_QK, qs, zero)
            qz_ref[slot, 2 * h + 1] = jnp.where(lane >= HEAD_QK, qs, zero)
        m_ref[slot] = jnp.full(m_ref.shape[1:], NEG_INF, F32)
        a_ref[slot] = jnp.zeros(a_ref.shape[1:], F32)

    def finalize(i):
        slot = i % 2
        for h in range(N_HEADS_A):
            a1, a2 = a_ref[slot, 2 * h], a_ref[slot, 2 * h + 1]
            o = (a1[:HEAD_V] / a1[HEAD_V:HEAD_V + 1]
                 - lam * (a2[:HEAD_V] / a2[HEAD_V:HEAD_V + 1]))
            ms = jnp.mean(o * o, axis=0, keepdims=True)
            y = (o * lax.rsqrt(ms + EPS)).T
            o_ref[rows(i), hcols(h)] = ((y * subg_ref[...]) * (1.0 - LAM_INIT)).astype(BF16)

    units = [(i, j, n) for i in range(n_q) for j in range(i + 1) for n in range(NS)]
    scores, probs, alphas = {}, {}, {}

    def score(u):
        i, j, n = units[u]
        kc = k_ref[rows(j), hcols(n // 2)]
        scores[u] = lax.dot_general(kc, qz_ref[i % 2, n], (((1,), (1,)), ((), ())),
                                    preferred_element_type=F32)

    def softmax(u):
        i, j, n = units[u]
        s = scores.pop(u)
        if j >= i - 1:
            s = s + bias_ref[n // 2, j - (i - 1)]
        m_old = m_ref[i % 2, n]
        m_new = jnp.maximum(m_old, jnp.max(s, axis=0, keepdims=True))
        probs[u] = jnp.exp2(s - m_new).astype(BF16)
        alphas[u] = jnp.exp2(m_old - m_new)
        m_ref[i % 2, n] = m_new

    def value(u):
        i, j, n = units[u]
        vc = vt_ref[n // 2, :, rows(j)]
        a_ref[i % 2, n] = alphas.pop(u) * a_ref[i % 2, n] + jnp.dot(
            vc, probs.pop(u), preferred_element_type=F32)

    for step in range(len(units) + 2 * ATTN_LAG):
        if step < len(units):
            if units[step][1:] == (0, 0):
                prepare(units[step][0])
            score(step)
        if 0 <= step - ATTN_LAG < len(units):
            softmax(step - ATTN_LAG)
        done = step - 2 * ATTN_LAG
        if 0 <= done < len(units):
            value(done)
            i, j, n = units[done]
            if (j, n) == (i, NS - 1):
                finalize(i)

    for w_ref, wb_ref in zip(cast_in, cast_out):
        wb_ref[...] = w_ref[...].astype(BF16)


def _attention(qkv, bias_tiles, lamv, subg, weights):
    bsz, s, _ = qkv.shape
    T = ATTN_TILE
    W = ATTN_WIDTH
    NS = 2 * N_HEADS_A
    steps = bsz

    def slab_spec(w, squeeze):
        rows, cols = w.shape[-2:]
        assert rows % (steps * CAST_ROWS) == 0
        block = (rows // steps, cols)
        if squeeze:
            return pl.BlockSpec((None,) + block, lambda b: (0, b, 0))
        return pl.BlockSpec(block, lambda b: (b, 0))

    seq = lambda col: pl.BlockSpec((None, s, W), lambda b: (b, 0, col))
    out = pl.pallas_call(
        _attn_kernel,
        grid=(bsz,),
        in_specs=[_resident((4, HEAD_QK)),
                  _resident((1, HEAD_V)),
                  seq(0), seq(1), seq(2),
                  _resident((N_HEADS_A, 2, T, T))] + [slab_spec(w, True) for w in weights],
        out_specs=[seq(0)] + [slab_spec(w, False) for w in weights],
        out_shape=[jax.ShapeDtypeStruct((bsz, s, W), BF16)]
        + [jax.ShapeDtypeStruct(w.shape[-2:], BF16) for w in weights],
        scratch_shapes=[pltpu.VMEM((N_HEADS_A, HEAD_V + ONES_ROWS, s), BF16),
                        pltpu.VMEM((2, NS, T, HEAD_V), BF16),
                        pltpu.VMEM((2, NS, 1, T), F32),
                        pltpu.VMEM((2, NS, HEAD_V + ONES_ROWS, T), F32)],
        compiler_params=_params(("arbitrary",)),
        name="diff_attn",
    )(lamv, subg.reshape(1, HEAD_V), qkv, qkv, qkv, bias_tiles, *weights)
    return out[0], out[1:]


def _mixffn_kernel(x_ref, ada_ref, a_ref, u_ref, gate_ref, woa_ref, woc_ref, boc_ref,
                   wout_ref, g2_ref, w1_ref, w2_ref, gf_ref, o_ref,
                   x1_ref, h2_ref, act_ref):
    D = D_MODEL
    a = jnp.dot(a_ref[...], woa_ref[...], preferred_element_type=F32)
    cv = jnp.dot(u_ref[...], woc_ref[...], preferred_element_type=F32) + boc_ref[...]
    gate = gate_ref[...]
    y = (_sigmoid(gate[:, :D].astype(F32)) * a
         + _sigmoid(gate[:, D:].astype(F32)) * cv)
    z = jnp.dot(y.astype(BF16), wout_ref[...], preferred_element_type=F32)
    x1 = x_ref[...] + ada_ref[2:3, :] * z
    x1_ref[...] = x1
    ms = jnp.mean(x1 * x1, axis=-1, keepdims=True)
    yn = x1 * lax.rsqrt(ms + EPS)
    h2_ref[...] = ((yn * g2_ref[...]) * (1.0 + ada_ref[4:5, :]) + ada_ref[3:4, :]).astype(BF16)

    h = h2_ref[...]
    for c in range(0, D_FF, FF_CHUNK):
        fg = jnp.dot(h, w1_ref[:, c:c + FF_CHUNK], preferred_element_type=F32)
        fu = jnp.dot(h, w1_ref[:, D_FF + c:D_FF + c + FF_CHUNK],
                     preferred_element_type=F32)
        act_ref[:, c:c + FF_CHUNK] = (_silu(fg) * fu).astype(BF16)
    z2 = jnp.dot(act_ref[...], w2_ref[...], preferred_element_type=F32)
    x2 = x1_ref[...] + ada_ref[5:6, :] * z2
    ms2 = jnp.mean(x2 * x2, axis=-1, keepdims=True)
    o_ref[...] = (x2 * lax.rsqrt(ms2 + EPS)) * gf_ref[...]


def _mix_ffn(x, ada3, a_n, u, gates, woa, woc, boc, wout, g2, w1, w2, gf):
    bsz, s, d = x.shape
    tm = ROW_TILE
    row = lambda n: pl.BlockSpec((None, tm, n), lambda b, i: (b, i, 0))
    return pl.pallas_call(
        _mixffn_kernel,
        grid=(bsz, s // tm),
        in_specs=[row(d),
                  pl.BlockSpec((None, 6, d), lambda b, i: (b, 0, 0)),
                  row(ATTN_WIDTH), row(CONV_WIDTH), row(GATE_COLS),
                  _resident((ATTN_WIDTH, d)), _resident((CONV_WIDTH, d)),
                  _resident((1, d)), _resident((d, d)), _resident((1, d)),
                  _resident((d, 2 * D_FF)), _resident((D_FF, d)), _resident((1, d))],
        out_specs=row(d),
        out_shape=jax.ShapeDtypeStruct((bsz, s, d), F32),
        scratch_shapes=[pltpu.VMEM((tm, d), F32), pltpu.VMEM((tm, d), BF16),
                        pltpu.VMEM((tm, D_FF), BF16)],
        compiler_params=_params(("parallel", "arbitrary")),
        name="mix_ffn",
    )(x, ada3, a_n, u, gates, woa, woc, boc.reshape(1, d), wout, g2.reshape(1, d),
      w1, w2, gf.reshape(1, d))


def kernel(x, c, w_ada, b_ada, norm1_g, norm2_g, final_g, w_in, lambda_q1, lambda_k1,
           lambda_q2, lambda_k2, rel_bias, attn_sub_g, w_o_attn, conv_w, conv_b,
           conv_ln_g, conv_ln_b, w_o_conv, b_o_conv, w_out, w_ffn_in, w_ffn_out):
    bsz, s, d = x.shape
    assert w_ada.shape[0] == 1, "single-layer block"
    assert (d, s % ROW_TILE, s % ATTN_TILE) == (D_MODEL, 0, 0)
    assert PROJ_CHUNK == CONV_WIDTH and ATTN_WIDTH % PROJ_CHUNK == 0
    l = 0
    ada, w_in_bf = _ada(c, w_ada, b_ada, w_in)
    ada3 = ada.reshape(bsz, 6, d)
    bias_tiles = _bias_tiles(rel_bias)
    qkv, u, gates = _in_proj(x, ada3, norm1_g[l], w_in_bf, conv_w[l],
                             conv_b[l], conv_ln_g[l], conv_ln_b[l])
    lamv = jnp.stack([lambda_q1[l], lambda_k1[l], lambda_q2[l], lambda_k2[l]])
    a_n, (woa, woc, wout, w1, w2) = _attention(
        qkv, bias_tiles, lamv, attn_sub_g[l],
        [w_o_attn, w_o_conv, w_out, w_ffn_in, w_ffn_out])
    return _mix_ffn(x, ada3, a_n, u, gates, woa, woc, b_o_conv[l], wout, norm2_g[l],
                    w1, w2, final_g)
```

```python
import math

import jax
import jax.numpy as jnp
from jax import lax
from jax.experimental import pallas as pl
from jax.experimental.pallas import tpu as pltpu

D_MODEL = 1024
N_HEADS_A = 4
HEAD_QK = 64
HEAD_V = 2 * HEAD_QK
ATTN_WIDTH = N_HEADS_A * HEAD_V
CONV_WIDTH = 512
CONV_KERNEL = 31
QKV_COLS = 3 * ATTN_WIDTH
GLU_COLS = 2 * CONV_WIDTH
GATE_COLS = 2 * D_MODEL
IN_COLS = QKV_COLS + GLU_COLS + GATE_COLS
D_FF = 2816
N_BUCKETS = 32
MAX_EXACT = 16
MAX_DISTANCE = 128
EPS = 1e-6
NEG_INF = -1e30
LAM_INIT = 0.8 - 0.6 * math.exp(-0.3 * 0)

V7X_VMEM_BYTES = 64 * 1024 * 1024
VMEM_LIMIT = V7X_VMEM_BYTES - 12 * 1024 * 1024

ADA_STEPS = 8
ROW_TILE = 512
ATTN_TILE = 256
ATTN_LAG = 4
ONES_ROWS = 16
CAST_ROWS = 16
LOG2E = math.log2(math.e)
Q_SCALE = HEAD_QK ** -0.5 * LOG2E
CONV_HALO = 32
CONV_CHUNK = 32
PROJ_CHUNK = 512
FF_CHUNK = 256

BF16 = jnp.bfloat16
F32 = jnp.float32


def _sigmoid(x):
    return 1.0 / (1.0 + jnp.exp(-x))


def _silu(x):
    return x * _sigmoid(x)


def _params(sem, vmem=VMEM_LIMIT, flags=None):
    return pltpu.CompilerParams(dimension_semantics=sem, vmem_limit_bytes=vmem, flags=flags)


def _resident(shape):
    nd = len(shape)
    return pl.BlockSpec(shape, lambda *_: (0,) * nd, pipeline_mode=pl.Buffered(1))


def _ada_kernel(c_ref, w_ref, b_ref, win_ref, o_ref, win_bf_ref):
    ca = _silu(c_ref[...])
    o_ref[...] = jnp.dot(ca, w_ref[...], preferred_element_type=F32,
                         precision=lax.Precision.HIGHEST) + b_ref[...]
    win_bf_ref[...] = win_ref[...].astype(BF16)


def _ada(c, w, b, w_in):
    bsz, d = c.shape
    n = w.shape[-1]
    steps = ADA_STEPS
    tn = n // steps
    _, rows, cols = w_in.shape
    tr = rows // steps
    return pl.pallas_call(
        _ada_kernel,
        grid=(steps,),
        in_specs=[pl.BlockSpec((bsz, d), lambda j: (0, 0)),
                  pl.BlockSpec((None, d, tn), lambda j: (0, 0, j)),
                  pl.BlockSpec((1, tn), lambda j: (0, j)),
                  pl.BlockSpec((None, tr, cols), lambda j: (0, j, 0))],
        out_specs=[pl.BlockSpec((bsz, tn), lambda j: (0, j)),
                   pl.BlockSpec((tr, cols), lambda j: (j, 0))],
        out_shape=[jax.ShapeDtypeStruct((bsz, n), F32),
                   jax.ShapeDtypeStruct((rows, cols), BF16)],
        compiler_params=_params(("arbitrary",)),
        name="ada",
    )(c, w, b, w_in)


def _bias_kernel(rel_ref, o_ref):
    h = pl.program_id(0)
    t = pl.program_id(1)
    T = ATTN_TILE
    key = lax.broadcasted_iota(jnp.int32, (T, T), 0)
    qry = lax.broadcasted_iota(jnp.int32, (T, T), 1)
    dist = (1 - t) * T + qry - key
    n = jnp.maximum(dist, 0)
    large = MAX_EXACT + (jnp.log(jnp.maximum(n, 1).astype(F32) / MAX_EXACT)
                         / math.log(MAX_DISTANCE / MAX_EXACT)
                         * (N_BUCKETS - MAX_EXACT)).astype(jnp.int32)
    large = jnp.minimum(large, N_BUCKETS - 1)
    bucket = jnp.where(n < MAX_EXACT, n, large)
    far = rel_ref[N_BUCKETS - 1, h]
    bias = jnp.zeros((T, T), F32)
    for b in range(N_BUCKETS - 1):
        bias = jnp.where(bucket == b, (rel_ref[b, h] - far) * LOG2E, bias)
    o_ref[...] = jnp.where(dist >= 0, bias, NEG_INF)


def _bias_tiles(rel_bias):
    T = ATTN_TILE
    return pl.pallas_call(
        _bias_kernel,
        grid=(N_HEADS_A, 2),
        in_specs=[pl.BlockSpec(memory_space=pltpu.SMEM)],
        out_specs=pl.BlockSpec((None, None, T, T), lambda h, t: (h, t, 0, 0)),
        out_shape=jax.ShapeDtypeStruct((N_HEADS_A, 2, T, T), F32),
        compiler_params=_params(("arbitrary", "arbitrary")),
        name="bias_tiles",
    )(rel_bias)


def _conv_chunk(r0, ubuf, w_ref, cb_ref, lng_ref, lnb_ref, o_ref):
    off = CONV_HALO - (CONV_KERNEL - 1)
    acc = jnp.zeros((CONV_CHUNK // 8, 8, CONV_WIDTH), F32)
    for k in range(CONV_KERNEL):
        a, b = divmod(k + off, 8)
        rows = ubuf[b, r0 + 8 * a:r0 + 8 * a + CONV_CHUNK, :]
        acc = acc + rows.reshape(acc.shape) * w_ref[k][None]
    u = acc.reshape(CONV_CHUNK, CONV_WIDTH) + cb_ref[...]
    mu = jnp.mean(u, axis=-1, keepdims=True)
    d = u - mu
    var = jnp.mean(d * d, axis=-1, keepdims=True)
    y = d * lax.rsqrt(var + EPS) * lng_ref[...] + lnb_ref[...]
    o_ref[r0:r0 + CONV_CHUNK, :] = _silu(y).astype(BF16)


def _inproj_kernel(x_ref, ada_ref, g_ref, w_ref, cw_ref, cb_ref, lng_ref, lnb_ref,
                   qkv_ref, u_ref, gate_ref, ubuf):
    TS, H, C = ROW_TILE, CONV_HALO, CONV_WIDTH
    i = pl.program_id(1)

    @pl.when(i == 0)
    def _():
        ubuf[0, 0:H, :] = jnp.zeros((H, C), F32)

    @pl.when(i > 0)
    def _():
        ubuf[0, 0:H, :] = ubuf[0, TS:TS + H, :]

    x = x_ref[...]
    ms = jnp.mean(x * x, axis=-1, keepdims=True)
    y = x * lax.rsqrt(ms + EPS)
    h = (y * g_ref[...]) * (1.0 + ada_ref[1:2, :]) + ada_ref[0:1, :]
    hb = h.astype(BF16)

    def proj(base):
        return jnp.dot(hb, w_ref[:, base:base + PROJ_CHUNK], preferred_element_type=F32)

    ubuf[0, H:H + TS, :] = proj(QKV_COLS) * _sigmoid(proj(QKV_COLS + C))
    n = TS + H - 8
    for b in range(1, 8):
        ubuf[b, 0:n, :] = ubuf[0, b:b + n, :]
    jobs = ([(qkv_ref, c, c) for c in range(0, QKV_COLS, PROJ_CHUNK)]
            + [(gate_ref, c, QKV_COLS + GLU_COLS + c) for c in range(0, GATE_COLS, PROJ_CHUNK)])
    rows = list(range(0, TS, CONV_CHUNK))
    for n, (ref, c, base) in enumerate(jobs):
        r = proj(base)
        if base < ATTN_WIDTH:
            r = r * Q_SCALE
        ref[:, c:c + PROJ_CHUNK] = r.astype(BF16)
        for r0 in rows[n * len(rows) // len(jobs):(n + 1) * len(rows) // len(jobs)]:
            _conv_chunk(r0, ubuf, cw_ref, cb_ref, lng_ref, lnb_ref, u_ref)


def _in_proj(x, ada3, g, w, cw, cb, lng, lnb):
    bsz, s, d = x.shape
    tm = ROW_TILE
    C = CONV_WIDTH
    row = lambda n: pl.BlockSpec((None, tm, n), lambda b, i: (b, i, 0))
    vec = lambda a: a.reshape(1, C)
    return pl.pallas_call(
        _inproj_kernel,
        grid=(bsz, s // tm),
        in_specs=[row(d),
                  pl.BlockSpec((None, 6, d), lambda b, i: (b, 0, 0)),
                  _resident((1, d)),
                  _resident((d, IN_COLS)),
                  _resident((CONV_KERNEL, 8, C)),
                  _resident((1, C)), _resident((1, C)), _resident((1, C))],
        out_specs=[row(QKV_COLS), row(C), row(GATE_COLS)],
        out_shape=[jax.ShapeDtypeStruct((bsz, s, QKV_COLS), BF16),
                   jax.ShapeDtypeStruct((bsz, s, C), BF16),
                   jax.ShapeDtypeStruct((bsz, s, GATE_COLS), BF16)],
        scratch_shapes=[pltpu.VMEM((8, tm + CONV_HALO, C), F32)],
        compiler_params=_params(("arbitrary", "arbitrary")),
        name="in_proj",
    )(x, ada3, g.reshape(1, d), w, jnp.broadcast_to(cw[:, None, :], (CONV_KERNEL, 8, C)),
      vec(cb), vec(lng), vec(lnb))


def _attn_kernel(lamv_ref, subg_ref, q_ref, k_ref, v_ref, bias_ref, *rest):
    n_cast = (len(rest) - 5) // 2
    cast_in, o_ref, cast_out = rest[:n_cast], rest[n_cast], rest[n_cast + 1:2 * n_cast + 1]
    vt_ref, qz_ref, m_ref, a_ref = rest[2 * n_cast + 1:]
    T = ATTN_TILE
    NS = 2 * N_HEADS_A
    n_q = q_ref.shape[0] // T
    hcols = lambda h: slice(h * HEAD_V, (h + 1) * HEAD_V)
    rows = lambda t: slice(t * T, (t + 1) * T)

    vt = v_ref[...].T
    for h in range(N_HEADS_A):
        vt_ref[h, 0:HEAD_V, :] = vt[hcols(h), :]
        vt_ref[h, HEAD_V:, :] = jnp.ones((ONES_ROWS, vt.shape[1]), BF16)

    lv = lamv_ref[...]
    lam = (jnp.exp(jnp.sum(lv[0:1] * lv[1:2], axis=-1, keepdims=True))
           - jnp.exp(jnp.sum(lv[2:3] * lv[3:4], axis=-1, keepdims=True)) + LAM_INIT)
    lane = lax.broadcasted_iota(jnp.int32, (T, HEAD_V), 1)

    def prepare(i):
        slot = i % 2
        for h in range(N_HEADS_A):
            qs = q_ref[rows(i), hcols(h)]
            zero = jnp.zeros_like(qs)
            qz_ref[slot, 2 * h] = jnp.where(lane < HEAD_QK, qs, zero)
            qz_ref[slot, 2 * h + 1] = jnp.where(lane >= HEAD_QK, qs, zero)
        m_ref[slot] = jnp.full(m_ref.shape[1:], NEG_INF, F32)
        a_ref[slot] = jnp.zeros(a_ref.shape[1:], F32)

    def finalize(i):
        slot = i % 2
        for h in range(N_HEADS_A):
            a1, a2 = a_ref[slot, 2 * h], a_ref[slot, 2 * h + 1]
            o = (a1[:HEAD_V] / a1[HEAD_V:HEAD_V + 1]
                 - lam * (a2[:HEAD_V] / a2[HEAD_V:HEAD_V + 1]))
            ms = jnp.mean(o * o, axis=0, keepdims=True)
            y = (o * lax.rsqrt(ms + EPS)).T
            o_ref[rows(i), hcols(h)] = ((y * subg_ref[...]) * (1.0 - LAM_INIT)).astype(BF16)

    units = [(i, j, n) for i in range(n_q) for j in range(i + 1) for n in range(NS)]
    scores, probs, alphas = {}, {}, {}

    def score(u):
        i, j, n = units[u]
        kc = k_ref[rows(j), hcols(n // 2)]
        scores[u] = lax.dot_general(kc, qz_ref[i % 2, n], (((1,), (1,)), ((), ())),
                                    preferred_element_type=F32)

    def softmax(u):
        i, j, n = units[u]
        s = scores.pop(u)
        if j >= i - 1:
            s = s + bias_ref[n // 2, j - (i - 1)]
        m_old = m_ref[i % 2, n]
        m_new = jnp.maximum(m_old, jnp.max(s, axis=0, keepdims=True))
        probs[u] = jnp.exp2((s - m_new).astype(BF16))
        alphas[u] = jnp.exp2(m_old - m_new)
        m_ref[i % 2, n] = m_new

    def value(u):
        i, j, n = units[u]
        vc = vt_ref[n // 2, :, rows(j)]
        a_ref[i % 2, n] = alphas.pop(u) * a_ref[i % 2, n] + jnp.dot(
            vc, probs.pop(u), preferred_element_type=F32)

    for step in range(len(units) + 2 * ATTN_LAG):
        if step < len(units):
            if units[step][1:] == (0, 0):
                prepare(units[step][0])
            score(step)
        if 0 <= step - ATTN_LAG < len(units):
            softmax(step - ATTN_LAG)
        done = step - 2 * ATTN_LAG
        if 0 <= done < len(units):
            value(done)
            i, j, n = units[done]
            if (j, n) == (i, NS - 1):
                finalize(i)

    for w_ref, wb_ref in zip(cast_in, cast_out):
        wb_ref[...] = w_ref[...].astype(BF16)


def _attention(qkv, bias_tiles, lamv, subg, weights):
    bsz, s, _ = qkv.shape
    T = ATTN_TILE
    W = ATTN_WIDTH
    NS = 2 * N_HEADS_A
    steps = bsz

    def slab_spec(w, squeeze):
        rows, cols = w.shape[-2:]
        assert rows % (steps * CAST_ROWS) == 0
        block = (rows // steps, cols)
        if squeeze:
            return pl.BlockSpec((None,) + block, lambda b: (0, b, 0))
        return pl.BlockSpec(block, lambda b: (b, 0))

    seq = lambda col: pl.BlockSpec((None, s, W), lambda b: (b, 0, col))
    out = pl.pallas_call(
        _attn_kernel,
        grid=(bsz,),
        in_specs=[_resident((4, HEAD_QK)),
                  _resident((1, HEAD_V)),
                  seq(0), seq(1), seq(2),
                  _resident((N_HEADS_A, 2, T, T))] + [slab_spec(w, True) for w in weights],
        out_specs=[seq(0)] + [slab_spec(w, False) for w in weights],
        out_shape=[jax.ShapeDtypeStruct((bsz, s, W), BF16)]
        + [jax.ShapeDtypeStruct(w.shape[-2:], BF16) for w in weights],
        scratch_shapes=[pltpu.VMEM((N_HEADS_A, HEAD_V + ONES_ROWS, s), BF16),
                        pltpu.VMEM((2, NS, T, HEAD_V), BF16),
                        pltpu.VMEM((2, NS, 1, T), F32),
                        pltpu.VMEM((2, NS, HEAD_V + ONES_ROWS, T), F32)],
        compiler_params=_params(("arbitrary",)),
        name="diff_attn",
    )(lamv, subg.reshape(1, HEAD_V), qkv, qkv, qkv, bias_tiles, *weights)
    return out[0], out[1:]


def _mixffn_kernel(x_ref, ada_ref, a_ref, u_ref, gate_ref, woa_ref, woc_ref, boc_ref,
                   wout_ref, g2_ref, w1_ref, w2_ref, gf_ref, o_ref,
                   x1_ref, h2_ref, act_ref):
    D = D_MODEL
    a = jnp.dot(a_ref[...], woa_ref[...], preferred_element_type=F32)
    cv = jnp.dot(u_ref[...], woc_ref[...], preferred_element_type=F32) + boc_ref[...]
    sig = _sigmoid(gate_ref[...])
    y = sig[:, :D].astype(F32) * a + sig[:, D:].astype(F32) * cv
    z = jnp.dot(y.astype(BF16), wout_ref[...], preferred_element_type=F32)
    x1 = x_ref[...] + ada_ref[2:3, :] * z
    x1_ref[...] = x1
    ms = jnp.mean(x1 * x1, axis=-1, keepdims=True)
    yn = x1 * lax.rsqrt(ms + EPS)
    h2_ref[...] = ((yn * g2_ref[...]) * (1.0 + ada_ref[4:5, :]) + ada_ref[3:4, :]).astype(BF16)

    h = h2_ref[...]
    for c in range(0, D_FF, FF_CHUNK):
        fg = jnp.dot(h, w1_ref[:, c:c + FF_CHUNK], preferred_element_type=F32)
        fu = jnp.dot(h, w1_ref[:, D_FF + c:D_FF + c + FF_CHUNK],
                     preferred_element_type=F32)
        act_ref[:, c:c + FF_CHUNK] = (_silu(fg) * fu).astype(BF16)
    z2 = jnp.dot(act_ref[...], w2_ref[...], preferred_element_type=F32)
    x2 = x1_ref[...] + ada_ref[5:6, :] * z2
    ms2 = jnp.mean(x2 * x2, axis=-1, keepdims=True)
    o_ref[...] = (x2 * lax.rsqrt(ms2 + EPS)) * gf_ref[...]


def _mix_ffn(x, ada3, a_n, u, gates, woa, woc, boc, wout, g2, w1, w2, gf):
    bsz, s, d = x.shape
    tm = ROW_TILE
    row = lambda n: pl.BlockSpec((None, tm, n), lambda b, i: (b, i, 0))
    return pl.pallas_call(
        _mixffn_kernel,
        grid=(bsz, s // tm),
        in_specs=[row(d),
                  pl.BlockSpec((None, 6, d), lambda b, i: (b, 0, 0)),
                  row(ATTN_WIDTH), row(CONV_WIDTH), row(GATE_COLS),
                  _resident((ATTN_WIDTH, d)), _resident((CONV_WIDTH, d)),
                  _resident((1, d)), _resident((d, d)), _resident((1, d)),
                  _resident((d, 2 * D_FF)), _resident((D_FF, d)), _resident((1, d))],
        out_specs=row(d),
        out_shape=jax.ShapeDtypeStruct((bsz, s, d), F32),
        scratch_shapes=[pltpu.VMEM((tm, d), F32), pltpu.VMEM((tm, d), BF16),
                        pltpu.VMEM((tm, D_FF), BF16)],
        compiler_params=_params(("parallel", "arbitrary")),
        name="mix_ffn",
    )(x, ada3, a_n, u, gates, woa, woc, boc.reshape(1, d), wout, g2.reshape(1, d),
      w1, w2, gf.reshape(1, d))


def kernel(x, c, w_ada, b_ada, norm1_g, norm2_g, final_g, w_in, lambda_q1, lambda_k1,
           lambda_q2, lambda_k2, rel_bias, attn_sub_g, w_o_attn, conv_w, conv_b,
           conv_ln_g, conv_ln_b, w_o_conv, b_o_conv, w_out, w_ffn_in, w_ffn_out):
    bsz, s, d = x.shape
    assert w_ada.shape[0] == 1, "single-layer block"
    assert (d, s % ROW_TILE, s % ATTN_TILE) == (D_MODEL, 0, 0)
    assert PROJ_CHUNK == CONV_WIDTH and ATTN_WIDTH % PROJ_CHUNK == 0
    l = 0
    ada, w_in_bf = _ada(c, w_ada, b_ada, w_in)
    ada3 = ada.reshape(bsz, 6, d)
    bias_tiles = _bias_tiles(rel_bias)
    qkv, u, gates = _in_proj(x, ada3, norm1_g[l], w_in_bf, conv_w[l],
                             conv_b[l], conv_ln_g[l], conv_ln_b[l])
    lamv = jnp.stack([lambda_q1[l], lambda_k1[l], lambda_q2[l], lambda_k2[l]])
    a_n, (woa, woc, wout, w1, w2) = _attention(
        qkv, bias_tiles, lamv, attn_sub_g[l],
        [w_o_attn, w_o_conv, w_out, w_ffn_in, w_ffn_out])
    return _mix_ffn(x, ada3, a_n, u, gates, woa, woc, b_o_conv[l], wout, norm2_g[l],
                    w1, w2, final_g)
```

```python
import math

import jax
import jax.numpy as jnp
from jax import lax
from jax.experimental import pallas as pl
from jax.experimental.pallas import tpu as pltpu

D_MODEL = 1024
N_HEADS_A = 4
HEAD_QK = 64
HEAD_V = 2 * HEAD_QK
ATTN_WIDTH = N_HEADS_A * HEAD_V
CONV_WIDTH = 512
CONV_KERNEL = 31
QKV_COLS = 3 * ATTN_WIDTH
GLU_COLS = 2 * CONV_WIDTH
GATE_COLS = 2 * D_MODEL
IN_COLS = QKV_COLS + GLU_COLS + GATE_COLS
D_FF = 2816
N_BUCKETS = 32
MAX_EXACT = 16
MAX_DISTANCE = 128
EPS = 1e-6
NEG_INF = -1e30
LAM_INIT = 0.8 - 0.6 * math.exp(-0.3 * 0)

V7X_VMEM_BYTES = 64 * 1024 * 1024
VMEM_LIMIT = V7X_VMEM_BYTES - 12 * 1024 * 1024

ADA_STEPS = 8
ROW_TILE = 512
ATTN_TILE = 256
ATTN_LAG = 4
ONES_ROWS = 16
CAST_ROWS = 16
LOG2E = math.log2(math.e)
Q_SCALE = HEAD_QK ** -0.5 * LOG2E
CONV_HALO = 32
CONV_CHUNK = 8
PROJ_CHUNK = 512
FF_CHUNK = 256

BF16 = jnp.bfloat16
F32 = jnp.float32


def _sigmoid(x):
    return 1.0 / (1.0 + jnp.exp(-x))


def _silu(x):
    return x * _sigmoid(x)


def _params(sem, vmem=VMEM_LIMIT, flags=None):
    return pltpu.CompilerParams(dimension_semantics=sem, vmem_limit_bytes=vmem, flags=flags)


def _resident(shape):
    nd = len(shape)
    return pl.BlockSpec(shape, lambda *_: (0,) * nd, pipeline_mode=pl.Buffered(1))


def _ada_kernel(c_ref, w_ref, b_ref, win_ref, o_ref, win_bf_ref):
    ca = _silu(c_ref[...])
    o_ref[...] = jnp.dot(ca, w_ref[...], preferred_element_type=F32,
                         precision=lax.Precision.HIGHEST) + b_ref[...]
    win_bf_ref[...] = win_ref[...].astype(BF16)


def _ada(c, w, b, w_in):
    bsz, d = c.shape
    n = w.shape[-1]
    steps = ADA_STEPS
    tn = n // steps
    _, rows, cols = w_in.shape
    tr = rows // steps
    return pl.pallas_call(
        _ada_kernel,
        grid=(steps,),
        in_specs=[pl.BlockSpec((bsz, d), lambda j: (0, 0)),
                  pl.BlockSpec((None, d, tn), lambda j: (0, 0, j)),
                  pl.BlockSpec((1, tn), lambda j: (0, j)),
                  pl.BlockSpec((None, tr, cols), lambda j: (0, j, 0))],
        out_specs=[pl.BlockSpec((bsz, tn), lambda j: (0, j)),
                   pl.BlockSpec((tr, cols), lambda j: (j, 0))],
        out_shape=[jax.ShapeDtypeStruct((bsz, n), F32),
                   jax.ShapeDtypeStruct((rows, cols), BF16)],
        compiler_params=_params(("arbitrary",)),
        name="ada",
    )(c, w, b, w_in)


def _bias_kernel(rel_ref, o_ref):
    h = pl.program_id(0)
    t = pl.program_id(1)
    T = ATTN_TILE
    key = lax.broadcasted_iota(jnp.int32, (T, T), 0)
    qry = lax.broadcasted_iota(jnp.int32, (T, T), 1)
    dist = (1 - t) * T + qry - key
    n = jnp.maximum(dist, 0)
    large = MAX_EXACT + (jnp.log(jnp.maximum(n, 1).astype(F32) / MAX_EXACT)
                         / math.log(MAX_DISTANCE / MAX_EXACT)
                         * (N_BUCKETS - MAX_EXACT)).astype(jnp.int32)
    large = jnp.minimum(large, N_BUCKETS - 1)
    bucket = jnp.where(n < MAX_EXACT, n, large)
    far = rel_ref[N_BUCKETS - 1, h]
    bias = jnp.zeros((T, T), F32)
    for b in range(N_BUCKETS - 1):
        bias = jnp.where(bucket == b, (rel_ref[b, h] - far) * LOG2E, bias)
    o_ref[...] = jnp.where(dist >= 0, bias, NEG_INF)


def _bias_tiles(rel_bias):
    T = ATTN_TILE
    return pl.pallas_call(
        _bias_kernel,
        grid=(N_HEADS_A, 2),
        in_specs=[pl.BlockSpec(memory_space=pltpu.SMEM)],
        out_specs=pl.BlockSpec((None, None, T, T), lambda h, t: (h, t, 0, 0)),
        out_shape=jax.ShapeDtypeStruct((N_HEADS_A, 2, T, T), F32),
        compiler_params=_params(("arbitrary", "arbitrary")),
        name="bias_tiles",
    )(rel_bias)


def _conv_chunk(r0, ubuf, w_ref, cb_ref, lng_ref, lnb_ref, o_ref, zero):
    off = CONV_HALO - (CONV_KERNEL - 1)
    acc = jnp.broadcast_to(zero[None], (CONV_CHUNK // 8, 8, CONV_WIDTH))
    for k in range(CONV_KERNEL):
        a, b = divmod(k + off, 8)
        rows = ubuf[b, r0 + 8 * a:r0 + 8 * a + CONV_CHUNK, :]
        acc = acc + rows.reshape(acc.shape) * w_ref[k][None]
    bits = pltpu.bitcast(acc[0], jnp.uint32)
    next_zero = pltpu.bitcast(lax.shift_right_logical(lax.shift_right_logical(
        bits, jnp.uint32(16)), jnp.uint32(16)), F32)
    u = acc.reshape(CONV_CHUNK, CONV_WIDTH) + cb_ref[...]
    mu = jnp.mean(u, axis=-1, keepdims=True)
    d = u - mu
    var = jnp.mean(d * d, axis=-1, keepdims=True)
    y = d * lax.rsqrt(var + EPS) * lng_ref[...] + lnb_ref[...]
    o_ref[r0:r0 + CONV_CHUNK, :] = _silu(y).astype(BF16)
    return next_zero


def _inproj_kernel(x_ref, ada_ref, g_ref, w_ref, cw_ref, cb_ref, lng_ref, lnb_ref,
                   qkv_ref, u_ref, gate_ref, ubuf):
    TS, H, C = ROW_TILE, CONV_HALO, CONV_WIDTH
    i = pl.program_id(1)

    @pl.when(i == 0)
    def _():
        ubuf[0, 0:H, :] = jnp.zeros((H, C), F32)

    @pl.when(i > 0)
    def _():
        ubuf[0, 0:H, :] = ubuf[0, TS:TS + H, :]

    x = x_ref[...]
    ms = jnp.mean(x * x, axis=-1, keepdims=True)
    y = x * lax.rsqrt(ms + EPS)
    h = (y * g_ref[...]) * (1.0 + ada_ref[1:2, :]) + ada_ref[0:1, :]
    hb = h.astype(BF16)

    def proj(base):
        return jnp.dot(hb, w_ref[:, base:base + PROJ_CHUNK], preferred_element_type=F32)

    ubuf[0, H:H + TS, :] = proj(QKV_COLS) * _sigmoid(proj(QKV_COLS + C))
    n = TS + H - 8
    for b in range(1, 8):
        ubuf[b, 0:n, :] = ubuf[0, b:b + n, :]
    jobs = ([(qkv_ref, c, c) for c in range(0, QKV_COLS, PROJ_CHUNK)]
            + [(gate_ref, c, QKV_COLS + GLU_COLS + c) for c in range(0, GATE_COLS, PROJ_CHUNK)])
    rows = list(range(0, TS, CONV_CHUNK))
    zero = jnp.zeros((8, C), F32)
    for n, (ref, c, base) in enumerate(jobs):
        r = proj(base)
        if base < ATTN_WIDTH:
            r = r * Q_SCALE
        ref[:, c:c + PROJ_CHUNK] = r.astype(BF16)
        for r0 in rows[n * len(rows) // len(jobs):(n + 1) * len(rows) // len(jobs)]:
            zero = _conv_chunk(r0, ubuf, cw_ref, cb_ref, lng_ref, lnb_ref, u_ref, zero)


def _in_proj(x, ada3, g, w, cw, cb, lng, lnb):
    bsz, s, d = x.shape
    tm = ROW_TILE
    C = CONV_WIDTH
    row = lambda n: pl.BlockSpec((None, tm, n), lambda b, i: (b, i, 0))
    vec = lambda a: a.reshape(1, C)
    return pl.pallas_call(
        _inproj_kernel,
        grid=(bsz, s // tm),
        in_specs=[row(d),
                  pl.BlockSpec((None, 6, d), lambda b, i: (b, 0, 0)),
                  _resident((1, d)),
                  _resident((d, IN_COLS)),
                  _resident((CONV_KERNEL, 8, C)),
                  _resident((1, C)), _resident((1, C)), _resident((1, C))],
        out_specs=[row(QKV_COLS), row(C), row(GATE_COLS)],
        out_shape=[jax.ShapeDtypeStruct((bsz, s, QKV_COLS), BF16),
                   jax.ShapeDtypeStruct((bsz, s, C), BF16),
                   jax.ShapeDtypeStruct((bsz, s, GATE_COLS), BF16)],
        scratch_shapes=[pltpu.VMEM((8, tm + CONV_HALO, C), F32)],
        compiler_params=_params(("arbitrary", "arbitrary")),
        name="in_proj",
    )(x, ada3, g.reshape(1, d), w, jnp.broadcast_to(cw[:, None, :], (CONV_KERNEL, 8, C)),
      vec(cb), vec(lng), vec(lnb))


def _attn_kernel(lamv_ref, subg_ref, q_ref, k_ref, v_ref, bias_ref, *rest):
    n_cast = (len(rest) - 5) // 2
    cast_in, o_ref, cast_out = rest[:n_cast], rest[n_cast], rest[n_cast + 1:2 * n_cast + 1]
    vt_ref, qz_ref, m_ref, a_ref = rest[2 * n_cast + 1:]
    T = ATTN_TILE
    NS = 2 * N_HEADS_A
    n_q = q_ref.shape[0] // T
    hcols = lambda h: slice(h * HEAD_V, (h + 1) * HEAD_V)
    rows = lambda t: slice(t * T, (t + 1) * T)

    vt = v_ref[...].T
    for h in range(N_HEADS_A):
        vt_ref[h, 0:HEAD_V, :] = vt[hcols(h), :]
        vt_ref[h, HEAD_V:, :] = jnp.ones((ONES_ROWS, vt.shape[1]), BF16)

    lv = lamv_ref[...]
    lam = (jnp.exp(jnp.sum(lv[0:1] * lv[1:2], axis=-1, keepdims=True))
           - jnp.exp(jnp.sum(lv[2:3] * lv[3:4], axis=-1, keepdims=True)) + LAM_INIT)
    lane = lax.broadcasted_iota(jnp.int32, (T, HEAD_V), 1)

    def prepare(i):
        slot = i % 2
        for h in range(N_HEADS_A):
            qs = q_ref[rows(i), hcols(h)]
            zero = jnp.zeros_like(qs)
            qz_ref[slot, 2 * h] = jnp.where(lane < HEAD_QK, qs, zero)
            qz_ref[slot, 2 * h + 1] = jnp.where(lane >= HEAD_QK, qs, zero)
        m_ref[slot] = jnp.full(m_ref.shape[1:], NEG_INF, F32)
        a_ref[slot] = jnp.zeros(a_ref.shape[1:], F32)

    def finalize(i):
        slot = i % 2
        for h in range(N_HEADS_A):
            a1, a2 = a_ref[slot, 2 * h], a_ref[slot, 2 * h + 1]
            o = (a1[:HEAD_V] / a1[HEAD_V:HEAD_V + 1]
                 - lam * (a2[:HEAD_V] / a2[HEAD_V:HEAD_V + 1]))
            ms = jnp.mean(o * o, axis=0, keepdims=True)
            y = (o * lax.rsqrt(ms + EPS)).T
            o_ref[rows(i), hcols(h)] = ((y * subg_ref[...]) * (1.0 - LAM_INIT)).astype(BF16)

    units = [(i, j, n) for i in range(n_q) for j in range(i + 1) for n in range(NS)]
    scores, probs, alphas = {}, {}, {}

    def score(u):
        i, j, n = units[u]
        kc = k_ref[rows(j), hcols(n // 2)]
        scores[u] = lax.dot_general(kc, qz_ref[i % 2, n], (((1,), (1,)), ((), ())),
                                    preferred_element_type=F32)

    def softmax(u):
        i, j, n = units[u]
        s = scores.pop(u)
        if j >= i - 1:
            s = s + bias_ref[n // 2, j - (i - 1)]
        m_old = m_ref[i % 2, n]
        m_new = jnp.maximum(m_old, jnp.max(s, axis=0, keepdims=True))
        probs[u] = jnp.exp2(s - m_new).astype(BF16)
        alphas[u] = jnp.exp2(m_old - m_new)
        m_ref[i % 2, n] = m_new

    def value(u):
        i, j, n = units[u]
        vc = vt_ref[n // 2, :, rows(j)]
        a_ref[i % 2, n] = alphas.pop(u) * a_ref[i % 2, n] + jnp.dot(
            vc, probs.pop(u), preferred_element_type=F32)

    for step in range(len(units) + 2 * ATTN_LAG):
        if step < len(units):
            if units[step][1:] == (0, 0):
                prepare(units[step][0])
            score(step)
        if 0 <= step - ATTN_LAG < len(units):
            softmax(step - ATTN_LAG)
        done = step - 2 * ATTN_LAG
        if 0 <= done < len(units):
            value(done)
            i, j, n = units[done]
            if (j, n) == (i, NS - 1):
                finalize(i)

    for w_ref, wb_ref in zip(cast_in, cast_out):
        wb_ref[...] = w_ref[...].astype(BF16)


def _attention(qkv, bias_tiles, lamv, subg, weights):
    bsz, s, _ = qkv.shape
    T = ATTN_TILE
    W = ATTN_WIDTH
    NS = 2 * N_HEADS_A
    steps = bsz

    def slab_spec(w, squeeze):
        rows, cols = w.shape[-2:]
        assert rows % (steps * CAST_ROWS) == 0
        block = (rows // steps, cols)
        if squeeze:
            return pl.BlockSpec((None,) + block, lambda b: (0, b, 0))
        return pl.BlockSpec(block, lambda b: (b, 0))

    seq = lambda col: pl.BlockSpec((None, s, W), lambda b: (b, 0, col))
    out = pl.pallas_call(
        _attn_kernel,
        grid=(bsz,),
        in_specs=[_resident((4, HEAD_QK)),
                  _resident((1, HEAD_V)),
                  seq(0), seq(1), seq(2),
                  _resident((N_HEADS_A, 2, T, T))] + [slab_spec(w, True) for w in weights],
        out_specs=[seq(0)] + [slab_spec(w, False) for w in weights],
        out_shape=[jax.ShapeDtypeStruct((bsz, s, W), BF16)]
        + [jax.ShapeDtypeStruct(w.shape[-2:], BF16) for w in weights],
        scratch_shapes=[pltpu.VMEM((N_HEADS_A, HEAD_V + ONES_ROWS, s), BF16),
                        pltpu.VMEM((2, NS, T, HEAD_V), BF16),
                        pltpu.VMEM((2, NS, 1, T), F32),
                        pltpu.VMEM((2, NS, HEAD_V + ONES_ROWS, T), F32)],
        compiler_params=_params(("arbitrary",)),
        name="diff_attn",
    )(lamv, subg.reshape(1, HEAD_V), qkv, qkv, qkv, bias_tiles, *weights)
    return out[0], out[1:]


def _mixffn_kernel(x_ref, ada_ref, a_ref, u_ref, gate_ref, woa_ref, woc_ref, boc_ref,
                   wout_ref, g2_ref, w1_ref, w2_ref, gf_ref, o_ref,
                   x1_ref, h2_ref, act_ref):
    D = D_MODEL
    a = jnp.dot(a_ref[...], woa_ref[...], preferred_element_type=F32)
    cv = jnp.dot(u_ref[...], woc_ref[...], preferred_element_type=F32) + boc_ref[...]
    gate = gate_ref[...]
    y = (_sigmoid(gate[:, :D].astype(F32)) * a
         + _sigmoid(gate[:, D:].astype(F32)) * cv)
    z = jnp.dot(y.astype(BF16), wout_ref[...], preferred_element_type=F32)
    x1 = x_ref[...] + ada_ref[2:3, :] * z
    x1_ref[...] = x1
    ms = jnp.mean(x1 * x1, axis=-1, keepdims=True)
    yn = x1 * lax.rsqrt(ms + EPS)
    h2_ref[...] = ((yn * g2_ref[...]) * (1.0 + ada_ref[4:5, :]) + ada_ref[3:4, :]).astype(BF16)

    h = h2_ref[...]
    for c in range(0, D_FF, FF_CHUNK):
        fg = jnp.dot(h, w1_ref[:, c:c + FF_CHUNK], preferred_element_type=F32)
        fu = jnp.dot(h, w1_ref[:, D_FF + c:D_FF + c + FF_CHUNK],
                     preferred_element_type=F32)
        act_ref[:, c:c + FF_CHUNK] = (_silu(fg) * fu).astype(BF16)
    z2 = jnp.dot(act_ref[...], w2_ref[...], preferred_element_type=F32)
    x2 = x1_ref[...] + ada_ref[5:6, :] * z2
    ms2 = jnp.mean(x2 * x2, axis=-1, keepdims=True)
    o_ref[...] = (x2 * lax.rsqrt(ms2 + EPS)) * gf_ref[...]


def _mix_ffn(x, ada3, a_n, u, gates, woa, woc, boc, wout, g2, w1, w2, gf):
    bsz, s, d = x.shape
    tm = ROW_TILE
    row = lambda n: pl.BlockSpec((None, tm, n), lambda b, i: (b, i, 0))
    return pl.pallas_call(
        _mixffn_kernel,
        grid=(bsz, s // tm),
        in_specs=[row(d),
                  pl.BlockSpec((None, 6, d), lambda b, i: (b, 0, 0)),
                  row(ATTN_WIDTH), row(CONV_WIDTH), row(GATE_COLS),
                  _resident((ATTN_WIDTH, d)), _resident((CONV_WIDTH, d)),
                  _resident((1, d)), _resident((d, d)), _resident((1, d)),
                  _resident((d, 2 * D_FF)), _resident((D_FF, d)), _resident((1, d))],
        out_specs=row(d),
        out_shape=jax.ShapeDtypeStruct((bsz, s, d), F32),
        scratch_shapes=[pltpu.VMEM((tm, d), F32), pltpu.VMEM((tm, d), BF16),
                        pltpu.VMEM((tm, D_FF), BF16)],
        compiler_params=_params(("parallel", "arbitrary")),
        name="mix_ffn",
    )(x, ada3, a_n, u, gates, woa, woc, boc.reshape(1, d), wout, g2.reshape(1, d),
      w1, w2, gf.reshape(1, d))


def kernel(x, c, w_ada, b_ada, norm1_g, norm2_g, final_g, w_in, lambda_q1, lambda_k1,
           lambda_q2, lambda_k2, rel_bias, attn_sub_g, w_o_attn, conv_w, conv_b,
           conv_ln_g, conv_ln_b, w_o_conv, b_o_conv, w_out, w_ffn_in, w_ffn_out):
    bsz, s, d = x.shape
    assert w_ada.shape[0] == 1, "single-layer block"
    assert (d, s % ROW_TILE, s % ATTN_TILE) == (D_MODEL, 0, 0)
    assert PROJ_CHUNK == CONV_WIDTH and ATTN_WIDTH % PROJ_CHUNK == 0
    l = 0
    ada, w_in_bf = _ada(c, w_ada, b_ada, w_in)
    ada3 = ada.reshape(bsz, 6, d)
    bias_tiles = _bias_tiles(rel_bias)
    qkv, u, gates = _in_proj(x, ada3, norm1_g[l], w_in_bf, conv_w[l],
                             conv_b[l], conv_ln_g[l], conv_ln_b[l])
    lamv = jnp.stack([lambda_q1[l], lambda_k1[l], lambda_q2[l], lambda_k2[l]])
    a_n, (woa, woc, wout, w1, w2) = _attention(
        qkv, bias_tiles, lamv, attn_sub_g[l],
        [w_o_attn, w_o_conv, w_out, w_ffn_in, w_ffn_out])
    return _mix_ffn(x, ada3, a_n, u, gates, woa, woc, b_o_conv[l], wout, norm2_g[l],
                    w1, w2, final_g)
```

```python
import math

import jax
import jax.numpy as jnp
from jax import lax
from jax.experimental import pallas as pl
from jax.experimental.pallas import tpu as pltpu

D_MODEL = 1024
N_HEADS_A = 4
HEAD_QK = 64
HEAD_V = 2 * HEAD_QK
ATTN_WIDTH = N_HEADS_A * HEAD_V
CONV_WIDTH = 512
CONV_KERNEL = 31
QKV_COLS = 3 * ATTN_WIDTH
GLU_COLS = 2 * CONV_WIDTH
GATE_COLS = 2 * D_MODEL
IN_COLS = QKV_COLS + GLU_COLS + GATE_COLS
D_FF = 2816
N_BUCKETS = 32
MAX_EXACT = 16
MAX_DISTANCE = 128
EPS = 1e-6
NEG_INF = -1e30
LAM_INIT = 0.8 - 0.6 * math.exp(-0.3 * 0)

V7X_VMEM_BYTES = 64 * 1024 * 1024
VMEM_LIMIT = V7X_VMEM_BYTES - 12 * 1024 * 1024

ROW_TILE = 512
ATTN_TILE = 256
ATTN_LAG = 4
ONES_ROWS = 16
CAST_ROWS = 16
LOG2E = math.log2(math.e)
Q_SCALE = HEAD_QK ** -0.5 * LOG2E
CONV_HALO = 32
CONV_CHUNK = 16
PROJ_CHUNK = 512
FF_CHUNK = 256

BF16 = jnp.bfloat16
F32 = jnp.float32


def _sigmoid(x):
    return 1.0 / (1.0 + jnp.exp(-x))


def _silu(x):
    return x * _sigmoid(x)


def _params(sem, vmem=VMEM_LIMIT, flags=None):
    return pltpu.CompilerParams(dimension_semantics=sem, vmem_limit_bytes=vmem, flags=flags)


def _resident(shape):
    nd = len(shape)
    return pl.BlockSpec(shape, lambda *_: (0,) * nd, pipeline_mode=pl.Buffered(1))


def _bias_tile(rel_ref, h, t):
    T = ATTN_TILE
    key = lax.broadcasted_iota(jnp.int32, (T, T), 0)
    qry = lax.broadcasted_iota(jnp.int32, (T, T), 1)
    dist = (1 - t) * T + qry - key
    n = jnp.maximum(dist, 0)
    large = MAX_EXACT + (jnp.log(jnp.maximum(n, 1).astype(F32) / MAX_EXACT)
                         / math.log(MAX_DISTANCE / MAX_EXACT)
                         * (N_BUCKETS - MAX_EXACT)).astype(jnp.int32)
    large = jnp.minimum(large, N_BUCKETS - 1)
    bucket = jnp.where(n < MAX_EXACT, n, large)
    far = rel_ref[N_BUCKETS - 1, h]
    bias = jnp.zeros((T, T), F32)
    for b in range(N_BUCKETS - 1):
        bias = jnp.where(bucket == b, (rel_ref[b, h] - far) * LOG2E, bias)
    return jnp.where(dist >= 0, bias, NEG_INF)


def _prologue_kernel(rel_ref, c_ref, w_ref, b_ref, win_ref, o_ref, win_bf_ref, bias_ref):
    j = pl.program_id(0)
    ca = _silu(c_ref[...])
    o_ref[...] = jnp.dot(ca, w_ref[...], preferred_element_type=F32,
                         precision=lax.Precision.HIGHEST) + b_ref[...]
    win_bf_ref[...] = win_ref[...].astype(BF16)
    bias_ref[...] = _bias_tile(rel_ref, j // 2, j % 2)


def _prologue(c, w, b, w_in, rel_bias):
    bsz, d = c.shape
    n = w.shape[-1]
    steps = 2 * N_HEADS_A
    tn = n // steps
    _, rows, cols = w_in.shape
    tr = rows // steps
    T = ATTN_TILE
    return pl.pallas_call(
        _prologue_kernel,
        grid=(steps,),
        in_specs=[pl.BlockSpec(memory_space=pltpu.SMEM),
                  pl.BlockSpec((bsz, d), lambda j: (0, 0)),
                  pl.BlockSpec((None, d, tn), lambda j: (0, 0, j)),
                  pl.BlockSpec((1, tn), lambda j: (0, j)),
                  pl.BlockSpec((None, tr, cols), lambda j: (0, j, 0))],
        out_specs=[pl.BlockSpec((bsz, tn), lambda j: (0, j)),
                   pl.BlockSpec((tr, cols), lambda j: (j, 0)),
                   pl.BlockSpec((None, None, T, T), lambda j: (j // 2, j % 2, 0, 0))],
        out_shape=[jax.ShapeDtypeStruct((bsz, n), F32),
                   jax.ShapeDtypeStruct((rows, cols), BF16),
                   jax.ShapeDtypeStruct((N_HEADS_A, 2, T, T), F32)],
        compiler_params=_params(("arbitrary",)),
        name="prologue",
    )(rel_bias, c, w, b, w_in)


def _conv_chunk(r0, ubuf, w_ref, cb_ref, lng_ref, lnb_ref, o_ref):
    off = CONV_HALO - (CONV_KERNEL - 1)
    acc = jnp.zeros((CONV_CHUNK // 8, 8, CONV_WIDTH), F32)
    for k in range(CONV_KERNEL):
        a, b = divmod(k + off, 8)
        rows = ubuf[b, r0 + 8 * a:r0 + 8 * a + CONV_CHUNK, :]
        acc = acc + rows.reshape(acc.shape) * w_ref[k][None]
    u = acc.reshape(CONV_CHUNK, CONV_WIDTH) + cb_ref[...]
    mu = jnp.mean(u, axis=-1, keepdims=True)
    d = u - mu
    var = jnp.mean(d * d, axis=-1, keepdims=True)
    y = d * lax.rsqrt(var + EPS) * lng_ref[...] + lnb_ref[...]
    o_ref[r0:r0 + CONV_CHUNK, :] = _silu(y).astype(BF16)


def _inproj_kernel(x_ref, ada_ref, g_ref, w_ref, cw_ref, cb_ref, lng_ref, lnb_ref,
                   qkv_ref, u_ref, gate_ref, ubuf):
    TS, H, C = ROW_TILE, CONV_HALO, CONV_WIDTH
    i = pl.program_id(1)

    @pl.when(i == 0)
    def _():
        ubuf[0, 0:H, :] = jnp.zeros((H, C), F32)

    @pl.when(i > 0)
    def _():
        ubuf[0, 0:H, :] = ubuf[0, TS:TS + H, :]

    x = x_ref[...]
    ms = jnp.mean(x * x, axis=-1, keepdims=True)
    y = x * lax.rsqrt(ms + EPS)
    h = (y * g_ref[...]) * (1.0 + ada_ref[1:2, :]) + ada_ref[0:1, :]
    hb = h.astype(BF16)

    def proj(base):
        return jnp.dot(hb, w_ref[:, base:base + PROJ_CHUNK], preferred_element_type=F32)

    ubuf[0, H:H + TS, :] = proj(QKV_COLS) * _sigmoid(proj(QKV_COLS + C))
    n = TS + H - 8
    for b in range(1, 8):
        ubuf[b, 0:n, :] = ubuf[0, b:b + n, :]
    jobs = ([(qkv_ref, c, c) for c in range(0, QKV_COLS, PROJ_CHUNK)]
            + [(gate_ref, c, QKV_COLS + GLU_COLS + c) for c in range(0, GATE_COLS, PROJ_CHUNK)])
    rows = list(range(0, TS, CONV_CHUNK))
    for n, (ref, c, base) in enumerate(jobs):
        r = proj(base)
        if base < ATTN_WIDTH:
            r = r * Q_SCALE
        ref[:, c:c + PROJ_CHUNK] = r.astype(BF16)
        for r0 in rows[n * len(rows) // len(jobs):(n + 1) * len(rows) // len(jobs)]:
            _conv_chunk(r0, ubuf, cw_ref, cb_ref, lng_ref, lnb_ref, u_ref)


def _in_proj(x, ada3, g, w, cw, cb, lng, lnb):
    bsz, s, d = x.shape
    tm = ROW_TILE
    C = CONV_WIDTH
    row = lambda n: pl.BlockSpec((None, tm, n), lambda b, i: (b, i, 0))
    vec = lambda a: a.reshape(1, C)
    return pl.pallas_call(
        _inproj_kernel,
        grid=(bsz, s // tm),
        in_specs=[row(d),
                  pl.BlockSpec((None, 6, d), lambda b, i: (b, 0, 0)),
                  _resident((1, d)),
                  _resident((d, IN_COLS)),
                  _resident((CONV_KERNEL, 8, C)),
                  _resident((1, C)), _resident((1, C)), _resident((1, C))],
        out_specs=[row(QKV_COLS), row(C), row(GATE_COLS)],
        out_shape=[jax.ShapeDtypeStruct((bsz, s, QKV_COLS), BF16),
                   jax.ShapeDtypeStruct((bsz, s, C), BF16),
                   jax.ShapeDtypeStruct((bsz, s, GATE_COLS), BF16)],
        scratch_shapes=[pltpu.VMEM((8, tm + CONV_HALO, C), F32)],
        compiler_params=_params(("arbitrary", "arbitrary")),
        name="in_proj",
    )(x, ada3, g.reshape(1, d), w, jnp.broadcast_to(cw[:, None, :], (CONV_KERNEL, 8, C)),
      vec(cb), vec(lng), vec(lnb))


def _attn_kernel(lamv_ref, subg_ref, q_ref, k_ref, v_ref, bias_ref, *rest):
    n_cast = (len(rest) - 5) // 2
    cast_in, o_ref, cast_out = rest[:n_cast], rest[n_cast], rest[n_cast + 1:2 * n_cast + 1]
    vt_ref, qz_ref, m_ref, a_ref = rest[2 * n_cast + 1:]
    T = ATTN_TILE
    NS = 2 * N_HEADS_A
    n_q = q_ref.shape[0] // T
    hcols = lambda h: slice(h * HEAD_V, (h + 1) * HEAD_V)
    rows = lambda t: slice(t * T, (t + 1) * T)

    vt = v_ref[...].T
    for h in range(N_HEADS_A):
        vt_ref[h, 0:HEAD_V, :] = vt[hcols(h), :]
        vt_ref[h, HEAD_V:, :] = jnp.ones((ONES_ROWS, vt.shape[1]), BF16)

    lv = lamv_ref[...]
    lam = (jnp.exp(jnp.sum(lv[0:1] * lv[1:2], axis=-1, keepdims=True))
           - jnp.exp(jnp.sum(lv[2:3] * lv[3:4], axis=-1, keepdims=True)) + LAM_INIT)
    lane = lax.broadcasted_iota(jnp.int32, (T, HEAD_V), 1)

    def prepare(i):
        slot = i % 2
        for h in range(N_HEADS_A):
            qs = q_ref[rows(i), hcols(h)]
            zero = jnp.zeros_like(qs)
            qz_ref[slot, 2 * h] = jnp.where(lane < HEAD_QK, qs, zero)
            qz_ref[slot, 2 * h + 1] = jnp.where(lane >= HEAD_QK, qs, zero)
        m_ref[slot] = jnp.full(m_ref.shape[1:], NEG_INF, F32)
        a_ref[slot] = jnp.zeros(a_ref.shape[1:], F32)

    def finalize(i):
        slot = i % 2
        for h in range(N_HEADS_A):
            a1, a2 = a_ref[slot, 2 * h], a_ref[slot, 2 * h + 1]
            o = (a1[:HEAD_V] / a1[HEAD_V:HEAD_V + 1]
                 - lam * (a2[:HEAD_V] / a2[HEAD_V:HEAD_V + 1]))
            ms = jnp.mean(o * o, axis=0, keepdims=True)
            y = (o * lax.rsqrt(ms + EPS)).T
            o_ref[rows(i), hcols(h)] = ((y * subg_ref[...]) * (1.0 - LAM_INIT)).astype(BF16)

    units = [(i, j, n) for i in range(n_q) for j in range(i + 1) for n in range(NS)]
    scores, probs, alphas = {}, {}, {}

    def score(u):
        i, j, n = units[u]
        kc = k_ref[rows(j), hcols(n // 2)]
        scores[u] = lax.dot_general(kc, qz_ref[i % 2, n], (((1,), (1,)), ((), ())),
                                    preferred_element_type=F32)

    def softmax(u):
        i, j, n = units[u]
        s = scores.pop(u)
        if j >= i - 1:
            s = s + bias_ref[n // 2, j - (i - 1)]
        m_old = m_ref[i % 2, n]
        m_new = jnp.maximum(m_old, jnp.max(s, axis=0, keepdims=True))
        probs[u] = jnp.exp2(s - m_new).astype(BF16)
        alphas[u] = jnp.exp2(m_old - m_new)
        m_ref[i % 2, n] = m_new

    def value(u):
        i, j, n = units[u]
        vc = vt_ref[n // 2, :, rows(j)]
        a_ref[i % 2, n] = alphas.pop(u) * a_ref[i % 2, n] + jnp.dot(
            vc, probs.pop(u), preferred_element_type=F32)

    for step in range(len(units) + 2 * ATTN_LAG):
        if step < len(units):
            if units[step][1:] == (0, 0):
                prepare(units[step][0])
            score(step)
        if 0 <= step - ATTN_LAG < len(units):
            softmax(step - ATTN_LAG)
        done = step - 2 * ATTN_LAG
        if 0 <= done < len(units):
            value(done)
            i, j, n = units[done]
            if (j, n) == (i, NS - 1):
                finalize(i)

    for w_ref, wb_ref in zip(cast_in, cast_out):
        wb_ref[...] = w_ref[...].astype(BF16)


def _attention(qkv, bias_tiles, lamv, subg, weights):
    bsz, s, _ = qkv.shape
    T = ATTN_TILE
    W = ATTN_WIDTH
    NS = 2 * N_HEADS_A
    steps = bsz

    def slab_spec(w, squeeze):
        rows, cols = w.shape[-2:]
        assert rows % (steps * CAST_ROWS) == 0
        block = (rows // steps, cols)
        if squeeze:
            return pl.BlockSpec((None,) + block, lambda b: (0, b, 0))
        return pl.BlockSpec(block, lambda b: (b, 0))

    seq = lambda col: pl.BlockSpec((None, s, W), lambda b: (b, 0, col))
    out = pl.pallas_call(
        _attn_kernel,
        grid=(bsz,),
        in_specs=[_resident((4, HEAD_QK)),
                  _resident((1, HEAD_V)),
                  seq(0), seq(1), seq(2),
                  _resident((N_HEADS_A, 2, T, T))] + [slab_spec(w, True) for w in weights],
        out_specs=[seq(0)] + [slab_spec(w, False) for w in weights],
        out_shape=[jax.ShapeDtypeStruct((bsz, s, W), BF16)]
        + [jax.ShapeDtypeStruct(w.shape[-2:], BF16) for w in weights],
        scratch_shapes=[pltpu.VMEM((N_HEADS_A, HEAD_V + ONES_ROWS, s), BF16),
                        pltpu.VMEM((2, NS, T, HEAD_V), BF16),
                        pltpu.VMEM((2, NS, 1, T), F32),
                        pltpu.VMEM((2, NS, HEAD_V + ONES_ROWS, T), F32)],
        compiler_params=_params(("arbitrary",)),
        name="diff_attn",
    )(lamv, subg.reshape(1, HEAD_V), qkv, qkv, qkv, bias_tiles, *weights)
    return out[0], out[1:]


def _mixffn_kernel(x_ref, ada_ref, a_ref, u_ref, gate_ref, woa_ref, woc_ref, boc_ref,
                   wout_ref, g2_ref, w1_ref, w2_ref, gf_ref, o_ref,
                   x1_ref, h2_ref, act_ref):
    D = D_MODEL
    a = jnp.dot(a_ref[...], woa_ref[...], preferred_element_type=F32)
    cv = jnp.dot(u_ref[...], woc_ref[...], preferred_element_type=F32) + boc_ref[...]
    gate = gate_ref[...]
    y = (_sigmoid(gate[:, :D].astype(F32)) * a
         + _sigmoid(gate[:, D:].astype(F32)) * cv)
    z = jnp.dot(y.astype(BF16), wout_ref[...], preferred_element_type=F32)
    x1 = x_ref[...] + ada_ref[2:3, :] * z
    x1_ref[...] = x1
    ms = jnp.mean(x1 * x1, axis=-1, keepdims=True)
    yn = x1 * lax.rsqrt(ms + EPS)
    h2_ref[...] = ((yn * g2_ref[...]) * (1.0 + ada_ref[4:5, :]) + ada_ref[3:4, :]).astype(BF16)

    h = h2_ref[...]
    for c in range(0, D_FF, FF_CHUNK):
        fg = jnp.dot(h, w1_ref[:, c:c + FF_CHUNK], preferred_element_type=F32)
        fu = jnp.dot(h, w1_ref[:, D_FF + c:D_FF + c + FF_CHUNK],
                     preferred_element_type=F32)
        act_ref[:, c:c + FF_CHUNK] = (_silu(fg) * fu).astype(BF16)
    z2 = jnp.dot(act_ref[...], w2_ref[...], preferred_element_type=F32)
    x2 = x1_ref[...] + ada_ref[5:6, :] * z2
    ms2 = jnp.mean(x2 * x2, axis=-1, keepdims=True)
    o_ref[...] = (x2 * lax.rsqrt(ms2 + EPS)) * gf_ref[...]


def _mix_ffn(x, ada3, a_n, u, gates, woa, woc, boc, wout, g2, w1, w2, gf):
    bsz, s, d = x.shape
    tm = ROW_TILE
    row = lambda n: pl.BlockSpec((None, tm, n), lambda b, i: (b, i, 0))
    return pl.pallas_call(
        _mixffn_kernel,
        grid=(bsz, s // tm),
        in_specs=[row(d),
                  pl.BlockSpec((None, 6, d), lambda b, i: (b, 0, 0)),
                  row(ATTN_WIDTH), row(CONV_WIDTH), row(GATE_COLS),
                  _resident((ATTN_WIDTH, d)), _resident((CONV_WIDTH, d)),
                  _resident((1, d)), _resident((d, d)), _resident((1, d)),
                  _resident((d, 2 * D_FF)), _resident((D_FF, d)), _resident((1, d))],
        out_specs=row(d),
        out_shape=jax.ShapeDtypeStruct((bsz, s, d), F32),
        scratch_shapes=[pltpu.VMEM((tm, d), F32), pltpu.VMEM((tm, d), BF16),
                        pltpu.VMEM((tm, D_FF), BF16)],
        compiler_params=_params(("parallel", "arbitrary")),
        name="mix_ffn",
    )(x, ada3, a_n, u, gates, woa, woc, boc.reshape(1, d), wout, g2.reshape(1, d),
      w1, w2, gf.reshape(1, d))


def kernel(x, c, w_ada, b_ada, norm1_g, norm2_g, final_g, w_in, lambda_q1, lambda_k1,
           lambda_q2, lambda_k2, rel_bias, attn_sub_g, w_o_attn, conv_w, conv_b,
           conv_ln_g, conv_ln_b, w_o_conv, b_o_conv, w_out, w_ffn_in, w_ffn_out):
    bsz, s, d = x.shape
    assert w_ada.shape[0] == 1, "single-layer block"
    assert (d, s % ROW_TILE, s % ATTN_TILE) == (D_MODEL, 0, 0)
    assert PROJ_CHUNK == CONV_WIDTH and ATTN_WIDTH % PROJ_CHUNK == 0
    l = 0
    ada, w_in_bf, bias_tiles = _prologue(c, w_ada, b_ada, w_in, rel_bias)
    ada3 = ada.reshape(bsz, 6, d)
    qkv, u, gates = _in_proj(x, ada3, norm1_g[l], w_in_bf, conv_w[l],
                             conv_b[l], conv_ln_g[l], conv_ln_b[l])
    lamv = jnp.stack([lambda_q1[l], lambda_k1[l], lambda_q2[l], lambda_k2[l]])
    a_n, (woa, woc, wout, w1, w2) = _attention(
        qkv, bias_tiles, lamv, attn_sub_g[l],
        [w_o_attn, w_o_conv, w_out, w_ffn_in, w_ffn_out])
    return _mix_ffn(x, ada3, a_n, u, gates, woa, woc, b_o_conv[l], wout, norm2_g[l],
                    w1, w2, final_g)
```

```python
import math

import jax
import jax.numpy as jnp
from jax import lax
from jax.experimental import pallas as pl
from jax.experimental.pallas import tpu as pltpu

D_MODEL = 1024
N_HEADS_A = 4
HEAD_QK = 64
HEAD_V = 2 * HEAD_QK
ATTN_WIDTH = N_HEADS_A * HEAD_V
CONV_WIDTH = 512
CONV_KERNEL = 31
QKV_COLS = 3 * ATTN_WIDTH
GLU_COLS = 2 * CONV_WIDTH
GATE_COLS = 2 * D_MODEL
IN_COLS = QKV_COLS + GLU_COLS + GATE_COLS
D_FF = 2816
N_BUCKETS = 32
MAX_EXACT = 16
MAX_DISTANCE = 128
EPS = 1e-6
NEG_INF = -1e30
LAM_INIT = 0.8 - 0.6 * math.exp(-0.3 * 0)

V7X_VMEM_BYTES = 64 * 1024 * 1024
VMEM_LIMIT = V7X_VMEM_BYTES - 12 * 1024 * 1024

ROW_TILE = 512
ATTN_TILE = 256
ATTN_LAG = 4
ONES_ROWS = 16
CAST_ROWS = 16
LOG2E = math.log2(math.e)
Q_SCALE = HEAD_QK ** -0.5 * LOG2E
CONV_HALO = 32
CONV_CHUNK = 16
PROJ_CHUNK = 512
FF_CHUNK = 256

BF16 = jnp.bfloat16
F32 = jnp.float32


def _sigmoid(x):
    return 1.0 / (1.0 + jnp.exp(-x))


def _silu(x):
    return x * _sigmoid(x)


def _params(sem, vmem=VMEM_LIMIT, flags=None):
    return pltpu.CompilerParams(dimension_semantics=sem, vmem_limit_bytes=vmem, flags=flags)


def _resident(shape):
    nd = len(shape)
    return pl.BlockSpec(shape, lambda *_: (0,) * nd, pipeline_mode=pl.Buffered(1))


def _bias_tile(rel_ref, h, t):
    T = ATTN_TILE
    key = lax.broadcasted_iota(jnp.int32, (T, T), 0)
    qry = lax.broadcasted_iota(jnp.int32, (T, T), 1)
    dist = (1 - t) * T + qry - key
    n = jnp.maximum(dist, 0)
    large = MAX_EXACT + (jnp.log(jnp.maximum(n, 1).astype(F32) / MAX_EXACT)
                         / math.log(MAX_DISTANCE / MAX_EXACT)
                         * (N_BUCKETS - MAX_EXACT)).astype(jnp.int32)
    large = jnp.minimum(large, N_BUCKETS - 1)
    bucket = jnp.where(n < MAX_EXACT, n, large)
    far = rel_ref[N_BUCKETS - 1, h]
    bias = jnp.zeros((T, T), F32)
    for b in range(N_BUCKETS - 1):
        bias = jnp.where(bucket == b, (rel_ref[b, h] - far) * LOG2E, bias)
    return jnp.where(dist >= 0, bias, NEG_INF)


def _prologue_kernel(rel_ref, c_ref, w_ref, b_ref, win_ref, o_ref, win_bf_ref, bias_ref):
    j = pl.program_id(0)
    ca = _silu(c_ref[...])
    o_ref[...] = jnp.dot(ca, w_ref[...], preferred_element_type=F32,
                         precision=lax.Precision.HIGHEST) + b_ref[...]
    win_bf_ref[...] = win_ref[...].astype(BF16)
    bias_ref[...] = _bias_tile(rel_ref, j // 2, j % 2)


def _prologue(c, w, b, w_in, rel_bias):
    bsz, d = c.shape
    n = w.shape[-1]
    steps = 2 * N_HEADS_A
    tn = n // steps
    _, rows, cols = w_in.shape
    tr = rows // steps
    T = ATTN_TILE
    return pl.pallas_call(
        _prologue_kernel,
        grid=(steps,),
        in_specs=[pl.BlockSpec(memory_space=pltpu.SMEM),
                  pl.BlockSpec((bsz, d), lambda j: (0, 0)),
                  pl.BlockSpec((None, d, tn), lambda j: (0, 0, j)),
                  pl.BlockSpec((1, tn), lambda j: (0, j)),
                  pl.BlockSpec((None, tr, cols), lambda j: (0, j, 0))],
        out_specs=[pl.BlockSpec((bsz, tn), lambda j: (0, j)),
                   pl.BlockSpec((tr, cols), lambda j: (j, 0)),
                   pl.BlockSpec((None, None, T, T), lambda j: (j // 2, j % 2, 0, 0))],
        out_shape=[jax.ShapeDtypeStruct((bsz, n), F32),
                   jax.ShapeDtypeStruct((rows, cols), BF16),
                   jax.ShapeDtypeStruct((N_HEADS_A, 2, T, T), F32)],
        compiler_params=_params(("arbitrary",)),
        name="prologue",
    )(rel_bias, c, w, b, w_in)


def _conv_chunk(r0, ubuf, w_ref, cb_ref, lng_ref, lnb_ref, o_ref):
    off = CONV_HALO - (CONV_KERNEL - 1)
    acc = jnp.zeros((CONV_CHUNK // 8, 8, CONV_WIDTH), F32)
    for k in range(CONV_KERNEL):
        a, b = divmod(k + off, 8)
        rows = ubuf[b, r0 + 8 * a:r0 + 8 * a + CONV_CHUNK, :]
        acc = acc + rows.reshape(acc.shape) * w_ref[k][None]
    u = acc.reshape(CONV_CHUNK, CONV_WIDTH) + cb_ref[...]
    mu = jnp.mean(u, axis=-1, keepdims=True)
    d = u - mu
    var = jnp.mean(d * d, axis=-1, keepdims=True)
    y = d * lax.rsqrt(var + EPS) * lng_ref[...] + lnb_ref[...]
    o_ref[r0:r0 + CONV_CHUNK, :] = _silu(y).astype(BF16)


def _inproj_kernel(x_ref, ada_ref, g_ref, w_ref, cw_ref, cb_ref, lng_ref, lnb_ref,
                   qkv_ref, u_ref, gate_ref, ubuf):
    TS, H, C = ROW_TILE, CONV_HALO, CONV_WIDTH
    i = pl.program_id(1)

    @pl.when(i == 0)
    def _():
        ubuf[0, 0:H, :] = jnp.zeros((H, C), F32)

    @pl.when(i > 0)
    def _():
        ubuf[0, 0:H, :] = ubuf[0, TS:TS + H, :]

    x = x_ref[...]
    ms = jnp.mean(x * x, axis=-1, keepdims=True)
    y = x * lax.rsqrt(ms + EPS)
    h = (y * g_ref[...]) * (1.0 + ada_ref[1:2, :]) + ada_ref[0:1, :]
    hb = h.astype(BF16)

    def proj(base):
        return jnp.dot(hb, w_ref[:, base:base + PROJ_CHUNK], preferred_element_type=F32)

    ubuf[0, H:H + TS, :] = proj(QKV_COLS) * _sigmoid(proj(QKV_COLS + C))
    n = TS + H - 8
    for b in range(1, 8):
        ubuf[b, 0:n, :] = ubuf[0, b:b + n, :]
    jobs = ([(qkv_ref, c, c) for c in range(0, QKV_COLS, PROJ_CHUNK)]
            + [(gate_ref, c, QKV_COLS + GLU_COLS + c) for c in range(0, GATE_COLS, PROJ_CHUNK)])
    rows = list(range(0, TS, CONV_CHUNK))
    for n, (ref, c, base) in enumerate(jobs):
        r = proj(base)
        if base < ATTN_WIDTH:
            r = r * Q_SCALE
        ref[:, c:c + PROJ_CHUNK] = r.astype(BF16)
        for r0 in rows[n * len(rows) // len(jobs):(n + 1) * len(rows) // len(jobs)]:
            _conv_chunk(r0, ubuf, cw_ref, cb_ref, lng_ref, lnb_ref, u_ref)


def _in_proj(x, ada3, g, w, cw, cb, lng, lnb):
    bsz, s, d = x.shape
    tm = ROW_TILE
    C = CONV_WIDTH
    row = lambda n: pl.BlockSpec((None, tm, n), lambda b, i: (b, i, 0))
    vec = lambda a: a.reshape(1, C)
    return pl.pallas_call(
        _inproj_kernel,
        grid=(bsz, s // tm),
        in_specs=[row(d),
                  pl.BlockSpec((None, 6, d), lambda b, i: (b, 0, 0)),
                  _resident((1, d)),
                  _resident((d, IN_COLS)),
                  _resident((CONV_KERNEL, 8, C)),
                  _resident((1, C)), _resident((1, C)), _resident((1, C))],
        out_specs=[row(QKV_COLS), row(C), row(GATE_COLS)],
        out_shape=[jax.ShapeDtypeStruct((bsz, s, QKV_COLS), BF16),
                   jax.ShapeDtypeStruct((bsz, s, C), BF16),
                   jax.ShapeDtypeStruct((bsz, s, GATE_COLS), BF16)],
        scratch_shapes=[pltpu.VMEM((8, tm + CONV_HALO, C), F32)],
        compiler_params=_params(("arbitrary", "arbitrary")),
        name="in_proj",
    )(x, ada3, g.reshape(1, d), w, jnp.broadcast_to(cw[:, None, :], (CONV_KERNEL, 8, C)),
      vec(cb), vec(lng), vec(lnb))


def _attn_kernel(lamv_ref, subg_ref, q_ref, k_ref, v_ref, bias_ref, *rest):
    n_cast = (len(rest) - 5) // 2
    cast_in, o_ref, cast_out = rest[:n_cast], rest[n_cast], rest[n_cast + 1:2 * n_cast + 1]
    vt_ref, qz_ref, m_ref, a_ref = rest[2 * n_cast + 1:]
    T = ATTN_TILE
    NS = 2 * N_HEADS_A
    n_q = q_ref.shape[0] // T
    hcols = lambda h: slice(h * HEAD_V, (h + 1) * HEAD_V)
    rows = lambda t: slice(t * T, (t + 1) * T)

    vt = v_ref[...].T
    for h in range(N_HEADS_A):
        vt_ref[h, 0:HEAD_V, :] = vt[hcols(h), :]
        vt_ref[h, HEAD_V:, :] = jnp.ones((ONES_ROWS, vt.shape[1]), BF16)

    lv = lamv_ref[...]
    lam = (jnp.exp(jnp.sum(lv[0:1] * lv[1:2], axis=-1, keepdims=True))
           - jnp.exp(jnp.sum(lv[2:3] * lv[3:4], axis=-1, keepdims=True)) + LAM_INIT)
    lane = lax.broadcasted_iota(jnp.int32, (T, HEAD_V), 1)

    def prepare(i):
        slot = i % 2
        for h in range(N_HEADS_A):
            qs = q_ref[rows(i), hcols(h)]
            zero = jnp.zeros_like(qs)
            qz_ref[slot, 2 * h] = jnp.where(lane < HEAD_QK, qs, zero)
            qz_ref[slot, 2 * h + 1] = jnp.where(lane >= HEAD_QK, qs, zero)

    def finalize(i):
        slot = i % 2
        for h in range(N_HEADS_A):
            a1, a2 = a_ref[slot, 2 * h], a_ref[slot, 2 * h + 1]
            o = (a1[:HEAD_V] / a1[HEAD_V:HEAD_V + 1]
                 - lam * (a2[:HEAD_V] / a2[HEAD_V:HEAD_V + 1]))
            ms = jnp.mean(o * o, axis=0, keepdims=True)
            y = (o * lax.rsqrt(ms + EPS)).T
            o_ref[rows(i), hcols(h)] = ((y * subg_ref[...]) * (1.0 - LAM_INIT)).astype(BF16)

    units = [(i, j, n) for i in range(n_q) for j in range(i + 1) for n in range(NS)]
    scores, probs, alphas = {}, {}, {}

    def score(u):
        i, j, n = units[u]
        kc = k_ref[rows(j), hcols(n // 2)]
        scores[u] = lax.dot_general(kc, qz_ref[i % 2, n], (((1,), (1,)), ((), ())),
                                    preferred_element_type=F32)

    def softmax(u):
        i, j, n = units[u]
        s = scores.pop(u)
        if j >= i - 1:
            s = s + bias_ref[n // 2, j - (i - 1)]
        m_new = jnp.max(s, axis=0, keepdims=True)
        if j > 0:
            m_old = m_ref[i % 2, n]
            m_new = jnp.maximum(m_old, m_new)
            alphas[u] = jnp.exp2(m_old - m_new)
        probs[u] = jnp.exp2(s - m_new).astype(BF16)
        m_ref[i % 2, n] = m_new

    def value(u):
        i, j, n = units[u]
        vc = vt_ref[n // 2, :, rows(j)]
        pv = jnp.dot(vc, probs.pop(u), preferred_element_type=F32)
        a_ref[i % 2, n] = pv if j == 0 else alphas.pop(u) * a_ref[i % 2, n] + pv

    for step in range(len(units) + 2 * ATTN_LAG):
        if step < len(units):
            if units[step][1:] == (0, 0):
                prepare(units[step][0])
            score(step)
        if 0 <= step - ATTN_LAG < len(units):
            softmax(step - ATTN_LAG)
        done = step - 2 * ATTN_LAG
        if 0 <= done < len(units):
            value(done)
            i, j, n = units[done]
            if (j, n) == (i, NS - 1):
                finalize(i)

    for w_ref, wb_ref in zip(cast_in, cast_out):
        wb_ref[...] = w_ref[...].astype(BF16)


def _attention(qkv, bias_tiles, lamv, subg, weights):
    bsz, s, _ = qkv.shape
    T = ATTN_TILE
    W = ATTN_WIDTH
    NS = 2 * N_HEADS_A
    steps = bsz

    def slab_spec(w, squeeze):
        rows, cols = w.shape[-2:]
        assert rows % (steps * CAST_ROWS) == 0
        block = (rows // steps, cols)
        if squeeze:
            return pl.BlockSpec((None,) + block, lambda b: (0, b, 0))
        return pl.BlockSpec(block, lambda b: (b, 0))

    seq = lambda col: pl.BlockSpec((None, s, W), lambda b: (b, 0, col))
    out = pl.pallas_call(
        _attn_kernel,
        grid=(bsz,),
        in_specs=[_resident((4, HEAD_QK)),
                  _resident((1, HEAD_V)),
                  seq(0), seq(1), seq(2),
                  _resident((N_HEADS_A, 2, T, T))] + [slab_spec(w, True) for w in weights],
        out_specs=[seq(0)] + [slab_spec(w, False) for w in weights],
        out_shape=[jax.ShapeDtypeStruct((bsz, s, W), BF16)]
        + [jax.ShapeDtypeStruct(w.shape[-2:], BF16) for w in weights],
        scratch_shapes=[pltpu.VMEM((N_HEADS_A, HEAD_V + ONES_ROWS, s), BF16),
                        pltpu.VMEM((2, NS, T, HEAD_V), BF16),
                        pltpu.VMEM((2, NS, 1, T), F32),
                        pltpu.VMEM((2, NS, HEAD_V + ONES_ROWS, T), F32)],
        compiler_params=_params(("arbitrary",)),
        name="diff_attn",
    )(lamv, subg.reshape(1, HEAD_V), qkv, qkv, qkv, bias_tiles, *weights)
    return out[0], out[1:]


def _mixffn_kernel(x_ref, ada_ref, a_ref, u_ref, gate_ref, woa_ref, woc_ref, boc_ref,
                   wout_ref, g2_ref, w1_ref, w2_ref, gf_ref, o_ref,
                   x1_ref, h2_ref, act_ref):
    D = D_MODEL
    a = jnp.dot(a_ref[...], woa_ref[...], preferred_element_type=F32)
    cv = jnp.dot(u_ref[...], woc_ref[...], preferred_element_type=F32) + boc_ref[...]
    gate = gate_ref[...]
    y = (_sigmoid(gate[:, :D].astype(F32)) * a
         + _sigmoid(gate[:, D:].astype(F32)) * cv)
    z = jnp.dot(y.astype(BF16), wout_ref[...], preferred_element_type=F32)
    x1 = x_ref[...] + ada_ref[2:3, :] * z
    x1_ref[...] = x1
    ms = jnp.mean(x1 * x1, axis=-1, keepdims=True)
    yn = x1 * lax.rsqrt(ms + EPS)
    h2_ref[...] = ((yn * g2_ref[...]) * (1.0 + ada_ref[4:5, :]) + ada_ref[3:4, :]).astype(BF16)

    h = h2_ref[...]
    for c in range(0, D_FF, FF_CHUNK):
        fg = jnp.dot(h, w1_ref[:, c:c + FF_CHUNK], preferred_element_type=F32)
        fu = jnp.dot(h, w1_ref[:, D_FF + c:D_FF + c + FF_CHUNK],
                     preferred_element_type=F32)
        act_ref[:, c:c + FF_CHUNK] = (_silu(fg) * fu).astype(BF16)
    z2 = jnp.dot(act_ref[...], w2_ref[...], preferred_element_type=F32)
    x2 = x1_ref[...] + ada_ref[5:6, :] * z2
    ms2 = jnp.mean(x2 * x2, axis=-1, keepdims=True)
    o_ref[...] = (x2 * lax.rsqrt(ms2 + EPS)) * gf_ref[...]


def _mix_ffn(x, ada3, a_n, u, gates, woa, woc, boc, wout, g2, w1, w2, gf):
    bsz, s, d = x.shape
    tm = ROW_TILE
    row = lambda n: pl.BlockSpec((None, tm, n), lambda b, i: (b, i, 0))
    return pl.pallas_call(
        _mixffn_kernel,
        grid=(bsz, s // tm),
        in_specs=[row(d),
                  pl.BlockSpec((None, 6, d), lambda b, i: (b, 0, 0)),
                  row(ATTN_WIDTH), row(CONV_WIDTH), row(GATE_COLS),
                  _resident((ATTN_WIDTH, d)), _resident((CONV_WIDTH, d)),
                  _resident((1, d)), _resident((d, d)), _resident((1, d)),
                  _resident((d, 2 * D_FF)), _resident((D_FF, d)), _resident((1, d))],
        out_specs=row(d),
        out_shape=jax.ShapeDtypeStruct((bsz, s, d), F32),
        scratch_shapes=[pltpu.VMEM((tm, d), F32), pltpu.VMEM((tm, d), BF16),
                        pltpu.VMEM((tm, D_FF), BF16)],
        compiler_params=_params(("parallel", "arbitrary")),
        name="mix_ffn",
    )(x, ada3, a_n, u, gates, woa, woc, boc.reshape(1, d), wout, g2.reshape(1, d),
      w1, w2, gf.reshape(1, d))


def kernel(x, c, w_ada, b_ada, norm1_g, norm2_g, final_g, w_in, lambda_q1, lambda_k1,
           lambda_q2, lambda_k2, rel_bias, attn_sub_g, w_o_attn, conv_w, conv_b,
           conv_ln_g, conv_ln_b, w_o_conv, b_o_conv, w_out, w_ffn_in, w_ffn_out):
    bsz, s, d = x.shape
    assert w_ada.shape[0] == 1, "single-layer block"
    assert (d, s % ROW_TILE, s % ATTN_TILE) == (D_MODEL, 0, 0)
    assert PROJ_CHUNK == CONV_WIDTH and ATTN_WIDTH % PROJ_CHUNK == 0
    l = 0
    ada, w_in_bf, bias_tiles = _prologue(c, w_ada, b_ada, w_in, rel_bias)
    ada3 = ada.reshape(bsz, 6, d)
    qkv, u, gates = _in_proj(x, ada3, norm1_g[l], w_in_bf, conv_w[l],
                             conv_b[l], conv_ln_g[l], conv_ln_b[l])
    lamv = jnp.stack([lambda_q1[l], lambda_k1[l], lambda_q2[l], lambda_k2[l]])
    a_n, (woa, woc, wout, w1, w2) = _attention(
        qkv, bias_tiles, lamv, attn_sub_g[l],
        [w_o_attn, w_o_conv, w_out, w_ffn_in, w_ffn_out])
    return _mix_ffn(x, ada3, a_n, u, gates, woa, woc, b_o_conv[l], wout, norm2_g[l],
                    w1, w2, final_g)
```

```python
import math

import jax
import jax.numpy as jnp
from jax import lax
from jax.experimental import pallas as pl
from jax.experimental.pallas import tpu as pltpu

D_MODEL = 1024
N_HEADS_A = 4
HEAD_QK = 64
HEAD_V = 2 * HEAD_QK
ATTN_WIDTH = N_HEADS_A * HEAD_V
CONV_WIDTH = 512
CONV_KERNEL = 31
QKV_COLS = 3 * ATTN_WIDTH
GLU_COLS = 2 * CONV_WIDTH
GATE_COLS = 2 * D_MODEL
IN_COLS = QKV_COLS + GLU_COLS + GATE_COLS
D_FF = 2816
N_BUCKETS = 32
MAX_EXACT = 16
MAX_DISTANCE = 128
EPS = 1e-6
NEG_INF = -1e30
LAM_INIT = 0.8 - 0.6 * math.exp(-0.3 * 0)

V7X_VMEM_BYTES = 64 * 1024 * 1024
VMEM_LIMIT = V7X_VMEM_BYTES - 12 * 1024 * 1024

ROW_TILE = 512
ATTN_TILE = 256
SCORE_LEAD = 4
VALUE_LAG = 2
ONES_ROWS = 16
CAST_ROWS = 16
LOG2E = math.log2(math.e)
Q_SCALE = HEAD_QK ** -0.5 * LOG2E
CONV_HALO = 32
CONV_CHUNK = 16
PROJ_CHUNK = 512
FF_CHUNK = 256

BF16 = jnp.bfloat16
F32 = jnp.float32


def _sigmoid(x):
    return 1.0 / (1.0 + jnp.exp(-x))


def _silu(x):
    return x * _sigmoid(x)


def _params(sem, vmem=VMEM_LIMIT, flags=None):
    return pltpu.CompilerParams(dimension_semantics=sem, vmem_limit_bytes=vmem, flags=flags)


def _resident(shape):
    nd = len(shape)
    return pl.BlockSpec(shape, lambda *_: (0,) * nd, pipeline_mode=pl.Buffered(1))


def _bias_tile(rel_ref, h, t):
    T = ATTN_TILE
    key = lax.broadcasted_iota(jnp.int32, (T, T), 0)
    qry = lax.broadcasted_iota(jnp.int32, (T, T), 1)
    dist = (1 - t) * T + qry - key
    n = jnp.maximum(dist, 0)
    large = MAX_EXACT + (jnp.log(jnp.maximum(n, 1).astype(F32) / MAX_EXACT)
                         / math.log(MAX_DISTANCE / MAX_EXACT)
                         * (N_BUCKETS - MAX_EXACT)).astype(jnp.int32)
    large = jnp.minimum(large, N_BUCKETS - 1)
    bucket = jnp.where(n < MAX_EXACT, n, large)
    far = rel_ref[N_BUCKETS - 1, h]
    bias = jnp.zeros((T, T), F32)
    for b in range(N_BUCKETS - 1):
        bias = jnp.where(bucket == b, (rel_ref[b, h] - far) * LOG2E, bias)
    return jnp.where(dist >= 0, bias, NEG_INF)


def _prologue_kernel(rel_ref, c_ref, w_ref, b_ref, win_ref, o_ref, win_bf_ref, bias_ref):
    j = pl.program_id(0)
    ca = _silu(c_ref[...])
    o_ref[...] = jnp.dot(ca, w_ref[...], preferred_element_type=F32,
                         precision=lax.Precision.HIGHEST) + b_ref[...]
    win_bf_ref[...] = win_ref[...].astype(BF16)
    bias_ref[...] = _bias_tile(rel_ref, j // 2, j % 2)


def _prologue(c, w, b, w_in, rel_bias):
    bsz, d = c.shape
    n = w.shape[-1]
    steps = 2 * N_HEADS_A
    tn = n // steps
    _, rows, cols = w_in.shape
    tr = rows // steps
    T = ATTN_TILE
    return pl.pallas_call(
        _prologue_kernel,
        grid=(steps,),
        in_specs=[pl.BlockSpec(memory_space=pltpu.SMEM),
                  pl.BlockSpec((bsz, d), lambda j: (0, 0)),
                  pl.BlockSpec((None, d, tn), lambda j: (0, 0, j)),
                  pl.BlockSpec((1, tn), lambda j: (0, j)),
                  pl.BlockSpec((None, tr, cols), lambda j: (0, j, 0))],
        out_specs=[pl.BlockSpec((bsz, tn), lambda j: (0, j)),
                   pl.BlockSpec((tr, cols), lambda j: (j, 0)),
                   pl.BlockSpec((None, None, T, T), lambda j: (j // 2, j % 2, 0, 0))],
        out_shape=[jax.ShapeDtypeStruct((bsz, n), F32),
                   jax.ShapeDtypeStruct((rows, cols), BF16),
                   jax.ShapeDtypeStruct((N_HEADS_A, 2, T, T), F32)],
        compiler_params=_params(("arbitrary",)),
        name="prologue",
    )(rel_bias, c, w, b, w_in)


def _conv_chunk(r0, ubuf, w_ref, cb_ref, lng_ref, lnb_ref, o_ref):
    off = CONV_HALO - (CONV_KERNEL - 1)
    acc = jnp.zeros((CONV_CHUNK // 8, 8, CONV_WIDTH), F32)
    for k in range(CONV_KERNEL):
        a, b = divmod(k + off, 8)
        rows = ubuf[b, r0 + 8 * a:r0 + 8 * a + CONV_CHUNK, :]
        acc = acc + rows.reshape(acc.shape) * w_ref[k][None]
    u = acc.reshape(CONV_CHUNK, CONV_WIDTH) + cb_ref[...]
    mu = jnp.mean(u, axis=-1, keepdims=True)
    d = u - mu
    var = jnp.mean(d * d, axis=-1, keepdims=True)
    y = d * lax.rsqrt(var + EPS) * lng_ref[...] + lnb_ref[...]
    o_ref[r0:r0 + CONV_CHUNK, :] = _silu(y).astype(BF16)


def _inproj_kernel(x_ref, ada_ref, g_ref, w_ref, cw_ref, cb_ref, lng_ref, lnb_ref,
                   qkv_ref, u_ref, gate_ref, ubuf):
    TS, H, C = ROW_TILE, CONV_HALO, CONV_WIDTH
    i = pl.program_id(1)

    @pl.when(i == 0)
    def _():
        ubuf[0, 0:H, :] = jnp.zeros((H, C), F32)

    @pl.when(i > 0)
    def _():
        ubuf[0, 0:H, :] = ubuf[0, TS:TS + H, :]

    x = x_ref[...]
    ms = jnp.mean(x * x, axis=-1, keepdims=True)
    y = x * lax.rsqrt(ms + EPS)
    h = (y * g_ref[...]) * (1.0 + ada_ref[1:2, :]) + ada_ref[0:1, :]
    hb = h.astype(BF16)

    def proj(base):
        return jnp.dot(hb, w_ref[:, base:base + PROJ_CHUNK], preferred_element_type=F32)

    ubuf[0, H:H + TS, :] = proj(QKV_COLS) * _sigmoid(proj(QKV_COLS + C))
    n = TS + H - 8
    for b in range(1, 8):
        ubuf[b, 0:n, :] = ubuf[0, b:b + n, :]
    jobs = ([(qkv_ref, c, c) for c in range(0, QKV_COLS, PROJ_CHUNK)]
            + [(gate_ref, c, QKV_COLS + GLU_COLS + c) for c in range(0, GATE_COLS, PROJ_CHUNK)])
    rows = list(range(0, TS, CONV_CHUNK))
    for n, (ref, c, base) in enumerate(jobs):
        r = proj(base)
        if base < ATTN_WIDTH:
            r = r * Q_SCALE
        ref[:, c:c + PROJ_CHUNK] = r.astype(BF16)
        for r0 in rows[n * len(rows) // len(jobs):(n + 1) * len(rows) // len(jobs)]:
            _conv_chunk(r0, ubuf, cw_ref, cb_ref, lng_ref, lnb_ref, u_ref)


def _in_proj(x, ada3, g, w, cw, cb, lng, lnb):
    bsz, s, d = x.shape
    tm = ROW_TILE
    C = CONV_WIDTH
    row = lambda n: pl.BlockSpec((None, tm, n), lambda b, i: (b, i, 0))
    vec = lambda a: a.reshape(1, C)
    return pl.pallas_call(
        _inproj_kernel,
        grid=(bsz, s // tm),
        in_specs=[row(d),
                  pl.BlockSpec((None, 6, d), lambda b, i: (b, 0, 0)),
                  _resident((1, d)),
                  _resident((d, IN_COLS)),
                  _resident((CONV_KERNEL, 8, C)),
                  _resident((1, C)), _resident((1, C)), _resident((1, C))],
        out_specs=[row(QKV_COLS), row(C), row(GATE_COLS)],
        out_shape=[jax.ShapeDtypeStruct((bsz, s, QKV_COLS), BF16),
                   jax.ShapeDtypeStruct((bsz, s, C), BF16),
                   jax.ShapeDtypeStruct((bsz, s, GATE_COLS), BF16)],
        scratch_shapes=[pltpu.VMEM((8, tm + CONV_HALO, C), F32)],
        compiler_params=_params(("arbitrary", "arbitrary")),
        name="in_proj",
    )(x, ada3, g.reshape(1, d), w, jnp.broadcast_to(cw[:, None, :], (CONV_KERNEL, 8, C)),
      vec(cb), vec(lng), vec(lnb))


def _attn_kernel(lamv_ref, subg_ref, q_ref, k_ref, v_ref, bias_ref, *rest):
    n_cast = (len(rest) - 5) // 2
    cast_in, o_ref, cast_out = rest[:n_cast], rest[n_cast], rest[n_cast + 1:2 * n_cast + 1]
    vt_ref, qz_ref, m_ref, a_ref = rest[2 * n_cast + 1:]
    T = ATTN_TILE
    NS = 2 * N_HEADS_A
    n_q = q_ref.shape[0] // T
    hcols = lambda h: slice(h * HEAD_V, (h + 1) * HEAD_V)
    rows = lambda t: slice(t * T, (t + 1) * T)

    vt = v_ref[...].T
    for h in range(N_HEADS_A):
        vt_ref[h, 0:HEAD_V, :] = vt[hcols(h), :]
        vt_ref[h, HEAD_V:, :] = jnp.ones((ONES_ROWS, vt.shape[1]), BF16)

    lv = lamv_ref[...]
    lam = (jnp.exp(jnp.sum(lv[0:1] * lv[1:2], axis=-1, keepdims=True))
           - jnp.exp(jnp.sum(lv[2:3] * lv[3:4], axis=-1, keepdims=True)) + LAM_INIT)
    lane = lax.broadcasted_iota(jnp.int32, (T, HEAD_V), 1)

    def prepare(i):
        slot = i % 2
        for h in range(N_HEADS_A):
            qs = q_ref[rows(i), hcols(h)]
            zero = jnp.zeros_like(qs)
            qz_ref[slot, 2 * h] = jnp.where(lane < HEAD_QK, qs, zero)
            qz_ref[slot, 2 * h + 1] = jnp.where(lane >= HEAD_QK, qs, zero)

    def finalize(i):
        slot = i % 2
        for h in range(N_HEADS_A):
            a1, a2 = a_ref[slot, 2 * h], a_ref[slot, 2 * h + 1]
            o = (a1[:HEAD_V] / a1[HEAD_V:HEAD_V + 1]
                 - lam * (a2[:HEAD_V] / a2[HEAD_V:HEAD_V + 1]))
            ms = jnp.mean(o * o, axis=0, keepdims=True)
            y = (o * lax.rsqrt(ms + EPS)).T
            o_ref[rows(i), hcols(h)] = ((y * subg_ref[...]) * (1.0 - LAM_INIT)).astype(BF16)

    units = [(i, j, n) for i in range(n_q) for j in range(i + 1) for n in range(NS)]
    scores, probs, alphas = {}, {}, {}

    def score(u):
        i, j, n = units[u]
        kc = k_ref[rows(j), hcols(n // 2)]
        scores[u] = lax.dot_general(kc, qz_ref[i % 2, n], (((1,), (1,)), ((), ())),
                                    preferred_element_type=F32)

    def softmax(u):
        i, j, n = units[u]
        s = scores.pop(u)
        if j >= i - 1:
            s = s + bias_ref[n // 2, j - (i - 1)]
        m_new = jnp.max(s, axis=0, keepdims=True)
        if j > 0:
            m_old = m_ref[i % 2, n]
            m_new = jnp.maximum(m_old, m_new)
            alphas[u] = jnp.exp2(m_old - m_new)
        probs[u] = jnp.exp2(s - m_new).astype(BF16)
        m_ref[i % 2, n] = m_new

    def value(u):
        i, j, n = units[u]
        vc = vt_ref[n // 2, :, rows(j)]
        pv = jnp.dot(vc, probs.pop(u), preferred_element_type=F32)
        a_ref[i % 2, n] = pv if j == 0 else alphas.pop(u) * a_ref[i % 2, n] + pv

    for step in range(len(units) + SCORE_LEAD + VALUE_LAG):
        if step < len(units):
            if units[step][1:] == (0, 0):
                prepare(units[step][0])
            score(step)
        if 0 <= step - SCORE_LEAD < len(units):
            softmax(step - SCORE_LEAD)
        done = step - SCORE_LEAD - VALUE_LAG
        if 0 <= done < len(units):
            value(done)
            i, j, n = units[done]
            if (j, n) == (i, NS - 1):
                finalize(i)

    for w_ref, wb_ref in zip(cast_in, cast_out):
        wb_ref[...] = w_ref[...].astype(BF16)


def _attention(qkv, bias_tiles, lamv, subg, weights):
    bsz, s, _ = qkv.shape
    T = ATTN_TILE
    W = ATTN_WIDTH
    NS = 2 * N_HEADS_A
    steps = bsz

    def slab_spec(w, squeeze):
        rows, cols = w.shape[-2:]
        assert rows % (steps * CAST_ROWS) == 0
        block = (rows // steps, cols)
        if squeeze:
            return pl.BlockSpec((None,) + block, lambda b: (0, b, 0))
        return pl.BlockSpec(block, lambda b: (b, 0))

    seq = lambda col: pl.BlockSpec((None, s, W), lambda b: (b, 0, col))
    out = pl.pallas_call(
        _attn_kernel,
        grid=(bsz,),
        in_specs=[_resident((4, HEAD_QK)),
                  _resident((1, HEAD_V)),
                  seq(0), seq(1), seq(2),
                  _resident((N_HEADS_A, 2, T, T))] + [slab_spec(w, True) for w in weights],
        out_specs=[seq(0)] + [slab_spec(w, False) for w in weights],
        out_shape=[jax.ShapeDtypeStruct((bsz, s, W), BF16)]
        + [jax.ShapeDtypeStruct(w.shape[-2:], BF16) for w in weights],
        scratch_shapes=[pltpu.VMEM((N_HEADS_A, HEAD_V + ONES_ROWS, s), BF16),
                        pltpu.VMEM((2, NS, T, HEAD_V), BF16),
                        pltpu.VMEM((2, NS, 1, T), F32),
                        pltpu.VMEM((2, NS, HEAD_V + ONES_ROWS, T), F32)],
        compiler_params=_params(("arbitrary",)),
        name="diff_attn",
    )(lamv, subg.reshape(1, HEAD_V), qkv, qkv, qkv, bias_tiles, *weights)
    return out[0], out[1:]


def _mixffn_kernel(x_ref, ada_ref, a_ref, u_ref, gate_ref, woa_ref, woc_ref, boc_ref,
                   wout_ref, g2_ref, w1_ref, w2_ref, gf_ref, o_ref,
                   x1_ref, h2_ref, act_ref):
    D = D_MODEL
    a = jnp.dot(a_ref[...], woa_ref[...], preferred_element_type=F32)
    cv = jnp.dot(u_ref[...], woc_ref[...], preferred_element_type=F32) + boc_ref[...]
    gate = gate_ref[...]
    y = (_sigmoid(gate[:, :D].astype(F32)) * a
         + _sigmoid(gate[:, D:].astype(F32)) * cv)
    z = jnp.dot(y.astype(BF16), wout_ref[...], preferred_element_type=F32)
    x1 = x_ref[...] + ada_ref[2:3, :] * z
    x1_ref[...] = x1
    ms = jnp.mean(x1 * x1, axis=-1, keepdims=True)
    yn = x1 * lax.rsqrt(ms + EPS)
    h2_ref[...] = ((yn * g2_ref[...]) * (1.0 + ada_ref[4:5, :]) + ada_ref[3:4, :]).astype(BF16)

    h = h2_ref[...]
    for c in range(0, D_FF, FF_CHUNK):
        fg = jnp.dot(h, w1_ref[:, c:c + FF_CHUNK], preferred_element_type=F32)
        fu = jnp.dot(h, w1_ref[:, D_FF + c:D_FF + c + FF_CHUNK],
                     preferred_element_type=F32)
        act_ref[:, c:c + FF_CHUNK] = (_silu(fg) * fu).astype(BF16)
    z2 = jnp.dot(act_ref[...], w2_ref[...], preferred_element_type=F32)
    x2 = x1_ref[...] + ada_ref[5:6, :] * z2
    ms2 = jnp.mean(x2 * x2, axis=-1, keepdims=True)
    o_ref[...] = (x2 * lax.rsqrt(ms2 + EPS)) * gf_ref[...]


def _mix_ffn(x, ada3, a_n, u, gates, woa, woc, boc, wout, g2, w1, w2, gf):
    bsz, s, d = x.shape
    tm = ROW_TILE
    row = lambda n: pl.BlockSpec((None, tm, n), lambda b, i: (b, i, 0))
    return pl.pallas_call(
        _mixffn_kernel,
        grid=(bsz, s // tm),
        in_specs=[row(d),
                  pl.BlockSpec((None, 6, d), lambda b, i: (b, 0, 0)),
                  row(ATTN_WIDTH), row(CONV_WIDTH), row(GATE_COLS),
                  _resident((ATTN_WIDTH, d)), _resident((CONV_WIDTH, d)),
                  _resident((1, d)), _resident((d, d)), _resident((1, d)),
                  _resident((d, 2 * D_FF)), _resident((D_FF, d)), _resident((1, d))],
        out_specs=row(d),
        out_shape=jax.ShapeDtypeStruct((bsz, s, d), F32),
        scratch_shapes=[pltpu.VMEM((tm, d), F32), pltpu.VMEM((tm, d), BF16),
                        pltpu.VMEM((tm, D_FF), BF16)],
        compiler_params=_params(("parallel", "arbitrary")),
        name="mix_ffn",
    )(x, ada3, a_n, u, gates, woa, woc, boc.reshape(1, d), wout, g2.reshape(1, d),
      w1, w2, gf.reshape(1, d))


def kernel(x, c, w_ada, b_ada, norm1_g, norm2_g, final_g, w_in, lambda_q1, lambda_k1,
           lambda_q2, lambda_k2, rel_bias, attn_sub_g, w_o_attn, conv_w, conv_b,
           conv_ln_g, conv_ln_b, w_o_conv, b_o_conv, w_out, w_ffn_in, w_ffn_out):
    bsz, s, d = x.shape
    assert w_ada.shape[0] == 1, "single-layer block"
    assert (d, s % ROW_TILE, s % ATTN_TILE) == (D_MODEL, 0, 0)
    assert PROJ_CHUNK == CONV_WIDTH and ATTN_WIDTH % PROJ_CHUNK == 0
    l = 0
    ada, w_in_bf, bias_tiles = _prologue(c, w_ada, b_ada, w_in, rel_bias)
    ada3 = ada.reshape(bsz, 6, d)
    qkv, u, gates = _in_proj(x, ada3, norm1_g[l], w_in_bf, conv_w[l],
                             conv_b[l], conv_ln_g[l], conv_ln_b[l])
    lamv = jnp.stack([lambda_q1[l], lambda_k1[l], lambda_q2[l], lambda_k2[l]])
    a_n, (woa, woc, wout, w1, w2) = _attention(
        qkv, bias_tiles, lamv, attn_sub_g[l],
        [w_o_attn, w_o_conv, w_out, w_ffn_in, w_ffn_out])
    return _mix_ffn(x, ada3, a_n, u, gates, woa, woc, b_o_conv[l], wout, norm2_g[l],
                    w1, w2, final_g)
```

```python
import math

import jax
import jax.numpy as jnp
from jax import lax
from jax.experimental import pallas as pl
from jax.experimental.pallas import tpu as pltpu

D_MODEL = 1024
N_HEADS_A = 4
HEAD_QK = 64
HEAD_V = 2 * HEAD_QK
ATTN_WIDTH = N_HEADS_A * HEAD_V
CONV_WIDTH = 512
CONV_KERNEL = 31
QKV_COLS = 3 * ATTN_WIDTH
GLU_COLS = 2 * CONV_WIDTH
GATE_COLS = 2 * D_MODEL
IN_COLS = QKV_COLS + GLU_COLS + GATE_COLS
D_FF = 2816
N_BUCKETS = 32
MAX_EXACT = 16
MAX_DISTANCE = 128
EPS = 1e-6
NEG_INF = -1e30
LAM_INIT = 0.8 - 0.6 * math.exp(-0.3 * 0)

V7X_VMEM_BYTES = 64 * 1024 * 1024
VMEM_LIMIT = V7X_VMEM_BYTES - 12 * 1024 * 1024

ROW_TILE = 512
ATTN_TILE = 256
SCORE_LEAD = 2
VALUE_LAG = 6
ONES_ROWS = 16
CAST_ROWS = 16
LOG2E = math.log2(math.e)
Q_SCALE = HEAD_QK ** -0.5 * LOG2E
CONV_HALO = 32
CONV_CHUNK = 16
PROJ_CHUNK = 512
FF_CHUNK = 256

BF16 = jnp.bfloat16
F32 = jnp.float32


def _sigmoid(x):
    return 1.0 / (1.0 + jnp.exp(-x))


def _silu(x):
    return x * _sigmoid(x)


def _params(sem, vmem=VMEM_LIMIT, flags=None):
    return pltpu.CompilerParams(dimension_semantics=sem, vmem_limit_bytes=vmem, flags=flags)


def _resident(shape):
    nd = len(shape)
    return pl.BlockSpec(shape, lambda *_: (0,) * nd, pipeline_mode=pl.Buffered(1))


def _bias_tile(rel_ref, h, t):
    T = ATTN_TILE
    key = lax.broadcasted_iota(jnp.int32, (T, T), 0)
    qry = lax.broadcasted_iota(jnp.int32, (T, T), 1)
    dist = (1 - t) * T + qry - key
    n = jnp.maximum(dist, 0)
    large = MAX_EXACT + (jnp.log(jnp.maximum(n, 1).astype(F32) / MAX_EXACT)
                         / math.log(MAX_DISTANCE / MAX_EXACT)
                         * (N_BUCKETS - MAX_EXACT)).astype(jnp.int32)
    large = jnp.minimum(large, N_BUCKETS - 1)
    bucket = jnp.where(n < MAX_EXACT, n, large)
    far = rel_ref[N_BUCKETS - 1, h]
    bias = jnp.zeros((T, T), F32)
    for b in range(N_BUCKETS - 1):
        bias = jnp.where(bucket == b, (rel_ref[b, h] - far) * LOG2E, bias)
    return jnp.where(dist >= 0, bias, NEG_INF)


def _prologue_kernel(rel_ref, c_ref, w_ref, b_ref, win_ref, o_ref, win_bf_ref, bias_ref):
    j = pl.program_id(0)
    ca = _silu(c_ref[...])
    o_ref[...] = jnp.dot(ca, w_ref[...], preferred_element_type=F32,
                         precision=lax.Precision.HIGHEST) + b_ref[...]
    win_bf_ref[...] = win_ref[...].astype(BF16)
    bias_ref[...] = _bias_tile(rel_ref, j // 2, j % 2)


def _prologue(c, w, b, w_in, rel_bias):
    bsz, d = c.shape
    n = w.shape[-1]
    steps = 2 * N_HEADS_A
    tn = n // steps
    _, rows, cols = w_in.shape
    tr = rows // steps
    T = ATTN_TILE
    return pl.pallas_call(
        _prologue_kernel,
        grid=(steps,),
        in_specs=[pl.BlockSpec(memory_space=pltpu.SMEM),
                  pl.BlockSpec((bsz, d), lambda j: (0, 0)),
                  pl.BlockSpec((None, d, tn), lambda j: (0, 0, j)),
                  pl.BlockSpec((1, tn), lambda j: (0, j)),
                  pl.BlockSpec((None, tr, cols), lambda j: (0, j, 0))],
        out_specs=[pl.BlockSpec((bsz, tn), lambda j: (0, j)),
                   pl.BlockSpec((tr, cols), lambda j: (j, 0)),
                   pl.BlockSpec((None, None, T, T), lambda j: (j // 2, j % 2, 0, 0))],
        out_shape=[jax.ShapeDtypeStruct((bsz, n), F32),
                   jax.ShapeDtypeStruct((rows, cols), BF16),
                   jax.ShapeDtypeStruct((N_HEADS_A, 2, T, T), F32)],
        compiler_params=_params(("arbitrary",)),
        name="prologue",
    )(rel_bias, c, w, b, w_in)


def _conv_chunk(r0, ubuf, w_ref, cb_ref, lng_ref, lnb_ref, o_ref):
    off = CONV_HALO - (CONV_KERNEL - 1)
    acc = jnp.zeros((CONV_CHUNK // 8, 8, CONV_WIDTH), F32)
    for k in range(CONV_KERNEL):
        a, b = divmod(k + off, 8)
        rows = ubuf[b, r0 + 8 * a:r0 + 8 * a + CONV_CHUNK, :]
        acc = acc + rows.reshape(acc.shape) * w_ref[k][None]
    u = acc.reshape(CONV_CHUNK, CONV_WIDTH) + cb_ref[...]
    mu = jnp.mean(u, axis=-1, keepdims=True)
    d = u - mu
    var = jnp.mean(d * d, axis=-1, keepdims=True)
    y = d * lax.rsqrt(var + EPS) * lng_ref[...] + lnb_ref[...]
    o_ref[r0:r0 + CONV_CHUNK, :] = _silu(y).astype(BF16)


def _inproj_kernel(x_ref, ada_ref, g_ref, w_ref, cw_ref, cb_ref, lng_ref, lnb_ref,
                   qkv_ref, u_ref, gate_ref, ubuf):
    TS, H, C = ROW_TILE, CONV_HALO, CONV_WIDTH
    i = pl.program_id(1)

    @pl.when(i == 0)
    def _():
        ubuf[0, 0:H, :] = jnp.zeros((H, C), F32)

    @pl.when(i > 0)
    def _():
        ubuf[0, 0:H, :] = ubuf[0, TS:TS + H, :]

    x = x_ref[...]
    ms = jnp.mean(x * x, axis=-1, keepdims=True)
    y = x * lax.rsqrt(ms + EPS)
    h = (y * g_ref[...]) * (1.0 + ada_ref[1:2, :]) + ada_ref[0:1, :]
    hb = h.astype(BF16)

    def proj(base):
        return jnp.dot(hb, w_ref[:, base:base + PROJ_CHUNK], preferred_element_type=F32)

    ubuf[0, H:H + TS, :] = proj(QKV_COLS) * _sigmoid(proj(QKV_COLS + C))
    n = TS + H - 8
    for b in range(1, 8):
        ubuf[b, 0:n, :] = ubuf[0, b:b + n, :]
    jobs = ([(qkv_ref, c, c) for c in range(0, QKV_COLS, PROJ_CHUNK)]
            + [(gate_ref, c, QKV_COLS + GLU_COLS + c) for c in range(0, GATE_COLS, PROJ_CHUNK)])
    rows = list(range(0, TS, CONV_CHUNK))
    for n, (ref, c, base) in enumerate(jobs):
        r = proj(base)
        if base < ATTN_WIDTH:
            r = r * Q_SCALE
        ref[:, c:c + PROJ_CHUNK] = r.astype(BF16)
        for r0 in rows[n * len(rows) // len(jobs):(n + 1) * len(rows) // len(jobs)]:
            _conv_chunk(r0, ubuf, cw_ref, cb_ref, lng_ref, lnb_ref, u_ref)


def _in_proj(x, ada3, g, w, cw, cb, lng, lnb):
    bsz, s, d = x.shape
    tm = ROW_TILE
    C = CONV_WIDTH
    row = lambda n: pl.BlockSpec((None, tm, n), lambda b, i: (b, i, 0))
    vec = lambda a: a.reshape(1, C)
    return pl.pallas_call(
        _inproj_kernel,
        grid=(bsz, s // tm),
        in_specs=[row(d),
                  pl.BlockSpec((None, 6, d), lambda b, i: (b, 0, 0)),
                  _resident((1, d)),
                  _resident((d, IN_COLS)),
                  _resident((CONV_KERNEL, 8, C)),
                  _resident((1, C)), _resident((1, C)), _resident((1, C))],
        out_specs=[row(QKV_COLS), row(C), row(GATE_COLS)],
        out_shape=[jax.ShapeDtypeStruct((bsz, s, QKV_COLS), BF16),
                   jax.ShapeDtypeStruct((bsz, s, C), BF16),
                   jax.ShapeDtypeStruct((bsz, s, GATE_COLS), BF16)],
        scratch_shapes=[pltpu.VMEM((8, tm + CONV_HALO, C), F32)],
        compiler_params=_params(("arbitrary", "arbitrary")),
        name="in_proj",
    )(x, ada3, g.reshape(1, d), w, jnp.broadcast_to(cw[:, None, :], (CONV_KERNEL, 8, C)),
      vec(cb), vec(lng), vec(lnb))


def _attn_kernel(lamv_ref, subg_ref, q_ref, k_ref, v_ref, bias_ref, *rest):
    n_cast = (len(rest) - 5) // 2
    cast_in, o_ref, cast_out = rest[:n_cast], rest[n_cast], rest[n_cast + 1:2 * n_cast + 1]
    vt_ref, qz_ref, m_ref, a_ref = rest[2 * n_cast + 1:]
    T = ATTN_TILE
    NS = 2 * N_HEADS_A
    n_q = q_ref.shape[0] // T
    hcols = lambda h: slice(h * HEAD_V, (h + 1) * HEAD_V)
    rows = lambda t: slice(t * T, (t + 1) * T)

    vt = v_ref[...].T
    for h in range(N_HEADS_A):
        vt_ref[h, 0:HEAD_V, :] = vt[hcols(h), :]
        vt_ref[h, HEAD_V:, :] = jnp.ones((ONES_ROWS, vt.shape[1]), BF16)

    lv = lamv_ref[...]
    lam = (jnp.exp(jnp.sum(lv[0:1] * lv[1:2], axis=-1, keepdims=True))
           - jnp.exp(jnp.sum(lv[2:3] * lv[3:4], axis=-1, keepdims=True)) + LAM_INIT)
    lane = lax.broadcasted_iota(jnp.int32, (T, HEAD_V), 1)

    def prepare(i):
        slot = i % 2
        for h in range(N_HEADS_A):
            qs = q_ref[rows(i), hcols(h)]
            zero = jnp.zeros_like(qs)
            qz_ref[slot, 2 * h] = jnp.where(lane < HEAD_QK, qs, zero)
            qz_ref[slot, 2 * h + 1] = jnp.where(lane >= HEAD_QK, qs, zero)

    def finalize(i):
        slot = i % 2
        for h in range(N_HEADS_A):
            a1, a2 = a_ref[slot, 2 * h], a_ref[slot, 2 * h + 1]
            o = (a1[:HEAD_V] / a1[HEAD_V:HEAD_V + 1]
                 - lam * (a2[:HEAD_V] / a2[HEAD_V:HEAD_V + 1]))
            ms = jnp.mean(o * o, axis=0, keepdims=True)
            y = (o * lax.rsqrt(ms + EPS)).T
            o_ref[rows(i), hcols(h)] = ((y * subg_ref[...]) * (1.0 - LAM_INIT)).astype(BF16)

    units = [(i, j, n) for i in range(n_q) for j in range(i + 1) for n in range(NS)]
    scores, probs, alphas = {}, {}, {}

    def score(u):
        i, j, n = units[u]
        kc = k_ref[rows(j), hcols(n // 2)]
        scores[u] = lax.dot_general(kc, qz_ref[i % 2, n], (((1,), (1,)), ((), ())),
                                    preferred_element_type=F32)

    def softmax(u):
        i, j, n = units[u]
        s = scores.pop(u)
        if j >= i - 1:
            s = s + bias_ref[n // 2, j - (i - 1)]
        m_new = jnp.max(s, axis=0, keepdims=True)
        if j > 0:
            m_old = m_ref[i % 2, n]
            m_new = jnp.maximum(m_old, m_new)
            alphas[u] = jnp.exp2(m_old - m_new)
        probs[u] = jnp.exp2(s - m_new).astype(BF16)
        m_ref[i % 2, n] = m_new

    def value(u):
        i, j, n = units[u]
        vc = vt_ref[n // 2, :, rows(j)]
        pv = jnp.dot(vc, probs.pop(u), preferred_element_type=F32)
        a_ref[i % 2, n] = pv if j == 0 else alphas.pop(u) * a_ref[i % 2, n] + pv

    for step in range(len(units) + SCORE_LEAD + VALUE_LAG):
        if step < len(units):
            if units[step][1:] == (0, 0):
                prepare(units[step][0])
            score(step)
        if 0 <= step - SCORE_LEAD < len(units):
            softmax(step - SCORE_LEAD)
        done = step - SCORE_LEAD - VALUE_LAG
        if 0 <= done < len(units):
            value(done)
            i, j, n = units[done]
            if (j, n) == (i, NS - 1):
                finalize(i)

    for w_ref, wb_ref in zip(cast_in, cast_out):
        wb_ref[...] = w_ref[...].astype(BF16)


def _attention(qkv, bias_tiles, lamv, subg, weights):
    bsz, s, _ = qkv.shape
    T = ATTN_TILE
    W = ATTN_WIDTH
    NS = 2 * N_HEADS_A
    steps = bsz

    def slab_spec(w, squeeze):
        rows, cols = w.shape[-2:]
        assert rows % (steps * CAST_ROWS) == 0
        block = (rows // steps, cols)
        if squeeze:
            return pl.BlockSpec((None,) + block, lambda b: (0, b, 0))
        return pl.BlockSpec(block, lambda b: (b, 0))

    seq = lambda col: pl.BlockSpec((None, s, W), lambda b: (b, 0, col))
    out = pl.pallas_call(
        _attn_kernel,
        grid=(bsz,),
        in_specs=[_resident((4, HEAD_QK)),
                  _resident((1, HEAD_V)),
                  seq(0), seq(1), seq(2),
                  _resident((N_HEADS_A, 2, T, T))] + [slab_spec(w, True) for w in weights],
        out_specs=[seq(0)] + [slab_spec(w, False) for w in weights],
        out_shape=[jax.ShapeDtypeStruct((bsz, s, W), BF16)]
        + [jax.ShapeDtypeStruct(w.shape[-2:], BF16) for w in weights],
        scratch_shapes=[pltpu.VMEM((N_HEADS_A, HEAD_V + ONES_ROWS, s), BF16),
                        pltpu.VMEM((2, NS, T, HEAD_V), BF16),
                        pltpu.VMEM((2, NS, 1, T), F32),
                        pltpu.VMEM((2, NS, HEAD_V + ONES_ROWS, T), F32)],
        compiler_params=_params(("arbitrary",)),
        name="diff_attn",
    )(lamv, subg.reshape(1, HEAD_V), qkv, qkv, qkv, bias_tiles, *weights)
    return out[0], out[1:]


def _mixffn_kernel(x_ref, ada_ref, a_ref, u_ref, gate_ref, woa_ref, woc_ref, boc_ref,
                   wout_ref, g2_ref, w1_ref, w2_ref, gf_ref, o_ref,
                   x1_ref, h2_ref, act_ref):
    D = D_MODEL
    a = jnp.dot(a_ref[...], woa_ref[...], preferred_element_type=F32)
    cv = jnp.dot(u_ref[...], woc_ref[...], preferred_element_type=F32) + boc_ref[...]
    gate = gate_ref[...]
    y = (_sigmoid(gate[:, :D].astype(F32)) * a
         + _sigmoid(gate[:, D:].astype(F32)) * cv)
    z = jnp.dot(y.astype(BF16), wout_ref[...], preferred_element_type=F32)
    x1 = x_ref[...] + ada_ref[2:3, :] * z
    x1_ref[...] = x1
    ms = jnp.mean(x1 * x1, axis=-1, keepdims=True)
    yn = x1 * lax.rsqrt(ms + EPS)
    h2_ref[...] = ((yn * g2_ref[...]) * (1.0 + ada_ref[4:5, :]) + ada_ref[3:4, :]).astype(BF16)

    h = h2_ref[...]
    for c in range(0, D_FF, FF_CHUNK):
        fg = jnp.dot(h, w1_ref[:, c:c + FF_CHUNK], preferred_element_type=F32)
        fu = jnp.dot(h, w1_ref[:, D_FF + c:D_FF + c + FF_CHUNK],
                     preferred_element_type=F32)
        act_ref[:, c:c + FF_CHUNK] = (_silu(fg) * fu).astype(BF16)
    z2 = jnp.dot(act_ref[...], w2_ref[...], preferred_element_type=F32)
    x2 = x1_ref[...] + ada_ref[5:6, :] * z2
    ms2 = jnp.mean(x2 * x2, axis=-1, keepdims=True)
    o_ref[...] = (x2 * lax.rsqrt(ms2 + EPS)) * gf_ref[...]


def _mix_ffn(x, ada3, a_n, u, gates, woa, woc, boc, wout, g2, w1, w2, gf):
    bsz, s, d = x.shape
    tm = ROW_TILE
    row = lambda n: pl.BlockSpec((None, tm, n), lambda b, i: (b, i, 0))
    return pl.pallas_call(
        _mixffn_kernel,
        grid=(bsz, s // tm),
        in_specs=[row(d),
                  pl.BlockSpec((None, 6, d), lambda b, i: (b, 0, 0)),
                  row(ATTN_WIDTH), row(CONV_WIDTH), row(GATE_COLS),
                  _resident((ATTN_WIDTH, d)), _resident((CONV_WIDTH, d)),
                  _resident((1, d)), _resident((d, d)), _resident((1, d)),
                  _resident((d, 2 * D_FF)), _resident((D_FF, d)), _resident((1, d))],
        out_specs=row(d),
        out_shape=jax.ShapeDtypeStruct((bsz, s, d), F32),
        scratch_shapes=[pltpu.VMEM((tm, d), F32), pltpu.VMEM((tm, d), BF16),
                        pltpu.VMEM((tm, D_FF), BF16)],
        compiler_params=_params(("parallel", "arbitrary")),
        name="mix_ffn",
    )(x, ada3, a_n, u, gates, woa, woc, boc.reshape(1, d), wout, g2.reshape(1, d),
      w1, w2, gf.reshape(1, d))


def kernel(x, c, w_ada, b_ada, norm1_g, norm2_g, final_g, w_in, lambda_q1, lambda_k1,
           lambda_q2, lambda_k2, rel_bias, attn_sub_g, w_o_attn, conv_w, conv_b,
           conv_ln_g, conv_ln_b, w_o_conv, b_o_conv, w_out, w_ffn_in, w_ffn_out):
    bsz, s, d = x.shape
    assert w_ada.shape[0] == 1, "single-layer block"
    assert (d, s % ROW_TILE, s % ATTN_TILE) == (D_MODEL, 0, 0)
    assert PROJ_CHUNK == CONV_WIDTH and ATTN_WIDTH % PROJ_CHUNK == 0
    l = 0
    ada, w_in_bf, bias_tiles = _prologue(c, w_ada, b_ada, w_in, rel_bias)
    ada3 = ada.reshape(bsz, 6, d)
    qkv, u, gates = _in_proj(x, ada3, norm1_g[l], w_in_bf, conv_w[l],
                             conv_b[l], conv_ln_g[l], conv_ln_b[l])
    lamv = jnp.stack([lambda_q1[l], lambda_k1[l], lambda_q2[l], lambda_k2[l]])
    a_n, (woa, woc, wout, w1, w2) = _attention(
        qkv, bias_tiles, lamv, attn_sub_g[l],
        [w_o_attn, w_o_conv, w_out, w_ffn_in, w_ffn_out])
    return _mix_ffn(x, ada3, a_n, u, gates, woa, woc, b_o_conv[l], wout, norm2_g[l],
                    w1, w2, final_g)
```

```python
import math

import jax
import jax.numpy as jnp
from jax import lax
from jax.experimental import pallas as pl
from jax.experimental.pallas import tpu as pltpu

D_MODEL = 1024
N_HEADS_A = 4
HEAD_QK = 64
HEAD_V = 2 * HEAD_QK
ATTN_WIDTH = N_HEADS_A * HEAD_V
CONV_WIDTH = 512
CONV_KERNEL = 31
QKV_COLS = 3 * ATTN_WIDTH
GLU_COLS = 2 * CONV_WIDTH
GATE_COLS = 2 * D_MODEL
IN_COLS = QKV_COLS + GLU_COLS + GATE_COLS
D_FF = 2816
N_BUCKETS = 32
MAX_EXACT = 16
MAX_DISTANCE = 128
EPS = 1e-6
NEG_INF = -1e30
LAM_INIT = 0.8 - 0.6 * math.exp(-0.3 * 0)

V7X_VMEM_BYTES = 64 * 1024 * 1024
VMEM_LIMIT = V7X_VMEM_BYTES - 12 * 1024 * 1024

ROW_TILE = 512
ATTN_TILE = 256
ATTN_LAG = 4
ONES_ROWS = 16
CAST_ROWS = 16
LOG2E = math.log2(math.e)
Q_SCALE = HEAD_QK ** -0.5 * LOG2E
CONV_HALO = 32
CONV_CHUNK = 16
PROJ_CHUNK = 512
FF_CHUNK = 256

BF16 = jnp.bfloat16
F32 = jnp.float32


def _sigmoid(x):
    return 1.0 / (1.0 + jnp.exp2(x * (-LOG2E)))


def _silu(x):
    return x * _sigmoid(x)


def _params(sem, vmem=VMEM_LIMIT, flags=None):
    return pltpu.CompilerParams(dimension_semantics=sem, vmem_limit_bytes=vmem, flags=flags)


def _resident(shape):
    nd = len(shape)
    return pl.BlockSpec(shape, lambda *_: (0,) * nd, pipeline_mode=pl.Buffered(1))


def _bias_tile(rel_ref, h, t):
    T = ATTN_TILE
    key = lax.broadcasted_iota(jnp.int32, (T, T), 0)
    qry = lax.broadcasted_iota(jnp.int32, (T, T), 1)
    dist = (1 - t) * T + qry - key
    n = jnp.maximum(dist, 0)
    large = MAX_EXACT + (jnp.log(jnp.maximum(n, 1).astype(F32) / MAX_EXACT)
                         / math.log(MAX_DISTANCE / MAX_EXACT)
                         * (N_BUCKETS - MAX_EXACT)).astype(jnp.int32)
    large = jnp.minimum(large, N_BUCKETS - 1)
    bucket = jnp.where(n < MAX_EXACT, n, large)
    far = rel_ref[N_BUCKETS - 1, h]
    bias = jnp.zeros((T, T), F32)
    for b in range(N_BUCKETS - 1):
        bias = jnp.where(bucket == b, (rel_ref[b, h] - far) * LOG2E, bias)
    return jnp.where(dist >= 0, bias, NEG_INF)


def _prologue_kernel(rel_ref, c_ref, w_ref, b_ref, win_ref, o_ref, win_bf_ref, bias_ref):
    j = pl.program_id(0)
    ca = _silu(c_ref[...])
    o_ref[...] = jnp.dot(ca, w_ref[...], preferred_element_type=F32,
                         precision=lax.Precision.HIGHEST) + b_ref[...]
    win_bf_ref[...] = win_ref[...].astype(BF16)
    bias_ref[...] = _bias_tile(rel_ref, j // 2, j % 2)


def _prologue(c, w, b, w_in, rel_bias):
    bsz, d = c.shape
    n = w.shape[-1]
    steps = 2 * N_HEADS_A
    tn = n // steps
    _, rows, cols = w_in.shape
    tr = rows // steps
    T = ATTN_TILE
    return pl.pallas_call(
        _prologue_kernel,
        grid=(steps,),
        in_specs=[pl.BlockSpec(memory_space=pltpu.SMEM),
                  pl.BlockSpec((bsz, d), lambda j: (0, 0)),
                  pl.BlockSpec((None, d, tn), lambda j: (0, 0, j)),
                  pl.BlockSpec((1, tn), lambda j: (0, j)),
                  pl.BlockSpec((None, tr, cols), lambda j: (0, j, 0))],
        out_specs=[pl.BlockSpec((bsz, tn), lambda j: (0, j)),
                   pl.BlockSpec((tr, cols), lambda j: (j, 0)),
                   pl.BlockSpec((None, None, T, T), lambda j: (j // 2, j % 2, 0, 0))],
        out_shape=[jax.ShapeDtypeStruct((bsz, n), F32),
                   jax.ShapeDtypeStruct((rows, cols), BF16),
                   jax.ShapeDtypeStruct((N_HEADS_A, 2, T, T), F32)],
        compiler_params=_params(("arbitrary",)),
        name="prologue",
    )(rel_bias, c, w, b, w_in)


def _conv_chunk(r0, ubuf, w_ref, cb_ref, lng_ref, lnb_ref, o_ref):
    off = CONV_HALO - (CONV_KERNEL - 1)
    acc = jnp.zeros((CONV_CHUNK // 8, 8, CONV_WIDTH), F32)
    for k in range(CONV_KERNEL):
        a, b = divmod(k + off, 8)
        rows = ubuf[b, r0 + 8 * a:r0 + 8 * a + CONV_CHUNK, :]
        acc = acc + rows.reshape(acc.shape) * w_ref[k][None]
    u = acc.reshape(CONV_CHUNK, CONV_WIDTH) + cb_ref[...]
    mu = jnp.mean(u, axis=-1, keepdims=True)
    d = u - mu
    var = jnp.mean(d * d, axis=-1, keepdims=True)
    y = d * lax.rsqrt(var + EPS) * lng_ref[...] + lnb_ref[...]
    o_ref[r0:r0 + CONV_CHUNK, :] = _silu(y).astype(BF16)


def _inproj_kernel(x_ref, ada_ref, g_ref, w_ref, cw_ref, cb_ref, lng_ref, lnb_ref,
                   qkv_ref, u_ref, gate_ref, ubuf):
    TS, H, C = ROW_TILE, CONV_HALO, CONV_WIDTH
    i = pl.program_id(1)

    @pl.when(i == 0)
    def _():
        ubuf[0, 0:H, :] = jnp.zeros((H, C), F32)

    @pl.when(i > 0)
    def _():
        ubuf[0, 0:H, :] = ubuf[0, TS:TS + H, :]

    x = x_ref[...]
    ms = jnp.mean(x * x, axis=-1, keepdims=True)
    y = x * lax.rsqrt(ms + EPS)
    h = (y * g_ref[...]) * (1.0 + ada_ref[1:2, :]) + ada_ref[0:1, :]
    hb = h.astype(BF16)

    def proj(base):
        return jnp.dot(hb, w_ref[:, base:base + PROJ_CHUNK], preferred_element_type=F32)

    ubuf[0, H:H + TS, :] = proj(QKV_COLS) * _sigmoid(proj(QKV_COLS + C))
    n = TS + H - 8
    for b in range(1, 8):
        ubuf[b, 0:n, :] = ubuf[0, b:b + n, :]
    jobs = ([(qkv_ref, c, c) for c in range(0, QKV_COLS, PROJ_CHUNK)]
            + [(gate_ref, c, QKV_COLS + GLU_COLS + c) for c in range(0, GATE_COLS, PROJ_CHUNK)])
    rows = list(range(0, TS, CONV_CHUNK))
    for n, (ref, c, base) in enumerate(jobs):
        r = proj(base)
        if base < ATTN_WIDTH:
            r = r * Q_SCALE
        ref[:, c:c + PROJ_CHUNK] = r.astype(BF16)
        for r0 in rows[n * len(rows) // len(jobs):(n + 1) * len(rows) // len(jobs)]:
            _conv_chunk(r0, ubuf, cw_ref, cb_ref, lng_ref, lnb_ref, u_ref)


def _in_proj(x, ada3, g, w, cw, cb, lng, lnb):
    bsz, s, d = x.shape
    tm = ROW_TILE
    C = CONV_WIDTH
    row = lambda n: pl.BlockSpec((None, tm, n), lambda b, i: (b, i, 0))
    vec = lambda a: a.reshape(1, C)
    return pl.pallas_call(
        _inproj_kernel,
        grid=(bsz, s // tm),
        in_specs=[row(d),
                  pl.BlockSpec((None, 6, d), lambda b, i: (b, 0, 0)),
                  _resident((1, d)),
                  _resident((d, IN_COLS)),
                  _resident((CONV_KERNEL, 8, C)),
                  _resident((1, C)), _resident((1, C)), _resident((1, C))],
        out_specs=[row(QKV_COLS), row(C), row(GATE_COLS)],
        out_shape=[jax.ShapeDtypeStruct((bsz, s, QKV_COLS), BF16),
                   jax.ShapeDtypeStruct((bsz, s, C), BF16),
                   jax.ShapeDtypeStruct((bsz, s, GATE_COLS), BF16)],
        scratch_shapes=[pltpu.VMEM((8, tm + CONV_HALO, C), F32)],
        compiler_params=_params(("arbitrary", "arbitrary")),
        name="in_proj",
    )(x, ada3, g.reshape(1, d), w, jnp.broadcast_to(cw[:, None, :], (CONV_KERNEL, 8, C)),
      vec(cb), vec(lng), vec(lnb))


def _attn_kernel(lamv_ref, subg_ref, q_ref, k_ref, v_ref, bias_ref, *rest):
    n_cast = (len(rest) - 5) // 2
    cast_in, o_ref, cast_out = rest[:n_cast], rest[n_cast], rest[n_cast + 1:2 * n_cast + 1]
    vt_ref, qz_ref, m_ref, a_ref = rest[2 * n_cast + 1:]
    T = ATTN_TILE
    NS = 2 * N_HEADS_A
    n_q = q_ref.shape[0] // T
    hcols = lambda h: slice(h * HEAD_V, (h + 1) * HEAD_V)
    rows = lambda t: slice(t * T, (t + 1) * T)

    vt = v_ref[...].T
    for h in range(N_HEADS_A):
        vt_ref[h, 0:HEAD_V, :] = vt[hcols(h), :]
        vt_ref[h, HEAD_V:, :] = jnp.ones((ONES_ROWS, vt.shape[1]), BF16)

    lv = lamv_ref[...]
    lam = (jnp.exp(jnp.sum(lv[0:1] * lv[1:2], axis=-1, keepdims=True))
           - jnp.exp(jnp.sum(lv[2:3] * lv[3:4], axis=-1, keepdims=True)) + LAM_INIT)
    lane = lax.broadcasted_iota(jnp.int32, (T, HEAD_V), 1)

    def prepare(i):
        slot = i % 2
        for h in range(N_HEADS_A):
            qs = q_ref[rows(i), hcols(h)]
            zero = jnp.zeros_like(qs)
            qz_ref[slot, 2 * h] = jnp.where(lane < HEAD_QK, qs, zero)
            qz_ref[slot, 2 * h + 1] = jnp.where(lane >= HEAD_QK, qs, zero)

    def finalize(i):
        slot = i % 2
        for h in range(N_HEADS_A):
            a1, a2 = a_ref[slot, 2 * h], a_ref[slot, 2 * h + 1]
            o = (a1[:HEAD_V] / a1[HEAD_V:HEAD_V + 1]
                 - lam * (a2[:HEAD_V] / a2[HEAD_V:HEAD_V + 1]))
            ms = jnp.mean(o * o, axis=0, keepdims=True)
            y = (o * lax.rsqrt(ms + EPS)).T
            o_ref[rows(i), hcols(h)] = ((y * subg_ref[...]) * (1.0 - LAM_INIT)).astype(BF16)

    units = [(i, j, n) for i in range(n_q) for j in range(i + 1) for n in range(NS)]
    scores, probs, alphas = {}, {}, {}

    def score(u):
        i, j, n = units[u]
        kc = k_ref[rows(j), hcols(n // 2)]
        scores[u] = lax.dot_general(kc, qz_ref[i % 2, n], (((1,), (1,)), ((), ())),
                                    preferred_element_type=F32)

    def softmax(u):
        i, j, n = units[u]
        s = scores.pop(u)
        if j >= i - 1:
            s = s + bias_ref[n // 2, j - (i - 1)]
        m_new = jnp.max(s, axis=0, keepdims=True)
        if j > 0:
            m_old = m_ref[i % 2, n]
            m_new = jnp.maximum(m_old, m_new)
            alphas[u] = jnp.exp2(m_old - m_new)
        probs[u] = jnp.exp2(s - m_new).astype(BF16)
        m_ref[i % 2, n] = m_new

    def value(u):
        i, j, n = units[u]
        vc = vt_ref[n // 2, :, rows(j)]
        pv = jnp.dot(vc, probs.pop(u), preferred_element_type=F32)
        a_ref[i % 2, n] = pv if j == 0 else alphas.pop(u) * a_ref[i % 2, n] + pv

    for step in range(len(units) + 2 * ATTN_LAG):
        if step < len(units):
            if units[step][1:] == (0, 0):
                prepare(units[step][0])
            score(step)
        if 0 <= step - ATTN_LAG < len(units):
            softmax(step - ATTN_LAG)
        done = step - 2 * ATTN_LAG
        if 0 <= done < len(units):
            value(done)
            i, j, n = units[done]
            if (j, n) == (i, NS - 1):
                finalize(i)

    for w_ref, wb_ref in zip(cast_in, cast_out):
        wb_ref[...] = w_ref[...].astype(BF16)


def _attention(qkv, bias_tiles, lamv, subg, weights):
    bsz, s, _ = qkv.shape
    T = ATTN_TILE
    W = ATTN_WIDTH
    NS = 2 * N_HEADS_A
    steps = bsz

    def slab_spec(w, squeeze):
        rows, cols = w.shape[-2:]
        assert rows % (steps * CAST_ROWS) == 0
        block = (rows // steps, cols)
        if squeeze:
            return pl.BlockSpec((None,) + block, lambda b: (0, b, 0))
        return pl.BlockSpec(block, lambda b: (b, 0))

    seq = lambda col: pl.BlockSpec((None, s, W), lambda b: (b, 0, col))
    out = pl.pallas_call(
        _attn_kernel,
        grid=(bsz,),
        in_specs=[_resident((4, HEAD_QK)),
                  _resident((1, HEAD_V)),
                  seq(0), seq(1), seq(2),
                  _resident((N_HEADS_A, 2, T, T))] + [slab_spec(w, True) for w in weights],
        out_specs=[seq(0)] + [slab_spec(w, False) for w in weights],
        out_shape=[jax.ShapeDtypeStruct((bsz, s, W), BF16)]
        + [jax.ShapeDtypeStruct(w.shape[-2:], BF16) for w in weights],
        scratch_shapes=[pltpu.VMEM((N_HEADS_A, HEAD_V + ONES_ROWS, s), BF16),
                        pltpu.VMEM((2, NS, T, HEAD_V), BF16),
                        pltpu.VMEM((2, NS, 1, T), F32),
                        pltpu.VMEM((2, NS, HEAD_V + ONES_ROWS, T), F32)],
        compiler_params=_params(("arbitrary",)),
        name="diff_attn",
    )(lamv, subg.reshape(1, HEAD_V), qkv, qkv, qkv, bias_tiles, *weights)
    return out[0], out[1:]


def _mixffn_kernel(x_ref, ada_ref, a_ref, u_ref, gate_ref, woa_ref, woc_ref, boc_ref,
                   wout_ref, g2_ref, w1_ref, w2_ref, gf_ref, o_ref,
                   x1_ref, h2_ref, act_ref):
    D = D_MODEL
    a = jnp.dot(a_ref[...], woa_ref[...], preferred_element_type=F32)
    cv = jnp.dot(u_ref[...], woc_ref[...], preferred_element_type=F32) + boc_ref[...]
    gate = gate_ref[...]
    y = (_sigmoid(gate[:, :D].astype(F32)) * a
         + _sigmoid(gate[:, D:].astype(F32)) * cv)
    z = jnp.dot(y.astype(BF16), wout_ref[...], preferred_element_type=F32)
    x1 = x_ref[...] + ada_ref[2:3, :] * z
    x1_ref[...] = x1
    ms = jnp.mean(x1 * x1, axis=-1, keepdims=True)
    yn = x1 * lax.rsqrt(ms + EPS)
    h2_ref[...] = ((yn * g2_ref[...]) * (1.0 + ada_ref[4:5, :]) + ada_ref[3:4, :]).astype(BF16)

    h = h2_ref[...]
    for c in range(0, D_FF, FF_CHUNK):
        fg = jnp.dot(h, w1_ref[:, c:c + FF_CHUNK], preferred_element_type=F32)
        fu = jnp.dot(h, w1_ref[:, D_FF + c:D_FF + c + FF_CHUNK],
                     preferred_element_type=F32)
        act_ref[:, c:c + FF_CHUNK] = (_silu(fg) * fu).astype(BF16)
    z2 = jnp.dot(act_ref[...], w2_ref[...], preferred_element_type=F32)
    x2 = x1_ref[...] + ada_ref[5:6, :] * z2
    ms2 = jnp.mean(x2 * x2, axis=-1, keepdims=True)
    o_ref[...] = (x2 * lax.rsqrt(ms2 + EPS)) * gf_ref[...]


def _mix_ffn(x, ada3, a_n, u, gates, woa, woc, boc, wout, g2, w1, w2, gf):
    bsz, s, d = x.shape
    tm = ROW_TILE
    row = lambda n: pl.BlockSpec((None, tm, n), lambda b, i: (b, i, 0))
    return pl.pallas_call(
        _mixffn_kernel,
        grid=(bsz, s // tm),
        in_specs=[row(d),
                  pl.BlockSpec((None, 6, d), lambda b, i: (b, 0, 0)),
                  row(ATTN_WIDTH), row(CONV_WIDTH), row(GATE_COLS),
                  _resident((ATTN_WIDTH, d)), _resident((CONV_WIDTH, d)),
                  _resident((1, d)), _resident((d, d)), _resident((1, d)),
                  _resident((d, 2 * D_FF)), _resident((D_FF, d)), _resident((1, d))],
        out_specs=row(d),
        out_shape=jax.ShapeDtypeStruct((bsz, s, d), F32),
        scratch_shapes=[pltpu.VMEM((tm, d), F32), pltpu.VMEM((tm, d), BF16),
                        pltpu.VMEM((tm, D_FF), BF16)],
        compiler_params=_params(("parallel", "arbitrary")),
        name="mix_ffn",
    )(x, ada3, a_n, u, gates, woa, woc, boc.reshape(1, d), wout, g2.reshape(1, d),
      w1, w2, gf.reshape(1, d))


def kernel(x, c, w_ada, b_ada, norm1_g, norm2_g, final_g, w_in, lambda_q1, lambda_k1,
           lambda_q2, lambda_k2, rel_bias, attn_sub_g, w_o_attn, conv_w, conv_b,
           conv_ln_g, conv_ln_b, w_o_conv, b_o_conv, w_out, w_ffn_in, w_ffn_out):
    bsz, s, d = x.shape
    assert w_ada.shape[0] == 1, "single-layer block"
    assert (d, s % ROW_TILE, s % ATTN_TILE) == (D_MODEL, 0, 0)
    assert PROJ_CHUNK == CONV_WIDTH and ATTN_WIDTH % PROJ_CHUNK == 0
    l = 0
    ada, w_in_bf, bias_tiles = _prologue(c, w_ada, b_ada, w_in, rel_bias)
    ada3 = ada.reshape(bsz, 6, d)
    qkv, u, gates = _in_proj(x, ada3, norm1_g[l], w_in_bf, conv_w[l],
                             conv_b[l], conv_ln_g[l], conv_ln_b[l])
    lamv = jnp.stack([lambda_q1[l], lambda_k1[l], lambda_q2[l], lambda_k2[l]])
    a_n, (woa, woc, wout, w1, w2) = _attention(
        qkv, bias_tiles, lamv, attn_sub_g[l],
        [w_o_attn, w_o_conv, w_out, w_ffn_in, w_ffn_out])
    return _mix_ffn(x, ada3, a_n, u, gates, woa, woc, b_o_conv[l], wout, norm2_g[l],
                    w1, w2, final_g)
```

```python
import math

import jax
import jax.numpy as jnp
from jax import lax
from jax.experimental import pallas as pl
from jax.experimental.pallas import tpu as pltpu

D_MODEL = 1024
N_HEADS_A = 4
HEAD_QK = 64
HEAD_V = 2 * HEAD_QK
ATTN_WIDTH = N_HEADS_A * HEAD_V
CONV_WIDTH = 512
CONV_KERNEL = 31
QKV_COLS = 3 * ATTN_WIDTH
GLU_COLS = 2 * CONV_WIDTH
GATE_COLS = 2 * D_MODEL
IN_COLS = QKV_COLS + GLU_COLS + GATE_COLS
D_FF = 2816
N_BUCKETS = 32
MAX_EXACT = 16
MAX_DISTANCE = 128
EPS = 1e-6
NEG_INF = -1e30
LAM_INIT = 0.8 - 0.6 * math.exp(-0.3 * 0)

V7X_VMEM_BYTES = 64 * 1024 * 1024
VMEM_LIMIT = V7X_VMEM_BYTES - 12 * 1024 * 1024

ROW_TILE = 512
ATTN_TILE = 256
ATTN_LAG = 4
ONES_ROWS = 16
CAST_ROWS = 16
LOG2E = math.log2(math.e)
Q_SCALE = HEAD_QK ** -0.5 * LOG2E
CONV_HALO = 32
CONV_CHUNK = 16
PROJ_CHUNK = 512
FF_CHUNK = 256

BF16 = jnp.bfloat16
F32 = jnp.float32


def _sigmoid(x):
    return 1.0 / (1.0 + jnp.exp2(x * (-LOG2E)))


def _silu(x):
    return x * _sigmoid(x)


def _params(sem, vmem=VMEM_LIMIT, flags=None):
    return pltpu.CompilerParams(dimension_semantics=sem, vmem_limit_bytes=vmem, flags=flags)


def _resident(shape):
    nd = len(shape)
    return pl.BlockSpec(shape, lambda *_: (0,) * nd, pipeline_mode=pl.Buffered(1))


def _bias_tile(rel_ref, h, t):
    T = ATTN_TILE
    key = lax.broadcasted_iota(jnp.int32, (T, T), 0)
    qry = lax.broadcasted_iota(jnp.int32, (T, T), 1)
    dist = (1 - t) * T + qry - key
    n = jnp.maximum(dist, 0)
    large = MAX_EXACT + (jnp.log(jnp.maximum(n, 1).astype(F32) / MAX_EXACT)
                         / math.log(MAX_DISTANCE / MAX_EXACT)
                         * (N_BUCKETS - MAX_EXACT)).astype(jnp.int32)
    large = jnp.minimum(large, N_BUCKETS - 1)
    bucket = jnp.where(n < MAX_EXACT, n, large)
    far = rel_ref[N_BUCKETS - 1, h]
    bias = jnp.zeros((T, T), F32)
    for b in range(N_BUCKETS - 1):
        bias = jnp.where(bucket == b, (rel_ref[b, h] - far) * LOG2E, bias)
    return jnp.where(dist >= 0, bias, NEG_INF)


def _prologue_kernel(rel_ref, c_ref, w_ref, b_ref, win_ref, o_ref, win_bf_ref, bias_ref):
    j = pl.program_id(0)
    ca = _silu(c_ref[...])
    o_ref[...] = jnp.dot(ca, w_ref[...], preferred_element_type=F32,
                         precision=lax.Precision.HIGHEST) + b_ref[...]
    win_bf_ref[...] = win_ref[...].astype(BF16)
    bias_ref[...] = _bias_tile(rel_ref, j // 2, j % 2)


def _prologue(c, w, b, w_in, rel_bias):
    bsz, d = c.shape
    n = w.shape[-1]
    steps = 2 * N_HEADS_A
    tn = n // steps
    _, rows, cols = w_in.shape
    tr = rows // steps
    T = ATTN_TILE
    return pl.pallas_call(
        _prologue_kernel,
        grid=(steps,),
        in_specs=[pl.BlockSpec(memory_space=pltpu.SMEM),
                  pl.BlockSpec((bsz, d), lambda j: (0, 0)),
                  pl.BlockSpec((None, d, tn), lambda j: (0, 0, j)),
                  pl.BlockSpec((1, tn), lambda j: (0, j)),
                  pl.BlockSpec((None, tr, cols), lambda j: (0, j, 0))],
        out_specs=[pl.BlockSpec((bsz, tn), lambda j: (0, j)),
                   pl.BlockSpec((tr, cols), lambda j: (j, 0)),
                   pl.BlockSpec((None, None, T, T), lambda j: (j // 2, j % 2, 0, 0))],
        out_shape=[jax.ShapeDtypeStruct((bsz, n), F32),
                   jax.ShapeDtypeStruct((rows, cols), BF16),
                   jax.ShapeDtypeStruct((N_HEADS_A, 2, T, T), F32)],
        compiler_params=_params(("arbitrary",)),
        name="prologue",
    )(rel_bias, c, w, b, w_in)


def _conv_chunk(r0, ubuf, w_ref, cb_ref, lng_ref, lnb_ref, o_ref):
    off = CONV_HALO - (CONV_KERNEL - 1)
    acc = jnp.broadcast_to(cb_ref[...][None], (CONV_CHUNK // 8, 8, CONV_WIDTH))
    for k in range(CONV_KERNEL):
        a, b = divmod(k + off, 8)
        rows = ubuf[b, r0 + 8 * a:r0 + 8 * a + CONV_CHUNK, :]
        acc = acc + rows.reshape(acc.shape) * w_ref[k][None]
    u = acc.reshape(CONV_CHUNK, CONV_WIDTH)
    mu = jnp.mean(u, axis=-1, keepdims=True)
    d = u - mu
    var = jnp.mean(d * d, axis=-1, keepdims=True)
    y = d * lax.rsqrt(var + EPS) * lng_ref[...] + lnb_ref[...]
    o_ref[r0:r0 + CONV_CHUNK, :] = _silu(y).astype(BF16)


def _inproj_kernel(x_ref, ada_ref, g_ref, w_ref, cw_ref, cb_ref, lng_ref, lnb_ref,
                   qkv_ref, u_ref, gate_ref, ubuf):
    TS, H, C = ROW_TILE, CONV_HALO, CONV_WIDTH
    i = pl.program_id(1)

    @pl.when(i == 0)
    def _():
        ubuf[0, 0:H, :] = jnp.zeros((H, C), F32)

    @pl.when(i > 0)
    def _():
        ubuf[0, 0:H, :] = ubuf[0, TS:TS + H, :]

    x = x_ref[...]
    ms = jnp.mean(x * x, axis=-1, keepdims=True)
    y = x * lax.rsqrt(ms + EPS)
    h = y * (g_ref[...] * (1.0 + ada_ref[1:2, :])) + ada_ref[0:1, :]
    hb = h.astype(BF16)

    def proj(base):
        return jnp.dot(hb, w_ref[:, base:base + PROJ_CHUNK], preferred_element_type=F32)

    ubuf[0, H:H + TS, :] = proj(QKV_COLS) * _sigmoid(proj(QKV_COLS + C))
    n = TS + H - 8
    for b in range(1, 8):
        ubuf[b, 0:n, :] = ubuf[0, b:b + n, :]
    jobs = ([(qkv_ref, c, c) for c in range(0, QKV_COLS, PROJ_CHUNK)]
            + [(gate_ref, c, QKV_COLS + GLU_COLS + c) for c in range(0, GATE_COLS, PROJ_CHUNK)])
    rows = list(range(0, TS, CONV_CHUNK))
    for n, (ref, c, base) in enumerate(jobs):
        r = proj(base)
        if base < ATTN_WIDTH:
            r = r * Q_SCALE
        ref[:, c:c + PROJ_CHUNK] = r.astype(BF16)
        for r0 in rows[n * len(rows) // len(jobs):(n + 1) * len(rows) // len(jobs)]:
            _conv_chunk(r0, ubuf, cw_ref, cb_ref, lng_ref, lnb_ref, u_ref)


def _in_proj(x, ada3, g, w, cw, cb, lng, lnb):
    bsz, s, d = x.shape
    tm = ROW_TILE
    C = CONV_WIDTH
    row = lambda n: pl.BlockSpec((None, tm, n), lambda b, i: (b, i, 0))
    vec = lambda a: a.reshape(1, C)
    return pl.pallas_call(
        _inproj_kernel,
        grid=(bsz, s // tm),
        in_specs=[row(d),
                  pl.BlockSpec((None, 6, d), lambda b, i: (b, 0, 0)),
                  _resident((1, d)),
                  _resident((d, IN_COLS)),
                  _resident((CONV_KERNEL, 8, C)),
                  _resident((8, C)), _resident((1, C)), _resident((1, C))],
        out_specs=[row(QKV_COLS), row(C), row(GATE_COLS)],
        out_shape=[jax.ShapeDtypeStruct((bsz, s, QKV_COLS), BF16),
                   jax.ShapeDtypeStruct((bsz, s, C), BF16),
                   jax.ShapeDtypeStruct((bsz, s, GATE_COLS), BF16)],
        scratch_shapes=[pltpu.VMEM((8, tm + CONV_HALO, C), F32)],
        compiler_params=_params(("arbitrary", "arbitrary")),
        name="in_proj",
    )(x, ada3, g.reshape(1, d), w, jnp.broadcast_to(cw[:, None, :], (CONV_KERNEL, 8, C)),
      jnp.broadcast_to(vec(cb), (8, C)), vec(lng), vec(lnb))


def _attn_kernel(lamv_ref, subg_ref, q_ref, k_ref, v_ref, bias_ref, *rest):
    n_cast = (len(rest) - 5) // 2
    cast_in, o_ref, cast_out = rest[:n_cast], rest[n_cast], rest[n_cast + 1:2 * n_cast + 1]
    vt_ref, qz_ref, m_ref, a_ref = rest[2 * n_cast + 1:]
    T = ATTN_TILE
    NS = 2 * N_HEADS_A
    n_q = q_ref.shape[0] // T
    hcols = lambda h: slice(h * HEAD_V, (h + 1) * HEAD_V)
    rows = lambda t: slice(t * T, (t + 1) * T)

    vt = v_ref[...].T
    for h in range(N_HEADS_A):
        vt_ref[h, 0:HEAD_V, :] = vt[hcols(h), :]
        vt_ref[h, HEAD_V:, :] = jnp.ones((ONES_ROWS, vt.shape[1]), BF16)

    lv = lamv_ref[...]
    lam = (jnp.exp(jnp.sum(lv[0:1] * lv[1:2], axis=-1, keepdims=True))
           - jnp.exp(jnp.sum(lv[2:3] * lv[3:4], axis=-1, keepdims=True)) + LAM_INIT)
    lane = lax.broadcasted_iota(jnp.int32, (T, HEAD_V), 1)

    def prepare(i):
        slot = i % 2
        for h in range(N_HEADS_A):
            qs = q_ref[rows(i), hcols(h)]
            zero = jnp.zeros_like(qs)
            qz_ref[slot, 2 * h] = jnp.where(lane < HEAD_QK, qs, zero)
            qz_ref[slot, 2 * h + 1] = jnp.where(lane >= HEAD_QK, qs, zero)

    def finalize(i):
        slot = i % 2
        for h in range(N_HEADS_A):
            a1, a2 = a_ref[slot, 2 * h], a_ref[slot, 2 * h + 1]
            o = (a1[:HEAD_V] / a1[HEAD_V:HEAD_V + 1]
                 - lam * (a2[:HEAD_V] / a2[HEAD_V:HEAD_V + 1]))
            ms = jnp.mean(o * o, axis=0, keepdims=True)
            y = (o * lax.rsqrt(ms + EPS)).T
            o_ref[rows(i), hcols(h)] = ((y * subg_ref[...]) * (1.0 - LAM_INIT)).astype(BF16)

    units = [(i, j, n) for i in range(n_q) for j in range(i + 1) for n in range(NS)]
    scores, probs, alphas = {}, {}, {}

    def score(u):
        i, j, n = units[u]
        kc = k_ref[rows(j), hcols(n // 2)]
        scores[u] = lax.dot_general(kc, qz_ref[i % 2, n], (((1,), (1,)), ((), ())),
                                    preferred_element_type=F32)

    def softmax(u):
        i, j, n = units[u]
        s = scores.pop(u)
        if j >= i - 1:
            s = s + bias_ref[n // 2, j - (i - 1)]
        m_new = jnp.max(s, axis=0, keepdims=True)
        if j > 0:
            m_old = m_ref[i % 2, n]
            m_new = jnp.maximum(m_old, m_new)
            alphas[u] = jnp.exp2(m_old - m_new)
        probs[u] = jnp.exp2(s - m_new).astype(BF16)
        m_ref[i % 2, n] = m_new

    def value(u):
        i, j, n = units[u]
        vc = vt_ref[n // 2, :, rows(j)]
        pv = jnp.dot(vc, probs.pop(u), preferred_element_type=F32)
        a_ref[i % 2, n] = pv if j == 0 else alphas.pop(u) * a_ref[i % 2, n] + pv

    for step in range(len(units) + 2 * ATTN_LAG):
        if step < len(units):
            if units[step][1:] == (0, 0):
                prepare(units[step][0])
            score(step)
        if 0 <= step - ATTN_LAG < len(units):
            softmax(step - ATTN_LAG)
        done = step - 2 * ATTN_LAG
        if 0 <= done < len(units):
            value(done)
            i, j, n = units[done]
            if (j, n) == (i, NS - 1):
                finalize(i)

    for w_ref, wb_ref in zip(cast_in, cast_out):
        wb_ref[...] = w_ref[...].astype(BF16)


def _attention(qkv, bias_tiles, lamv, subg, weights):
    bsz, s, _ = qkv.shape
    T = ATTN_TILE
    W = ATTN_WIDTH
    NS = 2 * N_HEADS_A
    steps = bsz

    def slab_spec(w, squeeze):
        rows, cols = w.shape[-2:]
        assert rows % (steps * CAST_ROWS) == 0
        block = (rows // steps, cols)
        if squeeze:
            return pl.BlockSpec((None,) + block, lambda b: (0, b, 0))
        return pl.BlockSpec(block, lambda b: (b, 0))

    seq = lambda col: pl.BlockSpec((None, s, W), lambda b: (b, 0, col))
    out = pl.pallas_call(
        _attn_kernel,
        grid=(bsz,),
        in_specs=[_resident((4, HEAD_QK)),
                  _resident((1, HEAD_V)),
                  seq(0), seq(1), seq(2),
                  _resident((N_HEADS_A, 2, T, T))] + [slab_spec(w, True) for w in weights],
        out_specs=[seq(0)] + [slab_spec(w, False) for w in weights],
        out_shape=[jax.ShapeDtypeStruct((bsz, s, W), BF16)]
        + [jax.ShapeDtypeStruct(w.shape[-2:], BF16) for w in weights],
        scratch_shapes=[pltpu.VMEM((N_HEADS_A, HEAD_V + ONES_ROWS, s), BF16),
                        pltpu.VMEM((2, NS, T, HEAD_V), BF16),
                        pltpu.VMEM((2, NS, 1, T), F32),
                        pltpu.VMEM((2, NS, HEAD_V + ONES_ROWS, T), F32)],
        compiler_params=_params(("arbitrary",)),
        name="diff_attn",
    )(lamv, subg.reshape(1, HEAD_V), qkv, qkv, qkv, bias_tiles, *weights)
    return out[0], out[1:]


def _mixffn_kernel(x_ref, ada_ref, a_ref, u_ref, gate_ref, woa_ref, woc_ref, boc_ref,
                   wout_ref, g2_ref, w1_ref, w2_ref, gf_ref, o_ref,
                   x1_ref, h2_ref, act_ref):
    D = D_MODEL
    a = jnp.dot(a_ref[...], woa_ref[...], preferred_element_type=F32)
    cv = jnp.dot(u_ref[...], woc_ref[...], preferred_element_type=F32) + boc_ref[...]
    gate = gate_ref[...]
    y = (_sigmoid(gate[:, :D].astype(F32)) * a
         + _sigmoid(gate[:, D:].astype(F32)) * cv)
    z = jnp.dot(y.astype(BF16), wout_ref[...], preferred_element_type=F32)
    x1 = x_ref[...] + ada_ref[2:3, :] * z
    x1_ref[...] = x1
    ms = jnp.mean(x1 * x1, axis=-1, keepdims=True)
    yn = x1 * lax.rsqrt(ms + EPS)
    h2_ref[...] = ((yn * g2_ref[...]) * (1.0 + ada_ref[4:5, :]) + ada_ref[3:4, :]).astype(BF16)

    h = h2_ref[...]
    for c in range(0, D_FF, FF_CHUNK):
        fg = jnp.dot(h, w1_ref[:, c:c + FF_CHUNK], preferred_element_type=F32)
        fu = jnp.dot(h, w1_ref[:, D_FF + c:D_FF + c + FF_CHUNK],
                     preferred_element_type=F32)
        act_ref[:, c:c + FF_CHUNK] = (_silu(fg) * fu).astype(BF16)
    z2 = jnp.dot(act_ref[...], w2_ref[...], preferred_element_type=F32)
    x2 = x1_ref[...] + ada_ref[5:6, :] * z2
    ms2 = jnp.mean(x2 * x2, axis=-1, keepdims=True)
    o_ref[...] = (x2 * lax.rsqrt(ms2 + EPS)) * gf_ref[...]


def _mix_ffn(x, ada3, a_n, u, gates, woa, woc, boc, wout, g2, w1, w2, gf):
    bsz, s, d = x.shape
    tm = ROW_TILE
    row = lambda n: pl.BlockSpec((None, tm, n), lambda b, i: (b, i, 0))
    return pl.pallas_call(
        _mixffn_kernel,
        grid=(bsz, s // tm),
        in_specs=[row(d),
                  pl.BlockSpec((None, 6, d), lambda b, i: (b, 0, 0)),
                  row(ATTN_WIDTH), row(CONV_WIDTH), row(GATE_COLS),
                  _resident((ATTN_WIDTH, d)), _resident((CONV_WIDTH, d)),
                  _resident((1, d)), _resident((d, d)), _resident((1, d)),
                  _resident((d, 2 * D_FF)), _resident((D_FF, d)), _resident((1, d))],
        out_specs=row(d),
        out_shape=jax.ShapeDtypeStruct((bsz, s, d), F32),
        scratch_shapes=[pltpu.VMEM((tm, d), F32), pltpu.VMEM((tm, d), BF16),
                        pltpu.VMEM((tm, D_FF), BF16)],
        compiler_params=_params(("parallel", "arbitrary")),
        name="mix_ffn",
    )(x, ada3, a_n, u, gates, woa, woc, boc.reshape(1, d), wout, g2.reshape(1, d),
      w1, w2, gf.reshape(1, d))


def kernel(x, c, w_ada, b_ada, norm1_g, norm2_g, final_g, w_in, lambda_q1, lambda_k1,
           lambda_q2, lambda_k2, rel_bias, attn_sub_g, w_o_attn, conv_w, conv_b,
           conv_ln_g, conv_ln_b, w_o_conv, b_o_conv, w_out, w_ffn_in, w_ffn_out):
    bsz, s, d = x.shape
    assert w_ada.shape[0] == 1, "single-layer block"
    assert (d, s % ROW_TILE, s % ATTN_TILE) == (D_MODEL, 0, 0)
    assert PROJ_CHUNK == CONV_WIDTH and ATTN_WIDTH % PROJ_CHUNK == 0
    l = 0
    ada, w_in_bf, bias_tiles = _prologue(c, w_ada, b_ada, w_in, rel_bias)
    ada3 = ada.reshape(bsz, 6, d)
    qkv, u, gates = _in_proj(x, ada3, norm1_g[l], w_in_bf, conv_w[l],
                             conv_b[l], conv_ln_g[l], conv_ln_b[l])
    lamv = jnp.stack([lambda_q1[l], lambda_k1[l], lambda_q2[l], lambda_k2[l]])
    a_n, (woa, woc, wout, w1, w2) = _attention(
        qkv, bias_tiles, lamv, attn_sub_g[l],
        [w_o_attn, w_o_conv, w_out, w_ffn_in, w_ffn_out])
    return _mix_ffn(x, ada3, a_n, u, gates, woa, woc, b_o_conv[l], wout, norm2_g[l],
                    w1, w2, final_g)
```

```python
import math

import jax
import jax.numpy as jnp
from jax import lax
from jax.experimental import pallas as pl
from jax.experimental.pallas import tpu as pltpu

D_MODEL = 1024
N_HEADS_A = 4
HEAD_QK = 64
HEAD_V = 2 * HEAD_QK
ATTN_WIDTH = N_HEADS_A * HEAD_V
CONV_WIDTH = 512
CONV_KERNEL = 31
QKV_COLS = 3 * ATTN_WIDTH
GLU_COLS = 2 * CONV_WIDTH
GATE_COLS = 2 * D_MODEL
IN_COLS = QKV_COLS + GLU_COLS + GATE_COLS
D_FF = 2816
N_BUCKETS = 32
MAX_EXACT = 16
MAX_DISTANCE = 128
EPS = 1e-6
NEG_INF = -1e30
LAM_INIT = 0.8 - 0.6 * math.exp(-0.3 * 0)

V7X_VMEM_BYTES = 64 * 1024 * 1024
VMEM_LIMIT = V7X_VMEM_BYTES - 12 * 1024 * 1024

ROW_TILE = 512
ATTN_TILE = 256
ATTN_LAG = 4
ONES_ROWS = 16
CAST_ROWS = 16
LOG2E = math.log2(math.e)
Q_SCALE = HEAD_QK ** -0.5 * LOG2E
CONV_HALO = 32
CONV_CHUNK = 16
PROJ_CHUNK = 512
FF_CHUNK = 256

BF16 = jnp.bfloat16
F32 = jnp.float32


def _sigmoid(x):
    return 1.0 / (1.0 + jnp.exp2(x * (-LOG2E)))


def _silu(x):
    return x * _sigmoid(x)


def _params(sem, vmem=VMEM_LIMIT, flags=None):
    return pltpu.CompilerParams(dimension_semantics=sem, vmem_limit_bytes=vmem, flags=flags)


def _resident(shape):
    nd = len(shape)
    return pl.BlockSpec(shape, lambda *_: (0,) * nd, pipeline_mode=pl.Buffered(1))


def _bias_tile(rel_ref, h, t):
    T = ATTN_TILE
    key = lax.broadcasted_iota(jnp.int32, (T, T), 0)
    qry = lax.broadcasted_iota(jnp.int32, (T, T), 1)
    dist = (1 - t) * T + qry - key
    n = jnp.maximum(dist, 0)
    large = MAX_EXACT + (jnp.log(jnp.maximum(n, 1).astype(F32) / MAX_EXACT)
                         / math.log(MAX_DISTANCE / MAX_EXACT)
                         * (N_BUCKETS - MAX_EXACT)).astype(jnp.int32)
    large = jnp.minimum(large, N_BUCKETS - 1)
    bucket = jnp.where(n < MAX_EXACT, n, large)
    far = rel_ref[N_BUCKETS - 1, h]
    bias = jnp.zeros((T, T), F32)
    for b in range(N_BUCKETS - 1):
        bias = jnp.where(bucket == b, (rel_ref[b, h] - far) * LOG2E, bias)
    return jnp.where(dist >= 0, bias, NEG_INF)


def _prologue_kernel(rel_ref, c_ref, w_ref, b_ref, win_ref, o_ref, win_bf_ref, bias_ref):
    j = pl.program_id(0)
    ca = _silu(c_ref[...])
    o_ref[...] = jnp.dot(ca.astype(BF16), w_ref[...].astype(BF16),
                         preferred_element_type=F32) + b_ref[...]
    win_bf_ref[...] = win_ref[...].astype(BF16)
    bias_ref[...] = _bias_tile(rel_ref, j // 2, j % 2)


def _prologue(c, w, b, w_in, rel_bias):
    bsz, d = c.shape
    n = w.shape[-1]
    steps = 2 * N_HEADS_A
    tn = n // steps
    _, rows, cols = w_in.shape
    tr = rows // steps
    T = ATTN_TILE
    return pl.pallas_call(
        _prologue_kernel,
        grid=(steps,),
        in_specs=[pl.BlockSpec(memory_space=pltpu.SMEM),
                  pl.BlockSpec((bsz, d), lambda j: (0, 0)),
                  pl.BlockSpec((None, d, tn), lambda j: (0, 0, j)),
                  pl.BlockSpec((1, tn), lambda j: (0, j)),
                  pl.BlockSpec((None, tr, cols), lambda j: (0, j, 0))],
        out_specs=[pl.BlockSpec((bsz, tn), lambda j: (0, j)),
                   pl.BlockSpec((tr, cols), lambda j: (j, 0)),
                   pl.BlockSpec((None, None, T, T), lambda j: (j // 2, j % 2, 0, 0))],
        out_shape=[jax.ShapeDtypeStruct((bsz, n), F32),
                   jax.ShapeDtypeStruct((rows, cols), BF16),
                   jax.ShapeDtypeStruct((N_HEADS_A, 2, T, T), F32)],
        compiler_params=_params(("arbitrary",)),
        name="prologue",
    )(rel_bias, c, w, b, w_in)


def _conv_chunk(r0, ubuf, w_ref, cb_ref, lng_ref, lnb_ref, o_ref):
    off = CONV_HALO - (CONV_KERNEL - 1)
    acc = jnp.zeros((CONV_CHUNK // 8, 8, CONV_WIDTH), F32)
    for k in range(CONV_KERNEL):
        a, b = divmod(k + off, 8)
        rows = ubuf[b, r0 + 8 * a:r0 + 8 * a + CONV_CHUNK, :]
        acc = acc + rows.reshape(acc.shape) * w_ref[k][None]
    u = acc.reshape(CONV_CHUNK, CONV_WIDTH) + cb_ref[...]
    mu = jnp.mean(u, axis=-1, keepdims=True)
    d = u - mu
    var = jnp.mean(d * d, axis=-1, keepdims=True)
    y = d * lax.rsqrt(var + EPS) * lng_ref[...] + lnb_ref[...]
    o_ref[r0:r0 + CONV_CHUNK, :] = _silu(y).astype(BF16)


def _inproj_kernel(x_ref, ada_ref, g_ref, w_ref, cw_ref, cb_ref, lng_ref, lnb_ref,
                   qkv_ref, u_ref, gate_ref, ubuf):
    TS, H, C = ROW_TILE, CONV_HALO, CONV_WIDTH
    i = pl.program_id(1)

    @pl.when(i == 0)
    def _():
        ubuf[0, 0:H, :] = jnp.zeros((H, C), F32)

    @pl.when(i > 0)
    def _():
        ubuf[0, 0:H, :] = ubuf[0, TS:TS + H, :]

    x = x_ref[...]
    ms = jnp.mean(x * x, axis=-1, keepdims=True)
    y = x * lax.rsqrt(ms + EPS)
    h = (y * g_ref[...]) * (1.0 + ada_ref[1:2, :]) + ada_ref[0:1, :]
    hb = h.astype(BF16)

    def proj(base):
        return jnp.dot(hb, w_ref[:, base:base + PROJ_CHUNK], preferred_element_type=F32)

    ubuf[0, H:H + TS, :] = proj(QKV_COLS) * _sigmoid(proj(QKV_COLS + C))
    n = TS + H - 8
    for b in range(1, 8):
        ubuf[b, 0:n, :] = ubuf[0, b:b + n, :]
    jobs = ([(qkv_ref, c, c) for c in range(0, QKV_COLS, PROJ_CHUNK)]
            + [(gate_ref, c, QKV_COLS + GLU_COLS + c) for c in range(0, GATE_COLS, PROJ_CHUNK)])
    rows = list(range(0, TS, CONV_CHUNK))
    for n, (ref, c, base) in enumerate(jobs):
        r = proj(base)
        if base < ATTN_WIDTH:
            r = r * Q_SCALE
        ref[:, c:c + PROJ_CHUNK] = r.astype(BF16)
        for r0 in rows[n * len(rows) // len(jobs):(n + 1) * len(rows) // len(jobs)]:
            _conv_chunk(r0, ubuf, cw_ref, cb_ref, lng_ref, lnb_ref, u_ref)


def _in_proj(x, ada3, g, w, cw, cb, lng, lnb):
    bsz, s, d = x.shape
    tm = ROW_TILE
    C = CONV_WIDTH
    row = lambda n: pl.BlockSpec((None, tm, n), lambda b, i: (b, i, 0))
    vec = lambda a: a.reshape(1, C)
    return pl.pallas_call(
        _inproj_kernel,
        grid=(bsz, s // tm),
        in_specs=[row(d),
                  pl.BlockSpec((None, 6, d), lambda b, i: (b, 0, 0)),
                  _resident((1, d)),
                  _resident((d, IN_COLS)),
                  _resident((CONV_KERNEL, 8, C)),
                  _resident((1, C)), _resident((1, C)), _resident((1, C))],
        out_specs=[row(QKV_COLS), row(C), row(GATE_COLS)],
        out_shape=[jax.ShapeDtypeStruct((bsz, s, QKV_COLS), BF16),
                   jax.ShapeDtypeStruct((bsz, s, C), BF16),
                   jax.ShapeDtypeStruct((bsz, s, GATE_COLS), BF16)],
        scratch_shapes=[pltpu.VMEM((8, tm + CONV_HALO, C), F32)],
        compiler_params=_params(("arbitrary", "arbitrary")),
        name="in_proj",
    )(x, ada3, g.reshape(1, d), w, jnp.broadcast_to(cw[:, None, :], (CONV_KERNEL, 8, C)),
      vec(cb), vec(lng), vec(lnb))


def _attn_kernel(lamv_ref, subg_ref, q_ref, k_ref, v_ref, bias_ref, *rest):
    n_cast = (len(rest) - 5) // 2
    cast_in, o_ref, cast_out = rest[:n_cast], rest[n_cast], rest[n_cast + 1:2 * n_cast + 1]
    vt_ref, qz_ref, m_ref, a_ref = rest[2 * n_cast + 1:]
    T = ATTN_TILE
    NS = 2 * N_HEADS_A
    n_q = q_ref.shape[0] // T
    hcols = lambda h: slice(h * HEAD_V, (h + 1) * HEAD_V)
    rows = lambda t: slice(t * T, (t + 1) * T)

    vt = v_ref[...].T
    for h in range(N_HEADS_A):
        vt_ref[h, 0:HEAD_V, :] = vt[hcols(h), :]
        vt_ref[h, HEAD_V:, :] = jnp.ones((ONES_ROWS, vt.shape[1]), BF16)

    lv = lamv_ref[...]
    lam = (jnp.exp(jnp.sum(lv[0:1] * lv[1:2], axis=-1, keepdims=True))
           - jnp.exp(jnp.sum(lv[2:3] * lv[3:4], axis=-1, keepdims=True)) + LAM_INIT)
    lane = lax.broadcasted_iota(jnp.int32, (T, HEAD_V), 1)

    def prepare(i):
        slot = i % 2
        for h in range(N_HEADS_A):
            qs = q_ref[rows(i), hcols(h)]
            zero = jnp.zeros_like(qs)
            qz_ref[slot, 2 * h] = jnp.where(lane < HEAD_QK, qs, zero)
            qz_ref[slot, 2 * h + 1] = jnp.where(lane >= HEAD_QK, qs, zero)

    def finalize(i):
        slot = i % 2
        for h in range(N_HEADS_A):
            a1, a2 = a_ref[slot, 2 * h], a_ref[slot, 2 * h + 1]
            o = (a1[:HEAD_V] / a1[HEAD_V:HEAD_V + 1]
                 - lam * (a2[:HEAD_V] / a2[HEAD_V:HEAD_V + 1]))
            ms = jnp.mean(o * o, axis=0, keepdims=True)
            y = (o * lax.rsqrt(ms + EPS)).T
            o_ref[rows(i), hcols(h)] = ((y * subg_ref[...]) * (1.0 - LAM_INIT)).astype(BF16)

    units = [(i, j, n) for i in range(n_q) for j in range(i + 1) for n in range(NS)]
    scores, probs, alphas = {}, {}, {}

    def score(u):
        i, j, n = units[u]
        kc = k_ref[rows(j), hcols(n // 2)]
        scores[u] = lax.dot_general(kc, qz_ref[i % 2, n], (((1,), (1,)), ((), ())),
                                    preferred_element_type=F32)

    def softmax(u):
        i, j, n = units[u]
        s = scores.pop(u)
        if j >= i - 1:
            s = s + bias_ref[n // 2, j - (i - 1)]
        m_new = jnp.max(s, axis=0, keepdims=True)
        if j > 0:
            m_old = m_ref[i % 2, n]
            m_new = jnp.maximum(m_old, m_new)
            alphas[u] = jnp.exp2(m_old - m_new)
        probs[u] = jnp.exp2(s - m_new).astype(BF16)
        m_ref[i % 2, n] = m_new

    def value(u):
        i, j, n = units[u]
        vc = vt_ref[n // 2, :, rows(j)]
        pv = jnp.dot(vc, probs.pop(u), preferred_element_type=F32)
        a_ref[i % 2, n] = pv if j == 0 else alphas.pop(u) * a_ref[i % 2, n] + pv

    for step in range(len(units) + 2 * ATTN_LAG):
        if step < len(units):
            if units[step][1:] == (0, 0):
                prepare(units[step][0])
            score(step)
        if 0 <= step - ATTN_LAG < len(units):
            softmax(step - ATTN_LAG)
        done = step - 2 * ATTN_LAG
        if 0 <= done < len(units):
            value(done)
            i, j, n = units[done]
            if (j, n) == (i, NS - 1):
                finalize(i)

    for w_ref, wb_ref in zip(cast_in, cast_out):
        wb_ref[...] = w_ref[...].astype(BF16)


def _attention(qkv, bias_tiles, lamv, subg, weights):
    bsz, s, _ = qkv.shape
    T = ATTN_TILE
    W = ATTN_WIDTH
    NS = 2 * N_HEADS_A
    steps = bsz

    def slab_spec(w, squeeze):
        rows, cols = w.shape[-2:]
        assert rows % (steps * CAST_ROWS) == 0
        block = (rows // steps, cols)
        if squeeze:
            return pl.BlockSpec((None,) + block, lambda b: (0, b, 0))
        return pl.BlockSpec(block, lambda b: (b, 0))

    seq = lambda col: pl.BlockSpec((None, s, W), lambda b: (b, 0, col))
    out = pl.pallas_call(
        _attn_kernel,
        grid=(bsz,),
        in_specs=[_resident((4, HEAD_QK)),
                  _resident((1, HEAD_V)),
                  seq(0), seq(1), seq(2),
                  _resident((N_HEADS_A, 2, T, T))] + [slab_spec(w, True) for w in weights],
        out_specs=[seq(0)] + [slab_spec(w, False) for w in weights],
        out_shape=[jax.ShapeDtypeStruct((bsz, s, W), BF16)]
        + [jax.ShapeDtypeStruct(w.shape[-2:], BF16) for w in weights],
        scratch_shapes=[pltpu.VMEM((N_HEADS_A, HEAD_V + ONES_ROWS, s), BF16),
                        pltpu.VMEM((2, NS, T, HEAD_V), BF16),
                        pltpu.VMEM((2, NS, 1, T), F32),
                        pltpu.VMEM((2, NS, HEAD_V + ONES_ROWS, T), F32)],
        compiler_params=_params(("arbitrary",)),
        name="diff_attn",
    )(lamv, subg.reshape(1, HEAD_V), qkv, qkv, qkv, bias_tiles, *weights)
    return out[0], out[1:]


def _mixffn_kernel(x_ref, ada_ref, a_ref, u_ref, gate_ref, woa_ref, woc_ref, boc_ref,
                   wout_ref, g2_ref, w1_ref, w2_ref, gf_ref, o_ref,
                   x1_ref, h2_ref, act_ref):
    D = D_MODEL
    a = jnp.dot(a_ref[...], woa_ref[...], preferred_element_type=F32)
    cv = jnp.dot(u_ref[...], woc_ref[...], preferred_element_type=F32) + boc_ref[...]
    gate = gate_ref[...]
    y = (_sigmoid(gate[:, :D].astype(F32)) * a
         + _sigmoid(gate[:, D:].astype(F32)) * cv)
    z = jnp.dot(y.astype(BF16), wout_ref[...], preferred_element_type=F32)
    x1 = x_ref[...] + ada_ref[2:3, :] * z
    x1_ref[...] = x1
    ms = jnp.mean(x1 * x1, axis=-1, keepdims=True)
    yn = x1 * lax.rsqrt(ms + EPS)
    h2_ref[...] = ((yn * g2_ref[...]) * (1.0 + ada_ref[4:5, :]) + ada_ref[3:4, :]).astype(BF16)

    h = h2_ref[...]
    for c in range(0, D_FF, FF_CHUNK):
        fg = jnp.dot(h, w1_ref[:, c:c + FF_CHUNK], preferred_element_type=F32)
        fu = jnp.dot(h, w1_ref[:, D_FF + c:D_FF + c + FF_CHUNK],
                     preferred_element_type=F32)
        act_ref[:, c:c + FF_CHUNK] = (_silu(fg) * fu).astype(BF16)
    z2 = jnp.dot(act_ref[...], w2_ref[...], preferred_element_type=F32)
    x2 = x1_ref[...] + ada_ref[5:6, :] * z2
    ms2 = jnp.mean(x2 * x2, axis=-1, keepdims=True)
    o_ref[...] = (x2 * lax.rsqrt(ms2 + EPS)) * gf_ref[...]


def _mix_ffn(x, ada3, a_n, u, gates, woa, woc, boc, wout, g2, w1, w2, gf):
    bsz, s, d = x.shape
    tm = ROW_TILE
    row = lambda n: pl.BlockSpec((None, tm, n), lambda b, i: (b, i, 0))
    return pl.pallas_call(
        _mixffn_kernel,
        grid=(bsz, s // tm),
        in_specs=[row(d),
                  pl.BlockSpec((None, 6, d), lambda b, i: (b, 0, 0)),
                  row(ATTN_WIDTH), row(CONV_WIDTH), row(GATE_COLS),
                  _resident((ATTN_WIDTH, d)), _resident((CONV_WIDTH, d)),
                  _resident((1, d)), _resident((d, d)), _resident((1, d)),
                  _resident((d, 2 * D_FF)), _resident((D_FF, d)), _resident((1, d))],
        out_specs=row(d),
        out_shape=jax.ShapeDtypeStruct((bsz, s, d), F32),
        scratch_shapes=[pltpu.VMEM((tm, d), F32), pltpu.VMEM((tm, d), BF16),
                        pltpu.VMEM((tm, D_FF), BF16)],
        compiler_params=_params(("parallel", "arbitrary")),
        name="mix_ffn",
    )(x, ada3, a_n, u, gates, woa, woc, boc.reshape(1, d), wout, g2.reshape(1, d),
      w1, w2, gf.reshape(1, d))


def kernel(x, c, w_ada, b_ada, norm1_g, norm2_g, final_g, w_in, lambda_q1, lambda_k1,
           lambda_q2, lambda_k2, rel_bias, attn_sub_g, w_o_attn, conv_w, conv_b,
           conv_ln_g, conv_ln_b, w_o_conv, b_o_conv, w_out, w_ffn_in, w_ffn_out):
    bsz, s, d = x.shape
    assert w_ada.shape[0] == 1, "single-layer block"
    assert (d, s % ROW_TILE, s % ATTN_TILE) == (D_MODEL, 0, 0)
    assert PROJ_CHUNK == CONV_WIDTH and ATTN_WIDTH % PROJ_CHUNK == 0
    l = 0
    ada, w_in_bf, bias_tiles = _prologue(c, w_ada, b_ada, w_in, rel_bias)
    ada3 = ada.reshape(bsz, 6, d)
    qkv, u, gates = _in_proj(x, ada3, norm1_g[l], w_in_bf, conv_w[l],
                             conv_b[l], conv_ln_g[l], conv_ln_b[l])
    lamv = jnp.stack([lambda_q1[l], lambda_k1[l], lambda_q2[l], lambda_k2[l]])
    a_n, (woa, woc, wout, w1, w2) = _attention(
        qkv, bias_tiles, lamv, attn_sub_g[l],
        [w_o_attn, w_o_conv, w_out, w_ffn_in, w_ffn_out])
    return _mix_ffn(x, ada3, a_n, u, gates, woa, woc, b_o_conv[l], wout, norm2_g[l],
                    w1, w2, final_g)
```

```python
import math

import jax
import jax.numpy as jnp
from jax import lax
from jax.experimental import pallas as pl
from jax.experimental.pallas import tpu as pltpu

D_MODEL = 1024
N_HEADS_A = 4
HEAD_QK = 64
HEAD_V = 2 * HEAD_QK
ATTN_WIDTH = N_HEADS_A * HEAD_V
CONV_WIDTH = 512
CONV_KERNEL = 31
QKV_COLS = 3 * ATTN_WIDTH
GLU_COLS = 2 * CONV_WIDTH
GATE_COLS = 2 * D_MODEL
IN_COLS = QKV_COLS + GLU_COLS + GATE_COLS
D_FF = 2816
N_BUCKETS = 32
MAX_EXACT = 16
MAX_DISTANCE = 128
EPS = 1e-6
NEG_INF = -1e30
LAM_INIT = 0.8 - 0.6 * math.exp(-0.3 * 0)

V7X_VMEM_BYTES = 64 * 1024 * 1024
VMEM_LIMIT = V7X_VMEM_BYTES - 12 * 1024 * 1024

ROW_TILE = 512
ATTN_TILE = 256
ATTN_LAG = 4
ONES_ROWS = 16
CAST_ROWS = 16
LOG2E = math.log2(math.e)
Q_SCALE = HEAD_QK ** -0.5 * LOG2E
CONV_HALO = 32
CONV_CHUNK = 16
PROJ_CHUNK = 512
FF_CHUNK = 256

BF16 = jnp.bfloat16
F32 = jnp.float32


def _sigmoid(x):
    return 1.0 / (1.0 + jnp.exp2(x * (-LOG2E)))


def _silu(x):
    return x * _sigmoid(x)


def _params(sem):
    return pltpu.CompilerParams(dimension_semantics=sem, vmem_limit_bytes=VMEM_LIMIT)


def _resident(shape):
    nd = len(shape)
    return pl.BlockSpec(shape, lambda *_: (0,) * nd, pipeline_mode=pl.Buffered(1))


def _bias_tile(rel_ref, h, t):
    T = ATTN_TILE
    key = lax.broadcasted_iota(jnp.int32, (T, T), 0)
    qry = lax.broadcasted_iota(jnp.int32, (T, T), 1)
    dist = (1 - t) * T + qry - key
    n = jnp.maximum(dist, 0)
    large = MAX_EXACT + (jnp.log(jnp.maximum(n, 1).astype(F32) / MAX_EXACT)
                         / math.log(MAX_DISTANCE / MAX_EXACT)
                         * (N_BUCKETS - MAX_EXACT)).astype(jnp.int32)
    large = jnp.minimum(large, N_BUCKETS - 1)
    bucket = jnp.where(n < MAX_EXACT, n, large)
    far = rel_ref[N_BUCKETS - 1, h]
    bias = jnp.zeros((T, T), F32)
    for b in range(N_BUCKETS - 1):
        bias = jnp.where(bucket == b, (rel_ref[b, h] - far) * LOG2E, bias)
    return jnp.where(dist >= 0, bias, NEG_INF)


def _prologue_kernel(rel_ref, c_ref, w_ref, b_ref, win_ref, o_ref, win_bf_ref, bias_ref):
    j = pl.program_id(0)
    ca = _silu(c_ref[...])
    o_ref[...] = jnp.dot(ca.astype(BF16), w_ref[...].astype(BF16),
                         preferred_element_type=F32) + b_ref[...]
    win_bf_ref[...] = win_ref[...].astype(BF16)
    bias_ref[...] = _bias_tile(rel_ref, j // 2, j % 2)


def _prologue(c, w, b, w_in, rel_bias):
    bsz, d = c.shape
    n = w.shape[-1]
    steps = 2 * N_HEADS_A
    tn = n // steps
    _, rows, cols = w_in.shape
    tr = rows // steps
    T = ATTN_TILE
    return pl.pallas_call(
        _prologue_kernel,
        grid=(steps,),
        in_specs=[pl.BlockSpec(memory_space=pltpu.SMEM),
                  pl.BlockSpec((bsz, d), lambda j: (0, 0)),
                  pl.BlockSpec((None, d, tn), lambda j: (0, 0, j)),
                  pl.BlockSpec((1, tn), lambda j: (0, j)),
                  pl.BlockSpec((None, tr, cols), lambda j: (0, j, 0))],
        out_specs=[pl.BlockSpec((bsz, tn), lambda j: (0, j)),
                   pl.BlockSpec((tr, cols), lambda j: (j, 0)),
                   pl.BlockSpec((None, None, T, T), lambda j: (j // 2, j % 2, 0, 0))],
        out_shape=[jax.ShapeDtypeStruct((bsz, n), F32),
                   jax.ShapeDtypeStruct((rows, cols), BF16),
                   jax.ShapeDtypeStruct((N_HEADS_A, 2, T, T), F32)],
        compiler_params=_params(("arbitrary",)),
        name="prologue",
    )(rel_bias, c, w, b, w_in)


def _conv_chunk(r0, ubuf, w_ref, cb_ref, lng_ref, lnb_ref, o_ref):
    off = CONV_HALO - (CONV_KERNEL - 1)
    acc = jnp.zeros((CONV_CHUNK // 8, 8, CONV_WIDTH), F32)
    for k in range(CONV_KERNEL):
        a, b = divmod(k + off, 8)
        rows = ubuf[b, r0 + 8 * a:r0 + 8 * a + CONV_CHUNK, :]
        acc = acc + rows.reshape(acc.shape) * w_ref[k][None]
    u = acc.reshape(CONV_CHUNK, CONV_WIDTH) + cb_ref[...]
    mu = jnp.mean(u, axis=-1, keepdims=True)
    d = u - mu
    var = jnp.mean(d * d, axis=-1, keepdims=True)
    y = d * lax.rsqrt(var + EPS) * lng_ref[...] + lnb_ref[...]
    o_ref[r0:r0 + CONV_CHUNK, :] = _silu(y).astype(BF16)


def _inproj_kernel(x_ref, ada_ref, g_ref, w_ref, cw_ref, cb_ref, lng_ref, lnb_ref,
                   qkv_ref, u_ref, gate_ref, ubuf):
    TS, H, C = ROW_TILE, CONV_HALO, CONV_WIDTH
    i = pl.program_id(1)

    @pl.when(i == 0)
    def _():
        ubuf[0, 0:H, :] = jnp.zeros((H, C), F32)

    @pl.when(i > 0)
    def _():
        ubuf[0, 0:H, :] = ubuf[0, TS:TS + H, :]

    x = x_ref[...]
    ms = jnp.mean(x * x, axis=-1, keepdims=True)
    y = x * lax.rsqrt(ms + EPS)
    h = (y * g_ref[...]) * (1.0 + ada_ref[1:2, :]) + ada_ref[0:1, :]
    hb = h.astype(BF16)

    def proj(base):
        return jnp.dot(hb, w_ref[:, base:base + PROJ_CHUNK], preferred_element_type=F32)

    ubuf[0, H:H + TS, :] = proj(QKV_COLS) * _sigmoid(proj(QKV_COLS + C))
    n = TS + H - 8
    for b in range(1, 8):
        ubuf[b, 0:n, :] = ubuf[0, b:b + n, :]
    jobs = ([(qkv_ref, c, c) for c in range(0, QKV_COLS, PROJ_CHUNK)]
            + [(gate_ref, c, QKV_COLS + GLU_COLS + c) for c in range(0, GATE_COLS, PROJ_CHUNK)])
    rows = list(range(0, TS, CONV_CHUNK))
    for n, (ref, c, base) in enumerate(jobs):
        r = proj(base)
        if base < ATTN_WIDTH:
            r = r * Q_SCALE
        ref[:, c:c + PROJ_CHUNK] = r.astype(BF16)
        for r0 in rows[n * len(rows) // len(jobs):(n + 1) * len(rows) // len(jobs)]:
            _conv_chunk(r0, ubuf, cw_ref, cb_ref, lng_ref, lnb_ref, u_ref)


def _in_proj(x, ada3, g, w, cw, cb, lng, lnb):
    bsz, s, d = x.shape
    tm = ROW_TILE
    C = CONV_WIDTH
    row = lambda n: pl.BlockSpec((None, tm, n), lambda b, i: (b, i, 0))
    vec = lambda a: a.reshape(1, C)
    return pl.pallas_call(
        _inproj_kernel,
        grid=(bsz, s // tm),
        in_specs=[row(d),
                  pl.BlockSpec((None, 6, d), lambda b, i: (b, 0, 0)),
                  _resident((1, d)),
                  _resident((d, IN_COLS)),
                  _resident((CONV_KERNEL, 8, C)),
                  _resident((1, C)), _resident((1, C)), _resident((1, C))],
        out_specs=[row(QKV_COLS), row(C), row(GATE_COLS)],
        out_shape=[jax.ShapeDtypeStruct((bsz, s, QKV_COLS), BF16),
                   jax.ShapeDtypeStruct((bsz, s, C), BF16),
                   jax.ShapeDtypeStruct((bsz, s, GATE_COLS), BF16)],
        scratch_shapes=[pltpu.VMEM((8, tm + CONV_HALO, C), F32)],
        compiler_params=_params(("arbitrary", "arbitrary")),
        name="in_proj",
    )(x, ada3, g.reshape(1, d), w, jnp.broadcast_to(cw[:, None, :], (CONV_KERNEL, 8, C)),
      vec(cb), vec(lng), vec(lnb))


def _attn_kernel(lamv_ref, subg_ref, q_ref, k_ref, v_ref, bias_ref, *rest):
    n_cast = (len(rest) - 5) // 2
    cast_in, o_ref, cast_out = rest[:n_cast], rest[n_cast], rest[n_cast + 1:2 * n_cast + 1]
    vt_ref, qz_ref, m_ref, a_ref = rest[2 * n_cast + 1:]
    T = ATTN_TILE
    NS = 2 * N_HEADS_A
    n_q = q_ref.shape[0] // T
    hcols = lambda h: slice(h * HEAD_V, (h + 1) * HEAD_V)
    rows = lambda t: slice(t * T, (t + 1) * T)

    vt = v_ref[...].T
    for h in range(N_HEADS_A):
        vt_ref[h, 0:HEAD_V, :] = vt[hcols(h), :]
        vt_ref[h, HEAD_V:, :] = jnp.ones((ONES_ROWS, vt.shape[1]), BF16)

    lv = lamv_ref[...]
    lam = (jnp.exp(jnp.sum(lv[0:1] * lv[1:2], axis=-1, keepdims=True))
           - jnp.exp(jnp.sum(lv[2:3] * lv[3:4], axis=-1, keepdims=True)) + LAM_INIT)
    lane = lax.broadcasted_iota(jnp.int32, (T, HEAD_V), 1)

    def prepare(i):
        slot = i % 2
        for h in range(N_HEADS_A):
            qs = q_ref[rows(i), hcols(h)]
            zero = jnp.zeros_like(qs)
            qz_ref[slot, 2 * h] = jnp.where(lane < HEAD_QK, qs, zero)
            qz_ref[slot, 2 * h + 1] = jnp.where(lane >= HEAD_QK, qs, zero)

    def finalize(i):
        slot = i % 2
        for h in range(N_HEADS_A):
            a1, a2 = a_ref[slot, 2 * h], a_ref[slot, 2 * h + 1]
            o = (a1[:HEAD_V] / a1[HEAD_V:HEAD_V + 1]
                 - lam * (a2[:HEAD_V] / a2[HEAD_V:HEAD_V + 1]))
            ms = jnp.mean(o * o, axis=0, keepdims=True)
            y = (o * lax.rsqrt(ms + EPS)).T
            o_ref[rows(i), hcols(h)] = ((y * subg_ref[...]) * (1.0 - LAM_INIT)).astype(BF16)

    units = [(i, j, n) for i in range(n_q) for j in range(i + 1) for n in range(NS)]
    scores, probs, alphas = {}, {}, {}

    def score(u):
        i, j, n = units[u]
        kc = k_ref[rows(j), hcols(n // 2)]
        scores[u] = lax.dot_general(kc, qz_ref[i % 2, n], (((1,), (1,)), ((), ())),
                                    preferred_element_type=F32)

    def softmax(u):
        i, j, n = units[u]
        s = scores.pop(u)
        if j >= i - 1:
            s = s + bias_ref[n // 2, j - (i - 1)]
        m_new = jnp.max(s, axis=0, keepdims=True)
        if j > 0:
            m_old = m_ref[i % 2, n]
            m_new = jnp.maximum(m_old, m_new)
            alphas[u] = jnp.exp2(m_old - m_new)
        probs[u] = jnp.exp2(s - m_new).astype(BF16)
        m_ref[i % 2, n] = m_new

    def value(u):
        i, j, n = units[u]
        vc = vt_ref[n // 2, :, rows(j)]
        pv = jnp.dot(vc, probs.pop(u), preferred_element_type=F32)
        a_ref[i % 2, n] = pv if j == 0 else alphas.pop(u) * a_ref[i % 2, n] + pv

    for step in range(len(units) + 2 * ATTN_LAG):
        if step < len(units):
            if units[step][1:] == (0, 0):
                prepare(units[step][0])
            score(step)
        if 0 <= step - ATTN_LAG < len(units):
            softmax(step - ATTN_LAG)
        done = step - 2 * ATTN_LAG
        if 0 <= done < len(units):
            value(done)
            i, j, n = units[done]
            if (j, n) == (i, NS - 1):
                finalize(i)

    for w_ref, wb_ref in zip(cast_in, cast_out):
        wb_ref[...] = w_ref[...].astype(BF16)


def _attention(qkv, bias_tiles, lamv, subg, weights):
    bsz, s, _ = qkv.shape
    T = ATTN_TILE
    W = ATTN_WIDTH
    NS = 2 * N_HEADS_A
    steps = bsz

    def slab_spec(w, squeeze):
        rows, cols = w.shape[-2:]
        assert rows % (steps * CAST_ROWS) == 0
        block = (rows // steps, cols)
        if squeeze:
            return pl.BlockSpec((None,) + block, lambda b: (0, b, 0))
        return pl.BlockSpec(block, lambda b: (b, 0))

    seq = lambda col: pl.BlockSpec((None, s, W), lambda b: (b, 0, col))
    out = pl.pallas_call(
        _attn_kernel,
        grid=(bsz,),
        in_specs=[_resident((4, HEAD_QK)),
                  _resident((1, HEAD_V)),
                  seq(0), seq(1), seq(2),
                  _resident((N_HEADS_A, 2, T, T))] + [slab_spec(w, True) for w in weights],
        out_specs=[seq(0)] + [slab_spec(w, False) for w in weights],
        out_shape=[jax.ShapeDtypeStruct((bsz, s, W), BF16)]
        + [jax.ShapeDtypeStruct(w.shape[-2:], BF16) for w in weights],
        scratch_shapes=[pltpu.VMEM((N_HEADS_A, HEAD_V + ONES_ROWS, s), BF16),
                        pltpu.VMEM((2, NS, T, HEAD_V), BF16),
                        pltpu.VMEM((2, NS, 1, T), F32),
                        pltpu.VMEM((2, NS, HEAD_V + ONES_ROWS, T), F32)],
        compiler_params=_params(("arbitrary",)),
        name="diff_attn",
    )(lamv, subg.reshape(1, HEAD_V), qkv, qkv, qkv, bias_tiles, *weights)
    return out[0], out[1:]


def _mixffn_kernel(x_ref, ada_ref, a_ref, u_ref, gate_ref, woa_ref, woc_ref, boc_ref,
                   wout_ref, g2_ref, w1_ref, w2_ref, gf_ref, o_ref,
                   x1_ref, h2_ref, act_ref):
    D = D_MODEL
    a = jnp.dot(a_ref[...], woa_ref[...], preferred_element_type=F32)
    cv = jnp.dot(u_ref[...], woc_ref[...], preferred_element_type=F32) + boc_ref[...]
    gate = gate_ref[...]
    y = (_sigmoid(gate[:, :D].astype(F32)) * a
         + _sigmoid(gate[:, D:].astype(F32)) * cv)
    z = jnp.dot(y.astype(BF16), wout_ref[...], preferred_element_type=F32)
    x1 = x_ref[...] + ada_ref[2:3, :] * z
    x1_ref[...] = x1
    ms = jnp.mean(x1 * x1, axis=-1, keepdims=True)
    yn = x1 * lax.rsqrt(ms + EPS)
    h2_ref[...] = ((yn * g2_ref[...]) * (1.0 + ada_ref[4:5, :]) + ada_ref[3:4, :]).astype(BF16)

    h = h2_ref[...]
    for c in range(0, D_FF, FF_CHUNK):
        fg = jnp.dot(h, w1_ref[:, c:c + FF_CHUNK], preferred_element_type=F32)
        fu = jnp.dot(h, w1_ref[:, D_FF + c:D_FF + c + FF_CHUNK],
                     preferred_element_type=F32)
        act_ref[:, c:c + FF_CHUNK] = (_silu(fg) * fu).astype(BF16)
    z2 = jnp.dot(act_ref[...], w2_ref[...], preferred_element_type=F32)
    x2 = x1_ref[...] + ada_ref[5:6, :] * z2
    ms2 = jnp.mean(x2 * x2, axis=-1, keepdims=True)
    o_ref[...] = (x2 * lax.rsqrt(ms2 + EPS)) * gf_ref[...]


def _mix_ffn(x, ada3, a_n, u, gates, woa, woc, boc, wout, g2, w1, w2, gf):
    bsz, s, d = x.shape
    tm = ROW_TILE
    row = lambda n: pl.BlockSpec((None, tm, n), lambda b, i: (b, i, 0))
    return pl.pallas_call(
        _mixffn_kernel,
        grid=(bsz, s // tm),
        in_specs=[row(d),
                  pl.BlockSpec((None, 6, d), lambda b, i: (b, 0, 0)),
                  row(ATTN_WIDTH), row(CONV_WIDTH), row(GATE_COLS),
                  _resident((ATTN_WIDTH, d)), _resident((CONV_WIDTH, d)),
                  _resident((1, d)), _resident((d, d)), _resident((1, d)),
                  _resident((d, 2 * D_FF)), _resident((D_FF, d)), _resident((1, d))],
        out_specs=row(d),
        out_shape=jax.ShapeDtypeStruct((bsz, s, d), F32),
        scratch_shapes=[pltpu.VMEM((tm, d), F32), pltpu.VMEM((tm, d), BF16),
                        pltpu.VMEM((tm, D_FF), BF16)],
        compiler_params=_params(("parallel", "arbitrary")),
        name="mix_ffn",
    )(x, ada3, a_n, u, gates, woa, woc, boc.reshape(1, d), wout, g2.reshape(1, d),
      w1, w2, gf.reshape(1, d))


def kernel(x, c, w_ada, b_ada, norm1_g, norm2_g, final_g, w_in, lambda_q1, lambda_k1,
           lambda_q2, lambda_k2, rel_bias, attn_sub_g, w_o_attn, conv_w, conv_b,
           conv_ln_g, conv_ln_b, w_o_conv, b_o_conv, w_out, w_ffn_in, w_ffn_out):
    bsz, s, d = x.shape
    assert w_ada.shape[0] == 1, "single-layer block"
    assert (d, s % ROW_TILE, s % ATTN_TILE) == (D_MODEL, 0, 0)
    assert PROJ_CHUNK == CONV_WIDTH and ATTN_WIDTH % PROJ_CHUNK == 0
    l = 0
    ada, w_in_bf, bias_tiles = _prologue(c, w_ada, b_ada, w_in, rel_bias)
    ada3 = ada.reshape(bsz, 6, d)
    qkv, u, gates = _in_proj(x, ada3, norm1_g[l], w_in_bf, conv_w[l],
                             conv_b[l], conv_ln_g[l], conv_ln_b[l])
    lamv = jnp.stack([lambda_q1[l], lambda_k1[l], lambda_q2[l], lambda_k2[l]])
    a_n, (woa, woc, wout, w1, w2) = _attention(
        qkv, bias_tiles, lamv, attn_sub_g[l],
        [w_o_attn, w_o_conv, w_out, w_ffn_in, w_ffn_out])
    return _mix_ffn(x, ada3, a_n, u, gates, woa, woc, b_o_conv[l], wout, norm2_g[l],
                    w1, w2, final_g)
```

```python
import math

import jax
import jax.numpy as jnp
from jax import lax
from jax.experimental import pallas as pl
from jax.experimental.pallas import tpu as pltpu

D_MODEL = 1024
N_HEADS_A = 4
HEAD_QK = 64
HEAD_V = 2 * HEAD_QK
ATTN_WIDTH = N_HEADS_A * HEAD_V
CONV_WIDTH = 512
CONV_KERNEL = 31
QKV_COLS = 3 * ATTN_WIDTH
GLU_COLS = 2 * CONV_WIDTH
GATE_COLS = 2 * D_MODEL
IN_COLS = QKV_COLS + GLU_COLS + GATE_COLS
D_FF = 2816
N_BUCKETS = 32
MAX_EXACT = 16
MAX_DISTANCE = 128
EPS = 1e-6
NEG_INF = -1e30
LAM_INIT = 0.8 - 0.6 * math.exp(-0.3 * 0)

V7X_VMEM_BYTES = 64 * 1024 * 1024
VMEM_LIMIT = V7X_VMEM_BYTES - 12 * 1024 * 1024

ROW_TILE = 512
ATTN_TILE = 256
ATTN_LAG = 4
ONES_ROWS = 16
CAST_ROWS = 16
LOG2E = math.log2(math.e)
Q_SCALE = HEAD_QK ** -0.5 * LOG2E
CONV_HALO = 32
CONV_CHUNK = 16
PROJ_CHUNK = 512
FF_CHUNK = 256

BF16 = jnp.bfloat16
F32 = jnp.float32


def _sigmoid(x):
    return 1.0 / (1.0 + jnp.exp2(x * (-LOG2E)))


def _silu(x):
    return x * _sigmoid(x)


def _params(sem):
    return pltpu.CompilerParams(dimension_semantics=sem, vmem_limit_bytes=VMEM_LIMIT)


def _resident(shape):
    nd = len(shape)
    return pl.BlockSpec(shape, lambda *_: (0,) * nd, pipeline_mode=pl.Buffered(1))


def _bias_tile(rel_ref, h, t):
    T = ATTN_TILE
    key = lax.broadcasted_iota(jnp.int32, (T, T), 0)
    qry = lax.broadcasted_iota(jnp.int32, (T, T), 1)
    dist = (1 - t) * T + qry - key
    n = jnp.maximum(dist, 0)
    large = MAX_EXACT + (jnp.log(jnp.maximum(n, 1).astype(F32) / MAX_EXACT)
                         / math.log(MAX_DISTANCE / MAX_EXACT)
                         * (N_BUCKETS - MAX_EXACT)).astype(jnp.int32)
    large = jnp.minimum(large, N_BUCKETS - 1)
    bucket = jnp.where(n < MAX_EXACT, n, large)
    far = rel_ref[N_BUCKETS - 1, h]
    bias = jnp.zeros((T, T), F32)
    for b in range(N_BUCKETS - 1):
        bias = jnp.where(bucket == b, (rel_ref[b, h] - far) * LOG2E, bias)
    return jnp.where(dist >= 0, bias, NEG_INF)


def _prologue_kernel(rel_ref, c_ref, w_ref, b_ref, win_ref, o_ref, win_bf_ref, bias_ref):
    j = pl.program_id(0)
    ca = _silu(c_ref[...])
    o_ref[...] = jnp.dot(ca.astype(BF16), w_ref[...].astype(BF16),
                         preferred_element_type=F32) + b_ref[...]
    win_bf_ref[...] = win_ref[...].astype(BF16)
    bias_ref[...] = _bias_tile(rel_ref, j // 2, j % 2)


def _prologue(c, w, b, w_in, rel_bias):
    bsz, d = c.shape
    n = w.shape[-1]
    steps = 2 * N_HEADS_A
    tn = n // steps
    _, rows, cols = w_in.shape
    tr = rows // steps
    T = ATTN_TILE
    return pl.pallas_call(
        _prologue_kernel,
        grid=(steps,),
        in_specs=[pl.BlockSpec(memory_space=pltpu.SMEM),
                  pl.BlockSpec((bsz, d), lambda j: (0, 0)),
                  pl.BlockSpec((None, d, tn), lambda j: (0, 0, j)),
                  pl.BlockSpec((1, tn), lambda j: (0, j)),
                  pl.BlockSpec((None, tr, cols), lambda j: (0, j, 0))],
        out_specs=[pl.BlockSpec((bsz, tn), lambda j: (0, j)),
                   pl.BlockSpec((tr, cols), lambda j: (j, 0)),
                   pl.BlockSpec((None, None, T, T), lambda j: (j // 2, j % 2, 0, 0))],
        out_shape=[jax.ShapeDtypeStruct((bsz, n), F32),
                   jax.ShapeDtypeStruct((rows, cols), BF16),
                   jax.ShapeDtypeStruct((N_HEADS_A, 2, T, T), F32)],
        compiler_params=_params(("arbitrary",)),
        name="prologue",
    )(rel_bias, c, w, b, w_in)


def _conv_chunk(r0, ubuf, w_ref, cb_ref, lng_ref, lnb_ref, o_ref):
    off = CONV_HALO - (CONV_KERNEL - 1)
    acc = jnp.zeros((CONV_CHUNK // 8, 8, CONV_WIDTH), F32)
    for k in range(CONV_KERNEL):
        a, b = divmod(k + off, 8)
        rows = ubuf[b, r0 + 8 * a:r0 + 8 * a + CONV_CHUNK, :]
        acc = acc + rows.reshape(acc.shape) * w_ref[k][None]
    u = acc.reshape(CONV_CHUNK, CONV_WIDTH) + cb_ref[...]
    mu = jnp.mean(u, axis=-1, keepdims=True)
    d = u - mu
    var = jnp.mean(d * d, axis=-1, keepdims=True)
    y = d * lax.rsqrt(var + EPS) * lng_ref[...] + lnb_ref[...]
    o_ref[r0:r0 + CONV_CHUNK, :] = _silu(y).astype(BF16)


def _inproj_kernel(x_ref, ada_ref, g_ref, w_ref, cw_ref, cb_ref, lng_ref, lnb_ref,
                   qkv_ref, u_ref, gate_ref, ubuf):
    TS, H, C = ROW_TILE, CONV_HALO, CONV_WIDTH
    i = pl.program_id(1)

    @pl.when(i == 0)
    def _():
        ubuf[0, 0:H, :] = jnp.zeros((H, C), F32)

    @pl.when(i > 0)
    def _():
        ubuf[0, 0:H, :] = ubuf[0, TS:TS + H, :]

    x = x_ref[...]
    ms = jnp.mean(x * x, axis=-1, keepdims=True)
    y = x * lax.rsqrt(ms + EPS)
    h = (y * g_ref[...]) * (1.0 + ada_ref[1:2, :]) + ada_ref[0:1, :]
    hb = h.astype(BF16)

    def proj(base):
        return jnp.dot(hb, w_ref[:, base:base + PROJ_CHUNK], preferred_element_type=F32)

    ubuf[0, H:H + TS, :] = proj(QKV_COLS) * _sigmoid(proj(QKV_COLS + C))
    n = TS + H - 8
    for b in range(1, 8):
        ubuf[b, 0:n, :] = ubuf[0, b:b + n, :]
    jobs = ([(qkv_ref, c, c) for c in range(0, QKV_COLS, PROJ_CHUNK)]
            + [(gate_ref, c, QKV_COLS + GLU_COLS + c) for c in range(0, GATE_COLS, PROJ_CHUNK)])
    rows = list(range(0, TS, CONV_CHUNK))
    for n, (ref, c, base) in enumerate(jobs):
        r = proj(base)
        if base < ATTN_WIDTH:
            r = r * Q_SCALE
        ref[:, c:c + PROJ_CHUNK] = r.astype(BF16)
        for r0 in rows[n * len(rows) // len(jobs):(n + 1) * len(rows) // len(jobs)]:
            _conv_chunk(r0, ubuf, cw_ref, cb_ref, lng_ref, lnb_ref, u_ref)


def _in_proj(x, ada3, g, w, cw, cb, lng, lnb):
    bsz, s, d = x.shape
    tm = ROW_TILE
    C = CONV_WIDTH
    row = lambda n: pl.BlockSpec((None, tm, n), lambda b, i: (b, i, 0))
    vec = lambda a: a.reshape(1, C)
    return pl.pallas_call(
        _inproj_kernel,
        grid=(bsz, s // tm),
        in_specs=[row(d),
                  pl.BlockSpec((None, 6, d), lambda b, i: (b, 0, 0)),
                  _resident((1, d)),
                  _resident((d, IN_COLS)),
                  _resident((CONV_KERNEL, 8, C)),
                  _resident((1, C)), _resident((1, C)), _resident((1, C))],
        out_specs=[row(QKV_COLS), row(C), row(GATE_COLS)],
        out_shape=[jax.ShapeDtypeStruct((bsz, s, QKV_COLS), BF16),
                   jax.ShapeDtypeStruct((bsz, s, C), BF16),
                   jax.ShapeDtypeStruct((bsz, s, GATE_COLS), BF16)],
        scratch_shapes=[pltpu.VMEM((8, tm + CONV_HALO, C), F32)],
        compiler_params=_params(("arbitrary", "arbitrary")),
        name="in_proj",
    )(x, ada3, g.reshape(1, d), w, jnp.broadcast_to(cw[:, None, :], (CONV_KERNEL, 8, C)),
      vec(cb), vec(lng), vec(lnb))


def _attn_kernel(lamv_ref, subg_ref, q_ref, k_ref, v_ref, bias_ref, *rest):
    n_cast = (len(rest) - 6) // 2
    cast_in, o_ref, cast_out = rest[:n_cast], rest[n_cast], rest[n_cast + 1:2 * n_cast + 1]
    vt_ref, qz_ref, m_ref, a_ref, l_ref = rest[2 * n_cast + 1:]
    T = ATTN_TILE
    NS = 2 * N_HEADS_A
    n_q = q_ref.shape[0] // T
    hcols = lambda h: slice(h * HEAD_V, (h + 1) * HEAD_V)
    rows = lambda t: slice(t * T, (t + 1) * T)

    vt = v_ref[...].T
    for h in range(N_HEADS_A):
        vt_ref[h] = vt[hcols(h), :]

    lv = lamv_ref[...]
    lam = (jnp.exp(jnp.sum(lv[0:1] * lv[1:2], axis=-1, keepdims=True))
           - jnp.exp(jnp.sum(lv[2:3] * lv[3:4], axis=-1, keepdims=True)) + LAM_INIT)
    lane = lax.broadcasted_iota(jnp.int32, (T, HEAD_V), 1)

    def prepare(i):
        slot = i % 2
        for h in range(N_HEADS_A):
            qs = q_ref[rows(i), hcols(h)]
            zero = jnp.zeros_like(qs)
            qz_ref[slot, 2 * h] = jnp.where(lane < HEAD_QK, qs, zero)
            qz_ref[slot, 2 * h + 1] = jnp.where(lane >= HEAD_QK, qs, zero)

    def finalize(i):
        slot = i % 2
        for h in range(N_HEADS_A):
            o = (a_ref[slot, 2 * h] / l_ref[slot, 2 * h]
                 - lam * (a_ref[slot, 2 * h + 1] / l_ref[slot, 2 * h + 1]))
            ms = jnp.mean(o * o, axis=0, keepdims=True)
            y = (o * lax.rsqrt(ms + EPS)).T
            o_ref[rows(i), hcols(h)] = ((y * subg_ref[...]) * (1.0 - LAM_INIT)).astype(BF16)

    units = [(i, j, n) for i in range(n_q) for j in range(i + 1) for n in range(NS)]
    scores, probs, alphas = {}, {}, {}

    def score(u):
        i, j, n = units[u]
        kc = k_ref[rows(j), hcols(n // 2)]
        scores[u] = lax.dot_general(kc, qz_ref[i % 2, n], (((1,), (1,)), ((), ())),
                                    preferred_element_type=F32)

    def softmax(u):
        i, j, n = units[u]
        s = scores.pop(u)
        if j >= i - 1:
            s = s + bias_ref[n // 2, j - (i - 1)]
        m_new = jnp.max(s, axis=0, keepdims=True)
        if j > 0:
            m_old = m_ref[i % 2, n]
            m_new = jnp.maximum(m_old, m_new)
            alphas[u] = jnp.exp2(m_old - m_new)
        p = jnp.exp2(s - m_new)
        l_new = jnp.sum(p, axis=0, keepdims=True)
        if j > 0:
            l_new = alphas[u] * l_ref[i % 2, n] + l_new
        l_ref[i % 2, n] = l_new
        probs[u] = p.astype(BF16)
        m_ref[i % 2, n] = m_new

    def value(u):
        i, j, n = units[u]
        vc = vt_ref[n // 2, :, rows(j)]
        pv = jnp.dot(vc, probs.pop(u), preferred_element_type=F32)
        a_ref[i % 2, n] = pv if j == 0 else alphas.pop(u) * a_ref[i % 2, n] + pv

    for step in range(len(units) + 2 * ATTN_LAG):
        if step < len(units):
            if units[step][1:] == (0, 0):
                prepare(units[step][0])
            score(step)
        if 0 <= step - ATTN_LAG < len(units):
            softmax(step - ATTN_LAG)
        done = step - 2 * ATTN_LAG
        if 0 <= done < len(units):
            value(done)
            i, j, n = units[done]
            if (j, n) == (i, NS - 1):
                finalize(i)

    for w_ref, wb_ref in zip(cast_in, cast_out):
        wb_ref[...] = w_ref[...].astype(BF16)


def _attention(qkv, bias_tiles, lamv, subg, weights):
    bsz, s, _ = qkv.shape
    T = ATTN_TILE
    W = ATTN_WIDTH
    NS = 2 * N_HEADS_A
    steps = bsz

    def slab_spec(w, squeeze):
        rows, cols = w.shape[-2:]
        assert rows % (steps * CAST_ROWS) == 0
        block = (rows // steps, cols)
        if squeeze:
            return pl.BlockSpec((None,) + block, lambda b: (0, b, 0))
        return pl.BlockSpec(block, lambda b: (b, 0))

    seq = lambda col: pl.BlockSpec((None, s, W), lambda b: (b, 0, col))
    out = pl.pallas_call(
        _attn_kernel,
        grid=(bsz,),
        in_specs=[_resident((4, HEAD_QK)),
                  _resident((1, HEAD_V)),
                  seq(0), seq(1), seq(2),
                  _resident((N_HEADS_A, 2, T, T))] + [slab_spec(w, True) for w in weights],
        out_specs=[seq(0)] + [slab_spec(w, False) for w in weights],
        out_shape=[jax.ShapeDtypeStruct((bsz, s, W), BF16)]
        + [jax.ShapeDtypeStruct(w.shape[-2:], BF16) for w in weights],
        scratch_shapes=[pltpu.VMEM((N_HEADS_A, HEAD_V, s), BF16),
                        pltpu.VMEM((2, NS, T, HEAD_V), BF16),
                        pltpu.VMEM((2, NS, 1, T), F32),
                        pltpu.VMEM((2, NS, HEAD_V, T), F32),
                        pltpu.VMEM((2, NS, 1, T), F32)],
        compiler_params=_params(("arbitrary",)),
        name="diff_attn",
    )(lamv, subg.reshape(1, HEAD_V), qkv, qkv, qkv, bias_tiles, *weights)
    return out[0], out[1:]


def _mixffn_kernel(x_ref, ada_ref, a_ref, u_ref, gate_ref, woa_ref, woc_ref, boc_ref,
                   wout_ref, g2_ref, w1_ref, w2_ref, gf_ref, o_ref,
                   x1_ref, h2_ref, act_ref):
    D = D_MODEL
    a = jnp.dot(a_ref[...], woa_ref[...], preferred_element_type=F32)
    cv = jnp.dot(u_ref[...], woc_ref[...], preferred_element_type=F32) + boc_ref[...]
    gate = gate_ref[...]
    y = (_sigmoid(gate[:, :D].astype(F32)) * a
         + _sigmoid(gate[:, D:].astype(F32)) * cv)
    z = jnp.dot(y.astype(BF16), wout_ref[...], preferred_element_type=F32)
    x1 = x_ref[...] + ada_ref[2:3, :] * z
    x1_ref[...] = x1
    ms = jnp.mean(x1 * x1, axis=-1, keepdims=True)
    yn = x1 * lax.rsqrt(ms + EPS)
    h2_ref[...] = ((yn * g2_ref[...]) * (1.0 + ada_ref[4:5, :]) + ada_ref[3:4, :]).astype(BF16)

    h = h2_ref[...]
    for c in range(0, D_FF, FF_CHUNK):
        fg = jnp.dot(h, w1_ref[:, c:c + FF_CHUNK], preferred_element_type=F32)
        fu = jnp.dot(h, w1_ref[:, D_FF + c:D_FF + c + FF_CHUNK],
                     preferred_element_type=F32)
        act_ref[:, c:c + FF_CHUNK] = (_silu(fg) * fu).astype(BF16)
    z2 = jnp.dot(act_ref[...], w2_ref[...], preferred_element_type=F32)
    x2 = x1_ref[...] + ada_ref[5:6, :] * z2
    ms2 = jnp.mean(x2 * x2, axis=-1, keepdims=True)
    o_ref[...] = (x2 * lax.rsqrt(ms2 + EPS)) * gf_ref[...]


def _mix_ffn(x, ada3, a_n, u, gates, woa, woc, boc, wout, g2, w1, w2, gf):
    bsz, s, d = x.shape
    tm = ROW_TILE
    row = lambda n: pl.BlockSpec((None, tm, n), lambda b, i: (b, i, 0))
    return pl.pallas_call(
        _mixffn_kernel,
        grid=(bsz, s // tm),
        in_specs=[row(d),
                  pl.BlockSpec((None, 6, d), lambda b, i: (b, 0, 0)),
                  row(ATTN_WIDTH), row(CONV_WIDTH), row(GATE_COLS),
                  _resident((ATTN_WIDTH, d)), _resident((CONV_WIDTH, d)),
                  _resident((1, d)), _resident((d, d)), _resident((1, d)),
                  _resident((d, 2 * D_FF)), _resident((D_FF, d)), _resident((1, d))],
        out_specs=row(d),
        out_shape=jax.ShapeDtypeStruct((bsz, s, d), F32),
        scratch_shapes=[pltpu.VMEM((tm, d), F32), pltpu.VMEM((tm, d), BF16),
                        pltpu.VMEM((tm, D_FF), BF16)],
        compiler_params=_params(("parallel", "arbitrary")),
        name="mix_ffn",
    )(x, ada3, a_n, u, gates, woa, woc, boc.reshape(1, d), wout, g2.reshape(1, d),
      w1, w2, gf.reshape(1, d))


def kernel(x, c, w_ada, b_ada, norm1_g, norm2_g, final_g, w_in, lambda_q1, lambda_k1,
           lambda_q2, lambda_k2, rel_bias, attn_sub_g, w_o_attn, conv_w, conv_b,
           conv_ln_g, conv_ln_b, w_o_conv, b_o_conv, w_out, w_ffn_in, w_ffn_out):
    bsz, s, d = x.shape
    assert w_ada.shape[0] == 1, "single-layer block"
    assert (d, s % ROW_TILE, s % ATTN_TILE) == (D_MODEL, 0, 0)
    assert PROJ_CHUNK == CONV_WIDTH and ATTN_WIDTH % PROJ_CHUNK == 0
    l = 0
    ada, w_in_bf, bias_tiles = _prologue(c, w_ada, b_ada, w_in, rel_bias)
    ada3 = ada.reshape(bsz, 6, d)
    qkv, u, gates = _in_proj(x, ada3, norm1_g[l], w_in_bf, conv_w[l],
                             conv_b[l], conv_ln_g[l], conv_ln_b[l])
    lamv = jnp.stack([lambda_q1[l], lambda_k1[l], lambda_q2[l], lambda_k2[l]])
    a_n, (woa, woc, wout, w1, w2) = _attention(
        qkv, bias_tiles, lamv, attn_sub_g[l],
        [w_o_attn, w_o_conv, w_out, w_ffn_in, w_ffn_out])
    return _mix_ffn(x, ada3, a_n, u, gates, woa, woc, b_o_conv[l], wout, norm2_g[l],
                    w1, w2, final_g)
```

```python
import math

import jax
import jax.numpy as jnp
from jax import lax
from jax.experimental import pallas as pl
from jax.experimental.pallas import tpu as pltpu

D_MODEL = 1024
N_HEADS_A = 4
HEAD_QK = 64
HEAD_V = 2 * HEAD_QK
ATTN_WIDTH = N_HEADS_A * HEAD_V
CONV_WIDTH = 512
CONV_KERNEL = 31
QKV_COLS = 3 * ATTN_WIDTH
GLU_COLS = 2 * CONV_WIDTH
GATE_COLS = 2 * D_MODEL
IN_COLS = QKV_COLS + GLU_COLS + GATE_COLS
D_FF = 2816
N_BUCKETS = 32
MAX_EXACT = 16
MAX_DISTANCE = 128
EPS = 1e-6
NEG_INF = -1e30
LAM_INIT = 0.8 - 0.6 * math.exp(-0.3 * 0)

V7X_VMEM_BYTES = 64 * 1024 * 1024
VMEM_LIMIT = V7X_VMEM_BYTES - 12 * 1024 * 1024

ROW_TILE = 512
ATTN_TILE = 256
ATTN_LAG = 4
ONES_ROWS = 16
CAST_ROWS = 16
LOG2E = math.log2(math.e)
Q_SCALE = HEAD_QK ** -0.5 * LOG2E
CONV_HALO = 32
CONV_CHUNK = 16
PROJ_CHUNK = 512
FF_CHUNK = 256

BF16 = jnp.bfloat16
F32 = jnp.float32


def _sigmoid(x):
    return 1.0 / (1.0 + jnp.exp2(x * (-LOG2E)))


def _silu(x):
    return x * _sigmoid(x)


def _params(sem):
    return pltpu.CompilerParams(dimension_semantics=sem, vmem_limit_bytes=VMEM_LIMIT)


def _resident(shape):
    nd = len(shape)
    return pl.BlockSpec(shape, lambda *_: (0,) * nd, pipeline_mode=pl.Buffered(1))


def _bias_tile(rel_ref, h, t):
    T = ATTN_TILE
    key = lax.broadcasted_iota(jnp.int32, (T, T), 0)
    qry = lax.broadcasted_iota(jnp.int32, (T, T), 1)
    dist = (1 - t) * T + qry - key
    n = jnp.maximum(dist, 0)
    large = MAX_EXACT + (jnp.log(jnp.maximum(n, 1).astype(F32) / MAX_EXACT)
                         / math.log(MAX_DISTANCE / MAX_EXACT)
                         * (N_BUCKETS - MAX_EXACT)).astype(jnp.int32)
    large = jnp.minimum(large, N_BUCKETS - 1)
    bucket = jnp.where(n < MAX_EXACT, n, large)
    far = rel_ref[N_BUCKETS - 1, h]
    bias = jnp.zeros((T, T), F32)
    for b in range(N_BUCKETS - 1):
        bias = jnp.where(bucket == b, (rel_ref[b, h] - far) * LOG2E, bias)
    return jnp.where(dist >= 0, bias, NEG_INF)


def _prologue_kernel(rel_ref, c_ref, w_ref, b_ref, win_ref, o_ref, win_bf_ref, bias_ref):
    j = pl.program_id(0)
    ca = _silu(c_ref[...])
    o_ref[...] = jnp.dot(ca.astype(BF16), w_ref[...].astype(BF16),
                         preferred_element_type=F32) + b_ref[...]
    win_bf_ref[...] = win_ref[...].astype(BF16)
    bias_ref[...] = _bias_tile(rel_ref, j // 2, j % 2)


def _prologue(c, w, b, w_in, rel_bias):
    bsz, d = c.shape
    n = w.shape[-1]
    steps = 2 * N_HEADS_A
    tn = n // steps
    _, rows, cols = w_in.shape
    tr = rows // steps
    T = ATTN_TILE
    return pl.pallas_call(
        _prologue_kernel,
        grid=(steps,),
        in_specs=[pl.BlockSpec(memory_space=pltpu.SMEM),
                  pl.BlockSpec((bsz, d), lambda j: (0, 0)),
                  pl.BlockSpec((None, d, tn), lambda j: (0, 0, j)),
                  pl.BlockSpec((1, tn), lambda j: (0, j)),
                  pl.BlockSpec((None, tr, cols), lambda j: (0, j, 0))],
        out_specs=[pl.BlockSpec((bsz, tn), lambda j: (0, j)),
                   pl.BlockSpec((tr, cols), lambda j: (j, 0)),
                   pl.BlockSpec((None, None, T, T), lambda j: (j // 2, j % 2, 0, 0))],
        out_shape=[jax.ShapeDtypeStruct((bsz, n), F32),
                   jax.ShapeDtypeStruct((rows, cols), BF16),
                   jax.ShapeDtypeStruct((N_HEADS_A, 2, T, T), F32)],
        compiler_params=_params(("arbitrary",)),
        name="prologue",
    )(rel_bias, c, w, b, w_in)


def _conv_chunk(r0, ubuf, w_ref, cb_ref, lng_ref, lnb_ref, o_ref):
    off = CONV_HALO - (CONV_KERNEL - 1)
    acc = jnp.zeros((CONV_CHUNK // 8, 8, CONV_WIDTH), F32)
    for k in range(CONV_KERNEL):
        a, b = divmod(k + off, 8)
        rows = ubuf[b, r0 + 8 * a:r0 + 8 * a + CONV_CHUNK, :]
        acc = acc + rows.reshape(acc.shape) * w_ref[k][None]
    u = acc.reshape(CONV_CHUNK, CONV_WIDTH) + cb_ref[...]
    mu = jnp.mean(u, axis=-1, keepdims=True)
    d = u - mu
    var = jnp.mean(d * d, axis=-1, keepdims=True)
    y = d * lax.rsqrt(var + EPS) * lng_ref[...] + lnb_ref[...]
    o_ref[r0:r0 + CONV_CHUNK, :] = _silu(y).astype(BF16)


def _inproj_kernel(x_ref, ada_ref, g_ref, w_ref, cw_ref, cb_ref, lng_ref, lnb_ref,
                   qkv_ref, u_ref, gate_ref, ubuf):
    TS, H, C = ROW_TILE, CONV_HALO, CONV_WIDTH
    i = pl.program_id(1)

    @pl.when(i == 0)
    def _():
        ubuf[0, 0:H, :] = jnp.zeros((H, C), F32)

    @pl.when(i > 0)
    def _():
        ubuf[0, 0:H, :] = ubuf[0, TS:TS + H, :]

    x = x_ref[...]
    ms = jnp.mean(x * x, axis=-1, keepdims=True)
    y = x * lax.rsqrt(ms + EPS)
    h = (y * g_ref[...]) * (1.0 + ada_ref[1:2, :]) + ada_ref[0:1, :]
    hb = h.astype(BF16)

    def proj(base):
        return jnp.dot(hb, w_ref[:, base:base + PROJ_CHUNK], preferred_element_type=F32)

    ubuf[0, H:H + TS, :] = proj(QKV_COLS) * _sigmoid(proj(QKV_COLS + C))
    n = TS + H - 8
    for b in range(1, 8):
        ubuf[b, 0:n, :] = ubuf[0, b:b + n, :]
    jobs = ([(qkv_ref, c, c) for c in range(0, QKV_COLS, PROJ_CHUNK)]
            + [(gate_ref, c, QKV_COLS + GLU_COLS + c) for c in range(0, GATE_COLS, PROJ_CHUNK)])
    rows = list(range(0, TS, CONV_CHUNK))
    for n, (ref, c, base) in enumerate(jobs):
        r = proj(base)
        if base < ATTN_WIDTH:
            r = r * Q_SCALE
        ref[:, c:c + PROJ_CHUNK] = r.astype(BF16)
        for r0 in rows[n * len(rows) // len(jobs):(n + 1) * len(rows) // len(jobs)]:
            _conv_chunk(r0, ubuf, cw_ref, cb_ref, lng_ref, lnb_ref, u_ref)


def _in_proj(x, ada3, g, w, cw, cb, lng, lnb):
    bsz, s, d = x.shape
    tm = ROW_TILE
    C = CONV_WIDTH
    row = lambda n: pl.BlockSpec((None, tm, n), lambda b, i: (b, i, 0))
    vec = lambda a: a.reshape(1, C)
    return pl.pallas_call(
        _inproj_kernel,
        grid=(bsz, s // tm),
        in_specs=[row(d),
                  pl.BlockSpec((None, 6, d), lambda b, i: (b, 0, 0)),
                  _resident((1, d)),
                  _resident((d, IN_COLS)),
                  _resident((CONV_KERNEL, 8, C)),
                  _resident((1, C)), _resident((1, C)), _resident((1, C))],
        out_specs=[row(QKV_COLS), row(C), row(GATE_COLS)],
        out_shape=[jax.ShapeDtypeStruct((bsz, s, QKV_COLS), BF16),
                   jax.ShapeDtypeStruct((bsz, s, C), BF16),
                   jax.ShapeDtypeStruct((bsz, s, GATE_COLS), BF16)],
        scratch_shapes=[pltpu.VMEM((8, tm + CONV_HALO, C), F32)],
        compiler_params=_params(("arbitrary", "arbitrary")),
        name="in_proj",
    )(x, ada3, g.reshape(1, d), w, jnp.broadcast_to(cw[:, None, :], (CONV_KERNEL, 8, C)),
      vec(cb), vec(lng), vec(lnb))


def _attn_kernel(lamv_ref, subg_ref, q_ref, k_ref, v_ref, bias_ref, *rest):
    n_cast = (len(rest) - 5) // 2
    cast_in, o_ref, cast_out = rest[:n_cast], rest[n_cast], rest[n_cast + 1:2 * n_cast + 1]
    vt_ref, qz_ref, m_ref, a_ref = rest[2 * n_cast + 1:]
    T = ATTN_TILE
    NS = 2 * N_HEADS_A
    n_q = q_ref.shape[0] // T
    hcols = lambda h: slice(h * HEAD_V, (h + 1) * HEAD_V)
    rows = lambda t: slice(t * T, (t + 1) * T)

    vt = v_ref[...].T
    for h in range(N_HEADS_A):
        vt_ref[h, 0:HEAD_V, :] = vt[hcols(h), :]
        vt_ref[h, HEAD_V:, :] = jnp.ones((ONES_ROWS, vt.shape[1]), BF16)

    lv = lamv_ref[...]
    lam = (jnp.exp(jnp.sum(lv[0:1] * lv[1:2], axis=-1, keepdims=True))
           - jnp.exp(jnp.sum(lv[2:3] * lv[3:4], axis=-1, keepdims=True)) + LAM_INIT)
    lane = lax.broadcasted_iota(jnp.int32, (T, HEAD_V), 1)

    def prepare(i):
        slot = i % 2
        for h in range(N_HEADS_A):
            qs = q_ref[rows(i), hcols(h)]
            zero = jnp.zeros_like(qs)
            qz_ref[slot, 2 * h] = jnp.where(lane < HEAD_QK, qs, zero)
            qz_ref[slot, 2 * h + 1] = jnp.where(lane >= HEAD_QK, qs, zero)

    def finalize(i):
        slot = i % 2
        for h in range(N_HEADS_A):
            a1, a2 = a_ref[slot, 2 * h], a_ref[slot, 2 * h + 1]
            o = (a1[:HEAD_V] / a1[HEAD_V:HEAD_V + 1]
                 - lam * (a2[:HEAD_V] / a2[HEAD_V:HEAD_V + 1]))
            ms = jnp.mean(o * o, axis=0, keepdims=True)
            y = (o * lax.rsqrt(ms + EPS)).T
            o_ref[rows(i), hcols(h)] = ((y * subg_ref[...]) * (1.0 - LAM_INIT)).astype(BF16)

    def groups(i):
        firsts = list(range((i + 1) % 2, i + 1, 2))
        return ([(0, 1)] if (i + 1) % 2 else []) + [(j0, 2) for j0 in firsts]

    units = [(i, j0, cnt, n) for i in range(n_q) for j0, cnt in groups(i) for n in range(NS)]
    scores, probs, alphas = {}, {}, {}
    keys = lambda j0, cnt: slice(j0 * T, (j0 + cnt) * T)

    def score(u):
        i, j0, cnt, n = units[u]
        kc = k_ref[keys(j0, cnt), hcols(n // 2)]
        scores[u] = lax.dot_general(kc, qz_ref[i % 2, n], (((1,), (1,)), ((), ())),
                                    preferred_element_type=F32)

    def softmax(u):
        i, j0, cnt, n = units[u]
        s = scores.pop(u)
        if j0 + cnt == i + 1:
            bias = bias_ref[n // 2]
            s = s + bias[2 - cnt:].reshape(cnt * T, T)
        m_new = jnp.max(s, axis=0, keepdims=True)
        if j0 > 0:
            m_old = m_ref[i % 2, n]
            m_new = jnp.maximum(m_old, m_new)
            alphas[u] = jnp.exp2(m_old - m_new)
        probs[u] = jnp.exp2(s - m_new).astype(BF16)
        m_ref[i % 2, n] = m_new

    def value(u):
        i, j0, cnt, n = units[u]
        vc = vt_ref[n // 2, :, keys(j0, cnt)]
        pv = jnp.dot(vc, probs.pop(u), preferred_element_type=F32)
        a_ref[i % 2, n] = pv if j0 == 0 else alphas.pop(u) * a_ref[i % 2, n] + pv

    for step in range(len(units) + 2 * ATTN_LAG):
        if step < len(units):
            if units[step][1] == 0 and units[step][3] == 0:
                prepare(units[step][0])
            score(step)
        if 0 <= step - ATTN_LAG < len(units):
            softmax(step - ATTN_LAG)
        done = step - 2 * ATTN_LAG
        if 0 <= done < len(units):
            value(done)
            i, j0, cnt, n = units[done]
            if (j0 + cnt, n) == (i + 1, NS - 1):
                finalize(i)

    for w_ref, wb_ref in zip(cast_in, cast_out):
        wb_ref[...] = w_ref[...].astype(BF16)


def _attention(qkv, bias_tiles, lamv, subg, weights):
    bsz, s, _ = qkv.shape
    T = ATTN_TILE
    W = ATTN_WIDTH
    NS = 2 * N_HEADS_A
    steps = bsz

    def slab_spec(w, squeeze):
        rows, cols = w.shape[-2:]
        assert rows % (steps * CAST_ROWS) == 0
        block = (rows // steps, cols)
        if squeeze:
            return pl.BlockSpec((None,) + block, lambda b: (0, b, 0))
        return pl.BlockSpec(block, lambda b: (b, 0))

    seq = lambda col: pl.BlockSpec((None, s, W), lambda b: (b, 0, col))
    out = pl.pallas_call(
        _attn_kernel,
        grid=(bsz,),
        in_specs=[_resident((4, HEAD_QK)),
                  _resident((1, HEAD_V)),
                  seq(0), seq(1), seq(2),
                  _resident((N_HEADS_A, 2, T, T))] + [slab_spec(w, True) for w in weights],
        out_specs=[seq(0)] + [slab_spec(w, False) for w in weights],
        out_shape=[jax.ShapeDtypeStruct((bsz, s, W), BF16)]
        + [jax.ShapeDtypeStruct(w.shape[-2:], BF16) for w in weights],
        scratch_shapes=[pltpu.VMEM((N_HEADS_A, HEAD_V + ONES_ROWS, s), BF16),
                        pltpu.VMEM((2, NS, T, HEAD_V), BF16),
                        pltpu.VMEM((2, NS, 1, T), F32),
                        pltpu.VMEM((2, NS, HEAD_V + ONES_ROWS, T), F32)],
        compiler_params=_params(("arbitrary",)),
        name="diff_attn",
    )(lamv, subg.reshape(1, HEAD_V), qkv, qkv, qkv, bias_tiles, *weights)
    return out[0], out[1:]


def _mixffn_kernel(x_ref, ada_ref, a_ref, u_ref, gate_ref, woa_ref, woc_ref, boc_ref,
                   wout_ref, g2_ref, w1_ref, w2_ref, gf_ref, o_ref,
                   x1_ref, h2_ref, act_ref):
    D = D_MODEL
    a = jnp.dot(a_ref[...], woa_ref[...], preferred_element_type=F32)
    cv = jnp.dot(u_ref[...], woc_ref[...], preferred_element_type=F32) + boc_ref[...]
    gate = gate_ref[...]
    y = (_sigmoid(gate[:, :D].astype(F32)) * a
         + _sigmoid(gate[:, D:].astype(F32)) * cv)
    z = jnp.dot(y.astype(BF16), wout_ref[...], preferred_element_type=F32)
    x1 = x_ref[...] + ada_ref[2:3, :] * z
    x1_ref[...] = x1
    ms = jnp.mean(x1 * x1, axis=-1, keepdims=True)
    yn = x1 * lax.rsqrt(ms + EPS)
    h2_ref[...] = ((yn * g2_ref[...]) * (1.0 + ada_ref[4:5, :]) + ada_ref[3:4, :]).astype(BF16)

    h = h2_ref[...]
    for c in range(0, D_FF, FF_CHUNK):
        fg = jnp.dot(h, w1_ref[:, c:c + FF_CHUNK], preferred_element_type=F32)
        fu = jnp.dot(h, w1_ref[:, D_FF + c:D_FF + c + FF_CHUNK],
                     preferred_element_type=F32)
        act_ref[:, c:c + FF_CHUNK] = (_silu(fg) * fu).astype(BF16)
    z2 = jnp.dot(act_ref[...], w2_ref[...], preferred_element_type=F32)
    x2 = x1_ref[...] + ada_ref[5:6, :] * z2
    ms2 = jnp.mean(x2 * x2, axis=-1, keepdims=True)
    o_ref[...] = (x2 * lax.rsqrt(ms2 + EPS)) * gf_ref[...]


def _mix_ffn(x, ada3, a_n, u, gates, woa, woc, boc, wout, g2, w1, w2, gf):
    bsz, s, d = x.shape
    tm = ROW_TILE
    row = lambda n: pl.BlockSpec((None, tm, n), lambda b, i: (b, i, 0))
    return pl.pallas_call(
        _mixffn_kernel,
        grid=(bsz, s // tm),
        in_specs=[row(d),
                  pl.BlockSpec((None, 6, d), lambda b, i: (b, 0, 0)),
                  row(ATTN_WIDTH), row(CONV_WIDTH), row(GATE_COLS),
                  _resident((ATTN_WIDTH, d)), _resident((CONV_WIDTH, d)),
                  _resident((1, d)), _resident((d, d)), _resident((1, d)),
                  _resident((d, 2 * D_FF)), _resident((D_FF, d)), _resident((1, d))],
        out_specs=row(d),
        out_shape=jax.ShapeDtypeStruct((bsz, s, d), F32),
        scratch_shapes=[pltpu.VMEM((tm, d), F32), pltpu.VMEM((tm, d), BF16),
                        pltpu.VMEM((tm, D_FF), BF16)],
        compiler_params=_params(("parallel", "arbitrary")),
        name="mix_ffn",
    )(x, ada3, a_n, u, gates, woa, woc, boc.reshape(1, d), wout, g2.reshape(1, d),
      w1, w2, gf.reshape(1, d))


def kernel(x, c, w_ada, b_ada, norm1_g, norm2_g, final_g, w_in, lambda_q1, lambda_k1,
           lambda_q2, lambda_k2, rel_bias, attn_sub_g, w_o_attn, conv_w, conv_b,
           conv_ln_g, conv_ln_b, w_o_conv, b_o_conv, w_out, w_ffn_in, w_ffn_out):
    bsz, s, d = x.shape
    assert w_ada.shape[0] == 1, "single-layer block"
    assert (d, s % ROW_TILE, s % ATTN_TILE) == (D_MODEL, 0, 0)
    assert PROJ_CHUNK == CONV_WIDTH and ATTN_WIDTH % PROJ_CHUNK == 0
    l = 0
    ada, w_in_bf, bias_tiles = _prologue(c, w_ada, b_ada, w_in, rel_bias)
    ada3 = ada.reshape(bsz, 6, d)
    qkv, u, gates = _in_proj(x, ada3, norm1_g[l], w_in_bf, conv_w[l],
                             conv_b[l], conv_ln_g[l], conv_ln_b[l])
    lamv = jnp.stack([lambda_q1[l], lambda_k1[l], lambda_q2[l], lambda_k2[l]])
    a_n, (woa, woc, wout, w1, w2) = _attention(
        qkv, bias_tiles, lamv, attn_sub_g[l],
        [w_o_attn, w_o_conv, w_out, w_ffn_in, w_ffn_out])
    return _mix_ffn(x, ada3, a_n, u, gates, woa, woc, b_o_conv[l], wout, norm2_g[l],
                    w1, w2, final_g)
```

```python
import math

import jax
import jax.numpy as jnp
from jax import lax
from jax.experimental import pallas as pl
from jax.experimental.pallas import tpu as pltpu

D_MODEL = 1024
N_HEADS_A = 4
HEAD_QK = 64
HEAD_V = 2 * HEAD_QK
ATTN_WIDTH = N_HEADS_A * HEAD_V
CONV_WIDTH = 512
CONV_KERNEL = 31
QKV_COLS = 3 * ATTN_WIDTH
GLU_COLS = 2 * CONV_WIDTH
GATE_COLS = 2 * D_MODEL
IN_COLS = QKV_COLS + GLU_COLS + GATE_COLS
D_FF = 2816
N_BUCKETS = 32
MAX_EXACT = 16
MAX_DISTANCE = 128
EPS = 1e-6
NEG_INF = -1e30
LAM_INIT = 0.8 - 0.6 * math.exp(-0.3 * 0)

V7X_VMEM_BYTES = 64 * 1024 * 1024
VMEM_LIMIT = V7X_VMEM_BYTES - 12 * 1024 * 1024

ROW_TILE = 512
ATTN_TILE = 256
ATTN_LAG = 2
ONES_ROWS = 16
CAST_ROWS = 16
LOG2E = math.log2(math.e)
Q_SCALE = HEAD_QK ** -0.5 * LOG2E
CONV_HALO = 32
CONV_CHUNK = 16
PROJ_CHUNK = 512
FF_CHUNK = 256

BF16 = jnp.bfloat16
F32 = jnp.float32


def _sigmoid(x):
    return 1.0 / (1.0 + jnp.exp2(x * (-LOG2E)))


def _silu(x):
    return x * _sigmoid(x)


def _params(sem):
    return pltpu.CompilerParams(dimension_semantics=sem, vmem_limit_bytes=VMEM_LIMIT)


def _resident(shape):
    nd = len(shape)
    return pl.BlockSpec(shape, lambda *_: (0,) * nd, pipeline_mode=pl.Buffered(1))


def _bias_tile(rel_ref, h, t):
    T = ATTN_TILE
    key = lax.broadcasted_iota(jnp.int32, (T, T), 0)
    qry = lax.broadcasted_iota(jnp.int32, (T, T), 1)
    dist = (1 - t) * T + qry - key
    n = jnp.maximum(dist, 0)
    large = MAX_EXACT + (jnp.log(jnp.maximum(n, 1).astype(F32) / MAX_EXACT)
                         / math.log(MAX_DISTANCE / MAX_EXACT)
                         * (N_BUCKETS - MAX_EXACT)).astype(jnp.int32)
    large = jnp.minimum(large, N_BUCKETS - 1)
    bucket = jnp.where(n < MAX_EXACT, n, large)
    far = rel_ref[N_BUCKETS - 1, h]
    bias = jnp.zeros((T, T), F32)
    for b in range(N_BUCKETS - 1):
        bias = jnp.where(bucket == b, (rel_ref[b, h] - far) * LOG2E, bias)
    return jnp.where(dist >= 0, bias, NEG_INF)


def _prologue_kernel(rel_ref, c_ref, w_ref, b_ref, win_ref, o_ref, win_bf_ref, bias_ref):
    j = pl.program_id(0)
    ca = _silu(c_ref[...])
    o_ref[...] = jnp.dot(ca.astype(BF16), w_ref[...].astype(BF16),
                         preferred_element_type=F32) + b_ref[...]
    win_bf_ref[...] = win_ref[...].astype(BF16)
    bias_ref[...] = _bias_tile(rel_ref, j // 2, j % 2)


def _prologue(c, w, b, w_in, rel_bias):
    bsz, d = c.shape
    n = w.shape[-1]
    steps = 2 * N_HEADS_A
    tn = n // steps
    _, rows, cols = w_in.shape
    tr = rows // steps
    T = ATTN_TILE
    return pl.pallas_call(
        _prologue_kernel,
        grid=(steps,),
        in_specs=[pl.BlockSpec(memory_space=pltpu.SMEM),
                  pl.BlockSpec((bsz, d), lambda j: (0, 0)),
                  pl.BlockSpec((None, d, tn), lambda j: (0, 0, j)),
                  pl.BlockSpec((1, tn), lambda j: (0, j)),
                  pl.BlockSpec((None, tr, cols), lambda j: (0, j, 0))],
        out_specs=[pl.BlockSpec((bsz, tn), lambda j: (0, j)),
                   pl.BlockSpec((tr, cols), lambda j: (j, 0)),
                   pl.BlockSpec((None, None, T, T), lambda j: (j // 2, j % 2, 0, 0))],
        out_shape=[jax.ShapeDtypeStruct((bsz, n), F32),
                   jax.ShapeDtypeStruct((rows, cols), BF16),
                   jax.ShapeDtypeStruct((N_HEADS_A, 2, T, T), F32)],
        compiler_params=_params(("arbitrary",)),
        name="prologue",
    )(rel_bias, c, w, b, w_in)


def _conv_chunk(r0, ubuf, w_ref, cb_ref, lng_ref, lnb_ref, o_ref):
    off = CONV_HALO - (CONV_KERNEL - 1)
    acc = jnp.zeros((CONV_CHUNK // 8, 8, CONV_WIDTH), F32)
    for k in range(CONV_KERNEL):
        a, b = divmod(k + off, 8)
        rows = ubuf[b, r0 + 8 * a:r0 + 8 * a + CONV_CHUNK, :]
        acc = acc + rows.reshape(acc.shape) * w_ref[k][None]
    u = acc.reshape(CONV_CHUNK, CONV_WIDTH) + cb_ref[...]
    mu = jnp.mean(u, axis=-1, keepdims=True)
    d = u - mu
    var = jnp.mean(d * d, axis=-1, keepdims=True)
    y = d * lax.rsqrt(var + EPS) * lng_ref[...] + lnb_ref[...]
    o_ref[r0:r0 + CONV_CHUNK, :] = _silu(y).astype(BF16)


def _inproj_kernel(x_ref, ada_ref, g_ref, w_ref, cw_ref, cb_ref, lng_ref, lnb_ref,
                   qkv_ref, u_ref, gate_ref, ubuf):
    TS, H, C = ROW_TILE, CONV_HALO, CONV_WIDTH
    i = pl.program_id(1)

    @pl.when(i == 0)
    def _():
        ubuf[0, 0:H, :] = jnp.zeros((H, C), F32)

    @pl.when(i > 0)
    def _():
        ubuf[0, 0:H, :] = ubuf[0, TS:TS + H, :]

    x = x_ref[...]
    ms = jnp.mean(x * x, axis=-1, keepdims=True)
    y = x * lax.rsqrt(ms + EPS)
    h = (y * g_ref[...]) * (1.0 + ada_ref[1:2, :]) + ada_ref[0:1, :]
    hb = h.astype(BF16)

    def proj(base):
        return jnp.dot(hb, w_ref[:, base:base + PROJ_CHUNK], preferred_element_type=F32)

    ubuf[0, H:H + TS, :] = proj(QKV_COLS) * _sigmoid(proj(QKV_COLS + C))
    n = TS + H - 8
    for b in range(1, 8):
        ubuf[b, 0:n, :] = ubuf[0, b:b + n, :]
    jobs = ([(qkv_ref, c, c) for c in range(0, QKV_COLS, PROJ_CHUNK)]
            + [(gate_ref, c, QKV_COLS + GLU_COLS + c) for c in range(0, GATE_COLS, PROJ_CHUNK)])
    rows = list(range(0, TS, CONV_CHUNK))
    for n, (ref, c, base) in enumerate(jobs):
        r = proj(base)
        if base < ATTN_WIDTH:
            r = r * Q_SCALE
        ref[:, c:c + PROJ_CHUNK] = r.astype(BF16)
        for r0 in rows[n * len(rows) // len(jobs):(n + 1) * len(rows) // len(jobs)]:
            _conv_chunk(r0, ubuf, cw_ref, cb_ref, lng_ref, lnb_ref, u_ref)


def _in_proj(x, ada3, g, w, cw, cb, lng, lnb):
    bsz, s, d = x.shape
    tm = ROW_TILE
    C = CONV_WIDTH
    row = lambda n: pl.BlockSpec((None, tm, n), lambda b, i: (b, i, 0))
    vec = lambda a: a.reshape(1, C)
    return pl.pallas_call(
        _inproj_kernel,
        grid=(bsz, s // tm),
        in_specs=[row(d),
                  pl.BlockSpec((None, 6, d), lambda b, i: (b, 0, 0)),
                  _resident((1, d)),
                  _resident((d, IN_COLS)),
                  _resident((CONV_KERNEL, 8, C)),
                  _resident((1, C)), _resident((1, C)), _resident((1, C))],
        out_specs=[row(QKV_COLS), row(C), row(GATE_COLS)],
        out_shape=[jax.ShapeDtypeStruct((bsz, s, QKV_COLS), BF16),
                   jax.ShapeDtypeStruct((bsz, s, C), BF16),
                   jax.ShapeDtypeStruct((bsz, s, GATE_COLS), BF16)],
        scratch_shapes=[pltpu.VMEM((8, tm + CONV_HALO, C), F32)],
        compiler_params=_params(("arbitrary", "arbitrary")),
        name="in_proj",
    )(x, ada3, g.reshape(1, d), w, jnp.broadcast_to(cw[:, None, :], (CONV_KERNEL, 8, C)),
      vec(cb), vec(lng), vec(lnb))


def _attn_kernel(lamv_ref, subg_ref, q_ref, k_ref, v_ref, bias_ref, *rest):
    n_cast = (len(rest) - 5) // 2
    cast_in, o_ref, cast_out = rest[:n_cast], rest[n_cast], rest[n_cast + 1:2 * n_cast + 1]
    vt_ref, qz_ref, m_ref, a_ref = rest[2 * n_cast + 1:]
    T = ATTN_TILE
    NS = 2 * N_HEADS_A
    n_q = q_ref.shape[0] // T
    hcols = lambda h: slice(h * HEAD_V, (h + 1) * HEAD_V)
    rows = lambda t: slice(t * T, (t + 1) * T)

    vt = v_ref[...].T
    for h in range(N_HEADS_A):
        vt_ref[h, 0:HEAD_V, :] = vt[hcols(h), :]
        vt_ref[h, HEAD_V:, :] = jnp.ones((ONES_ROWS, vt.shape[1]), BF16)

    lv = lamv_ref[...]
    lam = (jnp.exp(jnp.sum(lv[0:1] * lv[1:2], axis=-1, keepdims=True))
           - jnp.exp(jnp.sum(lv[2:3] * lv[3:4], axis=-1, keepdims=True)) + LAM_INIT)
    lane = lax.broadcasted_iota(jnp.int32, (T, HEAD_V), 1)

    def prepare(i):
        slot = i % 2
        for h in range(N_HEADS_A):
            qs = q_ref[rows(i), hcols(h)]
            zero = jnp.zeros_like(qs)
            qz_ref[slot, 2 * h] = jnp.where(lane < HEAD_QK, qs, zero)
            qz_ref[slot, 2 * h + 1] = jnp.where(lane >= HEAD_QK, qs, zero)

    def finalize(i):
        slot = i % 2
        for h in range(N_HEADS_A):
            a1, a2 = a_ref[slot, 2 * h], a_ref[slot, 2 * h + 1]
            o = (a1[:HEAD_V] / a1[HEAD_V:HEAD_V + 1]
                 - lam * (a2[:HEAD_V] / a2[HEAD_V:HEAD_V + 1]))
            ms = jnp.mean(o * o, axis=0, keepdims=True)
            y = (o * lax.rsqrt(ms + EPS)).T
            o_ref[rows(i), hcols(h)] = ((y * subg_ref[...]) * (1.0 - LAM_INIT)).astype(BF16)

    def groups(i):
        firsts = list(range((i + 1) % 2, i + 1, 2))
        return ([(0, 1)] if (i + 1) % 2 else []) + [(j0, 2) for j0 in firsts]

    units = [(i, j0, cnt, n) for i in range(n_q) for j0, cnt in groups(i) for n in range(NS)]
    scores, probs, alphas = {}, {}, {}
    keys = lambda j0, cnt: slice(j0 * T, (j0 + cnt) * T)

    def score(u):
        i, j0, cnt, n = units[u]
        kc = k_ref[keys(j0, cnt), hcols(n // 2)]
        scores[u] = lax.dot_general(kc, qz_ref[i % 2, n], (((1,), (1,)), ((), ())),
                                    preferred_element_type=F32)

    def softmax(u):
        i, j0, cnt, n = units[u]
        s = scores.pop(u)
        if j0 + cnt == i + 1:
            bias = bias_ref[n // 2]
            s = s + bias[2 - cnt:].reshape(cnt * T, T)
        m_new = jnp.max(s, axis=0, keepdims=True)
        if j0 > 0:
            m_old = m_ref[i % 2, n]
            m_new = jnp.maximum(m_old, m_new)
            alphas[u] = jnp.exp2(m_old - m_new)
        probs[u] = jnp.exp2(s - m_new).astype(BF16)
        m_ref[i % 2, n] = m_new

    def value(u):
        i, j0, cnt, n = units[u]
        vc = vt_ref[n // 2, :, keys(j0, cnt)]
        pv = jnp.dot(vc, probs.pop(u), preferred_element_type=F32)
        a_ref[i % 2, n] = pv if j0 == 0 else alphas.pop(u) * a_ref[i % 2, n] + pv

    for step in range(len(units) + 2 * ATTN_LAG):
        if step < len(units):
            if units[step][1] == 0 and units[step][3] == 0:
                prepare(units[step][0])
            score(step)
        if 0 <= step - ATTN_LAG < len(units):
            softmax(step - ATTN_LAG)
        done = step - 2 * ATTN_LAG
        if 0 <= done < len(units):
            value(done)
            i, j0, cnt, n = units[done]
            if (j0 + cnt, n) == (i + 1, NS - 1):
                finalize(i)

    for w_ref, wb_ref in zip(cast_in, cast_out):
        wb_ref[...] = w_ref[...].astype(BF16)


def _attention(qkv, bias_tiles, lamv, subg, weights):
    bsz, s, _ = qkv.shape
    T = ATTN_TILE
    W = ATTN_WIDTH
    NS = 2 * N_HEADS_A
    steps = bsz

    def slab_spec(w, squeeze):
        rows, cols = w.shape[-2:]
        assert rows % (steps * CAST_ROWS) == 0
        block = (rows // steps, cols)
        if squeeze:
            return pl.BlockSpec((None,) + block, lambda b: (0, b, 0))
        return pl.BlockSpec(block, lambda b: (b, 0))

    seq = lambda col: pl.BlockSpec((None, s, W), lambda b: (b, 0, col))
    out = pl.pallas_call(
        _attn_kernel,
        grid=(bsz,),
        in_specs=[_resident((4, HEAD_QK)),
                  _resident((1, HEAD_V)),
                  seq(0), seq(1), seq(2),
                  _resident((N_HEADS_A, 2, T, T))] + [slab_spec(w, True) for w in weights],
        out_specs=[seq(0)] + [slab_spec(w, False) for w in weights],
        out_shape=[jax.ShapeDtypeStruct((bsz, s, W), BF16)]
        + [jax.ShapeDtypeStruct(w.shape[-2:], BF16) for w in weights],
        scratch_shapes=[pltpu.VMEM((N_HEADS_A, HEAD_V + ONES_ROWS, s), BF16),
                        pltpu.VMEM((2, NS, T, HEAD_V), BF16),
                        pltpu.VMEM((2, NS, 1, T), F32),
                        pltpu.VMEM((2, NS, HEAD_V + ONES_ROWS, T), F32)],
        compiler_params=_params(("arbitrary",)),
        name="diff_attn",
    )(lamv, subg.reshape(1, HEAD_V), qkv, qkv, qkv, bias_tiles, *weights)
    return out[0], out[1:]


def _mixffn_kernel(x_ref, ada_ref, a_ref, u_ref, gate_ref, woa_ref, woc_ref, boc_ref,
                   wout_ref, g2_ref, w1_ref, w2_ref, gf_ref, o_ref,
                   x1_ref, h2_ref, act_ref):
    D = D_MODEL
    a = jnp.dot(a_ref[...], woa_ref[...], preferred_element_type=F32)
    cv = jnp.dot(u_ref[...], woc_ref[...], preferred_element_type=F32) + boc_ref[...]
    gate = gate_ref[...]
    y = (_sigmoid(gate[:, :D].astype(F32)) * a
         + _sigmoid(gate[:, D:].astype(F32)) * cv)
    z = jnp.dot(y.astype(BF16), wout_ref[...], preferred_element_type=F32)
    x1 = x_ref[...] + ada_ref[2:3, :] * z
    x1_ref[...] = x1
    ms = jnp.mean(x1 * x1, axis=-1, keepdims=True)
    yn = x1 * lax.rsqrt(ms + EPS)
    h2_ref[...] = ((yn * g2_ref[...]) * (1.0 + ada_ref[4:5, :]) + ada_ref[3:4, :]).astype(BF16)

    h = h2_ref[...]
    for c in range(0, D_FF, FF_CHUNK):
        fg = jnp.dot(h, w1_ref[:, c:c + FF_CHUNK], preferred_element_type=F32)
        fu = jnp.dot(h, w1_ref[:, D_FF + c:D_FF + c + FF_CHUNK],
                     preferred_element_type=F32)
        act_ref[:, c:c + FF_CHUNK] = (_silu(fg) * fu).astype(BF16)
    z2 = jnp.dot(act_ref[...], w2_ref[...], preferred_element_type=F32)
    x2 = x1_ref[...] + ada_ref[5:6, :] * z2
    ms2 = jnp.mean(x2 * x2, axis=-1, keepdims=True)
    o_ref[...] = (x2 * lax.rsqrt(ms2 + EPS)) * gf_ref[...]


def _mix_ffn(x, ada3, a_n, u, gates, woa, woc, boc, wout, g2, w1, w2, gf):
    bsz, s, d = x.shape
    tm = ROW_TILE
    row = lambda n: pl.BlockSpec((None, tm, n), lambda b, i: (b, i, 0))
    return pl.pallas_call(
        _mixffn_kernel,
        grid=(bsz, s // tm),
        in_specs=[row(d),
                  pl.BlockSpec((None, 6, d), lambda b, i: (b, 0, 0)),
                  row(ATTN_WIDTH), row(CONV_WIDTH), row(GATE_COLS),
                  _resident((ATTN_WIDTH, d)), _resident((CONV_WIDTH, d)),
                  _resident((1, d)), _resident((d, d)), _resident((1, d)),
                  _resident((d, 2 * D_FF)), _resident((D_FF, d)), _resident((1, d))],
        out_specs=row(d),
        out_shape=jax.ShapeDtypeStruct((bsz, s, d), F32),
        scratch_shapes=[pltpu.VMEM((tm, d), F32), pltpu.VMEM((tm, d), BF16),
                        pltpu.VMEM((tm, D_FF), BF16)],
        compiler_params=_params(("parallel", "arbitrary")),
        name="mix_ffn",
    )(x, ada3, a_n, u, gates, woa, woc, boc.reshape(1, d), wout, g2.reshape(1, d),
      w1, w2, gf.reshape(1, d))


def kernel(x, c, w_ada, b_ada, norm1_g, norm2_g, final_g, w_in, lambda_q1, lambda_k1,
           lambda_q2, lambda_k2, rel_bias, attn_sub_g, w_o_attn, conv_w, conv_b,
           conv_ln_g, conv_ln_b, w_o_conv, b_o_conv, w_out, w_ffn_in, w_ffn_out):
    bsz, s, d = x.shape
    assert w_ada.shape[0] == 1, "single-layer block"
    assert (d, s % ROW_TILE, s % ATTN_TILE) == (D_MODEL, 0, 0)
    assert PROJ_CHUNK == CONV_WIDTH and ATTN_WIDTH % PROJ_CHUNK == 0
    l = 0
    ada, w_in_bf, bias_tiles = _prologue(c, w_ada, b_ada, w_in, rel_bias)
    ada3 = ada.reshape(bsz, 6, d)
    qkv, u, gates = _in_proj(x, ada3, norm1_g[l], w_in_bf, conv_w[l],
                             conv_b[l], conv_ln_g[l], conv_ln_b[l])
    lamv = jnp.stack([lambda_q1[l], lambda_k1[l], lambda_q2[l], lambda_k2[l]])
    a_n, (woa, woc, wout, w1, w2) = _attention(
        qkv, bias_tiles, lamv, attn_sub_g[l],
        [w_o_attn, w_o_conv, w_out, w_ffn_in, w_ffn_out])
    return _mix_ffn(x, ada3, a_n, u, gates, woa, woc, b_o_conv[l], wout, norm2_g[l],
                    w1, w2, final_g)
```

```python
import math

import jax
import jax.numpy as jnp
from jax import lax
from jax.experimental import pallas as pl
from jax.experimental.pallas import tpu as pltpu

D_MODEL = 1024
N_HEADS_A = 4
HEAD_QK = 64
HEAD_V = 2 * HEAD_QK
ATTN_WIDTH = N_HEADS_A * HEAD_V
CONV_WIDTH = 512
CONV_KERNEL = 31
QKV_COLS = 3 * ATTN_WIDTH
GLU_COLS = 2 * CONV_WIDTH
GATE_COLS = 2 * D_MODEL
IN_COLS = QKV_COLS + GLU_COLS + GATE_COLS
D_FF = 2816
N_BUCKETS = 32
MAX_EXACT = 16
MAX_DISTANCE = 128
EPS = 1e-6
NEG_INF = -1e30
LAM_INIT = 0.8 - 0.6 * math.exp(-0.3 * 0)

V7X_VMEM_BYTES = 64 * 1024 * 1024
VMEM_LIMIT = V7X_VMEM_BYTES - 12 * 1024 * 1024

ROW_TILE = 512
ATTN_TILE = 256
ATTN_LAG = 3
ONES_ROWS = 16
CAST_ROWS = 16
LOG2E = math.log2(math.e)
Q_SCALE = HEAD_QK ** -0.5 * LOG2E
CONV_HALO = 32
CONV_CHUNK = 16
PROJ_CHUNK = 512
FF_CHUNK = 256

BF16 = jnp.bfloat16
F32 = jnp.float32


def _sigmoid(x):
    return 1.0 / (1.0 + jnp.exp2(x * (-LOG2E)))


def _silu(x):
    return x * _sigmoid(x)


def _params(sem):
    return pltpu.CompilerParams(dimension_semantics=sem, vmem_limit_bytes=VMEM_LIMIT)


def _resident(shape):
    nd = len(shape)
    return pl.BlockSpec(shape, lambda *_: (0,) * nd, pipeline_mode=pl.Buffered(1))


def _bias_tile(rel_ref, h, t):
    T = ATTN_TILE
    key = lax.broadcasted_iota(jnp.int32, (T, T), 0)
    qry = lax.broadcasted_iota(jnp.int32, (T, T), 1)
    dist = (1 - t) * T + qry - key
    n = jnp.maximum(dist, 0)
    large = MAX_EXACT + (jnp.log(jnp.maximum(n, 1).astype(F32) / MAX_EXACT)
                         / math.log(MAX_DISTANCE / MAX_EXACT)
                         * (N_BUCKETS - MAX_EXACT)).astype(jnp.int32)
    large = jnp.minimum(large, N_BUCKETS - 1)
    bucket = jnp.where(n < MAX_EXACT, n, large)
    far = rel_ref[N_BUCKETS - 1, h]
    bias = jnp.zeros((T, T), F32)
    for b in range(N_BUCKETS - 1):
        bias = jnp.where(bucket == b, (rel_ref[b, h] - far) * LOG2E, bias)
    return jnp.where(dist >= 0, bias, NEG_INF)


def _prologue_kernel(rel_ref, c_ref, w_ref, b_ref, win_ref, o_ref, win_bf_ref, bias_ref):
    j = pl.program_id(0)
    ca = _silu(c_ref[...])
    o_ref[...] = jnp.dot(ca.astype(BF16), w_ref[...].astype(BF16),
                         preferred_element_type=F32) + b_ref[...]
    win_bf_ref[...] = win_ref[...].astype(BF16)
    bias_ref[...] = _bias_tile(rel_ref, j // 2, j % 2)


def _prologue(c, w, b, w_in, rel_bias):
    bsz, d = c.shape
    n = w.shape[-1]
    steps = 2 * N_HEADS_A
    tn = n // steps
    _, rows, cols = w_in.shape
    tr = rows // steps
    T = ATTN_TILE
    return pl.pallas_call(
        _prologue_kernel,
        grid=(steps,),
        in_specs=[pl.BlockSpec(memory_space=pltpu.SMEM),
                  pl.BlockSpec((bsz, d), lambda j: (0, 0)),
                  pl.BlockSpec((None, d, tn), lambda j: (0, 0, j)),
                  pl.BlockSpec((1, tn), lambda j: (0, j)),
                  pl.BlockSpec((None, tr, cols), lambda j: (0, j, 0))],
        out_specs=[pl.BlockSpec((bsz, tn), lambda j: (0, j)),
                   pl.BlockSpec((tr, cols), lambda j: (j, 0)),
                   pl.BlockSpec((None, None, T, T), lambda j: (j // 2, j % 2, 0, 0))],
        out_shape=[jax.ShapeDtypeStruct((bsz, n), F32),
                   jax.ShapeDtypeStruct((rows, cols), BF16),
                   jax.ShapeDtypeStruct((N_HEADS_A, 2, T, T), F32)],
        compiler_params=_params(("arbitrary",)),
        name="prologue",
    )(rel_bias, c, w, b, w_in)


def _conv_chunk(r0, ubuf, w_ref, cb_ref, lng_ref, lnb_ref, o_ref):
    off = CONV_HALO - (CONV_KERNEL - 1)
    acc = jnp.zeros((CONV_CHUNK // 8, 8, CONV_WIDTH), F32)
    for k in range(CONV_KERNEL):
        a, b = divmod(k + off, 8)
        rows = ubuf[b, r0 + 8 * a:r0 + 8 * a + CONV_CHUNK, :]
        acc = acc + rows.reshape(acc.shape) * w_ref[k][None]
    u = acc.reshape(CONV_CHUNK, CONV_WIDTH) + cb_ref[...]
    mu = jnp.mean(u, axis=-1, keepdims=True)
    d = u - mu
    var = jnp.mean(d * d, axis=-1, keepdims=True)
    y = d * lax.rsqrt(var + EPS) * lng_ref[...] + lnb_ref[...]
    o_ref[r0:r0 + CONV_CHUNK, :] = _silu(y).astype(BF16)


def _inproj_kernel(x_ref, ada_ref, g_ref, w_ref, cw_ref, cb_ref, lng_ref, lnb_ref,
                   qkv_ref, u_ref, gate_ref, ubuf):
    TS, H, C = ROW_TILE, CONV_HALO, CONV_WIDTH
    i = pl.program_id(1)

    @pl.when(i == 0)
    def _():
        ubuf[0, 0:H, :] = jnp.zeros((H, C), F32)

    @pl.when(i > 0)
    def _():
        ubuf[0, 0:H, :] = ubuf[0, TS:TS + H, :]

    x = x_ref[...]
    ms = jnp.mean(x * x, axis=-1, keepdims=True)
    y = x * lax.rsqrt(ms + EPS)
    h = (y * g_ref[...]) * (1.0 + ada_ref[1:2, :]) + ada_ref[0:1, :]
    hb = h.astype(BF16)

    def proj(base):
        return jnp.dot(hb, w_ref[:, base:base + PROJ_CHUNK], preferred_element_type=F32)

    ubuf[0, H:H + TS, :] = proj(QKV_COLS) * _sigmoid(proj(QKV_COLS + C))
    n = TS + H - 8
    for b in range(1, 8):
        ubuf[b, 0:n, :] = ubuf[0, b:b + n, :]
    jobs = ([(qkv_ref, c, c) for c in range(0, QKV_COLS, PROJ_CHUNK)]
            + [(gate_ref, c, QKV_COLS + GLU_COLS + c) for c in range(0, GATE_COLS, PROJ_CHUNK)])
    rows = list(range(0, TS, CONV_CHUNK))
    for n, (ref, c, base) in enumerate(jobs):
        r = proj(base)
        if base < ATTN_WIDTH:
            r = r * Q_SCALE
        ref[:, c:c + PROJ_CHUNK] = r.astype(BF16)
        for r0 in rows[n * len(rows) // len(jobs):(n + 1) * len(rows) // len(jobs)]:
            _conv_chunk(r0, ubuf, cw_ref, cb_ref, lng_ref, lnb_ref, u_ref)


def _in_proj(x, ada3, g, w, cw, cb, lng, lnb):
    bsz, s, d = x.shape
    tm = ROW_TILE
    C = CONV_WIDTH
    row = lambda n: pl.BlockSpec((None, tm, n), lambda b, i: (b, i, 0))
    vec = lambda a: a.reshape(1, C)
    return pl.pallas_call(
        _inproj_kernel,
        grid=(bsz, s // tm),
        in_specs=[row(d),
                  pl.BlockSpec((None, 6, d), lambda b, i: (b, 0, 0)),
                  _resident((1, d)),
                  _resident((d, IN_COLS)),
                  _resident((CONV_KERNEL, 8, C)),
                  _resident((1, C)), _resident((1, C)), _resident((1, C))],
        out_specs=[row(QKV_COLS), row(C), row(GATE_COLS)],
        out_shape=[jax.ShapeDtypeStruct((bsz, s, QKV_COLS), BF16),
                   jax.ShapeDtypeStruct((bsz, s, C), BF16),
                   jax.ShapeDtypeStruct((bsz, s, GATE_COLS), BF16)],
        scratch_shapes=[pltpu.VMEM((8, tm + CONV_HALO, C), F32)],
        compiler_params=_params(("arbitrary", "arbitrary")),
        name="in_proj",
    )(x, ada3, g.reshape(1, d), w, jnp.broadcast_to(cw[:, None, :], (CONV_KERNEL, 8, C)),
      vec(cb), vec(lng), vec(lnb))


def _attn_kernel(lamv_ref, subg_ref, q_ref, k_ref, v_ref, bias_ref, *rest):
    n_cast = (len(rest) - 5) // 2
    cast_in, o_ref, cast_out = rest[:n_cast], rest[n_cast], rest[n_cast + 1:2 * n_cast + 1]
    vt_ref, qz_ref, m_ref, a_ref = rest[2 * n_cast + 1:]
    T = ATTN_TILE
    NS = 2 * N_HEADS_A
    n_q = q_ref.shape[0] // T
    hcols = lambda h: slice(h * HEAD_V, (h + 1) * HEAD_V)
    rows = lambda t: slice(t * T, (t + 1) * T)

    vt = v_ref[...].T
    for h in range(N_HEADS_A):
        vt_ref[h, 0:HEAD_V, :] = vt[hcols(h), :]
        vt_ref[h, HEAD_V:, :] = jnp.ones((ONES_ROWS, vt.shape[1]), BF16)

    lv = lamv_ref[...]
    lam = (jnp.exp(jnp.sum(lv[0:1] * lv[1:2], axis=-1, keepdims=True))
           - jnp.exp(jnp.sum(lv[2:3] * lv[3:4], axis=-1, keepdims=True)) + LAM_INIT)
    lane = lax.broadcasted_iota(jnp.int32, (T, HEAD_V), 1)

    def prepare(i):
        slot = i % 2
        for h in range(N_HEADS_A):
            qs = q_ref[rows(i), hcols(h)]
            zero = jnp.zeros_like(qs)
            qz_ref[slot, 2 * h] = jnp.where(lane < HEAD_QK, qs, zero)
            qz_ref[slot, 2 * h + 1] = jnp.where(lane >= HEAD_QK, qs, zero)

    def finalize(i):
        slot = i % 2
        for h in range(N_HEADS_A):
            a1, a2 = a_ref[slot, 2 * h], a_ref[slot, 2 * h + 1]
            o = (a1[:HEAD_V] / a1[HEAD_V:HEAD_V + 1]
                 - lam * (a2[:HEAD_V] / a2[HEAD_V:HEAD_V + 1]))
            ms = jnp.mean(o * o, axis=0, keepdims=True)
            y = (o * lax.rsqrt(ms + EPS)).T
            o_ref[rows(i), hcols(h)] = ((y * subg_ref[...]) * (1.0 - LAM_INIT)).astype(BF16)

    units = [(i, j, n) for i in range(n_q) for j in range(i + 1) for n in range(NS)]
    scores, probs, alphas = {}, {}, {}

    def score(u):
        i, j, n = units[u]
        kc = k_ref[rows(j), hcols(n // 2)]
        scores[u] = lax.dot_general(kc, qz_ref[i % 2, n], (((1,), (1,)), ((), ())),
                                    preferred_element_type=F32)

    def softmax(u):
        i, j, n = units[u]
        s = scores.pop(u)
        if j >= i - 1:
            s = s + bias_ref[n // 2, j - (i - 1)]
        m_new = jnp.max(s, axis=0, keepdims=True)
        if j > 0:
            m_old = m_ref[i % 2, n]
            m_new = jnp.maximum(m_old, m_new)
            alphas[u] = jnp.exp2(m_old - m_new)
        probs[u] = jnp.exp2(s - m_new).astype(BF16)
        m_ref[i % 2, n] = m_new

    def value(u):
        i, j, n = units[u]
        vc = vt_ref[n // 2, :, rows(j)]
        pv = jnp.dot(vc, probs.pop(u), preferred_element_type=F32)
        a_ref[i % 2, n] = pv if j == 0 else alphas.pop(u) * a_ref[i % 2, n] + pv

    for step in range(len(units) + 2 * ATTN_LAG):
        if step < len(units):
            if units[step][1:] == (0, 0):
                prepare(units[step][0])
            score(step)
        if 0 <= step - ATTN_LAG < len(units):
            softmax(step - ATTN_LAG)
        done = step - 2 * ATTN_LAG
        if 0 <= done < len(units):
            value(done)
            i, j, n = units[done]
            if (j, n) == (i, NS - 1):
                finalize(i)

    for w_ref, wb_ref in zip(cast_in, cast_out):
        wb_ref[...] = w_ref[...].astype(BF16)


def _attention(qkv, bias_tiles, lamv, subg, weights):
    bsz, s, _ = qkv.shape
    T = ATTN_TILE
    W = ATTN_WIDTH
    NS = 2 * N_HEADS_A
    steps = bsz

    def slab_spec(w, squeeze):
        rows, cols = w.shape[-2:]
        assert rows % (steps * CAST_ROWS) == 0
        block = (rows // steps, cols)
        if squeeze:
            return pl.BlockSpec((None,) + block, lambda b: (0, b, 0))
        return pl.BlockSpec(block, lambda b: (b, 0))

    seq = lambda col: pl.BlockSpec((None, s, W), lambda b: (b, 0, col))
    out = pl.pallas_call(
        _attn_kernel,
        grid=(bsz,),
        in_specs=[_resident((4, HEAD_QK)),
                  _resident((1, HEAD_V)),
                  seq(0), seq(1), seq(2),
                  _resident((N_HEADS_A, 2, T, T))] + [slab_spec(w, True) for w in weights],
        out_specs=[seq(0)] + [slab_spec(w, False) for w in weights],
        out_shape=[jax.ShapeDtypeStruct((bsz, s, W), BF16)]
        + [jax.ShapeDtypeStruct(w.shape[-2:], BF16) for w in weights],
        scratch_shapes=[pltpu.VMEM((N_HEADS_A, HEAD_V + ONES_ROWS, s), BF16),
                        pltpu.VMEM((2, NS, T, HEAD_V), BF16),
                        pltpu.VMEM((2, NS, 1, T), F32),
                        pltpu.VMEM((2, NS, HEAD_V + ONES_ROWS, T), F32)],
        compiler_params=_params(("arbitrary",)),
        name="diff_attn",
    )(lamv, subg.reshape(1, HEAD_V), qkv, qkv, qkv, bias_tiles, *weights)
    return out[0], out[1:]


def _mixffn_kernel(x_ref, ada_ref, a_ref, u_ref, gate_ref, woa_ref, woc_ref, boc_ref,
                   wout_ref, g2_ref, w1_ref, w2_ref, gf_ref, o_ref,
                   x1_ref, h2_ref, act_ref):
    D = D_MODEL
    a = jnp.dot(a_ref[...], woa_ref[...], preferred_element_type=F32)
    cv = jnp.dot(u_ref[...], woc_ref[...], preferred_element_type=F32) + boc_ref[...]
    gate = gate_ref[...]
    y = (_sigmoid(gate[:, :D].astype(F32)) * a
         + _sigmoid(gate[:, D:].astype(F32)) * cv)
    z = jnp.dot(y.astype(BF16), wout_ref[...], preferred_element_type=F32)
    x1 = x_ref[...] + ada_ref[2:3, :] * z
    x1_ref[...] = x1
    ms = jnp.mean(x1 * x1, axis=-1, keepdims=True)
    yn = x1 * lax.rsqrt(ms + EPS)
    h2_ref[...] = ((yn * g2_ref[...]) * (1.0 + ada_ref[4:5, :]) + ada_ref[3:4, :]).astype(BF16)

    h = h2_ref[...]
    for c in range(0, D_FF, FF_CHUNK):
        fg = jnp.dot(h, w1_ref[:, c:c + FF_CHUNK], preferred_element_type=F32)
        fu = jnp.dot(h, w1_ref[:, D_FF + c:D_FF + c + FF_CHUNK],
                     preferred_element_type=F32)
        act_ref[:, c:c + FF_CHUNK] = (_silu(fg) * fu).astype(BF16)
    z2 = jnp.dot(act_ref[...], w2_ref[...], preferred_element_type=F32)
    x2 = x1_ref[...] + ada_ref[5:6, :] * z2
    ms2 = jnp.mean(x2 * x2, axis=-1, keepdims=True)
    o_ref[...] = (x2 * lax.rsqrt(ms2 + EPS)) * gf_ref[...]


def _mix_ffn(x, ada3, a_n, u, gates, woa, woc, boc, wout, g2, w1, w2, gf):
    bsz, s, d = x.shape
    tm = ROW_TILE
    row = lambda n: pl.BlockSpec((None, tm, n), lambda b, i: (b, i, 0))
    return pl.pallas_call(
        _mixffn_kernel,
        grid=(bsz, s // tm),
        in_specs=[row(d),
                  pl.BlockSpec((None, 6, d), lambda b, i: (b, 0, 0)),
                  row(ATTN_WIDTH), row(CONV_WIDTH), row(GATE_COLS),
                  _resident((ATTN_WIDTH, d)), _resident((CONV_WIDTH, d)),
                  _resident((1, d)), _resident((d, d)), _resident((1, d)),
                  _resident((d, 2 * D_FF)), _resident((D_FF, d)), _resident((1, d))],
        out_specs=row(d),
        out_shape=jax.ShapeDtypeStruct((bsz, s, d), F32),
        scratch_shapes=[pltpu.VMEM((tm, d), F32), pltpu.VMEM((tm, d), BF16),
                        pltpu.VMEM((tm, D_FF), BF16)],
        compiler_params=_params(("parallel", "arbitrary")),
        name="mix_ffn",
    )(x, ada3, a_n, u, gates, woa, woc, boc.reshape(1, d), wout, g2.reshape(1, d),
      w1, w2, gf.reshape(1, d))


def kernel(x, c, w_ada, b_ada, norm1_g, norm2_g, final_g, w_in, lambda_q1, lambda_k1,
           lambda_q2, lambda_k2, rel_bias, attn_sub_g, w_o_attn, conv_w, conv_b,
           conv_ln_g, conv_ln_b, w_o_conv, b_o_conv, w_out, w_ffn_in, w_ffn_out):
    bsz, s, d = x.shape
    assert w_ada.shape[0] == 1, "single-layer block"
    assert (d, s % ROW_TILE, s % ATTN_TILE) == (D_MODEL, 0, 0)
    assert PROJ_CHUNK == CONV_WIDTH and ATTN_WIDTH % PROJ_CHUNK == 0
    l = 0
    ada, w_in_bf, bias_tiles = _prologue(c, w_ada, b_ada, w_in, rel_bias)
    ada3 = ada.reshape(bsz, 6, d)
    qkv, u, gates = _in_proj(x, ada3, norm1_g[l], w_in_bf, conv_w[l],
                             conv_b[l], conv_ln_g[l], conv_ln_b[l])
    lamv = jnp.stack([lambda_q1[l], lambda_k1[l], lambda_q2[l], lambda_k2[l]])
    a_n, (woa, woc, wout, w1, w2) = _attention(
        qkv, bias_tiles, lamv, attn_sub_g[l],
        [w_o_attn, w_o_conv, w_out, w_ffn_in, w_ffn_out])
    return _mix_ffn(x, ada3, a_n, u, gates, woa, woc, b_o_conv[l], wout, norm2_g[l],
                    w1, w2, final_g)
```

```python
import math

import jax
import jax.numpy as jnp
from jax import lax
from jax.experimental import pallas as pl
from jax.experimental.pallas import tpu as pltpu

D_MODEL = 1024
N_HEADS_A = 4
HEAD_QK = 64
HEAD_V = 2 * HEAD_QK
ATTN_WIDTH = N_HEADS_A * HEAD_V
CONV_WIDTH = 512
CONV_KERNEL = 31
QKV_COLS = 3 * ATTN_WIDTH
GLU_COLS = 2 * CONV_WIDTH
GATE_COLS = 2 * D_MODEL
IN_COLS = QKV_COLS + GLU_COLS + GATE_COLS
D_FF = 2816
N_BUCKETS = 32
MAX_EXACT = 16
MAX_DISTANCE = 128
EPS = 1e-6
NEG_INF = -1e30
LAM_INIT = 0.8 - 0.6 * math.exp(-0.3 * 0)

V7X_VMEM_BYTES = 64 * 1024 * 1024
VMEM_LIMIT = V7X_VMEM_BYTES - 12 * 1024 * 1024

ROW_TILE = 512
ATTN_TILE = 256
ATTN_LAG = 4
ONES_ROWS = 16
CAST_ROWS = 16
LOG2E = math.log2(math.e)
Q_SCALE = HEAD_QK ** -0.5 * LOG2E
CONV_HALO = 32
CONV_CHUNK = 16
PROJ_CHUNK = 512
FF_CHUNK = 256

BF16 = jnp.bfloat16
F32 = jnp.float32


def _sigmoid(x):
    return 1.0 / (1.0 + jnp.exp2(x * (-LOG2E)))


def _silu(x):
    return x * _sigmoid(x)


def _params(sem):
    return pltpu.CompilerParams(dimension_semantics=sem, vmem_limit_bytes=VMEM_LIMIT)


def _resident(shape):
    nd = len(shape)
    return pl.BlockSpec(shape, lambda *_: (0,) * nd, pipeline_mode=pl.Buffered(1))


def _bias_tile(rel_ref, h, t):
    T = ATTN_TILE
    key = lax.broadcasted_iota(jnp.int32, (T, T), 0)
    qry = lax.broadcasted_iota(jnp.int32, (T, T), 1)
    dist = (1 - t) * T + qry - key
    n = jnp.maximum(dist, 0)
    large = MAX_EXACT + (jnp.log(jnp.maximum(n, 1).astype(F32) / MAX_EXACT)
                         / math.log(MAX_DISTANCE / MAX_EXACT)
                         * (N_BUCKETS - MAX_EXACT)).astype(jnp.int32)
    large = jnp.minimum(large, N_BUCKETS - 1)
    bucket = jnp.where(n < MAX_EXACT, n, large)
    far = rel_ref[N_BUCKETS - 1, h]
    bias = jnp.zeros((T, T), F32)
    for b in range(N_BUCKETS - 1):
        bias = jnp.where(bucket == b, (rel_ref[b, h] - far) * LOG2E, bias)
    return jnp.where(dist >= 0, bias, NEG_INF)


def _prologue_kernel(rel_ref, c_ref, w_ref, b_ref, win_ref, o_ref, win_bf_ref, bias_ref):
    j = pl.program_id(0)
    ca = _silu(c_ref[...])
    o_ref[...] = jnp.dot(ca.astype(BF16), w_ref[...].astype(BF16),
                         preferred_element_type=F32) + b_ref[...]
    win_bf_ref[...] = win_ref[...].astype(BF16)
    bias_ref[...] = _bias_tile(rel_ref, j // 2, j % 2)


def _prologue(c, w, b, w_in, rel_bias):
    bsz, d = c.shape
    n = w.shape[-1]
    steps = 2 * N_HEADS_A
    tn = n // steps
    _, rows, cols = w_in.shape
    tr = rows // steps
    T = ATTN_TILE
    return pl.pallas_call(
        _prologue_kernel,
        grid=(steps,),
        in_specs=[pl.BlockSpec(memory_space=pltpu.SMEM),
                  pl.BlockSpec((bsz, d), lambda j: (0, 0)),
                  pl.BlockSpec((None, d, tn), lambda j: (0, 0, j)),
                  pl.BlockSpec((1, tn), lambda j: (0, j)),
                  pl.BlockSpec((None, tr, cols), lambda j: (0, j, 0))],
        out_specs=[pl.BlockSpec((bsz, tn), lambda j: (0, j)),
                   pl.BlockSpec((tr, cols), lambda j: (j, 0)),
                   pl.BlockSpec((None, None, T, T), lambda j: (j // 2, j % 2, 0, 0))],
        out_shape=[jax.ShapeDtypeStruct((bsz, n), F32),
                   jax.ShapeDtypeStruct((rows, cols), BF16),
                   jax.ShapeDtypeStruct((N_HEADS_A, 2, T, T), F32)],
        compiler_params=_params(("arbitrary",)),
        name="prologue",
    )(rel_bias, c, w, b, w_in)


def _conv_chunk(r0, ubuf, w_ref, cb_ref, lng_ref, lnb_ref, o_ref):
    off = CONV_HALO - (CONV_KERNEL - 1)
    acc = jnp.zeros((CONV_CHUNK // 8, 8, CONV_WIDTH), F32)
    for k in range(CONV_KERNEL):
        a, b = divmod(k + off, 8)
        rows = ubuf[b, r0 + 8 * a:r0 + 8 * a + CONV_CHUNK, :]
        acc = acc + rows.reshape(acc.shape) * w_ref[k][None]
    u = acc.reshape(CONV_CHUNK, CONV_WIDTH) + cb_ref[...]
    mu = jnp.mean(u, axis=-1, keepdims=True)
    d = u - mu
    var = jnp.mean(d * d, axis=-1, keepdims=True)
    y = d * lax.rsqrt(var + EPS) * lng_ref[...] + lnb_ref[...]
    o_ref[r0:r0 + CONV_CHUNK, :] = _silu(y).astype(BF16)


def _inproj_kernel(x_ref, ada_ref, g_ref, w_ref, cw_ref, cb_ref, lng_ref, lnb_ref,
                   qkv_ref, u_ref, gate_ref, ubuf):
    TS, H, C = ROW_TILE, CONV_HALO, CONV_WIDTH
    i = pl.program_id(1)

    @pl.when(i == 0)
    def _():
        ubuf[0, 0:H, :] = jnp.zeros((H, C), F32)

    @pl.when(i > 0)
    def _():
        ubuf[0, 0:H, :] = ubuf[0, TS:TS + H, :]

    x = x_ref[...]
    ms = jnp.mean(x * x, axis=-1, keepdims=True)
    y = x * lax.rsqrt(ms + EPS)
    h = (y * g_ref[...]) * (1.0 + ada_ref[1:2, :]) + ada_ref[0:1, :]
    hb = h.astype(BF16)

    def proj(base):
        return jnp.dot(hb, w_ref[:, base:base + PROJ_CHUNK], preferred_element_type=F32)

    ubuf[0, H:H + TS, :] = proj(QKV_COLS) * _sigmoid(proj(QKV_COLS + C))
    n = TS + H - 8
    for b in range(1, 8):
        ubuf[b, 0:n, :] = ubuf[0, b:b + n, :]
    jobs = ([(qkv_ref, c, c) for c in range(0, QKV_COLS, PROJ_CHUNK)]
            + [(gate_ref, c, QKV_COLS + GLU_COLS + c) for c in range(0, GATE_COLS, PROJ_CHUNK)])
    rows = list(range(0, TS, CONV_CHUNK))
    for n, (ref, c, base) in enumerate(jobs):
        r = proj(base)
        if base < ATTN_WIDTH:
            r = r * Q_SCALE
        ref[:, c:c + PROJ_CHUNK] = r.astype(BF16)
        for r0 in rows[n * len(rows) // len(jobs):(n + 1) * len(rows) // len(jobs)]:
            _conv_chunk(r0, ubuf, cw_ref, cb_ref, lng_ref, lnb_ref, u_ref)


def _in_proj(x, ada3, g, w, cw, cb, lng, lnb):
    bsz, s, d = x.shape
    tm = ROW_TILE
    C = CONV_WIDTH
    row = lambda n: pl.BlockSpec((None, tm, n), lambda b, i: (b, i, 0))
    vec = lambda a: a.reshape(1, C)
    return pl.pallas_call(
        _inproj_kernel,
        grid=(bsz, s // tm),
        in_specs=[row(d),
                  pl.BlockSpec((None, 6, d), lambda b, i: (b, 0, 0)),
                  _resident((1, d)),
                  _resident((d, IN_COLS)),
                  _resident((CONV_KERNEL, 8, C)),
                  _resident((1, C)), _resident((1, C)), _resident((1, C))],
        out_specs=[row(QKV_COLS), row(C), row(GATE_COLS)],
        out_shape=[jax.ShapeDtypeStruct((bsz, s, QKV_COLS), BF16),
                   jax.ShapeDtypeStruct((bsz, s, C), BF16),
                   jax.ShapeDtypeStruct((bsz, s, GATE_COLS), BF16)],
        scratch_shapes=[pltpu.VMEM((8, tm + CONV_HALO, C), F32)],
        compiler_params=_params(("arbitrary", "arbitrary")),
        name="in_proj",
    )(x, ada3, g.reshape(1, d), w, jnp.broadcast_to(cw[:, None, :], (CONV_KERNEL, 8, C)),
      vec(cb), vec(lng), vec(lnb))


def _attn_kernel(lamv_ref, subg_ref, q_ref, k_ref, v_ref, bias_ref, *rest):
    n_cast = (len(rest) - 5) // 2
    cast_in, o_ref, cast_out = rest[:n_cast], rest[n_cast], rest[n_cast + 1:2 * n_cast + 1]
    vt_ref, qz_ref, m_ref, a_ref = rest[2 * n_cast + 1:]
    T = ATTN_TILE
    NS = 2 * N_HEADS_A
    n_q = q_ref.shape[0] // T
    hcols = lambda h: slice(h * HEAD_V, (h + 1) * HEAD_V)
    rows = lambda t: slice(t * T, (t + 1) * T)

    vt = v_ref[...].T
    for h in range(N_HEADS_A):
        vt_ref[h, 0:HEAD_V, :] = vt[hcols(h), :]
        vt_ref[h, HEAD_V:, :] = jnp.ones((ONES_ROWS, vt.shape[1]), BF16)

    lv = lamv_ref[...]
    lam = (jnp.exp(jnp.sum(lv[0:1] * lv[1:2], axis=-1, keepdims=True))
           - jnp.exp(jnp.sum(lv[2:3] * lv[3:4], axis=-1, keepdims=True)) + LAM_INIT)
    lane = lax.broadcasted_iota(jnp.int32, (T, HEAD_V), 1)

    def prepare(i):
        slot = i % 2
        for h in range(N_HEADS_A):
            qs = q_ref[rows(i), hcols(h)]
            zero = jnp.zeros_like(qs)
            qz_ref[slot, 2 * h] = jnp.where(lane < HEAD_QK, qs, zero)
            qz_ref[slot, 2 * h + 1] = jnp.where(lane >= HEAD_QK, qs, zero)

    def finalize(i):
        slot = i % 2
        for h in range(N_HEADS_A):
            a1, a2 = a_ref[slot, 2 * h], a_ref[slot, 2 * h + 1]
            o = (a1[:HEAD_V] / a1[HEAD_V:HEAD_V + 1]
                 - lam * (a2[:HEAD_V] / a2[HEAD_V:HEAD_V + 1]))
            ms = jnp.mean(o * o, axis=0, keepdims=True)
            y = (o * lax.rsqrt(ms + EPS)).T
            o_ref[rows(i), hcols(h)] = ((y * subg_ref[...]) * (1.0 - LAM_INIT)).astype(BF16)

    units = [(i, j, n) for i in range(n_q) for j in range(i + 1) for n in range(NS)]
    scores, probs, alphas = {}, {}, {}

    def score(u):
        i, j, n = units[u]
        kc = k_ref[rows(j), hcols(n // 2)]
        scores[u] = lax.dot_general(kc, qz_ref[i % 2, n], (((1,), (1,)), ((), ())),
                                    preferred_element_type=F32)

    def softmax(u):
        i, j, n = units[u]
        s = scores.pop(u)
        h, near = n // 2, MAX_DISTANCE
        on_diag = j == i
        if j == i - 1:
            corner = s[near:, :near] + bias_ref[h, 0, near:, :near]
            s = jnp.concatenate(
                [s[:near], jnp.concatenate([corner, s[near:, near:]], axis=1)], axis=0)
        if on_diag:
            s_lo = s[:near, :near] + bias_ref[h, 1, :near, :near]
            s_hi = s[:, near:] + bias_ref[h, 1, :, near:]
            m_new = jnp.concatenate([jnp.max(s_lo, axis=0, keepdims=True),
                                     jnp.max(s_hi, axis=0, keepdims=True)], axis=1)
        else:
            m_new = jnp.max(s, axis=0, keepdims=True)
        if j > 0:
            m_old = m_ref[i % 2, n]
            m_new = jnp.maximum(m_old, m_new)
            alphas[u] = jnp.exp2(m_old - m_new)
        if on_diag:
            p_lo = jnp.exp2(s_lo - m_new[:, :near]).astype(BF16)
            p_hi = jnp.exp2(s_hi - m_new[:, near:]).astype(BF16)
            p_lo = jnp.concatenate([p_lo, jnp.zeros((T - near, near), BF16)], axis=0)
            probs[u] = jnp.concatenate([p_lo, p_hi], axis=1)
        else:
            probs[u] = jnp.exp2(s - m_new).astype(BF16)
        m_ref[i % 2, n] = m_new

    def value(u):
        i, j, n = units[u]
        vc = vt_ref[n // 2, :, rows(j)]
        pv = jnp.dot(vc, probs.pop(u), preferred_element_type=F32)
        a_ref[i % 2, n] = pv if j == 0 else alphas.pop(u) * a_ref[i % 2, n] + pv

    for step in range(len(units) + 2 * ATTN_LAG):
        if step < len(units):
            if units[step][1:] == (0, 0):
                prepare(units[step][0])
            score(step)
        if 0 <= step - ATTN_LAG < len(units):
            softmax(step - ATTN_LAG)
        done = step - 2 * ATTN_LAG
        if 0 <= done < len(units):
            value(done)
            i, j, n = units[done]
            if (j, n) == (i, NS - 1):
                finalize(i)

    for w_ref, wb_ref in zip(cast_in, cast_out):
        wb_ref[...] = w_ref[...].astype(BF16)


def _attention(qkv, bias_tiles, lamv, subg, weights):
    bsz, s, _ = qkv.shape
    T = ATTN_TILE
    W = ATTN_WIDTH
    NS = 2 * N_HEADS_A
    steps = bsz

    def slab_spec(w, squeeze):
        rows, cols = w.shape[-2:]
        assert rows % (steps * CAST_ROWS) == 0
        block = (rows // steps, cols)
        if squeeze:
            return pl.BlockSpec((None,) + block, lambda b: (0, b, 0))
        return pl.BlockSpec(block, lambda b: (b, 0))

    seq = lambda col: pl.BlockSpec((None, s, W), lambda b: (b, 0, col))
    out = pl.pallas_call(
        _attn_kernel,
        grid=(bsz,),
        in_specs=[_resident((4, HEAD_QK)),
                  _resident((1, HEAD_V)),
                  seq(0), seq(1), seq(2),
                  _resident((N_HEADS_A, 2, T, T))] + [slab_spec(w, True) for w in weights],
        out_specs=[seq(0)] + [slab_spec(w, False) for w in weights],
        out_shape=[jax.ShapeDtypeStruct((bsz, s, W), BF16)]
        + [jax.ShapeDtypeStruct(w.shape[-2:], BF16) for w in weights],
        scratch_shapes=[pltpu.VMEM((N_HEADS_A, HEAD_V + ONES_ROWS, s), BF16),
                        pltpu.VMEM((2, NS, T, HEAD_V), BF16),
                        pltpu.VMEM((2, NS, 1, T), F32),
                        pltpu.VMEM((2, NS, HEAD_V + ONES_ROWS, T), F32)],
        compiler_params=_params(("arbitrary",)),
        name="diff_attn",
    )(lamv, subg.reshape(1, HEAD_V), qkv, qkv, qkv, bias_tiles, *weights)
    return out[0], out[1:]


def _mixffn_kernel(x_ref, ada_ref, a_ref, u_ref, gate_ref, woa_ref, woc_ref, boc_ref,
                   wout_ref, g2_ref, w1_ref, w2_ref, gf_ref, o_ref,
                   x1_ref, h2_ref, act_ref):
    D = D_MODEL
    a = jnp.dot(a_ref[...], woa_ref[...], preferred_element_type=F32)
    cv = jnp.dot(u_ref[...], woc_ref[...], preferred_element_type=F32) + boc_ref[...]
    gate = gate_ref[...]
    y = (_sigmoid(gate[:, :D].astype(F32)) * a
         + _sigmoid(gate[:, D:].astype(F32)) * cv)
    z = jnp.dot(y.astype(BF16), wout_ref[...], preferred_element_type=F32)
    x1 = x_ref[...] + ada_ref[2:3, :] * z
    x1_ref[...] = x1
    ms = jnp.mean(x1 * x1, axis=-1, keepdims=True)
    yn = x1 * lax.rsqrt(ms + EPS)
    h2_ref[...] = ((yn * g2_ref[...]) * (1.0 + ada_ref[4:5, :]) + ada_ref[3:4, :]).astype(BF16)

    h = h2_ref[...]
    for c in range(0, D_FF, FF_CHUNK):
        fg = jnp.dot(h, w1_ref[:, c:c + FF_CHUNK], preferred_element_type=F32)
        fu = jnp.dot(h, w1_ref[:, D_FF + c:D_FF + c + FF_CHUNK],
                     preferred_element_type=F32)
        act_ref[:, c:c + FF_CHUNK] = (_silu(fg) * fu).astype(BF16)
    z2 = jnp.dot(act_ref[...], w2_ref[...], preferred_element_type=F32)
    x2 = x1_ref[...] + ada_ref[5:6, :] * z2
    ms2 = jnp.mean(x2 * x2, axis=-1, keepdims=True)
    o_ref[...] = (x2 * lax.rsqrt(ms2 + EPS)) * gf_ref[...]


def _mix_ffn(x, ada3, a_n, u, gates, woa, woc, boc, wout, g2, w1, w2, gf):
    bsz, s, d = x.shape
    tm = ROW_TILE
    row = lambda n: pl.BlockSpec((None, tm, n), lambda b, i: (b, i, 0))
    return pl.pallas_call(
        _mixffn_kernel,
        grid=(bsz, s // tm),
        in_specs=[row(d),
                  pl.BlockSpec((None, 6, d), lambda b, i: (b, 0, 0)),
                  row(ATTN_WIDTH), row(CONV_WIDTH), row(GATE_COLS),
                  _resident((ATTN_WIDTH, d)), _resident((CONV_WIDTH, d)),
                  _resident((1, d)), _resident((d, d)), _resident((1, d)),
                  _resident((d, 2 * D_FF)), _resident((D_FF, d)), _resident((1, d))],
        out_specs=row(d),
        out_shape=jax.ShapeDtypeStruct((bsz, s, d), F32),
        scratch_shapes=[pltpu.VMEM((tm, d), F32), pltpu.VMEM((tm, d), BF16),
                        pltpu.VMEM((tm, D_FF), BF16)],
        compiler_params=_params(("parallel", "arbitrary")),
        name="mix_ffn",
    )(x, ada3, a_n, u, gates, woa, woc, boc.reshape(1, d), wout, g2.reshape(1, d),
      w1, w2, gf.reshape(1, d))


def kernel(x, c, w_ada, b_ada, norm1_g, norm2_g, final_g, w_in, lambda_q1, lambda_k1,
           lambda_q2, lambda_k2, rel_bias, attn_sub_g, w_o_attn, conv_w, conv_b,
           conv_ln_g, conv_ln_b, w_o_conv, b_o_conv, w_out, w_ffn_in, w_ffn_out):
    bsz, s, d = x.shape
    assert w_ada.shape[0] == 1, "single-layer block"
    assert (d, s % ROW_TILE, s % ATTN_TILE) == (D_MODEL, 0, 0)
    assert ATTN_TILE == 2 * MAX_DISTANCE
    assert PROJ_CHUNK == CONV_WIDTH and ATTN_WIDTH % PROJ_CHUNK == 0
    l = 0
    ada, w_in_bf, bias_tiles = _prologue(c, w_ada, b_ada, w_in, rel_bias)
    ada3 = ada.reshape(bsz, 6, d)
    qkv, u, gates = _in_proj(x, ada3, norm1_g[l], w_in_bf, conv_w[l],
                             conv_b[l], conv_ln_g[l], conv_ln_b[l])
    lamv = jnp.stack([lambda_q1[l], lambda_k1[l], lambda_q2[l], lambda_k2[l]])
    a_n, (woa, woc, wout, w1, w2) = _attention(
        qkv, bias_tiles, lamv, attn_sub_g[l],
        [w_o_attn, w_o_conv, w_out, w_ffn_in, w_ffn_out])
    return _mix_ffn(x, ada3, a_n, u, gates, woa, woc, b_o_conv[l], wout, norm2_g[l],
                    w1, w2, final_g)
```

```python
import math

import jax
import jax.numpy as jnp
from jax import lax
from jax.experimental import pallas as pl
from jax.experimental.pallas import tpu as pltpu

D_MODEL = 1024
N_HEADS_A = 4
HEAD_QK = 64
HEAD_V = 2 * HEAD_QK
ATTN_WIDTH = N_HEADS_A * HEAD_V
CONV_WIDTH = 512
CONV_KERNEL = 31
QKV_COLS = 3 * ATTN_WIDTH
GLU_COLS = 2 * CONV_WIDTH
GATE_COLS = 2 * D_MODEL
IN_COLS = QKV_COLS + GLU_COLS + GATE_COLS
D_FF = 2816
N_BUCKETS = 32
MAX_EXACT = 16
MAX_DISTANCE = 128
EPS = 1e-6
NEG_INF = -1e30
LAM_INIT = 0.8 - 0.6 * math.exp(-0.3 * 0)

V7X_VMEM_BYTES = 64 * 1024 * 1024
VMEM_LIMIT = V7X_VMEM_BYTES - 12 * 1024 * 1024

ROW_TILE = 512
ATTN_TILE = 256
ATTN_LAG = 4
ONES_ROWS = 16
CAST_ROWS = 16
LOG2E = math.log2(math.e)
Q_SCALE = HEAD_QK ** -0.5 * LOG2E
CONV_HALO = 32
CONV_CHUNK = 16
PROJ_CHUNK = 512
FF_CHUNK = 256

BF16 = jnp.bfloat16
F32 = jnp.float32


def _sigmoid(x):
    return 1.0 / (1.0 + jnp.exp2(x * (-LOG2E)))


def _silu(x):
    return x * _sigmoid(x)


def _params(sem):
    return pltpu.CompilerParams(dimension_semantics=sem, vmem_limit_bytes=VMEM_LIMIT)


def _resident(shape):
    nd = len(shape)
    return pl.BlockSpec(shape, lambda *_: (0,) * nd, pipeline_mode=pl.Buffered(1))


def _bias_tile(rel_ref, h, t):
    T = ATTN_TILE
    key = lax.broadcasted_iota(jnp.int32, (T, T), 0)
    qry = lax.broadcasted_iota(jnp.int32, (T, T), 1)
    dist = (1 - t) * T + qry - key
    n = jnp.maximum(dist, 0)
    large = MAX_EXACT + (jnp.log(jnp.maximum(n, 1).astype(F32) / MAX_EXACT)
                         / math.log(MAX_DISTANCE / MAX_EXACT)
                         * (N_BUCKETS - MAX_EXACT)).astype(jnp.int32)
    large = jnp.minimum(large, N_BUCKETS - 1)
    bucket = jnp.where(n < MAX_EXACT, n, large)
    far = rel_ref[N_BUCKETS - 1, h]
    bias = jnp.zeros((T, T), F32)
    for b in range(N_BUCKETS - 1):
        bias = jnp.where(bucket == b, (rel_ref[b, h] - far) * LOG2E, bias)
    return jnp.where(dist >= 0, bias, NEG_INF)


def _prologue_kernel(rel_ref, c_ref, w_ref, b_ref, win_ref, o_ref, win_bf_ref, bias_ref):
    j = pl.program_id(0)
    ca = _silu(c_ref[...])
    o_ref[...] = jnp.dot(ca.astype(BF16), w_ref[...].astype(BF16),
                         preferred_element_type=F32) + b_ref[...]
    win_bf_ref[...] = win_ref[...].astype(BF16)
    bias_ref[...] = _bias_tile(rel_ref, j // 2, j % 2)


def _prologue(c, w, b, w_in, rel_bias):
    bsz, d = c.shape
    n = w.shape[-1]
    steps = 2 * N_HEADS_A
    tn = n // steps
    _, rows, cols = w_in.shape
    tr = rows // steps
    T = ATTN_TILE
    return pl.pallas_call(
        _prologue_kernel,
        grid=(steps,),
        in_specs=[pl.BlockSpec(memory_space=pltpu.SMEM),
                  pl.BlockSpec((bsz, d), lambda j: (0, 0)),
                  pl.BlockSpec((None, d, tn), lambda j: (0, 0, j)),
                  pl.BlockSpec((1, tn), lambda j: (0, j)),
                  pl.BlockSpec((None, tr, cols), lambda j: (0, j, 0))],
        out_specs=[pl.BlockSpec((bsz, tn), lambda j: (0, j)),
                   pl.BlockSpec((tr, cols), lambda j: (j, 0)),
                   pl.BlockSpec((None, None, T, T), lambda j: (j // 2, j % 2, 0, 0))],
        out_shape=[jax.ShapeDtypeStruct((bsz, n), F32),
                   jax.ShapeDtypeStruct((rows, cols), BF16),
                   jax.ShapeDtypeStruct((N_HEADS_A, 2, T, T), F32)],
        compiler_params=_params(("arbitrary",)),
        name="prologue",
    )(rel_bias, c, w, b, w_in)


def _conv_chunk(r0, ubuf, w_ref, cb_ref, lng_ref, lnb_ref, o_ref):
    off = CONV_HALO - (CONV_KERNEL - 1)
    acc = jnp.zeros((CONV_CHUNK // 8, 8, CONV_WIDTH), F32)
    for k in range(CONV_KERNEL):
        a, b = divmod(k + off, 8)
        rows = ubuf[b, r0 + 8 * a:r0 + 8 * a + CONV_CHUNK, :]
        acc = acc + rows.reshape(acc.shape) * w_ref[k][None]
    u = acc.reshape(CONV_CHUNK, CONV_WIDTH) + cb_ref[...]
    mu = jnp.mean(u, axis=-1, keepdims=True)
    d = u - mu
    var = jnp.mean(d * d, axis=-1, keepdims=True)
    y = d * lax.rsqrt(var + EPS) * lng_ref[...] + lnb_ref[...]
    o_ref[r0:r0 + CONV_CHUNK, :] = _silu(y).astype(BF16)


def _inproj_kernel(x_ref, ada_ref, g_ref, w_ref, cw_ref, cb_ref, lng_ref, lnb_ref,
                   qkv_ref, u_ref, gate_ref, ubuf):
    TS, H, C = ROW_TILE, CONV_HALO, CONV_WIDTH
    i = pl.program_id(1)

    @pl.when(i == 0)
    def _():
        ubuf[0, 0:H, :] = jnp.zeros((H, C), F32)

    @pl.when(i > 0)
    def _():
        ubuf[0, 0:H, :] = ubuf[0, TS:TS + H, :]

    x = x_ref[...]
    ms = jnp.mean(x * x, axis=-1, keepdims=True)
    y = x * lax.rsqrt(ms + EPS)
    h = (y * g_ref[...]) * (1.0 + ada_ref[1:2, :]) + ada_ref[0:1, :]
    hb = h.astype(BF16)

    def proj(base):
        return jnp.dot(hb, w_ref[:, base:base + PROJ_CHUNK], preferred_element_type=F32)

    ubuf[0, H:H + TS, :] = proj(QKV_COLS) * _sigmoid(proj(QKV_COLS + C))
    n = TS + H - 8
    for b in range(1, 8):
        ubuf[b, 0:n, :] = ubuf[0, b:b + n, :]
    jobs = ([(qkv_ref, c, c) for c in range(0, QKV_COLS, PROJ_CHUNK)]
            + [(gate_ref, c, QKV_COLS + GLU_COLS + c) for c in range(0, GATE_COLS, PROJ_CHUNK)])
    rows = list(range(0, TS, CONV_CHUNK))
    for n, (ref, c, base) in enumerate(jobs):
        r = proj(base)
        if base < ATTN_WIDTH:
            r = r * Q_SCALE
        ref[:, c:c + PROJ_CHUNK] = r.astype(BF16)
        for r0 in rows[n * len(rows) // len(jobs):(n + 1) * len(rows) // len(jobs)]:
            _conv_chunk(r0, ubuf, cw_ref, cb_ref, lng_ref, lnb_ref, u_ref)


def _in_proj(x, ada3, g, w, cw, cb, lng, lnb):
    bsz, s, d = x.shape
    tm = ROW_TILE
    C = CONV_WIDTH
    row = lambda n: pl.BlockSpec((None, tm, n), lambda b, i: (b, i, 0))
    vec = lambda a: a.reshape(1, C)
    return pl.pallas_call(
        _inproj_kernel,
        grid=(bsz, s // tm),
        in_specs=[row(d),
                  pl.BlockSpec((None, 6, d), lambda b, i: (b, 0, 0)),
                  _resident((1, d)),
                  _resident((d, IN_COLS)),
                  _resident((CONV_KERNEL, 8, C)),
                  _resident((1, C)), _resident((1, C)), _resident((1, C))],
        out_specs=[row(QKV_COLS), row(C), row(GATE_COLS)],
        out_shape=[jax.ShapeDtypeStruct((bsz, s, QKV_COLS), BF16),
                   jax.ShapeDtypeStruct((bsz, s, C), BF16),
                   jax.ShapeDtypeStruct((bsz, s, GATE_COLS), BF16)],
        scratch_shapes=[pltpu.VMEM((8, tm + CONV_HALO, C), F32)],
        compiler_params=_params(("arbitrary", "arbitrary")),
        name="in_proj",
    )(x, ada3, g.reshape(1, d), w, jnp.broadcast_to(cw[:, None, :], (CONV_KERNEL, 8, C)),
      vec(cb), vec(lng), vec(lnb))


def _attn_kernel(lamv_ref, subg_ref, q_ref, k_ref, v_ref, bias_ref, *rest):
    n_cast = (len(rest) - 5) // 2
    cast_in, o_ref, cast_out = rest[:n_cast], rest[n_cast], rest[n_cast + 1:2 * n_cast + 1]
    vt_ref, qz_ref, m_ref, a_ref = rest[2 * n_cast + 1:]
    T = ATTN_TILE
    NS = 2 * N_HEADS_A
    n_q = q_ref.shape[0] // T
    hcols = lambda h: slice(h * HEAD_V, (h + 1) * HEAD_V)
    rows = lambda t: slice(t * T, (t + 1) * T)

    vt = v_ref[...].T
    for h in range(N_HEADS_A):
        vt_ref[h, 0:HEAD_V, :] = vt[hcols(h), :]
        vt_ref[h, HEAD_V:, :] = jnp.ones((ONES_ROWS, vt.shape[1]), BF16)

    lv = lamv_ref[...]
    lam = (jnp.exp(jnp.sum(lv[0:1] * lv[1:2], axis=-1, keepdims=True))
           - jnp.exp(jnp.sum(lv[2:3] * lv[3:4], axis=-1, keepdims=True)) + LAM_INIT)
    lane = lax.broadcasted_iota(jnp.int32, (T, HEAD_V), 1)

    def prepare(i):
        slot = i % 2
        for h in range(N_HEADS_A):
            qs = q_ref[rows(i), hcols(h)]
            zero = jnp.zeros_like(qs)
            qz_ref[slot, 2 * h] = jnp.where(lane < HEAD_QK, qs, zero)
            qz_ref[slot, 2 * h + 1] = jnp.where(lane >= HEAD_QK, qs, zero)

    def finalize(i):
        slot = i % 2
        for h in range(N_HEADS_A):
            a1, a2 = a_ref[slot, 2 * h], a_ref[slot, 2 * h + 1]
            o = (a1[:HEAD_V] / a1[HEAD_V:HEAD_V + 1]
                 - lam * (a2[:HEAD_V] / a2[HEAD_V:HEAD_V + 1]))
            ms = jnp.mean(o * o, axis=0, keepdims=True)
            y = (o * lax.rsqrt(ms + EPS)).T
            o_ref[rows(i), hcols(h)] = ((y * subg_ref[...]) * (1.0 - LAM_INIT)).astype(BF16)

    units = [(i, j, n) for i in range(n_q) for j in range(i + 1) for n in range(NS)]
    scores, probs, alphas = {}, {}, {}

    def score(u):
        i, j, n = units[u]
        kc = k_ref[rows(j), hcols(n // 2)]
        scores[u] = lax.dot_general(kc, qz_ref[i % 2, n], (((1,), (1,)), ((), ())),
                                    preferred_element_type=F32)

    def softmax(u):
        i, j, n = units[u]
        s = scores.pop(u)
        h, near = n // 2, MAX_DISTANCE
        on_diag = j == i
        if j == i - 1:
            corner = s[near:, :near] + bias_ref[h, 0, near:, :near]
            s = jnp.concatenate(
                [s[:near], jnp.concatenate([corner, s[near:, near:]], axis=1)], axis=0)
        if on_diag:
            s_lo = s[:near, :near] + bias_ref[h, 1, :near, :near]
            s_hi = s[:, near:] + bias_ref[h, 1, :, near:]
            m_new = jnp.concatenate([jnp.max(s_lo, axis=0, keepdims=True),
                                     jnp.max(s_hi, axis=0, keepdims=True)], axis=1)
        else:
            m_new = jnp.max(s, axis=0, keepdims=True)
        if j > 0:
            m_old = m_ref[i % 2, n]
            m_new = jnp.maximum(m_old, m_new)
            alphas[u] = jnp.exp2(m_old - m_new)
        if on_diag:
            p_lo = jnp.exp2(s_lo - m_new[:, :near]).astype(BF16)
            p_hi = jnp.exp2(s_hi - m_new[:, near:]).astype(BF16)
            p_lo = jnp.concatenate([p_lo, jnp.zeros((T - near, near), BF16)], axis=0)
            probs[u] = jnp.concatenate([p_lo, p_hi], axis=1)
        else:
            probs[u] = jnp.exp2(s - m_new).astype(BF16)
        m_ref[i % 2, n] = m_new

    def value(u):
        i, j, n = units[u]
        vc = vt_ref[n // 2, :, rows(j)]
        pv = jnp.dot(vc, probs.pop(u), preferred_element_type=F32)
        a_ref[i % 2, n] = pv if j == 0 else pv + alphas.pop(u) * a_ref[i % 2, n]

    for step in range(len(units) + 2 * ATTN_LAG):
        if step < len(units):
            if units[step][1:] == (0, 0):
                prepare(units[step][0])
            score(step)
        if 0 <= step - ATTN_LAG < len(units):
            softmax(step - ATTN_LAG)
        done = step - 2 * ATTN_LAG
        if 0 <= done < len(units):
            value(done)
            i, j, n = units[done]
            if (j, n) == (i, NS - 1):
                finalize(i)

    for w_ref, wb_ref in zip(cast_in, cast_out):
        wb_ref[...] = w_ref[...].astype(BF16)


def _attention(qkv, bias_tiles, lamv, subg, weights):
    bsz, s, _ = qkv.shape
    T = ATTN_TILE
    W = ATTN_WIDTH
    NS = 2 * N_HEADS_A
    steps = bsz

    def slab_spec(w, squeeze):
        rows, cols = w.shape[-2:]
        assert rows % (steps * CAST_ROWS) == 0
        block = (rows // steps, cols)
        if squeeze:
            return pl.BlockSpec((None,) + block, lambda b: (0, b, 0))
        return pl.BlockSpec(block, lambda b: (b, 0))

    seq = lambda col: pl.BlockSpec((None, s, W), lambda b: (b, 0, col))
    out = pl.pallas_call(
        _attn_kernel,
        grid=(bsz,),
        in_specs=[_resident((4, HEAD_QK)),
                  _resident((1, HEAD_V)),
                  seq(0), seq(1), seq(2),
                  _resident((N_HEADS_A, 2, T, T))] + [slab_spec(w, True) for w in weights],
        out_specs=[seq(0)] + [slab_spec(w, False) for w in weights],
        out_shape=[jax.ShapeDtypeStruct((bsz, s, W), BF16)]
        + [jax.ShapeDtypeStruct(w.shape[-2:], BF16) for w in weights],
        scratch_shapes=[pltpu.VMEM((N_HEADS_A, HEAD_V + ONES_ROWS, s), BF16),
                        pltpu.VMEM((2, NS, T, HEAD_V), BF16),
                        pltpu.VMEM((2, NS, 1, T), F32),
                        pltpu.VMEM((2, NS, HEAD_V + ONES_ROWS, T), F32)],
        compiler_params=_params(("arbitrary",)),
        name="diff_attn",
    )(lamv, subg.reshape(1, HEAD_V), qkv, qkv, qkv, bias_tiles, *weights)
    return out[0], out[1:]


def _mixffn_kernel(x_ref, ada_ref, a_ref, u_ref, gate_ref, woa_ref, woc_ref, boc_ref,
                   wout_ref, g2_ref, w1_ref, w2_ref, gf_ref, o_ref,
                   x1_ref, h2_ref, act_ref):
    D = D_MODEL
    a = jnp.dot(a_ref[...], woa_ref[...], preferred_element_type=F32)
    cv = jnp.dot(u_ref[...], woc_ref[...], preferred_element_type=F32) + boc_ref[...]
    gate = gate_ref[...]
    y = (_sigmoid(gate[:, :D].astype(F32)) * a
         + _sigmoid(gate[:, D:].astype(F32)) * cv)
    z = jnp.dot(y.astype(BF16), wout_ref[...], preferred_element_type=F32)
    x1 = x_ref[...] + ada_ref[2:3, :] * z
    x1_ref[...] = x1
    ms = jnp.mean(x1 * x1, axis=-1, keepdims=True)
    yn = x1 * lax.rsqrt(ms + EPS)
    h2_ref[...] = ((yn * g2_ref[...]) * (1.0 + ada_ref[4:5, :]) + ada_ref[3:4, :]).astype(BF16)

    h = h2_ref[...]
    for c in range(0, D_FF, FF_CHUNK):
        fg = jnp.dot(h, w1_ref[:, c:c + FF_CHUNK], preferred_element_type=F32)
        fu = jnp.dot(h, w1_ref[:, D_FF + c:D_FF + c + FF_CHUNK],
                     preferred_element_type=F32)
        act_ref[:, c:c + FF_CHUNK] = (_silu(fg) * fu).astype(BF16)
    z2 = jnp.dot(act_ref[...], w2_ref[...], preferred_element_type=F32)
    x2 = x1_ref[...] + ada_ref[5:6, :] * z2
    ms2 = jnp.mean(x2 * x2, axis=-1, keepdims=True)
    o_ref[...] = (x2 * lax.rsqrt(ms2 + EPS)) * gf_ref[...]


def _mix_ffn(x, ada3, a_n, u, gates, woa, woc, boc, wout, g2, w1, w2, gf):
    bsz, s, d = x.shape
    tm = ROW_TILE
    row = lambda n: pl.BlockSpec((None, tm, n), lambda b, i: (b, i, 0))
    return pl.pallas_call(
        _mixffn_kernel,
        grid=(bsz, s // tm),
        in_specs=[row(d),
                  pl.BlockSpec((None, 6, d), lambda b, i: (b, 0, 0)),
                  row(ATTN_WIDTH), row(CONV_WIDTH), row(GATE_COLS),
                  _resident((ATTN_WIDTH, d)), _resident((CONV_WIDTH, d)),
                  _resident((1, d)), _resident((d, d)), _resident((1, d)),
                  _resident((d, 2 * D_FF)), _resident((D_FF, d)), _resident((1, d))],
        out_specs=row(d),
        out_shape=jax.ShapeDtypeStruct((bsz, s, d), F32),
        scratch_shapes=[pltpu.VMEM((tm, d), F32), pltpu.VMEM((tm, d), BF16),
                        pltpu.VMEM((tm, D_FF), BF16)],
        compiler_params=_params(("parallel", "arbitrary")),
        name="mix_ffn",
    )(x, ada3, a_n, u, gates, woa, woc, boc.reshape(1, d), wout, g2.reshape(1, d),
      w1, w2, gf.reshape(1, d))


def kernel(x, c, w_ada, b_ada, norm1_g, norm2_g, final_g, w_in, lambda_q1, lambda_k1,
           lambda_q2, lambda_k2, rel_bias, attn_sub_g, w_o_attn, conv_w, conv_b,
           conv_ln_g, conv_ln_b, w_o_conv, b_o_conv, w_out, w_ffn_in, w_ffn_out):
    bsz, s, d = x.shape
    assert w_ada.shape[0] == 1, "single-layer block"
    assert (d, s % ROW_TILE, s % ATTN_TILE) == (D_MODEL, 0, 0)
    assert ATTN_TILE == 2 * MAX_DISTANCE
    assert PROJ_CHUNK == CONV_WIDTH and ATTN_WIDTH % PROJ_CHUNK == 0
    l = 0
    ada, w_in_bf, bias_tiles = _prologue(c, w_ada, b_ada, w_in, rel_bias)
    ada3 = ada.reshape(bsz, 6, d)
    qkv, u, gates = _in_proj(x, ada3, norm1_g[l], w_in_bf, conv_w[l],
                             conv_b[l], conv_ln_g[l], conv_ln_b[l])
    lamv = jnp.stack([lambda_q1[l], lambda_k1[l], lambda_q2[l], lambda_k2[l]])
    a_n, (woa, woc, wout, w1, w2) = _attention(
        qkv, bias_tiles, lamv, attn_sub_g[l],
        [w_o_attn, w_o_conv, w_out, w_ffn_in, w_ffn_out])
    return _mix_ffn(x, ada3, a_n, u, gates, woa, woc, b_o_conv[l], wout, norm2_g[l],
                    w1, w2, final_g)
```

```python
import math

import jax
import jax.numpy as jnp
from jax import lax
from jax.experimental import pallas as pl
from jax.experimental.pallas import tpu as pltpu

D_MODEL = 1024
N_HEADS_A = 4
HEAD_QK = 64
HEAD_V = 2 * HEAD_QK
ATTN_WIDTH = N_HEADS_A * HEAD_V
CONV_WIDTH = 512
CONV_KERNEL = 31
QKV_COLS = 3 * ATTN_WIDTH
GLU_COLS = 2 * CONV_WIDTH
GATE_COLS = 2 * D_MODEL
IN_COLS = QKV_COLS + GLU_COLS + GATE_COLS
D_FF = 2816
N_BUCKETS = 32
MAX_EXACT = 16
MAX_DISTANCE = 128
EPS = 1e-6
NEG_INF = -1e30
LAM_INIT = 0.8 - 0.6 * math.exp(-0.3 * 0)

V7X_VMEM_BYTES = 64 * 1024 * 1024
VMEM_LIMIT = V7X_VMEM_BYTES - 12 * 1024 * 1024

ROW_TILE = 512
ATTN_TILE = 256
ATTN_LAG = 4
ONES_ROWS = 16
CAST_ROWS = 16
LOG2E = math.log2(math.e)
Q_SCALE = HEAD_QK ** -0.5 * LOG2E
CONV_HALO = 32
CONV_CHUNK = 16
PROJ_CHUNK = 512
FF_CHUNK = 256

BF16 = jnp.bfloat16
F32 = jnp.float32


def _sigmoid(x):
    return 1.0 / (1.0 + jnp.exp2(x * (-LOG2E)))


def _silu(x):
    return x * _sigmoid(x)


def _params(sem):
    return pltpu.CompilerParams(dimension_semantics=sem, vmem_limit_bytes=VMEM_LIMIT)


def _resident(shape):
    nd = len(shape)
    return pl.BlockSpec(shape, lambda *_: (0,) * nd, pipeline_mode=pl.Buffered(1))


def _bias_tile(rel_ref, h, t):
    T = ATTN_TILE
    key = lax.broadcasted_iota(jnp.int32, (T, T), 0)
    qry = lax.broadcasted_iota(jnp.int32, (T, T), 1)
    dist = (1 - t) * T + qry - key
    n = jnp.maximum(dist, 0)
    large = MAX_EXACT + (jnp.log(jnp.maximum(n, 1).astype(F32) / MAX_EXACT)
                         / math.log(MAX_DISTANCE / MAX_EXACT)
                         * (N_BUCKETS - MAX_EXACT)).astype(jnp.int32)
    large = jnp.minimum(large, N_BUCKETS - 1)
    bucket = jnp.where(n < MAX_EXACT, n, large)
    far = rel_ref[N_BUCKETS - 1, h]
    bias = jnp.zeros((T, T), F32)
    for b in range(N_BUCKETS - 1):
        bias = jnp.where(bucket == b, (rel_ref[b, h] - far) * LOG2E, bias)
    return jnp.where(dist >= 0, bias, NEG_INF)


def _prologue_kernel(rel_ref, c_ref, w_ref, b_ref, win_ref, o_ref, win_bf_ref, bias_ref):
    j = pl.program_id(0)
    ca = _silu(c_ref[...])
    o_ref[...] = jnp.dot(ca.astype(BF16), w_ref[...].astype(BF16),
                         preferred_element_type=F32) + b_ref[...]
    win_bf_ref[...] = win_ref[...].astype(BF16)
    bias_ref[...] = _bias_tile(rel_ref, j // 2, j % 2)


def _prologue(c, w, b, w_in, rel_bias):
    bsz, d = c.shape
    n = w.shape[-1]
    steps = 2 * N_HEADS_A
    tn = n // steps
    _, rows, cols = w_in.shape
    tr = rows // steps
    T = ATTN_TILE
    return pl.pallas_call(
        _prologue_kernel,
        grid=(steps,),
        in_specs=[pl.BlockSpec(memory_space=pltpu.SMEM),
                  pl.BlockSpec((bsz, d), lambda j: (0, 0)),
                  pl.BlockSpec((None, d, tn), lambda j: (0, 0, j)),
                  pl.BlockSpec((1, tn), lambda j: (0, j)),
                  pl.BlockSpec((None, tr, cols), lambda j: (0, j, 0))],
        out_specs=[pl.BlockSpec((bsz, tn), lambda j: (0, j)),
                   pl.BlockSpec((tr, cols), lambda j: (j, 0)),
                   pl.BlockSpec((None, None, T, T), lambda j: (j // 2, j % 2, 0, 0))],
        out_shape=[jax.ShapeDtypeStruct((bsz, n), F32),
                   jax.ShapeDtypeStruct((rows, cols), BF16),
                   jax.ShapeDtypeStruct((N_HEADS_A, 2, T, T), F32)],
        compiler_params=_params(("arbitrary",)),
        name="prologue",
    )(rel_bias, c, w, b, w_in)


def _conv_chunk(r0, ubuf, w_ref, cb_ref, lng_ref, lnb_ref, o_ref):
    off = CONV_HALO - (CONV_KERNEL - 1)
    acc = jnp.zeros((CONV_CHUNK // 8, 8, CONV_WIDTH), F32)
    for k in range(CONV_KERNEL):
        a, b = divmod(k + off, 8)
        rows = ubuf[b, r0 + 8 * a:r0 + 8 * a + CONV_CHUNK, :]
        acc = acc + rows.reshape(acc.shape) * w_ref[k][None]
    u = acc.reshape(CONV_CHUNK, CONV_WIDTH) + cb_ref[...]
    mu = jnp.mean(u, axis=-1, keepdims=True)
    d = u - mu
    var = jnp.mean(d * d, axis=-1, keepdims=True)
    y = d * lax.rsqrt(var + EPS) * lng_ref[...] + lnb_ref[...]
    o_ref[r0:r0 + CONV_CHUNK, :] = _silu(y).astype(BF16)


def _inproj_kernel(x_ref, ada_ref, g_ref, w_ref, cw_ref, cb_ref, lng_ref, lnb_ref,
                   qkv_ref, u_ref, gate_ref, ubuf):
    TS, H, C = ROW_TILE, CONV_HALO, CONV_WIDTH
    i = pl.program_id(1)

    @pl.when(i == 0)
    def _():
        ubuf[0, 0:H, :] = jnp.zeros((H, C), F32)

    @pl.when(i > 0)
    def _():
        ubuf[0, 0:H, :] = ubuf[0, TS:TS + H, :]

    x = x_ref[...]
    ms = jnp.mean(x * x, axis=-1, keepdims=True)
    y = x * lax.rsqrt(ms + EPS)
    hb = ((y.astype(BF16) * g_ref[...].astype(BF16)) * (1.0 + ada_ref[1:2, :]).astype(BF16)
          + ada_ref[0:1, :].astype(BF16))

    def proj(base):
        return jnp.dot(hb, w_ref[:, base:base + PROJ_CHUNK], preferred_element_type=F32)

    ubuf[0, H:H + TS, :] = proj(QKV_COLS) * _sigmoid(proj(QKV_COLS + C))
    n = TS + H - 8
    for b in range(1, 8):
        ubuf[b, 0:n, :] = ubuf[0, b:b + n, :]
    jobs = ([(qkv_ref, c, c) for c in range(0, QKV_COLS, PROJ_CHUNK)]
            + [(gate_ref, c, QKV_COLS + GLU_COLS + c) for c in range(0, GATE_COLS, PROJ_CHUNK)])
    rows = list(range(0, TS, CONV_CHUNK))
    for n, (ref, c, base) in enumerate(jobs):
        r = proj(base)
        if base < ATTN_WIDTH:
            r = r * Q_SCALE
        ref[:, c:c + PROJ_CHUNK] = r.astype(BF16)
        for r0 in rows[n * len(rows) // len(jobs):(n + 1) * len(rows) // len(jobs)]:
            _conv_chunk(r0, ubuf, cw_ref, cb_ref, lng_ref, lnb_ref, u_ref)


def _in_proj(x, ada3, g, w, cw, cb, lng, lnb):
    bsz, s, d = x.shape
    tm = ROW_TILE
    C = CONV_WIDTH
    row = lambda n: pl.BlockSpec((None, tm, n), lambda b, i: (b, i, 0))
    vec = lambda a: a.reshape(1, C)
    return pl.pallas_call(
        _inproj_kernel,
        grid=(bsz, s // tm),
        in_specs=[row(d),
                  pl.BlockSpec((None, 6, d), lambda b, i: (b, 0, 0)),
                  _resident((1, d)),
                  _resident((d, IN_COLS)),
                  _resident((CONV_KERNEL, 8, C)),
                  _resident((1, C)), _resident((1, C)), _resident((1, C))],
        out_specs=[row(QKV_COLS), row(C), row(GATE_COLS)],
        out_shape=[jax.ShapeDtypeStruct((bsz, s, QKV_COLS), BF16),
                   jax.ShapeDtypeStruct((bsz, s, C), BF16),
                   jax.ShapeDtypeStruct((bsz, s, GATE_COLS), BF16)],
        scratch_shapes=[pltpu.VMEM((8, tm + CONV_HALO, C), F32)],
        compiler_params=_params(("arbitrary", "arbitrary")),
        name="in_proj",
    )(x, ada3, g.reshape(1, d), w, jnp.broadcast_to(cw[:, None, :], (CONV_KERNEL, 8, C)),
      vec(cb), vec(lng), vec(lnb))


def _attn_kernel(lamv_ref, subg_ref, q_ref, k_ref, v_ref, bias_ref, *rest):
    n_cast = (len(rest) - 5) // 2
    cast_in, o_ref, cast_out = rest[:n_cast], rest[n_cast], rest[n_cast + 1:2 * n_cast + 1]
    vt_ref, qz_ref, m_ref, a_ref = rest[2 * n_cast + 1:]
    T = ATTN_TILE
    NS = 2 * N_HEADS_A
    n_q = q_ref.shape[0] // T
    hcols = lambda h: slice(h * HEAD_V, (h + 1) * HEAD_V)
    rows = lambda t: slice(t * T, (t + 1) * T)

    vt = v_ref[...].T
    for h in range(N_HEADS_A):
        vt_ref[h, 0:HEAD_V, :] = vt[hcols(h), :]
        vt_ref[h, HEAD_V:, :] = jnp.ones((ONES_ROWS, vt.shape[1]), BF16)

    lv = lamv_ref[...]
    lam = (jnp.exp(jnp.sum(lv[0:1] * lv[1:2], axis=-1, keepdims=True))
           - jnp.exp(jnp.sum(lv[2:3] * lv[3:4], axis=-1, keepdims=True)) + LAM_INIT)
    lane = lax.broadcasted_iota(jnp.int32, (T, HEAD_V), 1)

    def prepare(i):
        slot = i % 2
        for h in range(N_HEADS_A):
            qs = q_ref[rows(i), hcols(h)]
            zero = jnp.zeros_like(qs)
            qz_ref[slot, 2 * h] = jnp.where(lane < HEAD_QK, qs, zero)
            qz_ref[slot, 2 * h + 1] = jnp.where(lane >= HEAD_QK, qs, zero)

    def finalize(i):
        slot = i % 2
        for h in range(N_HEADS_A):
            a1, a2 = a_ref[slot, 2 * h], a_ref[slot, 2 * h + 1]
            o = (a1[:HEAD_V] / a1[HEAD_V:HEAD_V + 1]
                 - lam * (a2[:HEAD_V] / a2[HEAD_V:HEAD_V + 1]))
            ms = jnp.mean(o * o, axis=0, keepdims=True)
            y = (o * lax.rsqrt(ms + EPS)).T
            o_ref[rows(i), hcols(h)] = ((y * subg_ref[...]) * (1.0 - LAM_INIT)).astype(BF16)

    units = [(i, j, n) for i in range(n_q) for j in range(i + 1) for n in range(NS)]
    scores, probs, alphas = {}, {}, {}

    def score(u):
        i, j, n = units[u]
        kc = k_ref[rows(j), hcols(n // 2)]
        scores[u] = lax.dot_general(kc, qz_ref[i % 2, n], (((1,), (1,)), ((), ())),
                                    preferred_element_type=F32)

    def softmax(u):
        i, j, n = units[u]
        s = scores.pop(u)
        h, near = n // 2, MAX_DISTANCE
        on_diag = j == i
        if j == i - 1:
            corner = s[near:, :near] + bias_ref[h, 0, near:, :near]
            s = jnp.concatenate(
                [s[:near], jnp.concatenate([corner, s[near:, near:]], axis=1)], axis=0)
        if on_diag:
            s_lo = s[:near, :near] + bias_ref[h, 1, :near, :near]
            s_hi = s[:, near:] + bias_ref[h, 1, :, near:]
            m_new = jnp.concatenate([jnp.max(s_lo, axis=0, keepdims=True),
                                     jnp.max(s_hi, axis=0, keepdims=True)], axis=1)
        else:
            m_new = jnp.max(s, axis=0, keepdims=True)
        if j > 0:
            m_old = m_ref[i % 2, n]
            m_new = jnp.maximum(m_old, m_new)
            alphas[u] = jnp.exp2(m_old - m_new)
        if on_diag:
            p_lo = jnp.exp2(s_lo - m_new[:, :near]).astype(BF16)
            p_hi = jnp.exp2(s_hi - m_new[:, near:]).astype(BF16)
            p_lo = jnp.concatenate([p_lo, jnp.zeros((T - near, near), BF16)], axis=0)
            probs[u] = jnp.concatenate([p_lo, p_hi], axis=1)
        else:
            probs[u] = jnp.exp2(s - m_new).astype(BF16)
        m_ref[i % 2, n] = m_new

    def value(u):
        i, j, n = units[u]
        vc = vt_ref[n // 2, :, rows(j)]
        pv = jnp.dot(vc, probs.pop(u), preferred_element_type=F32)
        a_ref[i % 2, n] = pv if j == 0 else pv + alphas.pop(u) * a_ref[i % 2, n]

    for step in range(len(units) + 2 * ATTN_LAG):
        if step < len(units):
            if units[step][1:] == (0, 0):
                prepare(units[step][0])
            score(step)
        if 0 <= step - ATTN_LAG < len(units):
            softmax(step - ATTN_LAG)
        done = step - 2 * ATTN_LAG
        if 0 <= done < len(units):
            value(done)
            i, j, n = units[done]
            if (j, n) == (i, NS - 1):
                finalize(i)

    for w_ref, wb_ref in zip(cast_in, cast_out):
        wb_ref[...] = w_ref[...].astype(BF16)


def _attention(qkv, bias_tiles, lamv, subg, weights):
    bsz, s, _ = qkv.shape
    T = ATTN_TILE
    W = ATTN_WIDTH
    NS = 2 * N_HEADS_A
    steps = bsz

    def slab_spec(w, squeeze):
        rows, cols = w.shape[-2:]
        assert rows % (steps * CAST_ROWS) == 0
        block = (rows // steps, cols)
        if squeeze:
            return pl.BlockSpec((None,) + block, lambda b: (0, b, 0))
        return pl.BlockSpec(block, lambda b: (b, 0))

    seq = lambda col: pl.BlockSpec((None, s, W), lambda b: (b, 0, col))
    out = pl.pallas_call(
        _attn_kernel,
        grid=(bsz,),
        in_specs=[_resident((4, HEAD_QK)),
                  _resident((1, HEAD_V)),
                  seq(0), seq(1), seq(2),
                  _resident((N_HEADS_A, 2, T, T))] + [slab_spec(w, True) for w in weights],
        out_specs=[seq(0)] + [slab_spec(w, False) for w in weights],
        out_shape=[jax.ShapeDtypeStruct((bsz, s, W), BF16)]
        + [jax.ShapeDtypeStruct(w.shape[-2:], BF16) for w in weights],
        scratch_shapes=[pltpu.VMEM((N_HEADS_A, HEAD_V + ONES_ROWS, s), BF16),
                        pltpu.VMEM((2, NS, T, HEAD_V), BF16),
                        pltpu.VMEM((2, NS, 1, T), F32),
                        pltpu.VMEM((2, NS, HEAD_V + ONES_ROWS, T), F32)],
        compiler_params=_params(("arbitrary",)),
        name="diff_attn",
    )(lamv, subg.reshape(1, HEAD_V), qkv, qkv, qkv, bias_tiles, *weights)
    return out[0], out[1:]


def _mixffn_kernel(x_ref, ada_ref, a_ref, u_ref, gate_ref, woa_ref, woc_ref, boc_ref,
                   wout_ref, g2_ref, w1_ref, w2_ref, gf_ref, o_ref,
                   x1_ref, h2_ref, act_ref):
    D = D_MODEL
    a = jnp.dot(a_ref[...], woa_ref[...], preferred_element_type=F32)
    cv = jnp.dot(u_ref[...], woc_ref[...], preferred_element_type=F32) + boc_ref[...]
    gate = gate_ref[...]
    y = (_sigmoid(gate[:, :D].astype(F32)) * a
         + _sigmoid(gate[:, D:].astype(F32)) * cv)
    z = jnp.dot(y.astype(BF16), wout_ref[...], preferred_element_type=F32)
    x1 = x_ref[...] + ada_ref[2:3, :] * z
    x1_ref[...] = x1
    ms = jnp.mean(x1 * x1, axis=-1, keepdims=True)
    yn = x1 * lax.rsqrt(ms + EPS)
    h2_ref[...] = ((yn * g2_ref[...]) * (1.0 + ada_ref[4:5, :]) + ada_ref[3:4, :]).astype(BF16)

    h = h2_ref[...]
    for c in range(0, D_FF, FF_CHUNK):
        fg = jnp.dot(h, w1_ref[:, c:c + FF_CHUNK], preferred_element_type=F32)
        fu = jnp.dot(h, w1_ref[:, D_FF + c:D_FF + c + FF_CHUNK],
                     preferred_element_type=F32)
        act_ref[:, c:c + FF_CHUNK] = (_silu(fg) * fu).astype(BF16)
    z2 = jnp.dot(act_ref[...], w2_ref[...], preferred_element_type=F32)
    x2 = x1_ref[...] + ada_ref[5:6, :] * z2
    ms2 = jnp.mean(x2 * x2, axis=-1, keepdims=True)
    o_ref[...] = (x2 * lax.rsqrt(ms2 + EPS)) * gf_ref[...]


def _mix_ffn(x, ada3, a_n, u, gates, woa, woc, boc, wout, g2, w1, w2, gf):
    bsz, s, d = x.shape
    tm = ROW_TILE
    row = lambda n: pl.BlockSpec((None, tm, n), lambda b, i: (b, i, 0))
    return pl.pallas_call(
        _mixffn_kernel,
        grid=(bsz, s // tm),
        in_specs=[row(d),
                  pl.BlockSpec((None, 6, d), lambda b, i: (b, 0, 0)),
                  row(ATTN_WIDTH), row(CONV_WIDTH), row(GATE_COLS),
                  _resident((ATTN_WIDTH, d)), _resident((CONV_WIDTH, d)),
                  _resident((1, d)), _resident((d, d)), _resident((1, d)),
                  _resident((d, 2 * D_FF)), _resident((D_FF, d)), _resident((1, d))],
        out_specs=row(d),
        out_shape=jax.ShapeDtypeStruct((bsz, s, d), F32),
        scratch_shapes=[pltpu.VMEM((tm, d), F32), pltpu.VMEM((tm, d), BF16),
                        pltpu.VMEM((tm, D_FF), BF16)],
        compiler_params=_params(("parallel", "arbitrary")),
        name="mix_ffn",
    )(x, ada3, a_n, u, gates, woa, woc, boc.reshape(1, d), wout, g2.reshape(1, d),
      w1, w2, gf.reshape(1, d))


def kernel(x, c, w_ada, b_ada, norm1_g, norm2_g, final_g, w_in, lambda_q1, lambda_k1,
           lambda_q2, lambda_k2, rel_bias, attn_sub_g, w_o_attn, conv_w, conv_b,
           conv_ln_g, conv_ln_b, w_o_conv, b_o_conv, w_out, w_ffn_in, w_ffn_out):
    bsz, s, d = x.shape
    assert w_ada.shape[0] == 1, "single-layer block"
    assert (d, s % ROW_TILE, s % ATTN_TILE) == (D_MODEL, 0, 0)
    assert ATTN_TILE == 2 * MAX_DISTANCE
    assert PROJ_CHUNK == CONV_WIDTH and ATTN_WIDTH % PROJ_CHUNK == 0
    l = 0
    ada, w_in_bf, bias_tiles = _prologue(c, w_ada, b_ada, w_in, rel_bias)
    ada3 = ada.reshape(bsz, 6, d)
    qkv, u, gates = _in_proj(x, ada3, norm1_g[l], w_in_bf, conv_w[l],
                             conv_b[l], conv_ln_g[l], conv_ln_b[l])
    lamv = jnp.stack([lambda_q1[l], lambda_k1[l], lambda_q2[l], lambda_k2[l]])
    a_n, (woa, woc, wout, w1, w2) = _attention(
        qkv, bias_tiles, lamv, attn_sub_g[l],
        [w_o_attn, w_o_conv, w_out, w_ffn_in, w_ffn_out])
    return _mix_ffn(x, ada3, a_n, u, gates, woa, woc, b_o_conv[l], wout, norm2_g[l],
                    w1, w2, final_g)
```

```python
import math

import jax
import jax.numpy as jnp
from jax import lax
from jax.experimental import pallas as pl
from jax.experimental.pallas import tpu as pltpu

D_MODEL = 1024
N_HEADS_A = 4
HEAD_QK = 64
HEAD_V = 2 * HEAD_QK
ATTN_WIDTH = N_HEADS_A * HEAD_V
CONV_WIDTH = 512
CONV_KERNEL = 31
QKV_COLS = 3 * ATTN_WIDTH
GLU_COLS = 2 * CONV_WIDTH
GATE_COLS = 2 * D_MODEL
IN_COLS = QKV_COLS + GLU_COLS + GATE_COLS
D_FF = 2816
N_BUCKETS = 32
MAX_EXACT = 16
MAX_DISTANCE = 128
EPS = 1e-6
NEG_INF = -1e30
LAM_INIT = 0.8 - 0.6 * math.exp(-0.3 * 0)

V7X_VMEM_BYTES = 64 * 1024 * 1024
VMEM_LIMIT = V7X_VMEM_BYTES - 8 * 1024 * 1024

ROW_TILE = 512
ATTN_TILE = 256
ATTN_LAG = 4
ONES_ROWS = 16
CAST_ROWS = 16
LOG2E = math.log2(math.e)
Q_SCALE = HEAD_QK ** -0.5 * LOG2E
CONV_HALO = 32
CONV_CHUNK = 16
PROJ_CHUNK = 512
FF_CHUNK = 256

BF16 = jnp.bfloat16
F32 = jnp.float32


def _sigmoid(x):
    return 1.0 / (1.0 + jnp.exp2(x * (-LOG2E)))


def _silu(x):
    return x * _sigmoid(x)


def _params(sem):
    return pltpu.CompilerParams(dimension_semantics=sem, vmem_limit_bytes=VMEM_LIMIT)


def _resident(shape):
    nd = len(shape)
    return pl.BlockSpec(shape, lambda *_: (0,) * nd, pipeline_mode=pl.Buffered(1))


def _bias_tile(rel_ref, h, t):
    T = ATTN_TILE
    key = lax.broadcasted_iota(jnp.int32, (T, T), 0)
    qry = lax.broadcasted_iota(jnp.int32, (T, T), 1)
    dist = (1 - t) * T + qry - key
    n = jnp.maximum(dist, 0)
    large = MAX_EXACT + (jnp.log(jnp.maximum(n, 1).astype(F32) / MAX_EXACT)
                         / math.log(MAX_DISTANCE / MAX_EXACT)
                         * (N_BUCKETS - MAX_EXACT)).astype(jnp.int32)
    large = jnp.minimum(large, N_BUCKETS - 1)
    bucket = jnp.where(n < MAX_EXACT, n, large)
    far = rel_ref[N_BUCKETS - 1, h]
    bias = jnp.zeros((T, T), F32)
    for b in range(N_BUCKETS - 1):
        bias = jnp.where(bucket == b, (rel_ref[b, h] - far) * LOG2E, bias)
    return jnp.where(dist >= 0, bias, NEG_INF)


def _prologue_kernel(rel_ref, c_ref, w_ref, b_ref, win_ref, o_ref, win_bf_ref, bias_ref):
    j = pl.program_id(0)
    ca = _silu(c_ref[...])
    o_ref[...] = jnp.dot(ca.astype(BF16), w_ref[...].astype(BF16),
                         preferred_element_type=F32) + b_ref[...]
    win_bf_ref[...] = win_ref[...].astype(BF16)
    bias_ref[...] = _bias_tile(rel_ref, j // 2, j % 2)


def _prologue(c, w, b, w_in, rel_bias):
    bsz, d = c.shape
    n = w.shape[-1]
    steps = 2 * N_HEADS_A
    tn = n // steps
    _, rows, cols = w_in.shape
    tr = rows // steps
    T = ATTN_TILE
    return pl.pallas_call(
        _prologue_kernel,
        grid=(steps,),
        in_specs=[pl.BlockSpec(memory_space=pltpu.SMEM),
                  pl.BlockSpec((bsz, d), lambda j: (0, 0)),
                  pl.BlockSpec((None, d, tn), lambda j: (0, 0, j)),
                  pl.BlockSpec((1, tn), lambda j: (0, j)),
                  pl.BlockSpec((None, tr, cols), lambda j: (0, j, 0))],
        out_specs=[pl.BlockSpec((bsz, tn), lambda j: (0, j)),
                   pl.BlockSpec((tr, cols), lambda j: (j, 0)),
                   pl.BlockSpec((None, None, T, T), lambda j: (j // 2, j % 2, 0, 0))],
        out_shape=[jax.ShapeDtypeStruct((bsz, n), F32),
                   jax.ShapeDtypeStruct((rows, cols), BF16),
                   jax.ShapeDtypeStruct((N_HEADS_A, 2, T, T), F32)],
        compiler_params=_params(("arbitrary",)),
        name="prologue",
    )(rel_bias, c, w, b, w_in)


def _conv_chunk(r0, ubuf, w_ref, cb_ref, lng_ref, lnb_ref, o_ref):
    off = CONV_HALO - (CONV_KERNEL - 1)
    acc = jnp.zeros((CONV_CHUNK // 8, 8, CONV_WIDTH), F32)
    for k in range(CONV_KERNEL):
        a, b = divmod(k + off, 8)
        rows = ubuf[b, r0 + 8 * a:r0 + 8 * a + CONV_CHUNK, :]
        acc = acc + rows.reshape(acc.shape) * w_ref[k][None]
    u = acc.reshape(CONV_CHUNK, CONV_WIDTH) + cb_ref[...]
    mu = jnp.mean(u, axis=-1, keepdims=True)
    d = u - mu
    var = jnp.mean(d * d, axis=-1, keepdims=True)
    y = d * lax.rsqrt(var + EPS) * lng_ref[...] + lnb_ref[...]
    o_ref[r0:r0 + CONV_CHUNK, :] = _silu(y).astype(BF16)


def _inproj_kernel(x_ref, ada_ref, g_ref, w_ref, cw_ref, cb_ref, lng_ref, lnb_ref,
                   qkv_ref, u_ref, gate_ref, ubuf):
    TS, H, C = ROW_TILE, CONV_HALO, CONV_WIDTH
    i = pl.program_id(1)

    @pl.when(i == 0)
    def _():
        ubuf[0, 0:H, :] = jnp.zeros((H, C), F32)

    @pl.when(i > 0)
    def _():
        ubuf[0, 0:H, :] = ubuf[0, TS:TS + H, :]

    x = x_ref[...]
    ms = jnp.mean(x * x, axis=-1, keepdims=True)
    y = x * lax.rsqrt(ms + EPS)
    h = (y * g_ref[...]) * (1.0 + ada_ref[1:2, :]) + ada_ref[0:1, :]
    hb = h.astype(BF16)

    def proj(base):
        return jnp.dot(hb, w_ref[:, base:base + PROJ_CHUNK], preferred_element_type=F32)

    ubuf[0, H:H + TS, :] = proj(QKV_COLS) * _sigmoid(proj(QKV_COLS + C))
    n = TS + H - 8
    for b in range(1, 8):
        ubuf[b, 0:n, :] = ubuf[0, b:b + n, :]
    jobs = ([(qkv_ref, c, c) for c in range(0, QKV_COLS, PROJ_CHUNK)]
            + [(gate_ref, c, QKV_COLS + GLU_COLS + c) for c in range(0, GATE_COLS, PROJ_CHUNK)])
    rows = list(range(0, TS, CONV_CHUNK))
    for n, (ref, c, base) in enumerate(jobs):
        r = proj(base)
        if base < ATTN_WIDTH:
            r = r * Q_SCALE
        ref[:, c:c + PROJ_CHUNK] = r.astype(BF16)
        for r0 in rows[n * len(rows) // len(jobs):(n + 1) * len(rows) // len(jobs)]:
            _conv_chunk(r0, ubuf, cw_ref, cb_ref, lng_ref, lnb_ref, u_ref)


def _in_proj(x, ada3, g, w, cw, cb, lng, lnb):
    bsz, s, d = x.shape
    tm = ROW_TILE
    C = CONV_WIDTH
    row = lambda n: pl.BlockSpec((None, tm, n), lambda b, i: (b, i, 0))
    vec = lambda a: a.reshape(1, C)
    return pl.pallas_call(
        _inproj_kernel,
        grid=(bsz, s // tm),
        in_specs=[row(d),
                  pl.BlockSpec((None, 6, d), lambda b, i: (b, 0, 0)),
                  _resident((1, d)),
                  _resident((d, IN_COLS)),
                  _resident((CONV_KERNEL, 8, C)),
                  _resident((1, C)), _resident((1, C)), _resident((1, C))],
        out_specs=[row(QKV_COLS), row(C), row(GATE_COLS)],
        out_shape=[jax.ShapeDtypeStruct((bsz, s, QKV_COLS), BF16),
                   jax.ShapeDtypeStruct((bsz, s, C), BF16),
                   jax.ShapeDtypeStruct((bsz, s, GATE_COLS), BF16)],
        scratch_shapes=[pltpu.VMEM((8, tm + CONV_HALO, C), F32)],
        compiler_params=_params(("arbitrary", "arbitrary")),
        name="in_proj",
    )(x, ada3, g.reshape(1, d), w, jnp.broadcast_to(cw[:, None, :], (CONV_KERNEL, 8, C)),
      vec(cb), vec(lng), vec(lnb))


def _attn_kernel(lamv_ref, subg_ref, q_ref, k_ref, v_ref, bias_ref, *rest):
    n_cast = (len(rest) - 5) // 2
    cast_in, o_ref, cast_out = rest[:n_cast], rest[n_cast], rest[n_cast + 1:2 * n_cast + 1]
    vt_ref, qz_ref, m_ref, a_ref = rest[2 * n_cast + 1:]
    T = ATTN_TILE
    NS = 2 * N_HEADS_A
    n_q = q_ref.shape[0] // T
    hcols = lambda h: slice(h * HEAD_V, (h + 1) * HEAD_V)
    rows = lambda t: slice(t * T, (t + 1) * T)

    vt = v_ref[...].T
    for h in range(N_HEADS_A):
        vt_ref[h, 0:HEAD_V, :] = vt[hcols(h), :]
        vt_ref[h, HEAD_V:, :] = jnp.ones((ONES_ROWS, vt.shape[1]), BF16)

    lv = lamv_ref[...]
    lam = (jnp.exp(jnp.sum(lv[0:1] * lv[1:2], axis=-1, keepdims=True))
           - jnp.exp(jnp.sum(lv[2:3] * lv[3:4], axis=-1, keepdims=True)) + LAM_INIT)
    lane = lax.broadcasted_iota(jnp.int32, (T, HEAD_V), 1)

    def prepare(i):
        slot = i % 2
        for h in range(N_HEADS_A):
            qs = q_ref[rows(i), hcols(h)]
            zero = jnp.zeros_like(qs)
            qz_ref[slot, 2 * h] = jnp.where(lane < HEAD_QK, qs, zero)
            qz_ref[slot, 2 * h + 1] = jnp.where(lane >= HEAD_QK, qs, zero)

    def finalize(i):
        slot = i % 2
        for h in range(N_HEADS_A):
            a1, a2 = a_ref[slot, 2 * h], a_ref[slot, 2 * h + 1]
            o = (a1[:HEAD_V] / a1[HEAD_V:HEAD_V + 1]
                 - lam * (a2[:HEAD_V] / a2[HEAD_V:HEAD_V + 1]))
            ms = jnp.mean(o * o, axis=0, keepdims=True)
            y = (o * lax.rsqrt(ms + EPS)).T
            o_ref[rows(i), hcols(h)] = ((y * subg_ref[...]) * (1.0 - LAM_INIT)).astype(BF16)

    units = [(i, j, n) for i in range(n_q) for j in range(i + 1) for n in range(NS)]
    scores, probs, alphas = {}, {}, {}

    def score(u):
        i, j, n = units[u]
        kc = k_ref[rows(j), hcols(n // 2)]
        scores[u] = lax.dot_general(kc, qz_ref[i % 2, n], (((1,), (1,)), ((), ())),
                                    preferred_element_type=F32)

    def softmax(u):
        i, j, n = units[u]
        s = scores.pop(u)
        h, near = n // 2, MAX_DISTANCE
        on_diag = j == i
        if j == i - 1:
            corner = s[near:, :near] + bias_ref[h, 0, near:, :near]
            s = jnp.concatenate(
                [s[:near], jnp.concatenate([corner, s[near:, near:]], axis=1)], axis=0)
        if on_diag:
            s_lo = s[:near, :near] + bias_ref[h, 1, :near, :near]
            s_hi = s[:, near:] + bias_ref[h, 1, :, near:]
            m_new = jnp.concatenate([jnp.max(s_lo, axis=0, keepdims=True),
                                     jnp.max(s_hi, axis=0, keepdims=True)], axis=1)
        else:
            m_new = jnp.max(s, axis=0, keepdims=True)
        if j > 0:
            m_old = m_ref[i % 2, n]
            m_new = jnp.maximum(m_old, m_new)
            alphas[u] = jnp.exp2(m_old - m_new)
        if on_diag:
            p_lo = jnp.exp2(s_lo - m_new[:, :near]).astype(BF16)
            p_hi = jnp.exp2(s_hi - m_new[:, near:]).astype(BF16)
            p_lo = jnp.concatenate([p_lo, jnp.zeros((T - near, near), BF16)], axis=0)
            probs[u] = jnp.concatenate([p_lo, p_hi], axis=1)
        else:
            probs[u] = jnp.exp2(s - m_new).astype(BF16)
        m_ref[i % 2, n] = m_new

    def value(u):
        i, j, n = units[u]
        vc = vt_ref[n // 2, :, rows(j)]
        pv = jnp.dot(vc, probs.pop(u), preferred_element_type=F32)
        a_ref[i % 2, n] = pv if j == 0 else pv + alphas.pop(u) * a_ref[i % 2, n]

    for step in range(len(units) + 2 * ATTN_LAG):
        if step < len(units):
            if units[step][1:] == (0, 0):
                prepare(units[step][0])
            score(step)
        if 0 <= step - ATTN_LAG < len(units):
            softmax(step - ATTN_LAG)
        done = step - 2 * ATTN_LAG
        if 0 <= done < len(units):
            value(done)
            i, j, n = units[done]
            if (j, n) == (i, NS - 1):
                finalize(i)

    for w_ref, wb_ref in zip(cast_in, cast_out):
        wb_ref[...] = w_ref[...].astype(BF16)


def _attention(qkv, bias_tiles, lamv, subg, weights):
    bsz, s, _ = qkv.shape
    T = ATTN_TILE
    W = ATTN_WIDTH
    NS = 2 * N_HEADS_A
    steps = bsz

    def slab_spec(w, squeeze):
        rows, cols = w.shape[-2:]
        assert rows % (steps * CAST_ROWS) == 0
        block = (rows // steps, cols)
        if squeeze:
            return pl.BlockSpec((None,) + block, lambda b: (0, b, 0))
        return pl.BlockSpec(block, lambda b: (b, 0))

    seq = lambda col: pl.BlockSpec((None, s, W), lambda b: (b, 0, col))
    out = pl.pallas_call(
        _attn_kernel,
        grid=(bsz,),
        in_specs=[_resident((4, HEAD_QK)),
                  _resident((1, HEAD_V)),
                  seq(0), seq(1), seq(2),
                  _resident((N_HEADS_A, 2, T, T))] + [slab_spec(w, True) for w in weights],
        out_specs=[seq(0)] + [slab_spec(w, False) for w in weights],
        out_shape=[jax.ShapeDtypeStruct((bsz, s, W), BF16)]
        + [jax.ShapeDtypeStruct(w.shape[-2:], BF16) for w in weights],
        scratch_shapes=[pltpu.VMEM((N_HEADS_A, HEAD_V + ONES_ROWS, s), BF16),
                        pltpu.VMEM((2, NS, T, HEAD_V), BF16),
                        pltpu.VMEM((2, NS, 1, T), F32),
                        pltpu.VMEM((2, NS, HEAD_V + ONES_ROWS, T), F32)],
        compiler_params=_params(("arbitrary",)),
        name="diff_attn",
    )(lamv, subg.reshape(1, HEAD_V), qkv, qkv, qkv, bias_tiles, *weights)
    return out[0], out[1:]


def _mixffn_kernel(x_ref, ada_ref, a_ref, u_ref, gate_ref, woa_ref, woc_ref, boc_ref,
                   wout_ref, g2_ref, w1_ref, w2_ref, gf_ref, o_ref,
                   x1_ref, h2_ref, act_ref):
    D = D_MODEL
    a = jnp.dot(a_ref[...], woa_ref[...], preferred_element_type=F32)
    cv = jnp.dot(u_ref[...], woc_ref[...], preferred_element_type=F32) + boc_ref[...]
    gate = gate_ref[...]
    y = (_sigmoid(gate[:, :D].astype(F32)) * a
         + _sigmoid(gate[:, D:].astype(F32)) * cv)
    z = jnp.dot(y.astype(BF16), wout_ref[...], preferred_element_type=F32)
    x1 = x_ref[...] + ada_ref[2:3, :] * z
    x1_ref[...] = x1
    ms = jnp.mean(x1 * x1, axis=-1, keepdims=True)
    yn = x1 * lax.rsqrt(ms + EPS)
    h2_ref[...] = ((yn * g2_ref[...]) * (1.0 + ada_ref[4:5, :]) + ada_ref[3:4, :]).astype(BF16)

    h = h2_ref[...]
    for c in range(0, D_FF, FF_CHUNK):
        fg = jnp.dot(h, w1_ref[:, c:c + FF_CHUNK], preferred_element_type=F32)
        fu = jnp.dot(h, w1_ref[:, D_FF + c:D_FF + c + FF_CHUNK],
                     preferred_element_type=F32)
        act_ref[:, c:c + FF_CHUNK] = (_silu(fg) * fu).astype(BF16)
    z2 = jnp.dot(act_ref[...], w2_ref[...], preferred_element_type=F32)
    x2 = x1_ref[...] + ada_ref[5:6, :] * z2
    ms2 = jnp.mean(x2 * x2, axis=-1, keepdims=True)
    o_ref[...] = (x2 * lax.rsqrt(ms2 + EPS)) * gf_ref[...]


def _mix_ffn(x, ada3, a_n, u, gates, woa, woc, boc, wout, g2, w1, w2, gf):
    bsz, s, d = x.shape
    tm = ROW_TILE
    row = lambda n: pl.BlockSpec((None, tm, n), lambda b, i: (b, i, 0))
    return pl.pallas_call(
        _mixffn_kernel,
        grid=(bsz, s // tm),
        in_specs=[row(d),
                  pl.BlockSpec((None, 6, d), lambda b, i: (b, 0, 0)),
                  row(ATTN_WIDTH), row(CONV_WIDTH), row(GATE_COLS),
                  _resident((ATTN_WIDTH, d)), _resident((CONV_WIDTH, d)),
                  _resident((1, d)), _resident((d, d)), _resident((1, d)),
                  _resident((d, 2 * D_FF)), _resident((D_FF, d)), _resident((1, d))],
        out_specs=row(d),
        out_shape=jax.ShapeDtypeStruct((bsz, s, d), F32),
        scratch_shapes=[pltpu.VMEM((tm, d), F32), pltpu.VMEM((tm, d), BF16),
                        pltpu.VMEM((tm, D_FF), BF16)],
        compiler_params=_params(("parallel", "arbitrary")),
        name="mix_ffn",
    )(x, ada3, a_n, u, gates, woa, woc, boc.reshape(1, d), wout, g2.reshape(1, d),
      w1, w2, gf.reshape(1, d))


def kernel(x, c, w_ada, b_ada, norm1_g, norm2_g, final_g, w_in, lambda_q1, lambda_k1,
           lambda_q2, lambda_k2, rel_bias, attn_sub_g, w_o_attn, conv_w, conv_b,
           conv_ln_g, conv_ln_b, w_o_conv, b_o_conv, w_out, w_ffn_in, w_ffn_out):
    bsz, s, d = x.shape
    assert w_ada.shape[0] == 1, "single-layer block"
    assert (d, s % ROW_TILE, s % ATTN_TILE) == (D_MODEL, 0, 0)
    assert ATTN_TILE == 2 * MAX_DISTANCE
    assert PROJ_CHUNK == CONV_WIDTH and ATTN_WIDTH % PROJ_CHUNK == 0
    l = 0
    ada, w_in_bf, bias_tiles = _prologue(c, w_ada, b_ada, w_in, rel_bias)
    ada3 = ada.reshape(bsz, 6, d)
    qkv, u, gates = _in_proj(x, ada3, norm1_g[l], w_in_bf, conv_w[l],
                             conv_b[l], conv_ln_g[l], conv_ln_b[l])
    lamv = jnp.stack([lambda_q1[l], lambda_k1[l], lambda_q2[l], lambda_k2[l]])
    a_n, (woa, woc, wout, w1, w2) = _attention(
        qkv, bias_tiles, lamv, attn_sub_g[l],
        [w_o_attn, w_o_conv, w_out, w_ffn_in, w_ffn_out])
    return _mix_ffn(x, ada3, a_n, u, gates, woa, woc, b_o_conv[l], wout, norm2_g[l],
                    w1, w2, final_g)
```

```python
import math

import jax
import jax.numpy as jnp
from jax import lax
from jax.experimental import pallas as pl
from jax.experimental.pallas import tpu as pltpu

D_MODEL = 1024
N_HEADS_A = 4
HEAD_QK = 64
HEAD_V = 2 * HEAD_QK
ATTN_WIDTH = N_HEADS_A * HEAD_V
CONV_WIDTH = 512
CONV_KERNEL = 31
QKV_COLS = 3 * ATTN_WIDTH
GLU_COLS = 2 * CONV_WIDTH
GATE_COLS = 2 * D_MODEL
IN_COLS = QKV_COLS + GLU_COLS + GATE_COLS
D_FF = 2816
N_BUCKETS = 32
MAX_EXACT = 16
MAX_DISTANCE = 128
EPS = 1e-6
NEG_INF = -1e30
LAM_INIT = 0.8 - 0.6 * math.exp(-0.3 * 0)

V7X_VMEM_BYTES = 64 * 1024 * 1024
VMEM_LIMIT = V7X_VMEM_BYTES - 12 * 1024 * 1024

ROW_TILE = 512
ATTN_TILE = 256
ATTN_LAG = 4
ONES_ROWS = 16
CAST_ROWS = 16
LOG2E = math.log2(math.e)
Q_SCALE = HEAD_QK ** -0.5 * LOG2E
CONV_HALO = 32
CONV_CHUNK = 16
PROJ_CHUNK = 512
FF_CHUNK = 256

BF16 = jnp.bfloat16
F32 = jnp.float32


def _sigmoid(x):
    return 1.0 / (1.0 + jnp.exp2(x * (-LOG2E)))


def _silu(x):
    return x * _sigmoid(x)


def _params(sem):
    return pltpu.CompilerParams(dimension_semantics=sem, vmem_limit_bytes=VMEM_LIMIT)


def _resident(shape):
    nd = len(shape)
    return pl.BlockSpec(shape, lambda *_: (0,) * nd, pipeline_mode=pl.Buffered(1))


def _bias_tile(rel_ref, h, t):
    T = ATTN_TILE
    key = lax.broadcasted_iota(jnp.int32, (T, T), 0)
    qry = lax.broadcasted_iota(jnp.int32, (T, T), 1)
    dist = (1 - t) * T + qry - key
    n = jnp.maximum(dist, 0)
    large = MAX_EXACT + (jnp.log(jnp.maximum(n, 1).astype(F32) / MAX_EXACT)
                         / math.log(MAX_DISTANCE / MAX_EXACT)
                         * (N_BUCKETS - MAX_EXACT)).astype(jnp.int32)
    large = jnp.minimum(large, N_BUCKETS - 1)
    bucket = jnp.where(n < MAX_EXACT, n, large)
    far = rel_ref[N_BUCKETS - 1, h]
    bias = jnp.zeros((T, T), F32)
    for b in range(N_BUCKETS - 1):
        bias = jnp.where(bucket == b, (rel_ref[b, h] - far) * LOG2E, bias)
    return jnp.where(dist >= 0, bias, NEG_INF)


def _prologue_kernel(rel_ref, c_ref, w_ref, b_ref, win_ref, o_ref, win_bf_ref, bias_ref):
    j = pl.program_id(0)
    ca = _silu(c_ref[...])
    o_ref[...] = jnp.dot(ca.astype(BF16), w_ref[...].astype(BF16),
                         preferred_element_type=F32) + b_ref[...]
    win_bf_ref[...] = win_ref[...].astype(BF16)
    bias_ref[...] = _bias_tile(rel_ref, j // 2, j % 2)


def _prologue(c, w, b, w_in, rel_bias):
    bsz, d = c.shape
    n = w.shape[-1]
    steps = 2 * N_HEADS_A
    tn = n // steps
    _, rows, cols = w_in.shape
    tr = rows // steps
    T = ATTN_TILE
    return pl.pallas_call(
        _prologue_kernel,
        grid=(steps,),
        in_specs=[pl.BlockSpec(memory_space=pltpu.SMEM),
                  pl.BlockSpec((bsz, d), lambda j: (0, 0)),
                  pl.BlockSpec((None, d, tn), lambda j: (0, 0, j)),
                  pl.BlockSpec((1, tn), lambda j: (0, j)),
                  pl.BlockSpec((None, tr, cols), lambda j: (0, j, 0))],
        out_specs=[pl.BlockSpec((bsz, tn), lambda j: (0, j)),
                   pl.BlockSpec((tr, cols), lambda j: (j, 0)),
                   pl.BlockSpec((None, None, T, T), lambda j: (j // 2, j % 2, 0, 0))],
        out_shape=[jax.ShapeDtypeStruct((bsz, n), F32),
                   jax.ShapeDtypeStruct((rows, cols), BF16),
                   jax.ShapeDtypeStruct((N_HEADS_A, 2, T, T), F32)],
        compiler_params=_params(("arbitrary",)),
        name="prologue",
    )(rel_bias, c, w, b, w_in)


def _conv_chunk(r0, ubuf, w_ref, cb_ref, lng_ref, lnb_ref, o_ref):
    off = CONV_HALO - (CONV_KERNEL - 1)
    acc = jnp.zeros((CONV_CHUNK // 8, 8, CONV_WIDTH), F32)
    for k in range(CONV_KERNEL):
        a, b = divmod(k + off, 8)
        rows = ubuf[b, r0 + 8 * a:r0 + 8 * a + CONV_CHUNK, :]
        acc = acc + rows.reshape(acc.shape) * w_ref[k][None]
    u = acc.reshape(CONV_CHUNK, CONV_WIDTH) + cb_ref[...]
    mu = jnp.mean(u, axis=-1, keepdims=True)
    d = u - mu
    var = jnp.mean(d * d, axis=-1, keepdims=True)
    y = d * lax.rsqrt(var + EPS) * lng_ref[...] + lnb_ref[...]
    o_ref[r0:r0 + CONV_CHUNK, :] = _silu(y).astype(BF16)


def _inproj_kernel(x_ref, ada_ref, g_ref, w_ref, cw_ref, cb_ref, lng_ref, lnb_ref,
                   qkv_ref, u_ref, gate_ref, ubuf):
    TS, H, C = ROW_TILE, CONV_HALO, CONV_WIDTH
    i = pl.program_id(1)

    @pl.when(i == 0)
    def _():
        ubuf[0, 0:H, :] = jnp.zeros((H, C), F32)

    @pl.when(i > 0)
    def _():
        ubuf[0, 0:H, :] = ubuf[0, TS:TS + H, :]

    x = x_ref[...]
    ms = jnp.mean(x * x, axis=-1, keepdims=True)
    y = x * lax.rsqrt(ms + EPS)
    h = (y * g_ref[...]) * (1.0 + ada_ref[1:2, :]) + ada_ref[0:1, :]
    hb = h.astype(BF16)

    def proj(base):
        return jnp.dot(hb, w_ref[:, base:base + PROJ_CHUNK], preferred_element_type=F32)

    ubuf[0, H:H + TS, :] = proj(QKV_COLS) * _sigmoid(proj(QKV_COLS + C))
    n = TS + H - 8
    for b in range(1, 8):
        ubuf[b, 0:n, :] = ubuf[0, b:b + n, :]
    jobs = ([(qkv_ref, c, c) for c in range(0, QKV_COLS, PROJ_CHUNK)]
            + [(gate_ref, c, QKV_COLS + GLU_COLS + c) for c in range(0, GATE_COLS, PROJ_CHUNK)])
    for r0 in range(0, TS, CONV_CHUNK):
        _conv_chunk(r0, ubuf, cw_ref, cb_ref, lng_ref, lnb_ref, u_ref)
    for ref, c, base in jobs:
        r = proj(base)
        if base < ATTN_WIDTH:
            r = r * Q_SCALE
        ref[:, c:c + PROJ_CHUNK] = r.astype(BF16)


def _in_proj(x, ada3, g, w, cw, cb, lng, lnb):
    bsz, s, d = x.shape
    tm = ROW_TILE
    C = CONV_WIDTH
    row = lambda n: pl.BlockSpec((None, tm, n), lambda b, i: (b, i, 0))
    vec = lambda a: a.reshape(1, C)
    return pl.pallas_call(
        _inproj_kernel,
        grid=(bsz, s // tm),
        in_specs=[row(d),
                  pl.BlockSpec((None, 6, d), lambda b, i: (b, 0, 0)),
                  _resident((1, d)),
                  _resident((d, IN_COLS)),
                  _resident((CONV_KERNEL, 8, C)),
                  _resident((1, C)), _resident((1, C)), _resident((1, C))],
        out_specs=[row(QKV_COLS), row(C), row(GATE_COLS)],
        out_shape=[jax.ShapeDtypeStruct((bsz, s, QKV_COLS), BF16),
                   jax.ShapeDtypeStruct((bsz, s, C), BF16),
                   jax.ShapeDtypeStruct((bsz, s, GATE_COLS), BF16)],
        scratch_shapes=[pltpu.VMEM((8, tm + CONV_HALO, C), F32)],
        compiler_params=_params(("arbitrary", "arbitrary")),
        name="in_proj",
    )(x, ada3, g.reshape(1, d), w, jnp.broadcast_to(cw[:, None, :], (CONV_KERNEL, 8, C)),
      vec(cb), vec(lng), vec(lnb))


def _attn_kernel(lamv_ref, subg_ref, q_ref, k_ref, v_ref, bias_ref, *rest):
    n_cast = (len(rest) - 5) // 2
    cast_in, o_ref, cast_out = rest[:n_cast], rest[n_cast], rest[n_cast + 1:2 * n_cast + 1]
    vt_ref, qz_ref, m_ref, a_ref = rest[2 * n_cast + 1:]
    T = ATTN_TILE
    NS = 2 * N_HEADS_A
    n_q = q_ref.shape[0] // T
    hcols = lambda h: slice(h * HEAD_V, (h + 1) * HEAD_V)
    rows = lambda t: slice(t * T, (t + 1) * T)

    vt = v_ref[...].T
    for h in range(N_HEADS_A):
        vt_ref[h, 0:HEAD_V, :] = vt[hcols(h), :]
        vt_ref[h, HEAD_V:, :] = jnp.ones((ONES_ROWS, vt.shape[1]), BF16)

    lv = lamv_ref[...]
    lam = (jnp.exp(jnp.sum(lv[0:1] * lv[1:2], axis=-1, keepdims=True))
           - jnp.exp(jnp.sum(lv[2:3] * lv[3:4], axis=-1, keepdims=True)) + LAM_INIT)
    lane = lax.broadcasted_iota(jnp.int32, (T, HEAD_V), 1)

    def prepare(i):
        slot = i % 2
        for h in range(N_HEADS_A):
            qs = q_ref[rows(i), hcols(h)]
            zero = jnp.zeros_like(qs)
            qz_ref[slot, 2 * h] = jnp.where(lane < HEAD_QK, qs, zero)
            qz_ref[slot, 2 * h + 1] = jnp.where(lane >= HEAD_QK, qs, zero)

    def finalize(i):
        slot = i % 2
        for h in range(N_HEADS_A):
            a1, a2 = a_ref[slot, 2 * h], a_ref[slot, 2 * h + 1]
            o = (a1[:HEAD_V] / a1[HEAD_V:HEAD_V + 1]
                 - lam * (a2[:HEAD_V] / a2[HEAD_V:HEAD_V + 1]))
            ms = jnp.mean(o * o, axis=0, keepdims=True)
            y = (o * lax.rsqrt(ms + EPS)).T
            o_ref[rows(i), hcols(h)] = ((y * subg_ref[...]) * (1.0 - LAM_INIT)).astype(BF16)

    units = [(i, j, n) for i in range(n_q) for j in range(i + 1) for n in range(NS)]
    scores, probs, alphas = {}, {}, {}

    def score(u):
        i, j, n = units[u]
        kc = k_ref[rows(j), hcols(n // 2)]
        scores[u] = lax.dot_general(kc, qz_ref[i % 2, n], (((1,), (1,)), ((), ())),
                                    preferred_element_type=F32)

    def softmax(u):
        i, j, n = units[u]
        s = scores.pop(u)
        h, near = n // 2, MAX_DISTANCE
        on_diag = j == i
        if j == i - 1:
            corner = s[near:, :near] + bias_ref[h, 0, near:, :near]
            s = jnp.concatenate(
                [s[:near], jnp.concatenate([corner, s[near:, near:]], axis=1)], axis=0)
        if on_diag:
            s_lo = s[:near, :near] + bias_ref[h, 1, :near, :near]
            s_hi = s[:, near:] + bias_ref[h, 1, :, near:]
            m_new = jnp.concatenate([jnp.max(s_lo, axis=0, keepdims=True),
                                     jnp.max(s_hi, axis=0, keepdims=True)], axis=1)
        else:
            m_new = jnp.max(s, axis=0, keepdims=True)
        if j > 0:
            m_old = m_ref[i % 2, n]
            m_new = jnp.maximum(m_old, m_new)
            alphas[u] = jnp.exp2(m_old - m_new)
        if on_diag:
            p_lo = jnp.exp2(s_lo - m_new[:, :near]).astype(BF16)
            p_hi = jnp.exp2(s_hi - m_new[:, near:]).astype(BF16)
            p_lo = jnp.concatenate([p_lo, jnp.zeros((T - near, near), BF16)], axis=0)
            probs[u] = jnp.concatenate([p_lo, p_hi], axis=1)
        else:
            probs[u] = jnp.exp2(s - m_new).astype(BF16)
        m_ref[i % 2, n] = m_new

    def value(u):
        i, j, n = units[u]
        vc = vt_ref[n // 2, :, rows(j)]
        pv = jnp.dot(vc, probs.pop(u), preferred_element_type=F32)
        a_ref[i % 2, n] = pv if j == 0 else pv + alphas.pop(u) * a_ref[i % 2, n]

    for step in range(len(units) + 2 * ATTN_LAG):
        if step < len(units):
            if units[step][1:] == (0, 0):
                prepare(units[step][0])
            score(step)
        if 0 <= step - ATTN_LAG < len(units):
            softmax(step - ATTN_LAG)
        done = step - 2 * ATTN_LAG
        if 0 <= done < len(units):
            value(done)
            i, j, n = units[done]
            if (j, n) == (i, NS - 1):
                finalize(i)

    for w_ref, wb_ref in zip(cast_in, cast_out):
        wb_ref[...] = w_ref[...].astype(BF16)


def _attention(qkv, bias_tiles, lamv, subg, weights):
    bsz, s, _ = qkv.shape
    T = ATTN_TILE
    W = ATTN_WIDTH
    NS = 2 * N_HEADS_A
    steps = bsz

    def slab_spec(w, squeeze):
        rows, cols = w.shape[-2:]
        assert rows % (steps * CAST_ROWS) == 0
        block = (rows // steps, cols)
        if squeeze:
            return pl.BlockSpec((None,) + block, lambda b: (0, b, 0))
        return pl.BlockSpec(block, lambda b: (b, 0))

    seq = lambda col: pl.BlockSpec((None, s, W), lambda b: (b, 0, col))
    out = pl.pallas_call(
        _attn_kernel,
        grid=(bsz,),
        in_specs=[_resident((4, HEAD_QK)),
                  _resident((1, HEAD_V)),
                  seq(0), seq(1), seq(2),
                  _resident((N_HEADS_A, 2, T, T))] + [slab_spec(w, True) for w in weights],
        out_specs=[seq(0)] + [slab_spec(w, False) for w in weights],
        out_shape=[jax.ShapeDtypeStruct((bsz, s, W), BF16)]
        + [jax.ShapeDtypeStruct(w.shape[-2:], BF16) for w in weights],
        scratch_shapes=[pltpu.VMEM((N_HEADS_A, HEAD_V + ONES_ROWS, s), BF16),
                        pltpu.VMEM((2, NS, T, HEAD_V), BF16),
                        pltpu.VMEM((2, NS, 1, T), F32),
                        pltpu.VMEM((2, NS, HEAD_V + ONES_ROWS, T), F32)],
        compiler_params=_params(("arbitrary",)),
        name="diff_attn",
    )(lamv, subg.reshape(1, HEAD_V), qkv, qkv, qkv, bias_tiles, *weights)
    return out[0], out[1:]


def _mixffn_kernel(x_ref, ada_ref, a_ref, u_ref, gate_ref, woa_ref, woc_ref, boc_ref,
                   wout_ref, g2_ref, w1_ref, w2_ref, gf_ref, o_ref,
                   x1_ref, h2_ref, act_ref):
    D = D_MODEL
    a = jnp.dot(a_ref[...], woa_ref[...], preferred_element_type=F32)
    cv = jnp.dot(u_ref[...], woc_ref[...], preferred_element_type=F32) + boc_ref[...]
    gate = gate_ref[...]
    y = (_sigmoid(gate[:, :D].astype(F32)) * a
         + _sigmoid(gate[:, D:].astype(F32)) * cv)
    z = jnp.dot(y.astype(BF16), wout_ref[...], preferred_element_type=F32)
    x1 = x_ref[...] + ada_ref[2:3, :] * z
    x1_ref[...] = x1
    ms = jnp.mean(x1 * x1, axis=-1, keepdims=True)
    yn = x1 * lax.rsqrt(ms + EPS)
    h2_ref[...] = ((yn * g2_ref[...]) * (1.0 + ada_ref[4:5, :]) + ada_ref[3:4, :]).astype(BF16)

    h = h2_ref[...]
    for c in range(0, D_FF, FF_CHUNK):
        fg = jnp.dot(h, w1_ref[:, c:c + FF_CHUNK], preferred_element_type=F32)
        fu = jnp.dot(h, w1_ref[:, D_FF + c:D_FF + c + FF_CHUNK],
                     preferred_element_type=F32)
        act_ref[:, c:c + FF_CHUNK] = (_silu(fg) * fu).astype(BF16)
    z2 = jnp.dot(act_ref[...], w2_ref[...], preferred_element_type=F32)
    x2 = x1_ref[...] + ada_ref[5:6, :] * z2
    ms2 = jnp.mean(x2 * x2, axis=-1, keepdims=True)
    o_ref[...] = (x2 * lax.rsqrt(ms2 + EPS)) * gf_ref[...]


def _mix_ffn(x, ada3, a_n, u, gates, woa, woc, boc, wout, g2, w1, w2, gf):
    bsz, s, d = x.shape
    tm = ROW_TILE
    row = lambda n: pl.BlockSpec((None, tm, n), lambda b, i: (b, i, 0))
    return pl.pallas_call(
        _mixffn_kernel,
        grid=(bsz, s // tm),
        in_specs=[row(d),
                  pl.BlockSpec((None, 6, d), lambda b, i: (b, 0, 0)),
                  row(ATTN_WIDTH), row(CONV_WIDTH), row(GATE_COLS),
                  _resident((ATTN_WIDTH, d)), _resident((CONV_WIDTH, d)),
                  _resident((1, d)), _resident((d, d)), _resident((1, d)),
                  _resident((d, 2 * D_FF)), _resident((D_FF, d)), _resident((1, d))],
        out_specs=row(d),
        out_shape=jax.ShapeDtypeStruct((bsz, s, d), F32),
        scratch_shapes=[pltpu.VMEM((tm, d), F32), pltpu.VMEM((tm, d), BF16),
                        pltpu.VMEM((tm, D_FF), BF16)],
        compiler_params=_params(("parallel", "arbitrary")),
        name="mix_ffn",
    )(x, ada3, a_n, u, gates, woa, woc, boc.reshape(1, d), wout, g2.reshape(1, d),
      w1, w2, gf.reshape(1, d))


def kernel(x, c, w_ada, b_ada, norm1_g, norm2_g, final_g, w_in, lambda_q1, lambda_k1,
           lambda_q2, lambda_k2, rel_bias, attn_sub_g, w_o_attn, conv_w, conv_b,
           conv_ln_g, conv_ln_b, w_o_conv, b_o_conv, w_out, w_ffn_in, w_ffn_out):
    bsz, s, d = x.shape
    assert w_ada.shape[0] == 1, "single-layer block"
    assert (d, s % ROW_TILE, s % ATTN_TILE) == (D_MODEL, 0, 0)
    assert ATTN_TILE == 2 * MAX_DISTANCE
    assert PROJ_CHUNK == CONV_WIDTH and ATTN_WIDTH % PROJ_CHUNK == 0
    l = 0
    ada, w_in_bf, bias_tiles = _prologue(c, w_ada, b_ada, w_in, rel_bias)
    ada3 = ada.reshape(bsz, 6, d)
    qkv, u, gates = _in_proj(x, ada3, norm1_g[l], w_in_bf, conv_w[l],
                             conv_b[l], conv_ln_g[l], conv_ln_b[l])
    lamv = jnp.stack([lambda_q1[l], lambda_k1[l], lambda_q2[l], lambda_k2[l]])
    a_n, (woa, woc, wout, w1, w2) = _attention(
        qkv, bias_tiles, lamv, attn_sub_g[l],
        [w_o_attn, w_o_conv, w_out, w_ffn_in, w_ffn_out])
    return _mix_ffn(x, ada3, a_n, u, gates, woa, woc, b_o_conv[l], wout, norm2_g[l],
                    w1, w2, final_g)
```

```python
import math

import jax
import jax.numpy as jnp
from jax import lax
from jax.experimental import pallas as pl
from jax.experimental.pallas import tpu as pltpu

D_MODEL = 1024
N_HEADS_A = 4
HEAD_QK = 64
HEAD_V = 2 * HEAD_QK
ATTN_WIDTH = N_HEADS_A * HEAD_V
CONV_WIDTH = 512
CONV_KERNEL = 31
QKV_COLS = 3 * ATTN_WIDTH
GLU_COLS = 2 * CONV_WIDTH
GATE_COLS = 2 * D_MODEL
IN_COLS = QKV_COLS + GLU_COLS + GATE_COLS
D_FF = 2816
N_BUCKETS = 32
MAX_EXACT = 16
MAX_DISTANCE = 128
EPS = 1e-6
NEG_INF = -1e30
LAM_INIT = 0.8 - 0.6 * math.exp(-0.3 * 0)

V7X_VMEM_BYTES = 64 * 1024 * 1024
VMEM_LIMIT = V7X_VMEM_BYTES - 12 * 1024 * 1024

ROW_TILE = 512
ATTN_TILE = 256
ATTN_LAG = 4
ONES_ROWS = 16
CAST_ROWS = 16
LOG2E = math.log2(math.e)
Q_SCALE = HEAD_QK ** -0.5 * LOG2E
CONV_HALO = 32
CONV_CHUNK = 16
PROJ_CHUNK = 512
FF_CHUNK = 256

BF16 = jnp.bfloat16
F32 = jnp.float32


def _sigmoid(x):
    return 1.0 / (1.0 + jnp.exp2(x * (-LOG2E)))


def _silu(x):
    return x * _sigmoid(x)


def _params(sem):
    return pltpu.CompilerParams(dimension_semantics=sem, vmem_limit_bytes=VMEM_LIMIT)


def _resident(shape):
    nd = len(shape)
    return pl.BlockSpec(shape, lambda *_: (0,) * nd, pipeline_mode=pl.Buffered(1))


def _bias_tile(rel_ref, h, t):
    T = ATTN_TILE
    key = lax.broadcasted_iota(jnp.int32, (T, T), 0)
    qry = lax.broadcasted_iota(jnp.int32, (T, T), 1)
    dist = (1 - t) * T + qry - key
    n = jnp.maximum(dist, 0)
    large = MAX_EXACT + (jnp.log(jnp.maximum(n, 1).astype(F32) / MAX_EXACT)
                         / math.log(MAX_DISTANCE / MAX_EXACT)
                         * (N_BUCKETS - MAX_EXACT)).astype(jnp.int32)
    large = jnp.minimum(large, N_BUCKETS - 1)
    bucket = jnp.where(n < MAX_EXACT, n, large)
    far = rel_ref[N_BUCKETS - 1, h]
    bias = jnp.zeros((T, T), F32)
    for b in range(N_BUCKETS - 1):
        bias = jnp.where(bucket == b, (rel_ref[b, h] - far) * LOG2E, bias)
    return jnp.where(dist >= 0, bias, NEG_INF)


def _prologue_kernel(rel_ref, c_ref, w_ref, b_ref, win_ref, o_ref, win_bf_ref, bias_ref):
    j = pl.program_id(0)
    ca = _silu(c_ref[...])
    o_ref[...] = jnp.dot(ca.astype(BF16), w_ref[...].astype(BF16),
                         preferred_element_type=F32) + b_ref[...]
    win_bf_ref[...] = win_ref[...].astype(BF16)
    bias_ref[...] = _bias_tile(rel_ref, j // 2, j % 2)


def _prologue(c, w, b, w_in, rel_bias):
    bsz, d = c.shape
    n = w.shape[-1]
    steps = 2 * N_HEADS_A
    tn = n // steps
    _, rows, cols = w_in.shape
    tr = rows // steps
    T = ATTN_TILE
    return pl.pallas_call(
        _prologue_kernel,
        grid=(steps,),
        in_specs=[pl.BlockSpec(memory_space=pltpu.SMEM),
                  pl.BlockSpec((bsz, d), lambda j: (0, 0)),
                  pl.BlockSpec((None, d, tn), lambda j: (0, 0, j)),
                  pl.BlockSpec((1, tn), lambda j: (0, j)),
                  pl.BlockSpec((None, tr, cols), lambda j: (0, j, 0))],
        out_specs=[pl.BlockSpec((bsz, tn), lambda j: (0, j)),
                   pl.BlockSpec((tr, cols), lambda j: (j, 0)),
                   pl.BlockSpec((None, None, T, T), lambda j: (j // 2, j % 2, 0, 0))],
        out_shape=[jax.ShapeDtypeStruct((bsz, n), F32),
                   jax.ShapeDtypeStruct((rows, cols), BF16),
                   jax.ShapeDtypeStruct((N_HEADS_A, 2, T, T), F32)],
        compiler_params=_params(("arbitrary",)),
        name="prologue",
    )(rel_bias, c, w, b, w_in)


def _conv_chunk(r0, ubuf, w_ref, cb_ref, lng_ref, lnb_ref, o_ref):
    off = CONV_HALO - (CONV_KERNEL - 1)
    acc = jnp.zeros((CONV_CHUNK // 8, 8, CONV_WIDTH), F32)
    for k in range(CONV_KERNEL):
        a, b = divmod(k + off, 8)
        rows = ubuf[b, r0 + 8 * a:r0 + 8 * a + CONV_CHUNK, :]
        acc = acc + rows.reshape(acc.shape) * w_ref[k][None]
    u = acc.reshape(CONV_CHUNK, CONV_WIDTH) + cb_ref[...]
    mu = jnp.mean(u, axis=-1, keepdims=True)
    d = u - mu
    var = jnp.mean(d * d, axis=-1, keepdims=True)
    y = d * lax.rsqrt(var + EPS) * lng_ref[...] + lnb_ref[...]
    o_ref[r0:r0 + CONV_CHUNK, :] = _silu(y).astype(BF16)


def _inproj_kernel(x_ref, ada_ref, g_ref, w_ref, cw_ref, cb_ref, lng_ref, lnb_ref,
                   qkv_ref, u_ref, gate_ref, ubuf):
    TS, H, C = ROW_TILE, CONV_HALO, CONV_WIDTH
    i = pl.program_id(1)

    @pl.when(i == 0)
    def _():
        ubuf[0, 0:H, :] = jnp.zeros((H, C), F32)

    @pl.when(i > 0)
    def _():
        ubuf[0, 0:H, :] = ubuf[0, TS:TS + H, :]

    x = x_ref[...]
    ms = jnp.mean(x * x, axis=-1, keepdims=True)
    y = x * lax.rsqrt(ms + EPS)
    h = (y * g_ref[...]) * (1.0 + ada_ref[1:2, :]) + ada_ref[0:1, :]
    hb = h.astype(BF16)

    def proj(base):
        return jnp.dot(hb, w_ref[:, base:base + PROJ_CHUNK], preferred_element_type=F32)

    ubuf[0, H:H + TS, :] = proj(QKV_COLS) * _sigmoid(proj(QKV_COLS + C))
    n = TS + H - 8
    for b in range(1, 8):
        ubuf[b, 0:n, :] = ubuf[0, b:b + n, :]
    jobs = ([(qkv_ref, c, c) for c in range(0, QKV_COLS, PROJ_CHUNK)]
            + [(gate_ref, c, QKV_COLS + GLU_COLS + c) for c in range(0, GATE_COLS, PROJ_CHUNK)])
    rows = list(range(0, TS, CONV_CHUNK))
    for n, (ref, c, base) in enumerate(jobs):
        r = proj(base)
        if base < ATTN_WIDTH:
            r = r * Q_SCALE
        ref[:, c:c + PROJ_CHUNK] = r.astype(BF16)
        for r0 in rows[n * len(rows) // len(jobs):(n + 1) * len(rows) // len(jobs)]:
            _conv_chunk(r0, ubuf, cw_ref, cb_ref, lng_ref, lnb_ref, u_ref)


def _in_proj(x, ada3, g, w, cw, cb, lng, lnb):
    bsz, s, d = x.shape
    tm = ROW_TILE
    C = CONV_WIDTH
    row = lambda n: pl.BlockSpec((None, tm, n), lambda b, i: (b, i, 0))
    vec = lambda a: a.reshape(1, C)
    return pl.pallas_call(
        _inproj_kernel,
        grid=(bsz, s // tm),
        in_specs=[row(d),
                  pl.BlockSpec((None, 6, d), lambda b, i: (b, 0, 0)),
                  _resident((1, d)),
                  _resident((d, IN_COLS)),
                  _resident((CONV_KERNEL, 8, C)),
                  _resident((1, C)), _resident((1, C)), _resident((1, C))],
        out_specs=[row(QKV_COLS), row(C), row(GATE_COLS)],
        out_shape=[jax.ShapeDtypeStruct((bsz, s, QKV_COLS), BF16),
                   jax.ShapeDtypeStruct((bsz, s, C), BF16),
                   jax.ShapeDtypeStruct((bsz, s, GATE_COLS), BF16)],
        scratch_shapes=[pltpu.VMEM((8, tm + CONV_HALO, C), F32)],
        compiler_params=_params(("arbitrary", "arbitrary")),
        name="in_proj",
    )(x, ada3, g.reshape(1, d), w, jnp.broadcast_to(cw[:, None, :], (CONV_KERNEL, 8, C)),
      vec(cb), vec(lng), vec(lnb))


def _attn_kernel(lamv_ref, subg_ref, q_ref, k_ref, v_ref, bias_ref, *rest):
    n_cast = (len(rest) - 5) // 2
    cast_in, o_ref, cast_out = rest[:n_cast], rest[n_cast], rest[n_cast + 1:2 * n_cast + 1]
    vt_ref, qz_ref, m_ref, a_ref = rest[2 * n_cast + 1:]
    T = ATTN_TILE
    NS = 2 * N_HEADS_A
    n_q = q_ref.shape[0] // T
    hcols = lambda h: slice(h * HEAD_V, (h + 1) * HEAD_V)
    rows = lambda t: slice(t * T, (t + 1) * T)

    vt = v_ref[...].T
    for h in range(N_HEADS_A):
        vt_ref[h, 0:HEAD_V, :] = vt[hcols(h), :]
        vt_ref[h, HEAD_V:, :] = jnp.ones((ONES_ROWS, vt.shape[1]), BF16)

    lv = lamv_ref[...]
    lam = (jnp.exp(jnp.sum(lv[0:1] * lv[1:2], axis=-1, keepdims=True))
           - jnp.exp(jnp.sum(lv[2:3] * lv[3:4], axis=-1, keepdims=True)) + LAM_INIT)
    lane = lax.broadcasted_iota(jnp.int32, (T, HEAD_V), 1)

    def prepare(i):
        slot = i % 2
        for h in range(N_HEADS_A):
            qs = q_ref[rows(i), hcols(h)]
            zero = jnp.zeros_like(qs)
            qz_ref[slot, 2 * h] = jnp.where(lane < HEAD_QK, qs, zero)
            qz_ref[slot, 2 * h + 1] = jnp.where(lane >= HEAD_QK, qs, zero)

    def finalize(i):
        slot = i % 2
        for h in range(N_HEADS_A):
            a1, a2 = a_ref[slot, 2 * h], a_ref[slot, 2 * h + 1]
            o = (a1[:HEAD_V] / a1[HEAD_V:HEAD_V + 1]
                 - lam * (a2[:HEAD_V] / a2[HEAD_V:HEAD_V + 1]))
            ms = jnp.mean(o * o, axis=0, keepdims=True)
            y = (o * lax.rsqrt(ms + EPS)).T
            o_ref[rows(i), hcols(h)] = ((y * subg_ref[...]) * (1.0 - LAM_INIT)).astype(BF16)

    units = [(i, j, n) for i in range(n_q) for j in range(i + 1) for n in range(NS)]
    scores, probs, alphas = {}, {}, {}

    def score(u):
        i, j, n = units[u]
        kc = k_ref[rows(j), hcols(n // 2)]
        scores[u] = lax.dot_general(kc, qz_ref[i % 2, n], (((1,), (1,)), ((), ())),
                                    preferred_element_type=F32)

    def softmax(u):
        i, j, n = units[u]
        s = scores.pop(u)
        h, near = n // 2, MAX_DISTANCE
        on_diag = j == i
        if j == i - 1:
            corner = s[near:, :near] + bias_ref[h, 0, near:, :near]
            s = jnp.concatenate(
                [s[:near], jnp.concatenate([corner, s[near:, near:]], axis=1)], axis=0)
        if on_diag:
            s_lo = s[:near, :near] + bias_ref[h, 1, :near, :near]
            s_hi = s[:, near:] + bias_ref[h, 1, :, near:]
            m_new = jnp.concatenate([jnp.max(s_lo, axis=0, keepdims=True),
                                     jnp.max(s_hi, axis=0, keepdims=True)], axis=1)
        else:
            m_new = jnp.max(s, axis=0, keepdims=True)
        if j > 0:
            m_old = m_ref[i % 2, n]
            m_new = jnp.maximum(m_old, m_new)
            alphas[u] = jnp.exp2(m_old - m_new)
        if on_diag:
            p_lo = jnp.exp2(s_lo - m_new[:, :near]).astype(BF16)
            p_hi = jnp.exp2(s_hi - m_new[:, near:]).astype(BF16)
            p_lo = jnp.concatenate([p_lo, jnp.zeros((T - near, near), BF16)], axis=0)
            probs[u] = jnp.concatenate([p_lo, p_hi], axis=1)
        else:
            probs[u] = jnp.exp2(s - m_new).astype(BF16)
        m_ref[i % 2, n] = m_new

    def value(u):
        i, j, n = units[u]
        vc = vt_ref[n // 2, :, rows(j)]
        pv = jnp.dot(vc, probs.pop(u), preferred_element_type=F32)
        a_ref[i % 2, n] = pv if j == 0 else pv + alphas.pop(u) * a_ref[i % 2, n]

    for step in range(len(units) + 2 * ATTN_LAG):
        if step < len(units):
            if units[step][1:] == (0, 0):
                prepare(units[step][0])
            score(step)
        if 0 <= step - ATTN_LAG < len(units):
            softmax(step - ATTN_LAG)
        done = step - 2 * ATTN_LAG
        if 0 <= done < len(units):
            value(done)
            i, j, n = units[done]
            if (j, n) == (i, NS - 1):
                finalize(i)

    for w_ref, wb_ref in zip(cast_in, cast_out):
        wb_ref[...] = w_ref[...].astype(BF16)


def _attention(qkv, bias_tiles, lamv, subg, weights):
    bsz, s, _ = qkv.shape
    T = ATTN_TILE
    W = ATTN_WIDTH
    NS = 2 * N_HEADS_A
    steps = bsz

    def slab_spec(w, squeeze):
        rows, cols = w.shape[-2:]
        assert rows % (steps * CAST_ROWS) == 0
        block = (rows // steps, cols)
        if squeeze:
            return pl.BlockSpec((None,) + block, lambda b: (0, b, 0))
        return pl.BlockSpec(block, lambda b: (b, 0))

    seq = lambda col: pl.BlockSpec((None, s, W), lambda b: (b, 0, col))
    out = pl.pallas_call(
        _attn_kernel,
        grid=(bsz,),
        in_specs=[_resident((4, HEAD_QK)),
                  _resident((1, HEAD_V)),
                  seq(0), seq(1), seq(2),
                  _resident((N_HEADS_A, 2, T, T))] + [slab_spec(w, True) for w in weights],
        out_specs=[seq(0)] + [slab_spec(w, False) for w in weights],
        out_shape=[jax.ShapeDtypeStruct((bsz, s, W), BF16)]
        + [jax.ShapeDtypeStruct(w.shape[-2:], BF16) for w in weights],
        scratch_shapes=[pltpu.VMEM((N_HEADS_A, HEAD_V + ONES_ROWS, s), BF16),
                        pltpu.VMEM((2, NS, T, HEAD_V), BF16),
                        pltpu.VMEM((2, NS, 1, T), F32),
                        pltpu.VMEM((2, NS, HEAD_V + ONES_ROWS, T), F32)],
        compiler_params=_params(("arbitrary",)),
        name="diff_attn",
    )(lamv, subg.reshape(1, HEAD_V), qkv, qkv, qkv, bias_tiles, *weights)
    return out[0], out[1:]


def _mixffn_kernel(x_ref, ada_ref, a_ref, u_ref, gate_ref, woa_ref, woc_ref, boc_ref,
                   wout_ref, g2_ref, w1_ref, w2_ref, gf_ref, o_ref,
                   x1_ref, h2_ref, act_ref):
    D = D_MODEL
    a = jnp.dot(a_ref[...], woa_ref[...], preferred_element_type=F32)
    cv = jnp.dot(u_ref[...], woc_ref[...], preferred_element_type=F32) + boc_ref[...]
    gate = gate_ref[...]
    y = (_sigmoid(gate[:, :D].astype(F32)) * a
         + _sigmoid(gate[:, D:].astype(F32)) * cv)
    yb = y.astype(BF16)
    tm = yb.shape[0]
    halves = [(r, r + tm // 2) for r in (0, tm // 2)]
    for r0, r1 in halves:
        z = jnp.dot(yb[r0:r1], wout_ref[...], preferred_element_type=F32)
        x1 = x_ref[r0:r1, :] + ada_ref[2:3, :] * z
        x1_ref[r0:r1, :] = x1
        ms = jnp.mean(x1 * x1, axis=-1, keepdims=True)
        yn = x1 * lax.rsqrt(ms + EPS)
        h2_ref[r0:r1, :] = ((yn * g2_ref[...]) * (1.0 + ada_ref[4:5, :])
                            + ada_ref[3:4, :]).astype(BF16)

    def ffn_chunk(c, r0, r1):
        h = h2_ref[r0:r1, :]
        fg = jnp.dot(h, w1_ref[:, c:c + FF_CHUNK], preferred_element_type=F32)
        fu = jnp.dot(h, w1_ref[:, D_FF + c:D_FF + c + FF_CHUNK],
                     preferred_element_type=F32)
        act_ref[r0:r1, c:c + FF_CHUNK] = (_silu(fg) * fu).astype(BF16)

    for r0, r1 in halves:
        ffn_chunk(0, r0, r1)
    for c in range(FF_CHUNK, D_FF, FF_CHUNK):
        ffn_chunk(c, 0, tm)
    for r0, r1 in halves:
        z2 = jnp.dot(act_ref[r0:r1, :], w2_ref[...], preferred_element_type=F32)
        x2 = x1_ref[r0:r1, :] + ada_ref[5:6, :] * z2
        ms2 = jnp.mean(x2 * x2, axis=-1, keepdims=True)
        o_ref[r0:r1, :] = (x2 * lax.rsqrt(ms2 + EPS)) * gf_ref[...]


def _mix_ffn(x, ada3, a_n, u, gates, woa, woc, boc, wout, g2, w1, w2, gf):
    bsz, s, d = x.shape
    tm = ROW_TILE
    row = lambda n: pl.BlockSpec((None, tm, n), lambda b, i: (b, i, 0))
    return pl.pallas_call(
        _mixffn_kernel,
        grid=(bsz, s // tm),
        in_specs=[row(d),
                  pl.BlockSpec((None, 6, d), lambda b, i: (b, 0, 0)),
                  row(ATTN_WIDTH), row(CONV_WIDTH), row(GATE_COLS),
                  _resident((ATTN_WIDTH, d)), _resident((CONV_WIDTH, d)),
                  _resident((1, d)), _resident((d, d)), _resident((1, d)),
                  _resident((d, 2 * D_FF)), _resident((D_FF, d)), _resident((1, d))],
        out_specs=row(d),
        out_shape=jax.ShapeDtypeStruct((bsz, s, d), F32),
        scratch_shapes=[pltpu.VMEM((tm, d), F32), pltpu.VMEM((tm, d), BF16),
                        pltpu.VMEM((tm, D_FF), BF16)],
        compiler_params=_params(("parallel", "arbitrary")),
        name="mix_ffn",
    )(x, ada3, a_n, u, gates, woa, woc, boc.reshape(1, d), wout, g2.reshape(1, d),
      w1, w2, gf.reshape(1, d))


def kernel(x, c, w_ada, b_ada, norm1_g, norm2_g, final_g, w_in, lambda_q1, lambda_k1,
           lambda_q2, lambda_k2, rel_bias, attn_sub_g, w_o_attn, conv_w, conv_b,
           conv_ln_g, conv_ln_b, w_o_conv, b_o_conv, w_out, w_ffn_in, w_ffn_out):
    bsz, s, d = x.shape
    assert w_ada.shape[0] == 1, "single-layer block"
    assert (d, s % ROW_TILE, s % ATTN_TILE) == (D_MODEL, 0, 0)
    assert ATTN_TILE == 2 * MAX_DISTANCE
    assert PROJ_CHUNK == CONV_WIDTH and ATTN_WIDTH % PROJ_CHUNK == 0
    l = 0
    ada, w_in_bf, bias_tiles = _prologue(c, w_ada, b_ada, w_in, rel_bias)
    ada3 = ada.reshape(bsz, 6, d)
    qkv, u, gates = _in_proj(x, ada3, norm1_g[l], w_in_bf, conv_w[l],
                             conv_b[l], conv_ln_g[l], conv_ln_b[l])
    lamv = jnp.stack([lambda_q1[l], lambda_k1[l], lambda_q2[l], lambda_k2[l]])
    a_n, (woa, woc, wout, w1, w2) = _attention(
        qkv, bias_tiles, lamv, attn_sub_g[l],
        [w_o_attn, w_o_conv, w_out, w_ffn_in, w_ffn_out])
    return _mix_ffn(x, ada3, a_n, u, gates, woa, woc, b_o_conv[l], wout, norm2_g[l],
                    w1, w2, final_g)
```

```python
import math

import jax
import jax.numpy as jnp
from jax import lax
from jax.experimental import pallas as pl
from jax.experimental.pallas import tpu as pltpu

D_MODEL = 1024
N_HEADS_A = 4
HEAD_QK = 64
HEAD_V = 2 * HEAD_QK
ATTN_WIDTH = N_HEADS_A * HEAD_V
CONV_WIDTH = 512
CONV_KERNEL = 31
QKV_COLS = 3 * ATTN_WIDTH
GLU_COLS = 2 * CONV_WIDTH
GATE_COLS = 2 * D_MODEL
IN_COLS = QKV_COLS + GLU_COLS + GATE_COLS
D_FF = 2816
N_BUCKETS = 32
MAX_EXACT = 16
MAX_DISTANCE = 128
EPS = 1e-6
NEG_INF = -1e30
LAM_INIT = 0.8 - 0.6 * math.exp(-0.3 * 0)

V7X_VMEM_BYTES = 64 * 1024 * 1024
VMEM_LIMIT = V7X_VMEM_BYTES - 12 * 1024 * 1024

ROW_TILE = 512
ATTN_TILE = 256
ATTN_LAG = 4
ONES_ROWS = 16
CAST_ROWS = 16
LOG2E = math.log2(math.e)
Q_SCALE = HEAD_QK ** -0.5 * LOG2E
CONV_HALO = 32
CONV_CHUNK = 16
PROJ_CHUNK = 512
FF_CHUNK = 256

BF16 = jnp.bfloat16
F32 = jnp.float32


def _sigmoid(x):
    return 1.0 / (1.0 + jnp.exp2(x * (-LOG2E)))


def _silu(x):
    return x * _sigmoid(x)


def _params(sem):
    return pltpu.CompilerParams(dimension_semantics=sem, vmem_limit_bytes=VMEM_LIMIT)


def _resident(shape):
    nd = len(shape)
    return pl.BlockSpec(shape, lambda *_: (0,) * nd, pipeline_mode=pl.Buffered(1))


def _bias_tile(rel_ref, h, t):
    T = ATTN_TILE
    key = lax.broadcasted_iota(jnp.int32, (T, T), 0)
    qry = lax.broadcasted_iota(jnp.int32, (T, T), 1)
    dist = (1 - t) * T + qry - key
    n = jnp.maximum(dist, 0)
    large = MAX_EXACT + (jnp.log(jnp.maximum(n, 1).astype(F32) / MAX_EXACT)
                         / math.log(MAX_DISTANCE / MAX_EXACT)
                         * (N_BUCKETS - MAX_EXACT)).astype(jnp.int32)
    large = jnp.minimum(large, N_BUCKETS - 1)
    bucket = jnp.where(n < MAX_EXACT, n, large)
    far = rel_ref[N_BUCKETS - 1, h]
    bias = jnp.zeros((T, T), F32)
    for b in range(N_BUCKETS - 1):
        bias = jnp.where(bucket == b, (rel_ref[b, h] - far) * LOG2E, bias)
    return jnp.where(dist >= 0, bias, NEG_INF)


def _prologue_kernel(rel_ref, c_ref, w_ref, b_ref, win_ref, o_ref, win_bf_ref, bias_ref):
    j = pl.program_id(0)
    ca = _silu(c_ref[...])
    o_ref[...] = jnp.dot(ca.astype(BF16), w_ref[...].astype(BF16),
                         preferred_element_type=F32) + b_ref[...]
    win_bf_ref[...] = win_ref[...].astype(BF16)
    bias_ref[...] = _bias_tile(rel_ref, j // 2, j % 2)


def _prologue(c, w, b, w_in, rel_bias):
    bsz, d = c.shape
    n = w.shape[-1]
    steps = 2 * N_HEADS_A
    tn = n // steps
    _, rows, cols = w_in.shape
    tr = rows // steps
    T = ATTN_TILE
    return pl.pallas_call(
        _prologue_kernel,
        grid=(steps,),
        in_specs=[pl.BlockSpec(memory_space=pltpu.SMEM),
                  pl.BlockSpec((bsz, d), lambda j: (0, 0)),
                  pl.BlockSpec((None, d, tn), lambda j: (0, 0, j)),
                  pl.BlockSpec((1, tn), lambda j: (0, j)),
                  pl.BlockSpec((None, tr, cols), lambda j: (0, j, 0))],
        out_specs=[pl.BlockSpec((bsz, tn), lambda j: (0, j)),
                   pl.BlockSpec((tr, cols), lambda j: (j, 0)),
                   pl.BlockSpec((None, None, T, T), lambda j: (j // 2, j % 2, 0, 0))],
        out_shape=[jax.ShapeDtypeStruct((bsz, n), F32),
                   jax.ShapeDtypeStruct((rows, cols), BF16),
                   jax.ShapeDtypeStruct((N_HEADS_A, 2, T, T), F32)],
        compiler_params=_params(("arbitrary",)),
        name="prologue",
    )(rel_bias, c, w, b, w_in)


def _conv_chunk(r0, ubuf, w_ref, cb_ref, lng_ref, lnb_ref, o_ref):
    off = CONV_HALO - (CONV_KERNEL - 1)
    acc = jnp.zeros((CONV_CHUNK // 8, 8, CONV_WIDTH), F32)
    for k in range(CONV_KERNEL):
        a, b = divmod(k + off, 8)
        rows = ubuf[b, r0 + 8 * a:r0 + 8 * a + CONV_CHUNK, :]
        acc = acc + rows.reshape(acc.shape) * w_ref[k][None]
    u = acc.reshape(CONV_CHUNK, CONV_WIDTH) + cb_ref[...]
    mu = jnp.mean(u, axis=-1, keepdims=True)
    d = u - mu
    var = jnp.mean(d * d, axis=-1, keepdims=True)
    y = d * lax.rsqrt(var + EPS) * lng_ref[...] + lnb_ref[...]
    o_ref[r0:r0 + CONV_CHUNK, :] = _silu(y).astype(BF16)


def _inproj_kernel(x_ref, ada_ref, g_ref, w_ref, cw_ref, cb_ref, lng_ref, lnb_ref,
                   qkv_ref, u_ref, gate_ref, ubuf, hb_ref):
    TS, H, C = ROW_TILE, CONV_HALO, CONV_WIDTH
    i = pl.program_id(1)

    @pl.when(i == 0)
    def _():
        ubuf[0, 0:H, :] = jnp.zeros((H, C), F32)

    @pl.when(i > 0)
    def _():
        ubuf[0, 0:H, :] = ubuf[0, TS:TS + H, :]

    for r0 in (0, TS // 2):
        r1 = r0 + TS // 2
        x = x_ref[r0:r1, :]
        ms = jnp.mean(x * x, axis=-1, keepdims=True)
        y = x * lax.rsqrt(ms + EPS)
        h = (y * g_ref[...]) * (1.0 + ada_ref[1:2, :]) + ada_ref[0:1, :]
        hb_ref[r0:r1, :] = h.astype(BF16)
        hh = hb_ref[r0:r1, :]
        glu = [jnp.dot(hh, w_ref[:, base:base + C], preferred_element_type=F32)
               for base in (QKV_COLS, QKV_COLS + C)]
        ubuf[0, H + r0:H + r1, :] = glu[0] * _sigmoid(glu[1])
    hb = hb_ref[...]

    def proj(base):
        return jnp.dot(hb, w_ref[:, base:base + PROJ_CHUNK], preferred_element_type=F32)

    n = TS + H - 8
    for b in range(1, 8):
        ubuf[b, 0:n, :] = ubuf[0, b:b + n, :]
    jobs = ([(qkv_ref, c, c) for c in range(0, QKV_COLS, PROJ_CHUNK)]
            + [(gate_ref, c, QKV_COLS + GLU_COLS + c) for c in range(0, GATE_COLS, PROJ_CHUNK)])
    rows = list(range(0, TS, CONV_CHUNK))
    for n, (ref, c, base) in enumerate(jobs):
        r = proj(base)
        if base < ATTN_WIDTH:
            r = r * Q_SCALE
        ref[:, c:c + PROJ_CHUNK] = r.astype(BF16)
        for r0 in rows[n * len(rows) // len(jobs):(n + 1) * len(rows) // len(jobs)]:
            _conv_chunk(r0, ubuf, cw_ref, cb_ref, lng_ref, lnb_ref, u_ref)


def _in_proj(x, ada3, g, w, cw, cb, lng, lnb):
    bsz, s, d = x.shape
    tm = ROW_TILE
    C = CONV_WIDTH
    row = lambda n: pl.BlockSpec((None, tm, n), lambda b, i: (b, i, 0))
    vec = lambda a: a.reshape(1, C)
    return pl.pallas_call(
        _inproj_kernel,
        grid=(bsz, s // tm),
        in_specs=[row(d),
                  pl.BlockSpec((None, 6, d), lambda b, i: (b, 0, 0)),
                  _resident((1, d)),
                  _resident((d, IN_COLS)),
                  _resident((CONV_KERNEL, 8, C)),
                  _resident((1, C)), _resident((1, C)), _resident((1, C))],
        out_specs=[row(QKV_COLS), row(C), row(GATE_COLS)],
        out_shape=[jax.ShapeDtypeStruct((bsz, s, QKV_COLS), BF16),
                   jax.ShapeDtypeStruct((bsz, s, C), BF16),
                   jax.ShapeDtypeStruct((bsz, s, GATE_COLS), BF16)],
        scratch_shapes=[pltpu.VMEM((8, tm + CONV_HALO, C), F32), pltpu.VMEM((tm, d), BF16)],
        compiler_params=_params(("arbitrary", "arbitrary")),
        name="in_proj",
    )(x, ada3, g.reshape(1, d), w, jnp.broadcast_to(cw[:, None, :], (CONV_KERNEL, 8, C)),
      vec(cb), vec(lng), vec(lnb))


def _attn_kernel(lamv_ref, subg_ref, q_ref, k_ref, v_ref, bias_ref, *rest):
    n_cast = (len(rest) - 5) // 2
    cast_in, o_ref, cast_out = rest[:n_cast], rest[n_cast], rest[n_cast + 1:2 * n_cast + 1]
    vt_ref, qz_ref, m_ref, a_ref = rest[2 * n_cast + 1:]
    T = ATTN_TILE
    NS = 2 * N_HEADS_A
    n_q = q_ref.shape[0] // T
    hcols = lambda h: slice(h * HEAD_V, (h + 1) * HEAD_V)
    rows = lambda t: slice(t * T, (t + 1) * T)

    vt = v_ref[...].T
    for h in range(N_HEADS_A):
        vt_ref[h, 0:HEAD_V, :] = vt[hcols(h), :]
        vt_ref[h, HEAD_V:, :] = jnp.ones((ONES_ROWS, vt.shape[1]), BF16)

    lv = lamv_ref[...]
    lam = (jnp.exp(jnp.sum(lv[0:1] * lv[1:2], axis=-1, keepdims=True))
           - jnp.exp(jnp.sum(lv[2:3] * lv[3:4], axis=-1, keepdims=True)) + LAM_INIT)
    lane = lax.broadcasted_iota(jnp.int32, (T, HEAD_V), 1)

    def prepare(i):
        slot = i % 2
        for h in range(N_HEADS_A):
            qs = q_ref[rows(i), hcols(h)]
            zero = jnp.zeros_like(qs)
            qz_ref[slot, 2 * h] = jnp.where(lane < HEAD_QK, qs, zero)
            qz_ref[slot, 2 * h + 1] = jnp.where(lane >= HEAD_QK, qs, zero)

    def finalize(i):
        slot = i % 2
        for h in range(N_HEADS_A):
            a1, a2 = a_ref[slot, 2 * h], a_ref[slot, 2 * h + 1]
            o = (a1[:HEAD_V] / a1[HEAD_V:HEAD_V + 1]
                 - lam * (a2[:HEAD_V] / a2[HEAD_V:HEAD_V + 1]))
            ms = jnp.mean(o * o, axis=0, keepdims=True)
            y = (o * lax.rsqrt(ms + EPS)).T
            o_ref[rows(i), hcols(h)] = ((y * subg_ref[...]) * (1.0 - LAM_INIT)).astype(BF16)

    units = [(i, j, n) for i in range(n_q) for j in range(i + 1) for n in range(NS)]
    scores, probs, alphas = {}, {}, {}

    def score(u):
        i, j, n = units[u]
        kc = k_ref[rows(j), hcols(n // 2)]
        scores[u] = lax.dot_general(kc, qz_ref[i % 2, n], (((1,), (1,)), ((), ())),
                                    preferred_element_type=F32)

    def softmax(u):
        i, j, n = units[u]
        s = scores.pop(u)
        h, near = n // 2, MAX_DISTANCE
        on_diag = j == i
        if j == i - 1:
            corner = s[near:, :near] + bias_ref[h, 0, near:, :near]
            s = jnp.concatenate(
                [s[:near], jnp.concatenate([corner, s[near:, near:]], axis=1)], axis=0)
        if on_diag:
            s_lo = s[:near, :near] + bias_ref[h, 1, :near, :near]
            s_hi = s[:, near:] + bias_ref[h, 1, :, near:]
            m_new = jnp.concatenate([jnp.max(s_lo, axis=0, keepdims=True),
                                     jnp.max(s_hi, axis=0, keepdims=True)], axis=1)
        else:
            m_new = jnp.max(s, axis=0, keepdims=True)
        if j > 0:
            m_old = m_ref[i % 2, n]
            m_new = jnp.maximum(m_old, m_new)
            alphas[u] = jnp.exp2(m_old - m_new)
        if on_diag:
            p_lo = jnp.exp2(s_lo - m_new[:, :near]).astype(BF16)
            p_hi = jnp.exp2(s_hi - m_new[:, near:]).astype(BF16)
            p_lo = jnp.concatenate([p_lo, jnp.zeros((T - near, near), BF16)], axis=0)
            probs[u] = jnp.concatenate([p_lo, p_hi], axis=1)
        else:
            probs[u] = jnp.exp2(s - m_new).astype(BF16)
        m_ref[i % 2, n] = m_new

    def value(u):
        i, j, n = units[u]
        vc = vt_ref[n // 2, :, rows(j)]
        pv = jnp.dot(vc, probs.pop(u), preferred_element_type=F32)
        a_ref[i % 2, n] = pv if j == 0 else pv + alphas.pop(u) * a_ref[i % 2, n]

    for step in range(len(units) + 2 * ATTN_LAG):
        if step < len(units):
            if units[step][1:] == (0, 0):
                prepare(units[step][0])
            score(step)
        if 0 <= step - ATTN_LAG < len(units):
            softmax(step - ATTN_LAG)
        done = step - 2 * ATTN_LAG
        if 0 <= done < len(units):
            value(done)
            i, j, n = units[done]
            if (j, n) == (i, NS - 1):
                finalize(i)

    for w_ref, wb_ref in zip(cast_in, cast_out):
        wb_ref[...] = w_ref[...].astype(BF16)


def _attention(qkv, bias_tiles, lamv, subg, weights):
    bsz, s, _ = qkv.shape
    T = ATTN_TILE
    W = ATTN_WIDTH
    NS = 2 * N_HEADS_A
    steps = bsz

    def slab_spec(w, squeeze):
        rows, cols = w.shape[-2:]
        assert rows % (steps * CAST_ROWS) == 0
        block = (rows // steps, cols)
        if squeeze:
            return pl.BlockSpec((None,) + block, lambda b: (0, b, 0))
        return pl.BlockSpec(block, lambda b: (b, 0))

    seq = lambda col: pl.BlockSpec((None, s, W), lambda b: (b, 0, col))
    out = pl.pallas_call(
        _attn_kernel,
        grid=(bsz,),
        in_specs=[_resident((4, HEAD_QK)),
                  _resident((1, HEAD_V)),
                  seq(0), seq(1), seq(2),
                  _resident((N_HEADS_A, 2, T, T))] + [slab_spec(w, True) for w in weights],
        out_specs=[seq(0)] + [slab_spec(w, False) for w in weights],
        out_shape=[jax.ShapeDtypeStruct((bsz, s, W), BF16)]
        + [jax.ShapeDtypeStruct(w.shape[-2:], BF16) for w in weights],
        scratch_shapes=[pltpu.VMEM((N_HEADS_A, HEAD_V + ONES_ROWS, s), BF16),
                        pltpu.VMEM((2, NS, T, HEAD_V), BF16),
                        pltpu.VMEM((2, NS, 1, T), F32),
                        pltpu.VMEM((2, NS, HEAD_V + ONES_ROWS, T), F32)],
        compiler_params=_params(("arbitrary",)),
        name="diff_attn",
    )(lamv, subg.reshape(1, HEAD_V), qkv, qkv, qkv, bias_tiles, *weights)
    return out[0], out[1:]


def _mixffn_kernel(x_ref, ada_ref, a_ref, u_ref, gate_ref, woa_ref, woc_ref, boc_ref,
                   wout_ref, g2_ref, w1_ref, w2_ref, gf_ref, o_ref,
                   x1_ref, h2_ref, act_ref):
    D = D_MODEL
    a = jnp.dot(a_ref[...], woa_ref[...], preferred_element_type=F32)
    cv = jnp.dot(u_ref[...], woc_ref[...], preferred_element_type=F32) + boc_ref[...]
    gate = gate_ref[...]
    y = (_sigmoid(gate[:, :D].astype(F32)) * a
         + _sigmoid(gate[:, D:].astype(F32)) * cv)
    yb = y.astype(BF16)
    tm = yb.shape[0]
    halves = [(r, r + tm // 2) for r in (0, tm // 2)]
    for r0, r1 in halves:
        z = jnp.dot(yb[r0:r1], wout_ref[...], preferred_element_type=F32)
        x1 = x_ref[r0:r1, :] + ada_ref[2:3, :] * z
        x1_ref[r0:r1, :] = x1
        ms = jnp.mean(x1 * x1, axis=-1, keepdims=True)
        yn = x1 * lax.rsqrt(ms + EPS)
        h2_ref[r0:r1, :] = ((yn * g2_ref[...]) * (1.0 + ada_ref[4:5, :])
                            + ada_ref[3:4, :]).astype(BF16)

    def ffn_chunk(c, r0, r1):
        h = h2_ref[r0:r1, :]
        fg = jnp.dot(h, w1_ref[:, c:c + FF_CHUNK], preferred_element_type=F32)
        fu = jnp.dot(h, w1_ref[:, D_FF + c:D_FF + c + FF_CHUNK],
                     preferred_element_type=F32)
        act_ref[r0:r1, c:c + FF_CHUNK] = (_silu(fg) * fu).astype(BF16)

    for r0, r1 in halves:
        ffn_chunk(0, r0, r1)
    for c in range(FF_CHUNK, D_FF, FF_CHUNK):
        ffn_chunk(c, 0, tm)
    for r0, r1 in halves:
        z2 = jnp.dot(act_ref[r0:r1, :], w2_ref[...], preferred_element_type=F32)
        x2 = x1_ref[r0:r1, :] + ada_ref[5:6, :] * z2
        ms2 = jnp.mean(x2 * x2, axis=-1, keepdims=True)
        o_ref[r0:r1, :] = (x2 * lax.rsqrt(ms2 + EPS)) * gf_ref[...]


def _mix_ffn(x, ada3, a_n, u, gates, woa, woc, boc, wout, g2, w1, w2, gf):
    bsz, s, d = x.shape
    tm = ROW_TILE
    row = lambda n: pl.BlockSpec((None, tm, n), lambda b, i: (b, i, 0))
    return pl.pallas_call(
        _mixffn_kernel,
        grid=(bsz, s // tm),
        in_specs=[row(d),
                  pl.BlockSpec((None, 6, d), lambda b, i: (b, 0, 0)),
                  row(ATTN_WIDTH), row(CONV_WIDTH), row(GATE_COLS),
                  _resident((ATTN_WIDTH, d)), _resident((CONV_WIDTH, d)),
                  _resident((1, d)), _resident((d, d)), _resident((1, d)),
                  _resident((d, 2 * D_FF)), _resident((D_FF, d)), _resident((1, d))],
        out_specs=row(d),
        out_shape=jax.ShapeDtypeStruct((bsz, s, d), F32),
        scratch_shapes=[pltpu.VMEM((tm, d), F32), pltpu.VMEM((tm, d), BF16),
                        pltpu.VMEM((tm, D_FF), BF16)],
        compiler_params=_params(("parallel", "arbitrary")),
        name="mix_ffn",
    )(x, ada3, a_n, u, gates, woa, woc, boc.reshape(1, d), wout, g2.reshape(1, d),
      w1, w2, gf.reshape(1, d))


def kernel(x, c, w_ada, b_ada, norm1_g, norm2_g, final_g, w_in, lambda_q1, lambda_k1,
           lambda_q2, lambda_k2, rel_bias, attn_sub_g, w_o_attn, conv_w, conv_b,
           conv_ln_g, conv_ln_b, w_o_conv, b_o_conv, w_out, w_ffn_in, w_ffn_out):
    bsz, s, d = x.shape
    assert w_ada.shape[0] == 1, "single-layer block"
    assert (d, s % ROW_TILE, s % ATTN_TILE) == (D_MODEL, 0, 0)
    assert ATTN_TILE == 2 * MAX_DISTANCE
    assert PROJ_CHUNK == CONV_WIDTH and ATTN_WIDTH % PROJ_CHUNK == 0
    l = 0
    ada, w_in_bf, bias_tiles = _prologue(c, w_ada, b_ada, w_in, rel_bias)
    ada3 = ada.reshape(bsz, 6, d)
    qkv, u, gates = _in_proj(x, ada3, norm1_g[l], w_in_bf, conv_w[l],
                             conv_b[l], conv_ln_g[l], conv_ln_b[l])
    lamv = jnp.stack([lambda_q1[l], lambda_k1[l], lambda_q2[l], lambda_k2[l]])
    a_n, (woa, woc, wout, w1, w2) = _attention(
        qkv, bias_tiles, lamv, attn_sub_g[l],
        [w_o_attn, w_o_conv, w_out, w_ffn_in, w_ffn_out])
    return _mix_ffn(x, ada3, a_n, u, gates, woa, woc, b_o_conv[l], wout, norm2_g[l],
                    w1, w2, final_g)
```

```python
import math

import jax
import jax.numpy as jnp
from jax import lax
from jax.experimental import pallas as pl
from jax.experimental.pallas import tpu as pltpu

D_MODEL = 1024
N_HEADS_A = 4
HEAD_QK = 64
HEAD_V = 2 * HEAD_QK
ATTN_WIDTH = N_HEADS_A * HEAD_V
CONV_WIDTH = 512
CONV_KERNEL = 31
QKV_COLS = 3 * ATTN_WIDTH
GLU_COLS = 2 * CONV_WIDTH
GATE_COLS = 2 * D_MODEL
IN_COLS = QKV_COLS + GLU_COLS + GATE_COLS
D_FF = 2816
N_BUCKETS = 32
MAX_EXACT = 16
MAX_DISTANCE = 128
EPS = 1e-6
NEG_INF = -1e30
LAM_INIT = 0.8 - 0.6 * math.exp(-0.3 * 0)

V7X_VMEM_BYTES = 64 * 1024 * 1024
VMEM_LIMIT = V7X_VMEM_BYTES - 12 * 1024 * 1024

ROW_TILE = 512
ATTN_TILE = 256
ATTN_LAG = 4
ONES_ROWS = 16
CAST_ROWS = 16
LOG2E = math.log2(math.e)
Q_SCALE = HEAD_QK ** -0.5 * LOG2E
CONV_HALO = 32
CONV_CHUNK = 16
ANCHOR_ROWS = 8
PROJ_CHUNK = 512
FF_CHUNK = 256

BF16 = jnp.bfloat16
F32 = jnp.float32


def _sigmoid(x):
    return 1.0 / (1.0 + jnp.exp2(x * (-LOG2E)))


def _silu(x):
    return x * _sigmoid(x)


def _params(sem):
    return pltpu.CompilerParams(dimension_semantics=sem, vmem_limit_bytes=VMEM_LIMIT)


def _resident(shape):
    nd = len(shape)
    return pl.BlockSpec(shape, lambda *_: (0,) * nd, pipeline_mode=pl.Buffered(1))


def _bias_tile(rel_ref, h, t):
    T = ATTN_TILE
    key = lax.broadcasted_iota(jnp.int32, (T, T), 0)
    qry = lax.broadcasted_iota(jnp.int32, (T, T), 1)
    dist = (1 - t) * T + qry - key
    n = jnp.maximum(dist, 0)
    large = MAX_EXACT + (jnp.log(jnp.maximum(n, 1).astype(F32) / MAX_EXACT)
                         / math.log(MAX_DISTANCE / MAX_EXACT)
                         * (N_BUCKETS - MAX_EXACT)).astype(jnp.int32)
    large = jnp.minimum(large, N_BUCKETS - 1)
    bucket = jnp.where(n < MAX_EXACT, n, large)
    far = rel_ref[N_BUCKETS - 1, h]
    bias = jnp.zeros((T, T), F32)
    for b in range(N_BUCKETS - 1):
        bias = jnp.where(bucket == b, (rel_ref[b, h] - far) * LOG2E, bias)
    return jnp.where(dist >= 0, bias, NEG_INF)


def _prologue_kernel(rel_ref, c_ref, w_ref, b_ref, win_ref, o_ref, win_bf_ref, bias_ref):
    j = pl.program_id(0)
    ca = _silu(c_ref[...])
    o_ref[...] = jnp.dot(ca.astype(BF16), w_ref[...].astype(BF16),
                         preferred_element_type=F32) + b_ref[...]
    win_bf_ref[...] = win_ref[...].astype(BF16)
    bias_ref[...] = _bias_tile(rel_ref, j // 2, j % 2)


def _prologue(c, w, b, w_in, rel_bias):
    bsz, d = c.shape
    n = w.shape[-1]
    steps = 2 * N_HEADS_A
    tn = n // steps
    _, rows, cols = w_in.shape
    tr = rows // steps
    T = ATTN_TILE
    return pl.pallas_call(
        _prologue_kernel,
        grid=(steps,),
        in_specs=[pl.BlockSpec(memory_space=pltpu.SMEM),
                  pl.BlockSpec((bsz, d), lambda j: (0, 0)),
                  pl.BlockSpec((None, d, tn), lambda j: (0, 0, j)),
                  pl.BlockSpec((1, tn), lambda j: (0, j)),
                  pl.BlockSpec((None, tr, cols), lambda j: (0, j, 0))],
        out_specs=[pl.BlockSpec((bsz, tn), lambda j: (0, j)),
                   pl.BlockSpec((tr, cols), lambda j: (j, 0)),
                   pl.BlockSpec((None, None, T, T), lambda j: (j // 2, j % 2, 0, 0))],
        out_shape=[jax.ShapeDtypeStruct((bsz, n), F32),
                   jax.ShapeDtypeStruct((rows, cols), BF16),
                   jax.ShapeDtypeStruct((N_HEADS_A, 2, T, T), F32)],
        compiler_params=_params(("arbitrary",)),
        name="prologue",
    )(rel_bias, c, w, b, w_in)


def _conv_chunk(r0, zero, ubuf, w_ref, cb_ref, lng_ref, lnb_ref, o_ref):
    off = CONV_HALO - (CONV_KERNEL - 1)
    acc = jnp.zeros((CONV_CHUNK // 8, 8, CONV_WIDTH), F32)
    for k in range(CONV_KERNEL):
        a, b = divmod(k + off, 8)
        rows = ubuf[b, pl.ds(pl.multiple_of(zero + (r0 + 8 * a), 8), CONV_CHUNK), :]
        acc = acc + rows.reshape(acc.shape) * w_ref[k][None]
    u = acc.reshape(CONV_CHUNK, CONV_WIDTH) + cb_ref[...]
    mu = jnp.mean(u, axis=-1, keepdims=True)
    d = u - mu
    var = jnp.mean(d * d, axis=-1, keepdims=True)
    y = d * lax.rsqrt(var + EPS) * lng_ref[...] + lnb_ref[...]
    o_ref[r0:r0 + CONV_CHUNK, :] = _silu(y).astype(BF16)


def _inproj_kernel(x_ref, ada_ref, g_ref, w_ref, cw_ref, cb_ref, lng_ref, lnb_ref,
                   qkv_ref, u_ref, gate_ref, ubuf):
    TS, H, C = ROW_TILE, CONV_HALO, CONV_WIDTH
    i = pl.program_id(1)

    @pl.when(i == 0)
    def _():
        ubuf[0, 0:H, :] = jnp.zeros((H, C), F32)

    @pl.when(i > 0)
    def _():
        ubuf[0, 0:H, :] = ubuf[0, TS:TS + H, :]

    x = x_ref[...]
    ms = jnp.mean(x * x, axis=-1, keepdims=True)
    y = x * lax.rsqrt(ms + EPS)
    h = (y * g_ref[...]) * (1.0 + ada_ref[1:2, :]) + ada_ref[0:1, :]
    hb = h.astype(BF16)

    def proj(base):
        return jnp.dot(hb, w_ref[:, base:base + PROJ_CHUNK], preferred_element_type=F32)

    ubuf[0, H:H + TS, :] = proj(QKV_COLS) * _sigmoid(proj(QKV_COLS + C))
    n = TS + H - 8
    for b in range(1, 8):
        ubuf[b, 0:n, :] = ubuf[0, b:b + n, :]
    jobs = ([(qkv_ref, c, c) for c in range(0, QKV_COLS, PROJ_CHUNK)]
            + [(gate_ref, c, QKV_COLS + GLU_COLS + c) for c in range(0, GATE_COLS, PROJ_CHUNK)])
    rows = list(range(0, TS, CONV_CHUNK))
    zero = jnp.minimum(i, 0)
    for n, (ref, c, base) in enumerate(jobs):
        for r0 in rows[n * len(rows) // len(jobs):(n + 1) * len(rows) // len(jobs)]:
            _conv_chunk(r0, zero, ubuf, cw_ref, cb_ref, lng_ref, lnb_ref, u_ref)
        r = proj(base)
        if base < ATTN_WIDTH:
            r = r * Q_SCALE
        ref[:, c:c + PROJ_CHUNK] = r.astype(BF16)
        ubuf[0, TS + H:TS + H + ANCHOR_ROWS, :] = r[TS - ANCHOR_ROWS:, :]


def _in_proj(x, ada3, g, w, cw, cb, lng, lnb):
    bsz, s, d = x.shape
    tm = ROW_TILE
    C = CONV_WIDTH
    row = lambda n: pl.BlockSpec((None, tm, n), lambda b, i: (b, i, 0))
    vec = lambda a: a.reshape(1, C)
    return pl.pallas_call(
        _inproj_kernel,
        grid=(bsz, s // tm),
        in_specs=[row(d),
                  pl.BlockSpec((None, 6, d), lambda b, i: (b, 0, 0)),
                  _resident((1, d)),
                  _resident((d, IN_COLS)),
                  _resident((CONV_KERNEL, 8, C)),
                  _resident((1, C)), _resident((1, C)), _resident((1, C))],
        out_specs=[row(QKV_COLS), row(C), row(GATE_COLS)],
        out_shape=[jax.ShapeDtypeStruct((bsz, s, QKV_COLS), BF16),
                   jax.ShapeDtypeStruct((bsz, s, C), BF16),
                   jax.ShapeDtypeStruct((bsz, s, GATE_COLS), BF16)],
        scratch_shapes=[pltpu.VMEM((8, tm + CONV_HALO + ANCHOR_ROWS, C), F32)],
        compiler_params=_params(("arbitrary", "arbitrary")),
        name="in_proj",
    )(x, ada3, g.reshape(1, d), w, jnp.broadcast_to(cw[:, None, :], (CONV_KERNEL, 8, C)),
      vec(cb), vec(lng), vec(lnb))


def _attn_kernel(lamv_ref, subg_ref, q_ref, k_ref, v_ref, bias_ref, *rest):
    n_cast = (len(rest) - 5) // 2
    cast_in, o_ref, cast_out = rest[:n_cast], rest[n_cast], rest[n_cast + 1:2 * n_cast + 1]
    vt_ref, qz_ref, m_ref, a_ref = rest[2 * n_cast + 1:]
    T = ATTN_TILE
    NS = 2 * N_HEADS_A
    n_q = q_ref.shape[0] // T
    hcols = lambda h: slice(h * HEAD_V, (h + 1) * HEAD_V)
    rows = lambda t: slice(t * T, (t + 1) * T)

    vt = v_ref[...].T
    for h in range(N_HEADS_A):
        vt_ref[h, 0:HEAD_V, :] = vt[hcols(h), :]
        vt_ref[h, HEAD_V:, :] = jnp.ones((ONES_ROWS, vt.shape[1]), BF16)

    lv = lamv_ref[...]
    lam = (jnp.exp(jnp.sum(lv[0:1] * lv[1:2], axis=-1, keepdims=True))
           - jnp.exp(jnp.sum(lv[2:3] * lv[3:4], axis=-1, keepdims=True)) + LAM_INIT)
    lane = lax.broadcasted_iota(jnp.int32, (T, HEAD_V), 1)

    def prepare(i):
        slot = i % 2
        for h in range(N_HEADS_A):
            qs = q_ref[rows(i), hcols(h)]
            zero = jnp.zeros_like(qs)
            qz_ref[slot, 2 * h] = jnp.where(lane < HEAD_QK, qs, zero)
            qz_ref[slot, 2 * h + 1] = jnp.where(lane >= HEAD_QK, qs, zero)

    def finalize(i):
        slot = i % 2
        for h in range(N_HEADS_A):
            a1, a2 = a_ref[slot, 2 * h], a_ref[slot, 2 * h + 1]
            o = (a1[:HEAD_V] / a1[HEAD_V:HEAD_V + 1]
                 - lam * (a2[:HEAD_V] / a2[HEAD_V:HEAD_V + 1]))
            ms = jnp.mean(o * o, axis=0, keepdims=True)
            y = (o * lax.rsqrt(ms + EPS)).T
            o_ref[rows(i), hcols(h)] = ((y * subg_ref[...]) * (1.0 - LAM_INIT)).astype(BF16)

    units = [(i, j, n) for i in range(n_q) for j in range(i + 1) for n in range(NS)]
    scores, probs, alphas = {}, {}, {}

    def score(u):
        i, j, n = units[u]
        kc = k_ref[rows(j), hcols(n // 2)]
        scores[u] = lax.dot_general(kc, qz_ref[i % 2, n], (((1,), (1,)), ((), ())),
                                    preferred_element_type=F32)

    def softmax(u):
        i, j, n = units[u]
        s = scores.pop(u)
        h, near = n // 2, MAX_DISTANCE
        on_diag = j == i
        if j == i - 1:
            corner = s[near:, :near] + bias_ref[h, 0, near:, :near]
            s = jnp.concatenate(
                [s[:near], jnp.concatenate([corner, s[near:, near:]], axis=1)], axis=0)
        if on_diag:
            s_lo = s[:near, :near] + bias_ref[h, 1, :near, :near]
            s_hi = s[:, near:] + bias_ref[h, 1, :, near:]
            m_new = jnp.concatenate([jnp.max(s_lo, axis=0, keepdims=True),
                                     jnp.max(s_hi, axis=0, keepdims=True)], axis=1)
        else:
            m_new = jnp.max(s, axis=0, keepdims=True)
        if j > 0:
            m_old = m_ref[i % 2, n]
            m_new = jnp.maximum(m_old, m_new)
            alphas[u] = jnp.exp2(m_old - m_new)
        if on_diag:
            p_lo = jnp.exp2(s_lo - m_new[:, :near]).astype(BF16)
            p_hi = jnp.exp2(s_hi - m_new[:, near:]).astype(BF16)
            p_lo = jnp.concatenate([p_lo, jnp.zeros((T - near, near), BF16)], axis=0)
            probs[u] = jnp.concatenate([p_lo, p_hi], axis=1)
        else:
            probs[u] = jnp.exp2(s - m_new).astype(BF16)
        m_ref[i % 2, n] = m_new

    def value(u):
        i, j, n = units[u]
        vc = vt_ref[n // 2, :, rows(j)]
        pv = jnp.dot(vc, probs.pop(u), preferred_element_type=F32)
        a_ref[i % 2, n] = pv if j == 0 else pv + alphas.pop(u) * a_ref[i % 2, n]

    for step in range(len(units) + 2 * ATTN_LAG):
        if step < len(units):
            if units[step][1:] == (0, 0):
                prepare(units[step][0])
            score(step)
        if 0 <= step - ATTN_LAG < len(units):
            softmax(step - ATTN_LAG)
        done = step - 2 * ATTN_LAG
        if 0 <= done < len(units):
            value(done)
            i, j, n = units[done]
            if (j, n) == (i, NS - 1):
                finalize(i)

    for w_ref, wb_ref in zip(cast_in, cast_out):
        wb_ref[...] = w_ref[...].astype(BF16)


def _attention(qkv, bias_tiles, lamv, subg, weights):
    bsz, s, _ = qkv.shape
    T = ATTN_TILE
    W = ATTN_WIDTH
    NS = 2 * N_HEADS_A
    steps = bsz

    def slab_spec(w, squeeze):
        rows, cols = w.shape[-2:]
        assert rows % (steps * CAST_ROWS) == 0
        block = (rows // steps, cols)
        if squeeze:
            return pl.BlockSpec((None,) + block, lambda b: (0, b, 0))
        return pl.BlockSpec(block, lambda b: (b, 0))

    seq = lambda col: pl.BlockSpec((None, s, W), lambda b: (b, 0, col))
    out = pl.pallas_call(
        _attn_kernel,
        grid=(bsz,),
        in_specs=[_resident((4, HEAD_QK)),
                  _resident((1, HEAD_V)),
                  seq(0), seq(1), seq(2),
                  _resident((N_HEADS_A, 2, T, T))] + [slab_spec(w, True) for w in weights],
        out_specs=[seq(0)] + [slab_spec(w, False) for w in weights],
        out_shape=[jax.ShapeDtypeStruct((bsz, s, W), BF16)]
        + [jax.ShapeDtypeStruct(w.shape[-2:], BF16) for w in weights],
        scratch_shapes=[pltpu.VMEM((N_HEADS_A, HEAD_V + ONES_ROWS, s), BF16),
                        pltpu.VMEM((2, NS, T, HEAD_V), BF16),
                        pltpu.VMEM((2, NS, 1, T), F32),
                        pltpu.VMEM((2, NS, HEAD_V + ONES_ROWS, T), F32)],
        compiler_params=_params(("arbitrary",)),
        name="diff_attn",
    )(lamv, subg.reshape(1, HEAD_V), qkv, qkv, qkv, bias_tiles, *weights)
    return out[0], out[1:]


def _mixffn_kernel(x_ref, ada_ref, a_ref, u_ref, gate_ref, woa_ref, woc_ref, boc_ref,
                   wout_ref, g2_ref, w1_ref, w2_ref, gf_ref, o_ref,
                   x1_ref, h2_ref, act_ref):
    D = D_MODEL
    a = jnp.dot(a_ref[...], woa_ref[...], preferred_element_type=F32)
    cv = jnp.dot(u_ref[...], woc_ref[...], preferred_element_type=F32) + boc_ref[...]
    gate = gate_ref[...]
    y = (_sigmoid(gate[:, :D].astype(F32)) * a
         + _sigmoid(gate[:, D:].astype(F32)) * cv)
    yb = y.astype(BF16)
    tm = yb.shape[0]
    halves = [(r, r + tm // 2) for r in (0, tm // 2)]
    for r0, r1 in halves:
        z = jnp.dot(yb[r0:r1], wout_ref[...], preferred_element_type=F32)
        x1 = x_ref[r0:r1, :] + ada_ref[2:3, :] * z
        x1_ref[r0:r1, :] = x1
        ms = jnp.mean(x1 * x1, axis=-1, keepdims=True)
        yn = x1 * lax.rsqrt(ms + EPS)
        h2_ref[r0:r1, :] = ((yn * g2_ref[...]) * (1.0 + ada_ref[4:5, :])
                            + ada_ref[3:4, :]).astype(BF16)

    def ffn_chunk(c, r0, r1):
        h = h2_ref[r0:r1, :]
        fg = jnp.dot(h, w1_ref[:, c:c + FF_CHUNK], preferred_element_type=F32)
        fu = jnp.dot(h, w1_ref[:, D_FF + c:D_FF + c + FF_CHUNK],
                     preferred_element_type=F32)
        act_ref[r0:r1, c:c + FF_CHUNK] = (_silu(fg) * fu).astype(BF16)

    for r0, r1 in halves:
        ffn_chunk(0, r0, r1)
    for c in range(FF_CHUNK, D_FF, FF_CHUNK):
        ffn_chunk(c, 0, tm)
    for r0, r1 in halves:
        z2 = jnp.dot(act_ref[r0:r1, :], w2_ref[...], preferred_element_type=F32)
        x2 = x1_ref[r0:r1, :] + ada_ref[5:6, :] * z2
        ms2 = jnp.mean(x2 * x2, axis=-1, keepdims=True)
        o_ref[r0:r1, :] = (x2 * lax.rsqrt(ms2 + EPS)) * gf_ref[...]


def _mix_ffn(x, ada3, a_n, u, gates, woa, woc, boc, wout, g2, w1, w2, gf):
    bsz, s, d = x.shape
    tm = ROW_TILE
    row = lambda n: pl.BlockSpec((None, tm, n), lambda b, i: (b, i, 0))
    return pl.pallas_call(
        _mixffn_kernel,
        grid=(bsz, s // tm),
        in_specs=[row(d),
                  pl.BlockSpec((None, 6, d), lambda b, i: (b, 0, 0)),
                  row(ATTN_WIDTH), row(CONV_WIDTH), row(GATE_COLS),
                  _resident((ATTN_WIDTH, d)), _resident((CONV_WIDTH, d)),
                  _resident((1, d)), _resident((d, d)), _resident((1, d)),
                  _resident((d, 2 * D_FF)), _resident((D_FF, d)), _resident((1, d))],
        out_specs=row(d),
        out_shape=jax.ShapeDtypeStruct((bsz, s, d), F32),
        scratch_shapes=[pltpu.VMEM((tm, d), F32), pltpu.VMEM((tm, d), BF16),
                        pltpu.VMEM((tm, D_FF), BF16)],
        compiler_params=_params(("parallel", "arbitrary")),
        name="mix_ffn",
    )(x, ada3, a_n, u, gates, woa, woc, boc.reshape(1, d), wout, g2.reshape(1, d),
      w1, w2, gf.reshape(1, d))


def kernel(x, c, w_ada, b_ada, norm1_g, norm2_g, final_g, w_in, lambda_q1, lambda_k1,
           lambda_q2, lambda_k2, rel_bias, attn_sub_g, w_o_attn, conv_w, conv_b,
           conv_ln_g, conv_ln_b, w_o_conv, b_o_conv, w_out, w_ffn_in, w_ffn_out):
    bsz, s, d = x.shape
    assert w_ada.shape[0] == 1, "single-layer block"
    assert (d, s % ROW_TILE, s % ATTN_TILE) == (D_MODEL, 0, 0)
    assert ATTN_TILE == 2 * MAX_DISTANCE
    assert PROJ_CHUNK == CONV_WIDTH and ATTN_WIDTH % PROJ_CHUNK == 0
    l = 0
    ada, w_in_bf, bias_tiles = _prologue(c, w_ada, b_ada, w_in, rel_bias)
    ada3 = ada.reshape(bsz, 6, d)
    qkv, u, gates = _in_proj(x, ada3, norm1_g[l], w_in_bf, conv_w[l],
                             conv_b[l], conv_ln_g[l], conv_ln_b[l])
    lamv = jnp.stack([lambda_q1[l], lambda_k1[l], lambda_q2[l], lambda_k2[l]])
    a_n, (woa, woc, wout, w1, w2) = _attention(
        qkv, bias_tiles, lamv, attn_sub_g[l],
        [w_o_attn, w_o_conv, w_out, w_ffn_in, w_ffn_out])
    return _mix_ffn(x, ada3, a_n, u, gates, woa, woc, b_o_conv[l], wout, norm2_g[l],
                    w1, w2, final_g)
```

```python
import math

import jax
import jax.numpy as jnp
from jax import lax
from jax.experimental import pallas as pl
from jax.experimental.pallas import tpu as pltpu

D_MODEL = 1024
N_HEADS_A = 4
HEAD_QK = 64
HEAD_V = 2 * HEAD_QK
ATTN_WIDTH = N_HEADS_A * HEAD_V
CONV_WIDTH = 512
CONV_KERNEL = 31
QKV_COLS = 3 * ATTN_WIDTH
GLU_COLS = 2 * CONV_WIDTH
GATE_COLS = 2 * D_MODEL
IN_COLS = QKV_COLS + GLU_COLS + GATE_COLS
D_FF = 2816
N_BUCKETS = 32
MAX_EXACT = 16
MAX_DISTANCE = 128
EPS = 1e-6
NEG_INF = -1e30
LAM_INIT = 0.8 - 0.6 * math.exp(-0.3 * 0)

V7X_VMEM_BYTES = 64 * 1024 * 1024
VMEM_LIMIT = V7X_VMEM_BYTES - 12 * 1024 * 1024

ROW_TILE = 512
ATTN_TILE = 256
ATTN_LAG = 4
ONES_ROWS = 16
CAST_ROWS = 16
LOG2E = math.log2(math.e)
Q_SCALE = HEAD_QK ** -0.5 * LOG2E
CONV_HALO = 32
CONV_CHUNK = 16
CONV_PAD_LANES = 128
PROJ_CHUNK = 512
FF_CHUNK = 256

BF16 = jnp.bfloat16
F32 = jnp.float32


def _sigmoid(x):
    return 1.0 / (1.0 + jnp.exp2(x * (-LOG2E)))


def _silu(x):
    return x * _sigmoid(x)


def _params(sem):
    return pltpu.CompilerParams(dimension_semantics=sem, vmem_limit_bytes=VMEM_LIMIT)


def _resident(shape):
    nd = len(shape)
    return pl.BlockSpec(shape, lambda *_: (0,) * nd, pipeline_mode=pl.Buffered(1))


def _bias_tile(rel_ref, h, t):
    T = ATTN_TILE
    key = lax.broadcasted_iota(jnp.int32, (T, T), 0)
    qry = lax.broadcasted_iota(jnp.int32, (T, T), 1)
    dist = (1 - t) * T + qry - key
    n = jnp.maximum(dist, 0)
    large = MAX_EXACT + (jnp.log(jnp.maximum(n, 1).astype(F32) / MAX_EXACT)
                         / math.log(MAX_DISTANCE / MAX_EXACT)
                         * (N_BUCKETS - MAX_EXACT)).astype(jnp.int32)
    large = jnp.minimum(large, N_BUCKETS - 1)
    bucket = jnp.where(n < MAX_EXACT, n, large)
    far = rel_ref[N_BUCKETS - 1, h]
    bias = jnp.zeros((T, T), F32)
    for b in range(N_BUCKETS - 1):
        bias = jnp.where(bucket == b, (rel_ref[b, h] - far) * LOG2E, bias)
    return jnp.where(dist >= 0, bias, NEG_INF)


def _prologue_kernel(rel_ref, c_ref, w_ref, b_ref, win_ref, o_ref, win_bf_ref, bias_ref):
    j = pl.program_id(0)
    ca = _silu(c_ref[...])
    o_ref[...] = jnp.dot(ca.astype(BF16), w_ref[...].astype(BF16),
                         preferred_element_type=F32) + b_ref[...]
    win_bf_ref[...] = win_ref[...].astype(BF16)
    bias_ref[...] = _bias_tile(rel_ref, j // 2, j % 2)


def _prologue(c, w, b, w_in, rel_bias):
    bsz, d = c.shape
    n = w.shape[-1]
    steps = 2 * N_HEADS_A
    tn = n // steps
    _, rows, cols = w_in.shape
    tr = rows // steps
    T = ATTN_TILE
    return pl.pallas_call(
        _prologue_kernel,
        grid=(steps,),
        in_specs=[pl.BlockSpec(memory_space=pltpu.SMEM),
                  pl.BlockSpec((bsz, d), lambda j: (0, 0)),
                  pl.BlockSpec((None, d, tn), lambda j: (0, 0, j)),
                  pl.BlockSpec((1, tn), lambda j: (0, j)),
                  pl.BlockSpec((None, tr, cols), lambda j: (0, j, 0))],
        out_specs=[pl.BlockSpec((bsz, tn), lambda j: (0, j)),
                   pl.BlockSpec((tr, cols), lambda j: (j, 0)),
                   pl.BlockSpec((None, None, T, T), lambda j: (j // 2, j % 2, 0, 0))],
        out_shape=[jax.ShapeDtypeStruct((bsz, n), F32),
                   jax.ShapeDtypeStruct((rows, cols), BF16),
                   jax.ShapeDtypeStruct((N_HEADS_A, 2, T, T), F32)],
        compiler_params=_params(("arbitrary",)),
        name="prologue",
    )(rel_bias, c, w, b, w_in)


def _conv_chunk(r0, ubuf, w_ref, cb_ref, lng_ref, lnb_ref, o_ref):
    off = CONV_HALO - (CONV_KERNEL - 1)
    acc = jnp.zeros((CONV_CHUNK // 8, 8, CONV_WIDTH), F32)
    for k in range(CONV_KERNEL):
        a, b = divmod(k + off, 8)
        rows = ubuf[b, r0 + 8 * a:r0 + 8 * a + CONV_CHUNK, :]
        acc = acc + rows.reshape(acc.shape) * w_ref[k][None]
    u = acc.reshape(CONV_CHUNK, CONV_WIDTH) + cb_ref[...]
    mu = jnp.mean(u, axis=-1, keepdims=True)
    d = u - mu
    var = jnp.mean(d * d, axis=-1, keepdims=True)
    y = d * lax.rsqrt(var + EPS) * lng_ref[...] + lnb_ref[...]
    o_ref[r0:r0 + CONV_CHUNK, :] = _silu(y).astype(BF16)


def _inproj_kernel(x_ref, ada_ref, g_ref, w_ref, cw_ref, cb_ref, lng_ref, lnb_ref,
                   qkv_ref, u_ref, gate_ref, ubuf_pad):
    TS, H, C = ROW_TILE, CONV_HALO, CONV_WIDTH
    i = pl.program_id(1)
    ubuf = ubuf_pad.at[:, :, 0:C]

    @pl.when(i == 0)
    def _():
        ubuf[0, 0:H, :] = jnp.zeros((H, C), F32)

    @pl.when(i > 0)
    def _():
        ubuf[0, 0:H, :] = ubuf[0, TS:TS + H, :]

    x = x_ref[...]
    ms = jnp.mean(x * x, axis=-1, keepdims=True)
    y = x * lax.rsqrt(ms + EPS)
    h = (y * g_ref[...]) * (1.0 + ada_ref[1:2, :]) + ada_ref[0:1, :]
    hb = h.astype(BF16)

    def proj(base):
        return jnp.dot(hb, w_ref[:, base:base + PROJ_CHUNK], preferred_element_type=F32)

    ubuf[0, H:H + TS, :] = proj(QKV_COLS) * _sigmoid(proj(QKV_COLS + C))
    n = TS + H - 8
    for b in range(1, 8):
        ubuf[b, 0:n, :] = ubuf[0, b:b + n, :]
    jobs = ([(qkv_ref, c, c) for c in range(0, QKV_COLS, PROJ_CHUNK)]
            + [(gate_ref, c, QKV_COLS + GLU_COLS + c) for c in range(0, GATE_COLS, PROJ_CHUNK)])
    rows = list(range(0, TS, CONV_CHUNK))
    for n, (ref, c, base) in enumerate(jobs):
        r = proj(base)
        if base < ATTN_WIDTH:
            r = r * Q_SCALE
        ref[:, c:c + PROJ_CHUNK] = r.astype(BF16)
        for r0 in rows[n * len(rows) // len(jobs):(n + 1) * len(rows) // len(jobs)]:
            _conv_chunk(r0, ubuf, cw_ref, cb_ref, lng_ref, lnb_ref, u_ref)


def _in_proj(x, ada3, g, w, cw, cb, lng, lnb):
    bsz, s, d = x.shape
    tm = ROW_TILE
    C = CONV_WIDTH
    row = lambda n: pl.BlockSpec((None, tm, n), lambda b, i: (b, i, 0))
    vec = lambda a: a.reshape(1, C)
    return pl.pallas_call(
        _inproj_kernel,
        grid=(bsz, s // tm),
        in_specs=[row(d),
                  pl.BlockSpec((None, 6, d), lambda b, i: (b, 0, 0)),
                  _resident((1, d)),
                  _resident((d, IN_COLS)),
                  _resident((CONV_KERNEL, 8, C)),
                  _resident((1, C)), _resident((1, C)), _resident((1, C))],
        out_specs=[row(QKV_COLS), row(C), row(GATE_COLS)],
        out_shape=[jax.ShapeDtypeStruct((bsz, s, QKV_COLS), BF16),
                   jax.ShapeDtypeStruct((bsz, s, C), BF16),
                   jax.ShapeDtypeStruct((bsz, s, GATE_COLS), BF16)],
        scratch_shapes=[pltpu.VMEM((8, tm + CONV_HALO, C + CONV_PAD_LANES), F32)],
        compiler_params=_params(("arbitrary", "arbitrary")),
        name="in_proj",
    )(x, ada3, g.reshape(1, d), w, jnp.broadcast_to(cw[:, None, :], (CONV_KERNEL, 8, C)),
      vec(cb), vec(lng), vec(lnb))


def _attn_kernel(lamv_ref, subg_ref, q_ref, k_ref, v_ref, bias_ref, *rest):
    n_cast = (len(rest) - 5) // 2
    cast_in, o_ref, cast_out = rest[:n_cast], rest[n_cast], rest[n_cast + 1:2 * n_cast + 1]
    vt_ref, qz_ref, m_ref, a_ref = rest[2 * n_cast + 1:]
    T = ATTN_TILE
    NS = 2 * N_HEADS_A
    n_q = q_ref.shape[0] // T
    hcols = lambda h: slice(h * HEAD_V, (h + 1) * HEAD_V)
    rows = lambda t: slice(t * T, (t + 1) * T)

    vt = v_ref[...].T
    for h in range(N_HEADS_A):
        vt_ref[h, 0:HEAD_V, :] = vt[hcols(h), :]
        vt_ref[h, HEAD_V:, :] = jnp.ones((ONES_ROWS, vt.shape[1]), BF16)

    lv = lamv_ref[...]
    lam = (jnp.exp(jnp.sum(lv[0:1] * lv[1:2], axis=-1, keepdims=True))
           - jnp.exp(jnp.sum(lv[2:3] * lv[3:4], axis=-1, keepdims=True)) + LAM_INIT)
    lane = lax.broadcasted_iota(jnp.int32, (T, HEAD_V), 1)

    def prepare(i):
        slot = i % 2
        for h in range(N_HEADS_A):
            qs = q_ref[rows(i), hcols(h)]
            zero = jnp.zeros_like(qs)
            qz_ref[slot, 2 * h] = jnp.where(lane < HEAD_QK, qs, zero)
            qz_ref[slot, 2 * h + 1] = jnp.where(lane >= HEAD_QK, qs, zero)

    def finalize(i):
        slot = i % 2
        for h in range(N_HEADS_A):
            a1, a2 = a_ref[slot, 2 * h], a_ref[slot, 2 * h + 1]
            o = (a1[:HEAD_V] / a1[HEAD_V:HEAD_V + 1]
                 - lam * (a2[:HEAD_V] / a2[HEAD_V:HEAD_V + 1]))
            ms = jnp.mean(o * o, axis=0, keepdims=True)
            y = (o * lax.rsqrt(ms + EPS)).T
            o_ref[rows(i), hcols(h)] = ((y * subg_ref[...]) * (1.0 - LAM_INIT)).astype(BF16)

    units = [(i, j, n) for i in range(n_q) for j in range(i + 1) for n in range(NS)]
    scores, probs, alphas = {}, {}, {}

    def score(u):
        i, j, n = units[u]
        kc = k_ref[rows(j), hcols(n // 2)]
        scores[u] = lax.dot_general(kc, qz_ref[i % 2, n], (((1,), (1,)), ((), ())),
                                    preferred_element_type=F32)

    def softmax(u):
        i, j, n = units[u]
        s = scores.pop(u)
        h, near = n // 2, MAX_DISTANCE
        on_diag = j == i
        if j == i - 1:
            corner = s[near:, :near] + bias_ref[h, 0, near:, :near]
            s = jnp.concatenate(
                [s[:near], jnp.concatenate([corner, s[near:, near:]], axis=1)], axis=0)
        if on_diag:
            s_lo = s[:near, :near] + bias_ref[h, 1, :near, :near]
            s_hi = s[:, near:] + bias_ref[h, 1, :, near:]
            m_new = jnp.concatenate([jnp.max(s_lo, axis=0, keepdims=True),
                                     jnp.max(s_hi, axis=0, keepdims=True)], axis=1)
        else:
            m_new = jnp.max(s, axis=0, keepdims=True)
        if j > 0:
            m_old = m_ref[i % 2, n]
            m_new = jnp.maximum(m_old, m_new)
            alphas[u] = jnp.exp2(m_old - m_new)
        if on_diag:
            p_lo = jnp.exp2(s_lo - m_new[:, :near]).astype(BF16)
            p_hi = jnp.exp2(s_hi - m_new[:, near:]).astype(BF16)
            p_lo = jnp.concatenate([p_lo, jnp.zeros((T - near, near), BF16)], axis=0)
            probs[u] = jnp.concatenate([p_lo, p_hi], axis=1)
        else:
            probs[u] = jnp.exp2(s - m_new).astype(BF16)
        m_ref[i % 2, n] = m_new

    def value(u):
        i, j, n = units[u]
        vc = vt_ref[n // 2, :, rows(j)]
        pv = jnp.dot(vc, probs.pop(u), preferred_element_type=F32)
        a_ref[i % 2, n] = pv if j == 0 else pv + alphas.pop(u) * a_ref[i % 2, n]

    for step in range(len(units) + 2 * ATTN_LAG):
        if step < len(units):
            if units[step][1:] == (0, 0):
                prepare(units[step][0])
            score(step)
        if 0 <= step - ATTN_LAG < len(units):
            softmax(step - ATTN_LAG)
        done = step - 2 * ATTN_LAG
        if 0 <= done < len(units):
            value(done)
            i, j, n = units[done]
            if (j, n) == (i, NS - 1):
                finalize(i)

    for w_ref, wb_ref in zip(cast_in, cast_out):
        wb_ref[...] = w_ref[...].astype(BF16)


def _attention(qkv, bias_tiles, lamv, subg, weights):
    bsz, s, _ = qkv.shape
    T = ATTN_TILE
    W = ATTN_WIDTH
    NS = 2 * N_HEADS_A
    steps = bsz

    def slab_spec(w, squeeze):
        rows, cols = w.shape[-2:]
        assert rows % (steps * CAST_ROWS) == 0
        block = (rows // steps, cols)
        if squeeze:
            return pl.BlockSpec((None,) + block, lambda b: (0, b, 0))
        return pl.BlockSpec(block, lambda b: (b, 0))

    seq = lambda col: pl.BlockSpec((None, s, W), lambda b: (b, 0, col))
    out = pl.pallas_call(
        _attn_kernel,
        grid=(bsz,),
        in_specs=[_resident((4, HEAD_QK)),
                  _resident((1, HEAD_V)),
                  seq(0), seq(1), seq(2),
                  _resident((N_HEADS_A, 2, T, T))] + [slab_spec(w, True) for w in weights],
        out_specs=[seq(0)] + [slab_spec(w, False) for w in weights],
        out_shape=[jax.ShapeDtypeStruct((bsz, s, W), BF16)]
        + [jax.ShapeDtypeStruct(w.shape[-2:], BF16) for w in weights],
        scratch_shapes=[pltpu.VMEM((N_HEADS_A, HEAD_V + ONES_ROWS, s), BF16),
                        pltpu.VMEM((2, NS, T, HEAD_V), BF16),
                        pltpu.VMEM((2, NS, 1, T), F32),
                        pltpu.VMEM((2, NS, HEAD_V + ONES_ROWS, T), F32)],
        compiler_params=_params(("arbitrary",)),
        name="diff_attn",
    )(lamv, subg.reshape(1, HEAD_V), qkv, qkv, qkv, bias_tiles, *weights)
    return out[0], out[1:]


def _mixffn_kernel(x_ref, ada_ref, a_ref, u_ref, gate_ref, woa_ref, woc_ref, boc_ref,
                   wout_ref, g2_ref, w1_ref, w2_ref, gf_ref, o_ref,
                   x1_ref, h2_ref, act_ref):
    D = D_MODEL
    a = jnp.dot(a_ref[...], woa_ref[...], preferred_element_type=F32)
    cv = jnp.dot(u_ref[...], woc_ref[...], preferred_element_type=F32) + boc_ref[...]
    gate = gate_ref[...]
    y = (_sigmoid(gate[:, :D].astype(F32)) * a
         + _sigmoid(gate[:, D:].astype(F32)) * cv)
    yb = y.astype(BF16)
    tm = yb.shape[0]
    halves = [(r, r + tm // 2) for r in (0, tm // 2)]
    for r0, r1 in halves:
        z = jnp.dot(yb[r0:r1], wout_ref[...], preferred_element_type=F32)
        x1 = x_ref[r0:r1, :] + ada_ref[2:3, :] * z
        x1_ref[r0:r1, :] = x1
        ms = jnp.mean(x1 * x1, axis=-1, keepdims=True)
        yn = x1 * lax.rsqrt(ms + EPS)
        h2_ref[r0:r1, :] = ((yn * g2_ref[...]) * (1.0 + ada_ref[4:5, :])
                            + ada_ref[3:4, :]).astype(BF16)

    def ffn_chunk(c, r0, r1):
        h = h2_ref[r0:r1, :]
        fg = jnp.dot(h, w1_ref[:, c:c + FF_CHUNK], preferred_element_type=F32)
        fu = jnp.dot(h, w1_ref[:, D_FF + c:D_FF + c + FF_CHUNK],
                     preferred_element_type=F32)
        act_ref[r0:r1, c:c + FF_CHUNK] = (_silu(fg) * fu).astype(BF16)

    for r0, r1 in halves:
        ffn_chunk(0, r0, r1)
    for c in range(FF_CHUNK, D_FF, FF_CHUNK):
        ffn_chunk(c, 0, tm)
    for r0, r1 in halves:
        z2 = jnp.dot(act_ref[r0:r1, :], w2_ref[...], preferred_element_type=F32)
        x2 = x1_ref[r0:r1, :] + ada_ref[5:6, :] * z2
        ms2 = jnp.mean(x2 * x2, axis=-1, keepdims=True)
        o_ref[r0:r1, :] = (x2 * lax.rsqrt(ms2 + EPS)) * gf_ref[...]


def _mix_ffn(x, ada3, a_n, u, gates, woa, woc, boc, wout, g2, w1, w2, gf):
    bsz, s, d = x.shape
    tm = ROW_TILE
    row = lambda n: pl.BlockSpec((None, tm, n), lambda b, i: (b, i, 0))
    return pl.pallas_call(
        _mixffn_kernel,
        grid=(bsz, s // tm),
        in_specs=[row(d),
                  pl.BlockSpec((None, 6, d), lambda b, i: (b, 0, 0)),
                  row(ATTN_WIDTH), row(CONV_WIDTH), row(GATE_COLS),
                  _resident((ATTN_WIDTH, d)), _resident((CONV_WIDTH, d)),
                  _resident((1, d)), _resident((d, d)), _resident((1, d)),
                  _resident((d, 2 * D_FF)), _resident((D_FF, d)), _resident((1, d))],
        out_specs=row(d),
        out_shape=jax.ShapeDtypeStruct((bsz, s, d), F32),
        scratch_shapes=[pltpu.VMEM((tm, d), F32), pltpu.VMEM((tm, d), BF16),
                        pltpu.VMEM((tm, D_FF), BF16)],
        compiler_params=_params(("parallel", "arbitrary")),
        name="mix_ffn",
    )(x, ada3, a_n, u, gates, woa, woc, boc.reshape(1, d), wout, g2.reshape(1, d),
      w1, w2, gf.reshape(1, d))


def kernel(x, c, w_ada, b_ada, norm1_g, norm2_g, final_g, w_in, lambda_q1, lambda_k1,
           lambda_q2, lambda_k2, rel_bias, attn_sub_g, w_o_attn, conv_w, conv_b,
           conv_ln_g, conv_ln_b, w_o_conv, b_o_conv, w_out, w_ffn_in, w_ffn_out):
    bsz, s, d = x.shape
    assert w_ada.shape[0] == 1, "single-layer block"
    assert (d, s % ROW_TILE, s % ATTN_TILE) == (D_MODEL, 0, 0)
    assert ATTN_TILE == 2 * MAX_DISTANCE
    assert PROJ_CHUNK == CONV_WIDTH and ATTN_WIDTH % PROJ_CHUNK == 0
    l = 0
    ada, w_in_bf, bias_tiles = _prologue(c, w_ada, b_ada, w_in, rel_bias)
    ada3 = ada.reshape(bsz, 6, d)
    qkv, u, gates = _in_proj(x, ada3, norm1_g[l], w_in_bf, conv_w[l],
                             conv_b[l], conv_ln_g[l], conv_ln_b[l])
    lamv = jnp.stack([lambda_q1[l], lambda_k1[l], lambda_q2[l], lambda_k2[l]])
    a_n, (woa, woc, wout, w1, w2) = _attention(
        qkv, bias_tiles, lamv, attn_sub_g[l],
        [w_o_attn, w_o_conv, w_out, w_ffn_in, w_ffn_out])
    return _mix_ffn(x, ada3, a_n, u, gates, woa, woc, b_o_conv[l], wout, norm2_g[l],
                    w1, w2, final_g)
```

```python
import math

import jax
import jax.numpy as jnp
from jax import lax
from jax.experimental import pallas as pl
from jax.experimental.pallas import tpu as pltpu

D_MODEL = 1024
N_HEADS_A = 4
HEAD_QK = 64
HEAD_V = 2 * HEAD_QK
ATTN_WIDTH = N_HEADS_A * HEAD_V
CONV_WIDTH = 512
CONV_KERNEL = 31
QKV_COLS = 3 * ATTN_WIDTH
GLU_COLS = 2 * CONV_WIDTH
GATE_COLS = 2 * D_MODEL
IN_COLS = QKV_COLS + GLU_COLS + GATE_COLS
D_FF = 2816
N_BUCKETS = 32
MAX_EXACT = 16
MAX_DISTANCE = 128
EPS = 1e-6
NEG_INF = -1e30
LAM_INIT = 0.8 - 0.6 * math.exp(-0.3 * 0)

V7X_VMEM_BYTES = 64 * 1024 * 1024
VMEM_LIMIT = V7X_VMEM_BYTES - 12 * 1024 * 1024

ROW_TILE = 512
ATTN_TILE = 256
ATTN_LAG = 4
ONES_ROWS = 16
CAST_ROWS = 16
LOG2E = math.log2(math.e)
Q_SCALE = HEAD_QK ** -0.5 * LOG2E
CONV_HALO = 32
CONV_CHUNK = 32
PROJ_CHUNK = 512
FF_CHUNK = 256

BF16 = jnp.bfloat16
F32 = jnp.float32


def _sigmoid(x):
    return 1.0 / (1.0 + jnp.exp2(x * (-LOG2E)))


def _silu(x):
    return x * _sigmoid(x)


def _params(sem):
    return pltpu.CompilerParams(dimension_semantics=sem, vmem_limit_bytes=VMEM_LIMIT)


def _resident(shape):
    nd = len(shape)
    return pl.BlockSpec(shape, lambda *_: (0,) * nd, pipeline_mode=pl.Buffered(1))


def _bias_tile(rel_ref, h, t):
    T = ATTN_TILE
    key = lax.broadcasted_iota(jnp.int32, (T, T), 0)
    qry = lax.broadcasted_iota(jnp.int32, (T, T), 1)
    dist = (1 - t) * T + qry - key
    n = jnp.maximum(dist, 0)
    large = MAX_EXACT + (jnp.log(jnp.maximum(n, 1).astype(F32) / MAX_EXACT)
                         / math.log(MAX_DISTANCE / MAX_EXACT)
                         * (N_BUCKETS - MAX_EXACT)).astype(jnp.int32)
    large = jnp.minimum(large, N_BUCKETS - 1)
    bucket = jnp.where(n < MAX_EXACT, n, large)
    far = rel_ref[N_BUCKETS - 1, h]
    bias = jnp.zeros((T, T), F32)
    for b in range(N_BUCKETS - 1):
        bias = jnp.where(bucket == b, (rel_ref[b, h] - far) * LOG2E, bias)
    return jnp.where(dist >= 0, bias, NEG_INF)


def _prologue_kernel(rel_ref, c_ref, w_ref, b_ref, win_ref, o_ref, win_bf_ref, bias_ref):
    j = pl.program_id(0)
    ca = _silu(c_ref[...])
    o_ref[...] = jnp.dot(ca.astype(BF16), w_ref[...].astype(BF16),
                         preferred_element_type=F32) + b_ref[...]
    win_bf_ref[...] = win_ref[...].astype(BF16)
    bias_ref[...] = _bias_tile(rel_ref, j // 2, j % 2)


def _prologue(c, w, b, w_in, rel_bias):
    bsz, d = c.shape
    n = w.shape[-1]
    steps = 2 * N_HEADS_A
    tn = n // steps
    _, rows, cols = w_in.shape
    tr = rows // steps
    T = ATTN_TILE
    return pl.pallas_call(
        _prologue_kernel,
        grid=(steps,),
        in_specs=[pl.BlockSpec(memory_space=pltpu.SMEM),
                  pl.BlockSpec((bsz, d), lambda j: (0, 0)),
                  pl.BlockSpec((None, d, tn), lambda j: (0, 0, j)),
                  pl.BlockSpec((1, tn), lambda j: (0, j)),
                  pl.BlockSpec((None, tr, cols), lambda j: (0, j, 0))],
        out_specs=[pl.BlockSpec((bsz, tn), lambda j: (0, j)),
                   pl.BlockSpec((tr, cols), lambda j: (j, 0)),
                   pl.BlockSpec((None, None, T, T), lambda j: (j // 2, j % 2, 0, 0))],
        out_shape=[jax.ShapeDtypeStruct((bsz, n), F32),
                   jax.ShapeDtypeStruct((rows, cols), BF16),
                   jax.ShapeDtypeStruct((N_HEADS_A, 2, T, T), F32)],
        compiler_params=_params(("arbitrary",)),
        name="prologue",
    )(rel_bias, c, w, b, w_in)


def _conv_chunk(r0, ubuf, w_ref, cb_ref, lng_ref, lnb_ref, o_ref):
    off = CONV_HALO - (CONV_KERNEL - 1)
    acc = jnp.zeros((CONV_CHUNK // 8, 8, CONV_WIDTH), F32)
    for k in range(CONV_KERNEL):
        a, b = divmod(k + off, 8)
        rows = ubuf[b, r0 + 8 * a:r0 + 8 * a + CONV_CHUNK, :]
        acc = acc + rows.reshape(acc.shape) * w_ref[k][None]
    u = acc.reshape(CONV_CHUNK, CONV_WIDTH) + cb_ref[...]
    mu = jnp.mean(u, axis=-1, keepdims=True)
    d = u - mu
    var = jnp.mean(d * d, axis=-1, keepdims=True)
    y = d * lax.rsqrt(var + EPS) * lng_ref[...] + lnb_ref[...]
    o_ref[r0:r0 + CONV_CHUNK, :] = _silu(y).astype(BF16)


def _inproj_kernel(x_ref, ada_ref, g_ref, w_ref, cw_ref, cb_ref, lng_ref, lnb_ref,
                   qkv_ref, u_ref, gate_ref, ubuf):
    TS, H, C = ROW_TILE, CONV_HALO, CONV_WIDTH
    i = pl.program_id(1)

    @pl.when(i == 0)
    def _():
        ubuf[0, 0:H, :] = jnp.zeros((H, C), F32)

    @pl.when(i > 0)
    def _():
        ubuf[0, 0:H, :] = ubuf[0, TS:TS + H, :]

    x = x_ref[...]
    ms = jnp.mean(x * x, axis=-1, keepdims=True)
    y = x * lax.rsqrt(ms + EPS)
    h = (y * g_ref[...]) * (1.0 + ada_ref[1:2, :]) + ada_ref[0:1, :]
    hb = h.astype(BF16)

    def proj(base):
        return jnp.dot(hb, w_ref[:, base:base + PROJ_CHUNK], preferred_element_type=F32)

    ubuf[0, H:H + TS, :] = proj(QKV_COLS) * _sigmoid(proj(QKV_COLS + C))
    n = TS + H - 8
    for b in range(1, 8):
        ubuf[b, 0:n, :] = ubuf[0, b:b + n, :]
    jobs = ([(qkv_ref, c, c) for c in range(0, QKV_COLS, PROJ_CHUNK)]
            + [(gate_ref, c, QKV_COLS + GLU_COLS + c) for c in range(0, GATE_COLS, PROJ_CHUNK)])
    rows = list(range(0, TS, CONV_CHUNK))
    for n, (ref, c, base) in enumerate(jobs):
        r = proj(base)
        if base < ATTN_WIDTH:
            r = r * Q_SCALE
        ref[:, c:c + PROJ_CHUNK] = r.astype(BF16)
        for r0 in rows[n * len(rows) // len(jobs):(n + 1) * len(rows) // len(jobs)]:
            _conv_chunk(r0, ubuf, cw_ref, cb_ref, lng_ref, lnb_ref, u_ref)


def _in_proj(x, ada3, g, w, cw, cb, lng, lnb):
    bsz, s, d = x.shape
    tm = ROW_TILE
    C = CONV_WIDTH
    row = lambda n: pl.BlockSpec((None, tm, n), lambda b, i: (b, i, 0))
    vec = lambda a: a.reshape(1, C)
    return pl.pallas_call(
        _inproj_kernel,
        grid=(bsz, s // tm),
        in_specs=[row(d),
                  pl.BlockSpec((None, 6, d), lambda b, i: (b, 0, 0)),
                  _resident((1, d)),
                  _resident((d, IN_COLS)),
                  _resident((CONV_KERNEL, 8, C)),
                  _resident((1, C)), _resident((1, C)), _resident((1, C))],
        out_specs=[row(QKV_COLS), row(C), row(GATE_COLS)],
        out_shape=[jax.ShapeDtypeStruct((bsz, s, QKV_COLS), BF16),
                   jax.ShapeDtypeStruct((bsz, s, C), BF16),
                   jax.ShapeDtypeStruct((bsz, s, GATE_COLS), BF16)],
        scratch_shapes=[pltpu.VMEM((8, tm + CONV_HALO, C), F32)],
        compiler_params=_params(("arbitrary", "arbitrary")),
        name="in_proj",
    )(x, ada3, g.reshape(1, d), w, jnp.broadcast_to(cw[:, None, :], (CONV_KERNEL, 8, C)),
      vec(cb), vec(lng), vec(lnb))


def _attn_kernel(lamv_ref, subg_ref, q_ref, k_ref, v_ref, bias_ref, *rest):
    n_cast = (len(rest) - 5) // 2
    cast_in, o_ref, cast_out = rest[:n_cast], rest[n_cast], rest[n_cast + 1:2 * n_cast + 1]
    vt_ref, qz_ref, m_ref, a_ref = rest[2 * n_cast + 1:]
    T = ATTN_TILE
    NS = 2 * N_HEADS_A
    n_q = q_ref.shape[0] // T
    hcols = lambda h: slice(h * HEAD_V, (h + 1) * HEAD_V)
    rows = lambda t: slice(t * T, (t + 1) * T)

    vt = v_ref[...].T
    for h in range(N_HEADS_A):
        vt_ref[h, 0:HEAD_V, :] = vt[hcols(h), :]
        vt_ref[h, HEAD_V:, :] = jnp.ones((ONES_ROWS, vt.shape[1]), BF16)

    lv = lamv_ref[...]
    lam = (jnp.exp(jnp.sum(lv[0:1] * lv[1:2], axis=-1, keepdims=True))
           - jnp.exp(jnp.sum(lv[2:3] * lv[3:4], axis=-1, keepdims=True)) + LAM_INIT)
    lane = lax.broadcasted_iota(jnp.int32, (T, HEAD_V), 1)

    def prepare(i):
        slot = i % 2
        for h in range(N_HEADS_A):
            qs = q_ref[rows(i), hcols(h)]
            zero = jnp.zeros_like(qs)
            qz_ref[slot, 2 * h] = jnp.where(lane < HEAD_QK, qs, zero)
            qz_ref[slot, 2 * h + 1] = jnp.where(lane >= HEAD_QK, qs, zero)

    def finalize(i):
        slot = i % 2
        for h in range(N_HEADS_A):
            a1, a2 = a_ref[slot, 2 * h], a_ref[slot, 2 * h + 1]
            o = (a1[:HEAD_V] / a1[HEAD_V:HEAD_V + 1]
                 - lam * (a2[:HEAD_V] / a2[HEAD_V:HEAD_V + 1]))
            ms = jnp.mean(o * o, axis=0, keepdims=True)
            y = (o * lax.rsqrt(ms + EPS)).T
            o_ref[rows(i), hcols(h)] = ((y * subg_ref[...]) * (1.0 - LAM_INIT)).astype(BF16)

    units = [(i, j, n) for i in range(n_q) for j in range(i + 1) for n in range(NS)]
    scores, probs, alphas = {}, {}, {}

    def score(u):
        i, j, n = units[u]
        kc = k_ref[rows(j), hcols(n // 2)]
        scores[u] = lax.dot_general(kc, qz_ref[i % 2, n], (((1,), (1,)), ((), ())),
                                    preferred_element_type=F32)

    def softmax(u):
        i, j, n = units[u]
        s = scores.pop(u)
        h, near = n // 2, MAX_DISTANCE
        on_diag = j == i
        if j == i - 1:
            corner = s[near:, :near] + bias_ref[h, 0, near:, :near]
            s = jnp.concatenate(
                [s[:near], jnp.concatenate([corner, s[near:, near:]], axis=1)], axis=0)
        if on_diag:
            s_lo = s[:near, :near] + bias_ref[h, 1, :near, :near]
            s_hi = s[:, near:] + bias_ref[h, 1, :, near:]
            m_new = jnp.concatenate([jnp.max(s_lo, axis=0, keepdims=True),
                                     jnp.max(s_hi, axis=0, keepdims=True)], axis=1)
        else:
            m_new = jnp.max(s, axis=0, keepdims=True)
        if j > 0:
            m_old = m_ref[i % 2, n]
            m_new = jnp.maximum(m_old, m_new)
            alphas[u] = jnp.exp2(m_old - m_new)
        if on_diag:
            p_lo = jnp.exp2(s_lo - m_new[:, :near]).astype(BF16)
            p_hi = jnp.exp2(s_hi - m_new[:, near:]).astype(BF16)
            p_lo = jnp.concatenate([p_lo, jnp.zeros((T - near, near), BF16)], axis=0)
            probs[u] = jnp.concatenate([p_lo, p_hi], axis=1)
        else:
            probs[u] = jnp.exp2(s - m_new).astype(BF16)
        m_ref[i % 2, n] = m_new

    def value(u):
        i, j, n = units[u]
        vc = vt_ref[n // 2, :, rows(j)]
        pv = jnp.dot(vc, probs.pop(u), preferred_element_type=F32)
        a_ref[i % 2, n] = pv if j == 0 else pv + alphas.pop(u) * a_ref[i % 2, n]

    for step in range(len(units) + 2 * ATTN_LAG):
        if step < len(units):
            if units[step][1:] == (0, 0):
                prepare(units[step][0])
            score(step)
        if 0 <= step - ATTN_LAG < len(units):
            softmax(step - ATTN_LAG)
        done = step - 2 * ATTN_LAG
        if 0 <= done < len(units):
            value(done)
            i, j, n = units[done]
            if (j, n) == (i, NS - 1):
                finalize(i)

    for w_ref, wb_ref in zip(cast_in, cast_out):
        wb_ref[...] = w_ref[...].astype(BF16)


def _attention(qkv, bias_tiles, lamv, subg, weights):
    bsz, s, _ = qkv.shape
    T = ATTN_TILE
    W = ATTN_WIDTH
    NS = 2 * N_HEADS_A
    steps = bsz

    def slab_spec(w, squeeze):
        rows, cols = w.shape[-2:]
        assert rows % (steps * CAST_ROWS) == 0
        block = (rows // steps, cols)
        if squeeze:
            return pl.BlockSpec((None,) + block, lambda b: (0, b, 0))
        return pl.BlockSpec(block, lambda b: (b, 0))

    seq = lambda col: pl.BlockSpec((None, s, W), lambda b: (b, 0, col))
    out = pl.pallas_call(
        _attn_kernel,
        grid=(bsz,),
        in_specs=[_resident((4, HEAD_QK)),
                  _resident((1, HEAD_V)),
                  seq(0), seq(1), seq(2),
                  _resident((N_HEADS_A, 2, T, T))] + [slab_spec(w, True) for w in weights],
        out_specs=[seq(0)] + [slab_spec(w, False) for w in weights],
        out_shape=[jax.ShapeDtypeStruct((bsz, s, W), BF16)]
        + [jax.ShapeDtypeStruct(w.shape[-2:], BF16) for w in weights],
        scratch_shapes=[pltpu.VMEM((N_HEADS_A, HEAD_V + ONES_ROWS, s), BF16),
                        pltpu.VMEM((2, NS, T, HEAD_V), BF16),
                        pltpu.VMEM((2, NS, 1, T), F32),
                        pltpu.VMEM((2, NS, HEAD_V + ONES_ROWS, T), F32)],
        compiler_params=_params(("arbitrary",)),
        name="diff_attn",
    )(lamv, subg.reshape(1, HEAD_V), qkv, qkv, qkv, bias_tiles, *weights)
    return out[0], out[1:]


def _mixffn_kernel(x_ref, ada_ref, a_ref, u_ref, gate_ref, woa_ref, woc_ref, boc_ref,
                   wout_ref, g2_ref, w1_ref, w2_ref, gf_ref, o_ref,
                   x1_ref, h2_ref, act_ref):
    D = D_MODEL
    a = jnp.dot(a_ref[...], woa_ref[...], preferred_element_type=F32)
    cv = jnp.dot(u_ref[...], woc_ref[...], preferred_element_type=F32) + boc_ref[...]
    gate = gate_ref[...]
    y = (_sigmoid(gate[:, :D].astype(F32)) * a
         + _sigmoid(gate[:, D:].astype(F32)) * cv)
    yb = y.astype(BF16)
    tm = yb.shape[0]
    halves = [(r, r + tm // 2) for r in (0, tm // 2)]
    for r0, r1 in halves:
        z = jnp.dot(yb[r0:r1], wout_ref[...], preferred_element_type=F32)
        x1 = x_ref[r0:r1, :] + ada_ref[2:3, :] * z
        x1_ref[r0:r1, :] = x1
        ms = jnp.mean(x1 * x1, axis=-1, keepdims=True)
        yn = x1 * lax.rsqrt(ms + EPS)
        h2_ref[r0:r1, :] = ((yn * g2_ref[...]) * (1.0 + ada_ref[4:5, :])
                            + ada_ref[3:4, :]).astype(BF16)

    def ffn_chunk(c, r0, r1):
        h = h2_ref[r0:r1, :]
        fg = jnp.dot(h, w1_ref[:, c:c + FF_CHUNK], preferred_element_type=F32)
        fu = jnp.dot(h, w1_ref[:, D_FF + c:D_FF + c + FF_CHUNK],
                     preferred_element_type=F32)
        act_ref[r0:r1, c:c + FF_CHUNK] = (_silu(fg) * fu).astype(BF16)

    for r0, r1 in halves:
        ffn_chunk(0, r0, r1)
    for c in range(FF_CHUNK, D_FF, FF_CHUNK):
        ffn_chunk(c, 0, tm)
    for r0, r1 in halves:
        z2 = jnp.dot(act_ref[r0:r1, :], w2_ref[...], preferred_element_type=F32)
        x2 = x1_ref[r0:r1, :] + ada_ref[5:6, :] * z2
        ms2 = jnp.mean(x2 * x2, axis=-1, keepdims=True)
        o_ref[r0:r1, :] = (x2 * lax.rsqrt(ms2 + EPS)) * gf_ref[...]


def _mix_ffn(x, ada3, a_n, u, gates, woa, woc, boc, wout, g2, w1, w2, gf):
    bsz, s, d = x.shape
    tm = ROW_TILE
    row = lambda n: pl.BlockSpec((None, tm, n), lambda b, i: (b, i, 0))
    return pl.pallas_call(
        _mixffn_kernel,
        grid=(bsz, s // tm),
        in_specs=[row(d),
                  pl.BlockSpec((None, 6, d), lambda b, i: (b, 0, 0)),
                  row(ATTN_WIDTH), row(CONV_WIDTH), row(GATE_COLS),
                  _resident((ATTN_WIDTH, d)), _resident((CONV_WIDTH, d)),
                  _resident((1, d)), _resident((d, d)), _resident((1, d)),
                  _resident((d, 2 * D_FF)), _resident((D_FF, d)), _resident((1, d))],
        out_specs=row(d),
        out_shape=jax.ShapeDtypeStruct((bsz, s, d), F32),
        scratch_shapes=[pltpu.VMEM((tm, d), F32), pltpu.VMEM((tm, d), BF16),
                        pltpu.VMEM((tm, D_FF), BF16)],
        compiler_params=_params(("parallel", "arbitrary")),
        name="mix_ffn",
    )(x, ada3, a_n, u, gates, woa, woc, boc.reshape(1, d), wout, g2.reshape(1, d),
      w1, w2, gf.reshape(1, d))


def kernel(x, c, w_ada, b_ada, norm1_g, norm2_g, final_g, w_in, lambda_q1, lambda_k1,
           lambda_q2, lambda_k2, rel_bias, attn_sub_g, w_o_attn, conv_w, conv_b,
           conv_ln_g, conv_ln_b, w_o_conv, b_o_conv, w_out, w_ffn_in, w_ffn_out):
    bsz, s, d = x.shape
    assert w_ada.shape[0] == 1, "single-layer block"
    assert (d, s % ROW_TILE, s % ATTN_TILE) == (D_MODEL, 0, 0)
    assert ATTN_TILE == 2 * MAX_DISTANCE
    assert PROJ_CHUNK == CONV_WIDTH and ATTN_WIDTH % PROJ_CHUNK == 0
    l = 0
    ada, w_in_bf, bias_tiles = _prologue(c, w_ada, b_ada, w_in, rel_bias)
    ada3 = ada.reshape(bsz, 6, d)
    qkv, u, gates = _in_proj(x, ada3, norm1_g[l], w_in_bf, conv_w[l],
                             conv_b[l], conv_ln_g[l], conv_ln_b[l])
    lamv = jnp.stack([lambda_q1[l], lambda_k1[l], lambda_q2[l], lambda_k2[l]])
    a_n, (woa, woc, wout, w1, w2) = _attention(
        qkv, bias_tiles, lamv, attn_sub_g[l],
        [w_o_attn, w_o_conv, w_out, w_ffn_in, w_ffn_out])
    return _mix_ffn(x, ada3, a_n, u, gates, woa, woc, b_o_conv[l], wout, norm2_g[l],
                    w1, w2, final_g)
```

```python
import math

import jax
import jax.numpy as jnp
from jax import lax
from jax.experimental import pallas as pl
from jax.experimental.pallas import tpu as pltpu

D_MODEL = 1024
N_HEADS_A = 4
HEAD_QK = 64
HEAD_V = 2 * HEAD_QK
ATTN_WIDTH = N_HEADS_A * HEAD_V
CONV_WIDTH = 512
CONV_KERNEL = 31
QKV_COLS = 3 * ATTN_WIDTH
GLU_COLS = 2 * CONV_WIDTH
GATE_COLS = 2 * D_MODEL
IN_COLS = QKV_COLS + GLU_COLS + GATE_COLS
D_FF = 2816
N_BUCKETS = 32
MAX_EXACT = 16
MAX_DISTANCE = 128
EPS = 1e-6
NEG_INF = -1e30
LAM_INIT = 0.8 - 0.6 * math.exp(-0.3 * 0)

V7X_VMEM_BYTES = 64 * 1024 * 1024
VMEM_LIMIT = V7X_VMEM_BYTES - 12 * 1024 * 1024

ROW_TILE = 512
ATTN_TILE = 256
ATTN_LAG = 4
ONES_ROWS = 16
CAST_ROWS = 16
LOG2E = math.log2(math.e)
Q_SCALE = HEAD_QK ** -0.5 * LOG2E
CONV_HALO = 32
CONV_CHUNK = 8
PROJ_CHUNK = 512
FF_CHUNK = 256

BF16 = jnp.bfloat16
F32 = jnp.float32


def _sigmoid(x):
    return 1.0 / (1.0 + jnp.exp2(x * (-LOG2E)))


def _silu(x):
    return x * _sigmoid(x)


def _params(sem):
    return pltpu.CompilerParams(dimension_semantics=sem, vmem_limit_bytes=VMEM_LIMIT)


def _resident(shape):
    nd = len(shape)
    return pl.BlockSpec(shape, lambda *_: (0,) * nd, pipeline_mode=pl.Buffered(1))


def _bias_tile(rel_ref, h, t):
    T = ATTN_TILE
    key = lax.broadcasted_iota(jnp.int32, (T, T), 0)
    qry = lax.broadcasted_iota(jnp.int32, (T, T), 1)
    dist = (1 - t) * T + qry - key
    n = jnp.maximum(dist, 0)
    large = MAX_EXACT + (jnp.log(jnp.maximum(n, 1).astype(F32) / MAX_EXACT)
                         / math.log(MAX_DISTANCE / MAX_EXACT)
                         * (N_BUCKETS - MAX_EXACT)).astype(jnp.int32)
    large = jnp.minimum(large, N_BUCKETS - 1)
    bucket = jnp.where(n < MAX_EXACT, n, large)
    far = rel_ref[N_BUCKETS - 1, h]
    bias = jnp.zeros((T, T), F32)
    for b in range(N_BUCKETS - 1):
        bias = jnp.where(bucket == b, (rel_ref[b, h] - far) * LOG2E, bias)
    return jnp.where(dist >= 0, bias, NEG_INF)


def _prologue_kernel(rel_ref, c_ref, w_ref, b_ref, win_ref, o_ref, win_bf_ref, bias_ref):
    j = pl.program_id(0)
    ca = _silu(c_ref[...])
    o_ref[...] = jnp.dot(ca.astype(BF16), w_ref[...].astype(BF16),
                         preferred_element_type=F32) + b_ref[...]
    win_bf_ref[...] = win_ref[...].astype(BF16)
    bias_ref[...] = _bias_tile(rel_ref, j // 2, j % 2)


def _prologue(c, w, b, w_in, rel_bias):
    bsz, d = c.shape
    n = w.shape[-1]
    steps = 2 * N_HEADS_A
    tn = n // steps
    _, rows, cols = w_in.shape
    tr = rows // steps
    T = ATTN_TILE
    return pl.pallas_call(
        _prologue_kernel,
        grid=(steps,),
        in_specs=[pl.BlockSpec(memory_space=pltpu.SMEM),
                  pl.BlockSpec((bsz, d), lambda j: (0, 0)),
                  pl.BlockSpec((None, d, tn), lambda j: (0, 0, j)),
                  pl.BlockSpec((1, tn), lambda j: (0, j)),
                  pl.BlockSpec((None, tr, cols), lambda j: (0, j, 0))],
        out_specs=[pl.BlockSpec((bsz, tn), lambda j: (0, j)),
                   pl.BlockSpec((tr, cols), lambda j: (j, 0)),
                   pl.BlockSpec((None, None, T, T), lambda j: (j // 2, j % 2, 0, 0))],
        out_shape=[jax.ShapeDtypeStruct((bsz, n), F32),
                   jax.ShapeDtypeStruct((rows, cols), BF16),
                   jax.ShapeDtypeStruct((N_HEADS_A, 2, T, T), F32)],
        compiler_params=_params(("arbitrary",)),
        name="prologue",
    )(rel_bias, c, w, b, w_in)


def _conv_chunk(r0, ubuf, w_ref, cb_ref, lng_ref, lnb_ref, o_ref):
    off = CONV_HALO - (CONV_KERNEL - 1)
    acc = jnp.zeros((CONV_CHUNK // 8, 8, CONV_WIDTH), F32)
    for k in range(CONV_KERNEL):
        a, b = divmod(k + off, 8)
        rows = ubuf[b, r0 + 8 * a:r0 + 8 * a + CONV_CHUNK, :]
        acc = acc + rows.reshape(acc.shape) * w_ref[k][None]
    u = acc.reshape(CONV_CHUNK, CONV_WIDTH) + cb_ref[...]
    mu = jnp.mean(u, axis=-1, keepdims=True)
    d = u - mu
    var = jnp.mean(d * d, axis=-1, keepdims=True)
    y = d * lax.rsqrt(var + EPS) * lng_ref[...] + lnb_ref[...]
    o_ref[r0:r0 + CONV_CHUNK, :] = _silu(y).astype(BF16)


def _inproj_kernel(x_ref, ada_ref, g_ref, w_ref, cw_ref, cb_ref, lng_ref, lnb_ref,
                   qkv_ref, u_ref, gate_ref, ubuf):
    TS, H, C = ROW_TILE, CONV_HALO, CONV_WIDTH
    i = pl.program_id(1)

    @pl.when(i == 0)
    def _():
        ubuf[0, 0:H, :] = jnp.zeros((H, C), F32)

    @pl.when(i > 0)
    def _():
        ubuf[0, 0:H, :] = ubuf[0, TS:TS + H, :]

    x = x_ref[...]
    ms = jnp.mean(x * x, axis=-1, keepdims=True)
    y = x * lax.rsqrt(ms + EPS)
    h = (y * g_ref[...]) * (1.0 + ada_ref[1:2, :]) + ada_ref[0:1, :]
    hb = h.astype(BF16)

    def proj(base):
        return jnp.dot(hb, w_ref[:, base:base + PROJ_CHUNK], preferred_element_type=F32)

    ubuf[0, H:H + TS, :] = proj(QKV_COLS) * _sigmoid(proj(QKV_COLS + C))
    n = TS + H - 8
    for b in range(1, 8):
        ubuf[b, 0:n, :] = ubuf[0, b:b + n, :]
    jobs = ([(qkv_ref, c, c) for c in range(0, QKV_COLS, PROJ_CHUNK)]
            + [(gate_ref, c, QKV_COLS + GLU_COLS + c) for c in range(0, GATE_COLS, PROJ_CHUNK)])
    rows = list(range(0, TS, CONV_CHUNK))
    for n, (ref, c, base) in enumerate(jobs):
        r = proj(base)
        if base < ATTN_WIDTH:
            r = r * Q_SCALE
        ref[:, c:c + PROJ_CHUNK] = r.astype(BF16)
        for r0 in rows[n * len(rows) // len(jobs):(n + 1) * len(rows) // len(jobs)]:
            _conv_chunk(r0, ubuf, cw_ref, cb_ref, lng_ref, lnb_ref, u_ref)


def _in_proj(x, ada3, g, w, cw, cb, lng, lnb):
    bsz, s, d = x.shape
    tm = ROW_TILE
    C = CONV_WIDTH
    row = lambda n: pl.BlockSpec((None, tm, n), lambda b, i: (b, i, 0))
    vec = lambda a: a.reshape(1, C)
    return pl.pallas_call(
        _inproj_kernel,
        grid=(bsz, s // tm),
        in_specs=[row(d),
                  pl.BlockSpec((None, 6, d), lambda b, i: (b, 0, 0)),
                  _resident((1, d)),
                  _resident((d, IN_COLS)),
                  _resident((CONV_KERNEL, 8, C)),
                  _resident((1, C)), _resident((1, C)), _resident((1, C))],
        out_specs=[row(QKV_COLS), row(C), row(GATE_COLS)],
        out_shape=[jax.ShapeDtypeStruct((bsz, s, QKV_COLS), BF16),
                   jax.ShapeDtypeStruct((bsz, s, C), BF16),
                   jax.ShapeDtypeStruct((bsz, s, GATE_COLS), BF16)],
        scratch_shapes=[pltpu.VMEM((8, tm + CONV_HALO, C), F32)],
        compiler_params=_params(("arbitrary", "arbitrary")),
        name="in_proj",
    )(x, ada3, g.reshape(1, d), w, jnp.broadcast_to(cw[:, None, :], (CONV_KERNEL, 8, C)),
      vec(cb), vec(lng), vec(lnb))


def _attn_kernel(lamv_ref, subg_ref, q_ref, k_ref, v_ref, bias_ref, *rest):
    n_cast = (len(rest) - 5) // 2
    cast_in, o_ref, cast_out = rest[:n_cast], rest[n_cast], rest[n_cast + 1:2 * n_cast + 1]
    vt_ref, qz_ref, m_ref, a_ref = rest[2 * n_cast + 1:]
    T = ATTN_TILE
    NS = 2 * N_HEADS_A
    n_q = q_ref.shape[0] // T
    hcols = lambda h: slice(h * HEAD_V, (h + 1) * HEAD_V)
    rows = lambda t: slice(t * T, (t + 1) * T)

    vt = v_ref[...].T
    for h in range(N_HEADS_A):
        vt_ref[h, 0:HEAD_V, :] = vt[hcols(h), :]
        vt_ref[h, HEAD_V:, :] = jnp.ones((ONES_ROWS, vt.shape[1]), BF16)

    lv = lamv_ref[...]
    lam = (jnp.exp(jnp.sum(lv[0:1] * lv[1:2], axis=-1, keepdims=True))
           - jnp.exp(jnp.sum(lv[2:3] * lv[3:4], axis=-1, keepdims=True)) + LAM_INIT)
    lane = lax.broadcasted_iota(jnp.int32, (T, HEAD_V), 1)

    def prepare(i):
        slot = i % 2
        for h in range(N_HEADS_A):
            qs = q_ref[rows(i), hcols(h)]
            zero = jnp.zeros_like(qs)
            qz_ref[slot, 2 * h] = jnp.where(lane < HEAD_QK, qs, zero)
            qz_ref[slot, 2 * h + 1] = jnp.where(lane >= HEAD_QK, qs, zero)

    def finalize(i):
        slot = i % 2
        for h in range(N_HEADS_A):
            a1, a2 = a_ref[slot, 2 * h], a_ref[slot, 2 * h + 1]
            o = (a1[:HEAD_V] / a1[HEAD_V:HEAD_V + 1]
                 - lam * (a2[:HEAD_V] / a2[HEAD_V:HEAD_V + 1]))
            ms = jnp.mean(o * o, axis=0, keepdims=True)
            y = (o * lax.rsqrt(ms + EPS)).T
            o_ref[rows(i), hcols(h)] = ((y * subg_ref[...]) * (1.0 - LAM_INIT)).astype(BF16)

    units = [(i, j, n) for i in range(n_q) for j in range(i + 1) for n in range(NS)]
    scores, probs, alphas = {}, {}, {}

    def score(u):
        i, j, n = units[u]
        kc = k_ref[rows(j), hcols(n // 2)]
        scores[u] = lax.dot_general(kc, qz_ref[i % 2, n], (((1,), (1,)), ((), ())),
                                    preferred_element_type=F32)

    def softmax(u):
        i, j, n = units[u]
        s = scores.pop(u)
        h, near = n // 2, MAX_DISTANCE
        on_diag = j == i
        if j == i - 1:
            corner = s[near:, :near] + bias_ref[h, 0, near:, :near]
            s = jnp.concatenate(
                [s[:near], jnp.concatenate([corner, s[near:, near:]], axis=1)], axis=0)
        if on_diag:
            s_lo = s[:near, :near] + bias_ref[h, 1, :near, :near]
            s_hi = s[:, near:] + bias_ref[h, 1, :, near:]
            m_new = jnp.concatenate([jnp.max(s_lo, axis=0, keepdims=True),
                                     jnp.max(s_hi, axis=0, keepdims=True)], axis=1)
        else:
            m_new = jnp.max(s, axis=0, keepdims=True)
        if j > 0:
            m_old = m_ref[i % 2, n]
            m_new = jnp.maximum(m_old, m_new)
            alphas[u] = jnp.exp2(m_old - m_new)
        if on_diag:
            p_lo = jnp.exp2(s_lo - m_new[:, :near]).astype(BF16)
            p_hi = jnp.exp2(s_hi - m_new[:, near:]).astype(BF16)
            p_lo = jnp.concatenate([p_lo, jnp.zeros((T - near, near), BF16)], axis=0)
            probs[u] = jnp.concatenate([p_lo, p_hi], axis=1)
        else:
            probs[u] = jnp.exp2(s - m_new).astype(BF16)
        m_ref[i % 2, n] = m_new

    def value(u):
        i, j, n = units[u]
        vc = vt_ref[n // 2, :, rows(j)]
        pv = jnp.dot(vc, probs.pop(u), preferred_element_type=F32)
        a_ref[i % 2, n] = pv if j == 0 else pv + alphas.pop(u) * a_ref[i % 2, n]

    for step in range(len(units) + 2 * ATTN_LAG):
        if step < len(units):
            if units[step][1:] == (0, 0):
                prepare(units[step][0])
            score(step)
        if 0 <= step - ATTN_LAG < len(units):
            softmax(step - ATTN_LAG)
        done = step - 2 * ATTN_LAG
        if 0 <= done < len(units):
            value(done)
            i, j, n = units[done]
            if (j, n) == (i, NS - 1):
                finalize(i)

    for w_ref, wb_ref in zip(cast_in, cast_out):
        wb_ref[...] = w_ref[...].astype(BF16)


def _attention(qkv, bias_tiles, lamv, subg, weights):
    bsz, s, _ = qkv.shape
    T = ATTN_TILE
    W = ATTN_WIDTH
    NS = 2 * N_HEADS_A
    steps = bsz

    def slab_spec(w, squeeze):
        rows, cols = w.shape[-2:]
        assert rows % (steps * CAST_ROWS) == 0
        block = (rows // steps, cols)
        if squeeze:
            return pl.BlockSpec((None,) + block, lambda b: (0, b, 0))
        return pl.BlockSpec(block, lambda b: (b, 0))

    seq = lambda col: pl.BlockSpec((None, s, W), lambda b: (b, 0, col))
    out = pl.pallas_call(
        _attn_kernel,
        grid=(bsz,),
        in_specs=[_resident((4, HEAD_QK)),
                  _resident((1, HEAD_V)),
                  seq(0), seq(1), seq(2),
                  _resident((N_HEADS_A, 2, T, T))] + [slab_spec(w, True) for w in weights],
        out_specs=[seq(0)] + [slab_spec(w, False) for w in weights],
        out_shape=[jax.ShapeDtypeStruct((bsz, s, W), BF16)]
        + [jax.ShapeDtypeStruct(w.shape[-2:], BF16) for w in weights],
        scratch_shapes=[pltpu.VMEM((N_HEADS_A, HEAD_V + ONES_ROWS, s), BF16),
                        pltpu.VMEM((2, NS, T, HEAD_V), BF16),
                        pltpu.VMEM((2, NS, 1, T), F32),
                        pltpu.VMEM((2, NS, HEAD_V + ONES_ROWS, T), F32)],
        compiler_params=_params(("arbitrary",)),
        name="diff_attn",
    )(lamv, subg.reshape(1, HEAD_V), qkv, qkv, qkv, bias_tiles, *weights)
    return out[0], out[1:]


def _mixffn_kernel(x_ref, ada_ref, a_ref, u_ref, gate_ref, woa_ref, woc_ref, boc_ref,
                   wout_ref, g2_ref, w1_ref, w2_ref, gf_ref, o_ref,
                   x1_ref, h2_ref, act_ref):
    D = D_MODEL
    a = jnp.dot(a_ref[...], woa_ref[...], preferred_element_type=F32)
    cv = jnp.dot(u_ref[...], woc_ref[...], preferred_element_type=F32) + boc_ref[...]
    gate = gate_ref[...]
    y = (_sigmoid(gate[:, :D].astype(F32)) * a
         + _sigmoid(gate[:, D:].astype(F32)) * cv)
    yb = y.astype(BF16)
    tm = yb.shape[0]
    halves = [(r, r + tm // 2) for r in (0, tm // 2)]
    for r0, r1 in halves:
        z = jnp.dot(yb[r0:r1], wout_ref[...], preferred_element_type=F32)
        x1 = x_ref[r0:r1, :] + ada_ref[2:3, :] * z
        x1_ref[r0:r1, :] = x1
        ms = jnp.mean(x1 * x1, axis=-1, keepdims=True)
        yn = x1 * lax.rsqrt(ms + EPS)
        h2_ref[r0:r1, :] = ((yn * g2_ref[...]) * (1.0 + ada_ref[4:5, :])
                            + ada_ref[3:4, :]).astype(BF16)

    def ffn_chunk(c, r0, r1):
        h = h2_ref[r0:r1, :]
        fg = jnp.dot(h, w1_ref[:, c:c + FF_CHUNK], preferred_element_type=F32)
        fu = jnp.dot(h, w1_ref[:, D_FF + c:D_FF + c + FF_CHUNK],
                     preferred_element_type=F32)
        act_ref[r0:r1, c:c + FF_CHUNK] = (_silu(fg) * fu).astype(BF16)

    for r0, r1 in halves:
        ffn_chunk(0, r0, r1)
    for c in range(FF_CHUNK, D_FF, FF_CHUNK):
        ffn_chunk(c, 0, tm)
    for r0, r1 in halves:
        z2 = jnp.dot(act_ref[r0:r1, :], w2_ref[...], preferred_element_type=F32)
        x2 = x1_ref[r0:r1, :] + ada_ref[5:6, :] * z2
        ms2 = jnp.mean(x2 * x2, axis=-1, keepdims=True)
        o_ref[r0:r1, :] = (x2 * lax.rsqrt(ms2 + EPS)) * gf_ref[...]


def _mix_ffn(x, ada3, a_n, u, gates, woa, woc, boc, wout, g2, w1, w2, gf):
    bsz, s, d = x.shape
    tm = ROW_TILE
    row = lambda n: pl.BlockSpec((None, tm, n), lambda b, i: (b, i, 0))
    return pl.pallas_call(
        _mixffn_kernel,
        grid=(bsz, s // tm),
        in_specs=[row(d),
                  pl.BlockSpec((None, 6, d), lambda b, i: (b, 0, 0)),
                  row(ATTN_WIDTH), row(CONV_WIDTH), row(GATE_COLS),
                  _resident((ATTN_WIDTH, d)), _resident((CONV_WIDTH, d)),
                  _resident((1, d)), _resident((d, d)), _resident((1, d)),
                  _resident((d, 2 * D_FF)), _resident((D_FF, d)), _resident((1, d))],
        out_specs=row(d),
        out_shape=jax.ShapeDtypeStruct((bsz, s, d), F32),
        scratch_shapes=[pltpu.VMEM((tm, d), F32), pltpu.VMEM((tm, d), BF16),
                        pltpu.VMEM((tm, D_FF), BF16)],
        compiler_params=_params(("parallel", "arbitrary")),
        name="mix_ffn",
    )(x, ada3, a_n, u, gates, woa, woc, boc.reshape(1, d), wout, g2.reshape(1, d),
      w1, w2, gf.reshape(1, d))


def kernel(x, c, w_ada, b_ada, norm1_g, norm2_g, final_g, w_in, lambda_q1, lambda_k1,
           lambda_q2, lambda_k2, rel_bias, attn_sub_g, w_o_attn, conv_w, conv_b,
           conv_ln_g, conv_ln_b, w_o_conv, b_o_conv, w_out, w_ffn_in, w_ffn_out):
    bsz, s, d = x.shape
    assert w_ada.shape[0] == 1, "single-layer block"
    assert (d, s % ROW_TILE, s % ATTN_TILE) == (D_MODEL, 0, 0)
    assert ATTN_TILE == 2 * MAX_DISTANCE
    assert PROJ_CHUNK == CONV_WIDTH and ATTN_WIDTH % PROJ_CHUNK == 0
    l = 0
    ada, w_in_bf, bias_tiles = _prologue(c, w_ada, b_ada, w_in, rel_bias)
    ada3 = ada.reshape(bsz, 6, d)
    qkv, u, gates = _in_proj(x, ada3, norm1_g[l], w_in_bf, conv_w[l],
                             conv_b[l], conv_ln_g[l], conv_ln_b[l])
    lamv = jnp.stack([lambda_q1[l], lambda_k1[l], lambda_q2[l], lambda_k2[l]])
    a_n, (woa, woc, wout, w1, w2) = _attention(
        qkv, bias_tiles, lamv, attn_sub_g[l],
        [w_o_attn, w_o_conv, w_out, w_ffn_in, w_ffn_out])
    return _mix_ffn(x, ada3, a_n, u, gates, woa, woc, b_o_conv[l], wout, norm2_g[l],
                    w1, w2, final_g)
```
